```python
import jax, jax.numpy as jnp
from jax import lax
import numpy as np

D_MODEL = 1024
BATCH = 8
SEQ = 2048
DEPTH = 1
DEC_BATCH = 128
DEC_SEQ = 8
PAST_LEN = 16384
PAGE_SIZE = 128

D_CONV = 512
CONV_WIDTH = 31
HG_HEADS = 4
HG_DK = 128
HG_DV = 128
D_HGRN = HG_HEADS * HG_DK
MEM_HEADS = 4
MEM_HEAD_DIM = 128
D_MEM = MEM_HEADS * MEM_HEAD_DIM
N_MEM = 256
N_BRANCH = 3
D_FF = -(-(8 * D_MODEL) // (3 * 256)) * 256
CHUNK = 32
EPS = 1e-6
D_IN = 2 * D_CONV + 4 * D_HGRN + D_MEM + N_BRANCH * D_MODEL

kernel_name = 'gated_conformer_hgrn2_memory_decoder_step'


def _rmsnorm(x, gain):
    xf = x.astype(jnp.float32)
    y = xf * lax.rsqrt(jnp.mean(xf * xf, axis=-1, keepdims=True) + EPS)
    return (y * gain.astype(jnp.float32)).astype(x.dtype)


def _layernorm(x, gain, bias):
    xf = x.astype(jnp.float32)
    mu = jnp.mean(xf, axis=-1, keepdims=True)
    var = jnp.mean(jnp.square(xf - mu), axis=-1, keepdims=True)
    y = (xf - mu) * lax.rsqrt(var + EPS)
    return (y * gain.astype(jnp.float32) + bias.astype(jnp.float32)).astype(x.dtype)


def _causal_dwconv(u_ext, conv_w, conv_b):
    out = lax.conv_general_dilated(
        u_ext, conv_w[:, None, :].astype(u_ext.dtype), window_strides=(1,), padding='VALID',
        dimension_numbers=('NWC', 'WIO', 'NWC'), feature_group_count=u_ext.shape[-1])
    return out + conv_b.astype(out.dtype)


def _hgrn2_chunkwise(q, k, v, logf, s0):
    B, L = q.shape[0], q.shape[1]
    n_chunks = -(-L // CHUNK)
    pad = n_chunks * CHUNK - L

    def prep(t):
        t = jnp.pad(t, ((0, 0), (0, pad), (0, 0), (0, 0)))
        return t.reshape((B, n_chunks, CHUNK) + t.shape[2:])

    q, k, v, logf = prep(q), prep(k), prep(v), prep(logf)
    b = jnp.cumsum(logf, axis=2)
    b_last = b[:, :, -1:]
    q_dec = q * jnp.exp(b)
    k_inv = k * jnp.exp(-b)
    k_end = k * jnp.exp(b_last - b)
    causal = jnp.tril(jnp.ones((CHUNK, CHUNK), dtype=bool))
    scores = jnp.where(causal, jnp.einsum('bnthd,bnshd->bnhts', q_dec, k_inv), 0.0)
    o_intra = jnp.einsum('bnhts,bnshv->bnthv', scores, v)
    chunk_kv = jnp.einsum('bnshd,bnshv->bnhdv', k_end, v)
    decay = jnp.exp(b[:, :, -1])

    def step(S, xs):
        qd, dec, kv = xs
        o = jnp.einsum('bthd,bhdv->bthv', qd, S)
        S = dec[..., None] * S + kv
        return S, o

    xs = (jnp.moveaxis(q_dec, 1, 0), jnp.moveaxis(decay, 1, 0), jnp.moveaxis(chunk_kv, 1, 0))
    s_final, o_inter = lax.scan(step, s0, xs)
    o = o_intra + jnp.moveaxis(o_inter, 0, 1)
    o = o.reshape((B, n_chunks * CHUNK) + o.shape[3:])[:, :L]
    return o, s_final


def _memory_kv(mem, mem_norm_g, w_mem_kv):
    m = _rmsnorm(mem, mem_norm_g)
    kv = jnp.einsum('bmd,de->bme', m, w_mem_kv)
    k, v = jnp.split(kv, 2, axis=-1)
    shp = (mem.shape[0], mem.shape[1], MEM_HEADS, MEM_HEAD_DIM)
    return k.reshape(shp), v.reshape(shp)


def _mixer(h, conv_prefix, hg_state, mem_k, mem_v, lb, w_in, conv_w, conv_b, conv_ln_g, conv_ln_b,
           w_conv_out, hg_norm_g, w_hg_out, w_mem_out, w_out):
    B, L, _ = h.shape
    z = jnp.einsum('bld,de->ble', h, w_in)
    sizes = (D_CONV, D_CONV, D_HGRN, D_HGRN, D_HGRN, D_HGRN, D_MEM)
    idx = [int(i) for i in np.cumsum(sizes)]
    ca, cb, hq, hf, hi, hgate, mq, gl = jnp.split(z, idx, axis=-1)

    u = ca * jax.nn.sigmoid(cb)
    u_ext = jnp.concatenate([conv_prefix.astype(u.dtype), u], axis=1)
    c = _causal_dwconv(u_ext, conv_w, conv_b)
    c = jax.nn.silu(_layernorm(c, conv_ln_g, conv_ln_b))
    p_conv = jnp.einsum('blc,cd->bld', c, w_conv_out)
    new_conv = u_ext[:, u_ext.shape[1] - (CONV_WIDTH - 1):]

    fx = hf.astype(jnp.float32).reshape(B, L, HG_HEADS, HG_DK)
    lbh = lb.reshape(HG_HEADS, HG_DK)
    logf = jnp.log(lbh + (1.0 - lbh) * jax.nn.sigmoid(fx))
    k = (1.0 - lbh) * jax.nn.sigmoid(-fx)
    q = jax.nn.silu(hq.astype(jnp.float32)).reshape(B, L, HG_HEADS, HG_DK)
    v = hi.astype(jnp.float32).reshape(B, L, HG_HEADS, HG_DV)
    o, new_hg = _hgrn2_chunkwise(q, k, v, logf, hg_state.astype(jnp.float32))
    o = _rmsnorm(o, hg_norm_g).astype(h.dtype).reshape(B, L, D_HGRN) * jax.nn.silu(hgate)
    p_hg = jnp.einsum('ble,ed->bld', o, w_hg_out)

    qm = mq.reshape(B, L, MEM_HEADS, MEM_HEAD_DIM)
    s = jnp.einsum('blhd,bmhd->bhlm', qm, mem_k).astype(jnp.float32) * (MEM_HEAD_DIM ** -0.5)
    p = jax.nn.softmax(s, axis=-1).astype(mem_v.dtype)
    om = jnp.einsum('bhlm,bmhd->blhd', p, mem_v).reshape(B, L, D_MEM)
    p_mem = jnp.einsum('ble,ed->bld', om, w_mem_out)

    g = jax.nn.sigmoid(gl.reshape(B, L, N_BRANCH, D_MODEL))
    merged = g[:, :, 0] * p_conv + g[:, :, 1] * p_hg + g[:, :, 2] * p_mem
    out = jnp.einsum('bld,de->ble', merged, w_out)
    return out, new_conv, new_hg.astype(hg_state.dtype)


def _layer(x, conv_prefix, hg_state, mem_k, mem_v, lb, norm_pre_mix, norm_post_mix, norm_pre_ffn,
           norm_post_ffn, w_in, conv_w, conv_b, conv_ln_g, conv_ln_b, w_conv_out, hg_norm_g, w_hg_out,
           w_mem_out, w_out, w_ffn_gate, w_ffn_up, w_ffn_down):
    h = _rmsnorm(x, norm_pre_mix)
    m, new_conv, new_hg = _mixer(h, conv_prefix, hg_state, mem_k, mem_v, lb, w_in, conv_w, conv_b,
                                 conv_ln_g, conv_ln_b, w_conv_out, hg_norm_g, w_hg_out, w_mem_out, w_out)
    x = x + _rmsnorm(m, norm_post_mix)
    h = _rmsnorm(x, norm_pre_ffn)
    f = jax.nn.silu(jnp.einsum('bld,df->blf', h, w_ffn_gate)) * jnp.einsum('bld,df->blf', h, w_ffn_up)
    f = jnp.einsum('blf,fd->bld', f, w_ffn_down)
    x = x + _rmsnorm(f, norm_post_ffn)
    return x, new_conv, new_hg


def setup_inputs(seed: int = 0) -> dict:
    key = jax.random.key(seed)
    ks = iter(jax.random.split(key, 32))

    def nrm(shape, scale):
        return jax.random.normal(next(ks), shape, jnp.float32) * scale

    def gain(shape):
        return 1.0 + nrm(shape, 0.02)

    return {
        'x_prompt': nrm((BATCH, SEQ, D_MODEL), 1.0),
        'x_sample': nrm((DEC_BATCH, DEC_SEQ, D_MODEL), 1.0),
        'mem_prompt': nrm((BATCH, N_MEM, D_MODEL), 1.0),
        'state_conv': nrm((DEPTH, DEC_BATCH, CONV_WIDTH - 1, D_CONV), 0.5),
        'state_hgrn': nrm((DEPTH, DEC_BATCH, HG_HEADS, HG_DK, HG_DV), 0.3),
        'cache_mem_k': nrm((DEPTH, DEC_BATCH, N_MEM, MEM_HEADS, MEM_HEAD_DIM), 1.0),
        'cache_mem_v': nrm((DEPTH, DEC_BATCH, N_MEM, MEM_HEADS, MEM_HEAD_DIM), 1.0),
        'norm_pre_mix': gain((DEPTH, D_MODEL)),
        'norm_post_mix': gain((DEPTH, D_MODEL)),
        'norm_pre_ffn': gain((DEPTH, D_MODEL)),
        'norm_post_ffn': gain((DEPTH, D_MODEL)),
        'w_in': nrm((DEPTH, D_MODEL, D_IN), D_MODEL ** -0.5),
        'conv_w': nrm((DEPTH, CONV_WIDTH, D_CONV), CONV_WIDTH ** -0.5),
        'conv_b': nrm((DEPTH, D_CONV), 0.02),
        'conv_ln_g': gain((DEPTH, D_CONV)),
        'conv_ln_b': nrm((DEPTH, D_CONV), 0.02),
        'w_conv_out': nrm((DEPTH, D_CONV, D_MODEL), D_CONV ** -0.5),
        'hg_lb_logits': nrm((DEPTH + 1, D_HGRN), 0.5),
        'hg_norm_g': gain((DEPTH, HG_DV)),
        'w_hg_out': nrm((DEPTH, D_HGRN, D_MODEL), D_HGRN ** -0.5),
        'mem_norm_g': gain((DEPTH, D_MODEL)),
        'w_mem_kv': nrm((DEPTH, D_MODEL, 2 * D_MEM), D_MODEL ** -0.5),
        'w_mem_out': nrm((DEPTH, D_MEM, D_MODEL), D_MEM ** -0.5),
        'w_out': nrm((DEPTH, D_MODEL, D_MODEL), D_MODEL ** -0.5),
        'w_ffn_gate': nrm((DEPTH, D_MODEL, D_FF), D_MODEL ** -0.5),
        'w_ffn_up': nrm((DEPTH, D_MODEL, D_FF), D_MODEL ** -0.5),
        'w_ffn_down': nrm((DEPTH, D_FF, D_MODEL), D_FF ** -0.5),
    }


def reference(x_prompt, x_sample, mem_prompt, state_conv, state_hgrn, cache_mem_k, cache_mem_v,
              norm_pre_mix, norm_post_mix, norm_pre_ffn, norm_post_ffn, w_in, conv_w, conv_b,
              conv_ln_g, conv_ln_b, w_conv_out, hg_lb_logits, hg_norm_g, w_hg_out, mem_norm_g,
              w_mem_kv, w_mem_out, w_out, w_ffn_gate, w_ffn_up, w_ffn_down):
    lb_all = jnp.cumsum(jax.nn.softmax(hg_lb_logits.astype(jnp.float32), axis=0), axis=0)
    xp, xs = x_prompt, x_sample
    conv_p, hg_p, mk_p, mv_p, conv_s, hg_s = [], [], [], [], [], []
    for l in range(DEPTH):
        shared = (norm_pre_mix[l], norm_post_mix[l], norm_pre_ffn[l], norm_post_ffn[l], w_in[l],
                  conv_w[l], conv_b[l], conv_ln_g[l], conv_ln_b[l], w_conv_out[l], hg_norm_g[l],
                  w_hg_out[l], w_mem_out[l], w_out[l], w_ffn_gate[l], w_ffn_up[l], w_ffn_down[l])
        lb = lb_all[l]
        mk, mv = _memory_kv(mem_prompt, mem_norm_g[l], w_mem_kv[l])
        prefix0 = jnp.zeros((xp.shape[0], CONV_WIDTH - 1, D_CONV), xp.dtype)
        s00 = jnp.zeros((xp.shape[0], HG_HEADS, HG_DK, HG_DV), xp.dtype)
        xp, nc_p, nh_p = _layer(xp, prefix0, s00, mk, mv, lb, *shared)
        xs, nc_s, nh_s = _layer(xs, state_conv[l], state_hgrn[l], cache_mem_k[l], cache_mem_v[l], lb, *shared)
        conv_p.append(nc_p)
        hg_p.append(nh_p)
        mk_p.append(mk)
        mv_p.append(mv)
        conv_s.append(nc_s)
        hg_s.append(nh_s)
    return (xp, xs, jnp.stack(conv_p), jnp.stack(hg_p), jnp.stack(mk_p), jnp.stack(mv_p),
            jnp.stack(conv_s), jnp.stack(hg_s))
```

```python
import functools

import jax
import jax.numpy as jnp
from jax import lax
from jax.experimental import pallas as pl
from jax.experimental.pallas import tpu as pltpu

D_MODEL = 1024
D_CONV = 512
CONV_WIDTH = 31
CONV_PREFIX = CONV_WIDTH - 1
HG_HEADS = 4
HG_DK = 128
HG_DV = 128
D_HGRN = HG_HEADS * HG_DK
MEM_HEADS = 4
MEM_HEAD_DIM = 128
D_MEM = MEM_HEADS * MEM_HEAD_DIM
CHUNK = 32
EPS = 1e-6

COL_CA = 0
COL_CB = COL_CA + D_CONV
COL_HQ = COL_CB + D_CONV
COL_HF = COL_HQ + D_HGRN
COL_HI = COL_HF + D_HGRN
COL_HGATE = COL_HI + D_HGRN
COL_MQ = COL_HGATE + D_HGRN
COL_GATES = COL_MQ + D_MEM

SUBLANES = 8
VMEM_LIMIT = 56 * 1024 * 1024

PROMPT_TILE = 256
SAMPLE_GROUP = 8
FFN_TILE = 512
CONV_ROWS = 32

BF16 = jnp.bfloat16
F32 = jnp.float32


def _dot(a, b):
    return jnp.dot(a, b, preferred_element_type=F32)


def _dot_nt(a, b):
    return lax.dot_general(a, b, (((1,), (1,)), ((), ())), preferred_element_type=F32)


def _dot_tn(a, b):
    return lax.dot_general(a, b, (((0,), (0,)), ((), ())), preferred_element_type=F32)


def _rms(x, gain):
    return x * lax.rsqrt(jnp.mean(x * x, axis=-1, keepdims=True) + EPS) * gain


def _sigmoid(x):
    return 1.0 / (1.0 + jnp.exp(-x))


def _silu(x):
    return x * _sigmoid(x)


def _lower_bound(lb_logits):
    m = jnp.max(lb_logits, axis=0, keepdims=True)
    e = jnp.exp(lb_logits - m)
    return e[0:1] / jnp.sum(e, axis=0, keepdims=True)


def _segment_cumsum(x, seg):
    pos = lax.broadcasted_iota(jnp.int32, x.shape, 0) & (seg - 1)
    s = 1
    while s < seg:
        x = x + jnp.where(pos >= s, pltpu.roll(x, s, axis=0), 0.0)
        s *= 2
    return x


def _conv_ln_silu(c, conv_b, ln_g, ln_b):
    c = c + conv_b
    mu = jnp.mean(c, axis=-1, keepdims=True)
    d = c - mu
    var = jnp.mean(d * d, axis=-1, keepdims=True)
    return _silu(d * lax.rsqrt(var + EPS) * ln_g + ln_b)


def _hgrn_gates(hq, hf, lb, seg):
    q = _silu(hq)
    logf = jnp.log(lb + (1.0 - lb) * _sigmoid(hf))
    k = (1.0 - lb) * _sigmoid(-hf)
    b = _segment_cumsum(logf, seg)
    rows = b.shape[0]
    b3 = b.reshape(rows // seg, seg, D_HGRN)
    b_last3 = b3[:, seg - 1:seg, :]
    rest = (b_last3 - b3).reshape(rows, D_HGRN)
    q_dec = q * jnp.exp(b)
    k_inv = k * jnp.exp(-b)
    k_end = k * jnp.exp(rest)
    return q_dec, k_inv, k_end, b_last3.reshape(rows // seg, D_HGRN)


def _head_rms_gate(o, hgate, gain4):
    parts = []
    for h in range(HG_HEADS):
        oh = o[:, h * HG_DV:(h + 1) * HG_DV]
        parts.append(oh * lax.rsqrt(jnp.mean(oh * oh, axis=-1, keepdims=True) + EPS))
    return jnp.concatenate(parts, axis=-1) * gain4 * _silu(hgate)


def _merge_out(hb, x, p_conv, p_hg, p_mem, w_in_ref, w_out_ref, post_g):
    acc = None
    for i, p in enumerate((p_conv, p_hg, p_mem)):
        c0 = COL_GATES + i * D_MODEL
        g = _sigmoid(_dot(hb, w_in_ref[:, c0:c0 + D_MODEL]))
        acc = g * p if acc is None else acc + g * p
    m = _dot(acc.astype(BF16), w_out_ref[...])
    return x + _rms(m, post_g)


def _memory_kv_kernel(mem_ref, g_ref, w_ref, k_ref, v_ref):
    m = _rms(mem_ref[...], g_ref[...]).astype(BF16)
    k_ref[...] = _dot(m, w_ref[:, :D_MEM])
    v_ref[...] = _dot(m, w_ref[:, D_MEM:])


def _memory_kv(mem2d, gain, w_bf16, tile):
    rows = mem2d.shape[0]
    return pl.pallas_call(
        _memory_kv_kernel,
        grid=(rows // tile,),
        in_specs=[
            pl.BlockSpec((tile, D_MODEL), lambda i: (i, 0)),
            pl.BlockSpec((1, D_MODEL), lambda i: (0, 0)),
            pl.BlockSpec((D_MODEL, 2 * D_MEM), lambda i: (0, 0)),
        ],
        out_specs=[
            pl.BlockSpec((tile, D_MEM), lambda i: (i, 0)),
            pl.BlockSpec((tile, D_MEM), lambda i: (i, 0)),
        ],
        out_shape=[jax.ShapeDtypeStruct((rows, D_MEM), F32)] * 2,
        compiler_params=pltpu.CompilerParams(dimension_semantics=("parallel",)),
        name="memory_kv",
    )(mem2d, gain, w_bf16)


def _prompt_mixer_kernel(x_ref, mk_ref, mv_ref, pre_g_ref, post_g_ref, conv_w_ref, conv_b_ref,
                         ln_g_ref, ln_b_ref, lb_logits_ref, hg_g_ref, w_in_ref, w_conv_out_ref,
                         w_hg_out_ref, w_mem_out_ref, w_out_ref,
                         y_ref, new_conv_ref, new_hg_ref,
                         shift_ref, state_t_ref, c_ref, kb_ref, vb_ref):
    t = pl.program_id(1)
    last_t = pl.num_programs(1) - 1
    tl = x_ref.shape[0]

    @pl.when(t == 0)
    def _():
        shift_ref[0, 0:CONV_ROWS, :] = jnp.zeros((CONV_ROWS, D_CONV), F32)
        state_t_ref[...] = jnp.zeros_like(state_t_ref)
        kb_ref[...] = mk_ref[...].astype(BF16)
        vb_ref[...] = mv_ref[...].astype(BF16)

    x = x_ref[...]
    hb = _rms(x, pre_g_ref[...]).astype(BF16)

    u = _dot(hb, w_in_ref[:, COL_CA:COL_CB]) * _sigmoid(_dot(hb, w_in_ref[:, COL_CB:COL_HQ]))
    shift_ref[0, CONV_ROWS:CONV_ROWS + tl, :] = u
    for r in range(1, SUBLANES):
        shift_ref[r, 0:tl + CONV_ROWS - SUBLANES, :] = shift_ref[0, r:r + tl + CONV_ROWS - SUBLANES, :]

    @pl.when(t == last_t)
    def _():
        new_conv_ref[...] = shift_ref[0, tl + CONV_ROWS - CONV_PREFIX:tl + CONV_ROWS, :]

    def conv_block(i, carry):
        base = pl.multiple_of(i * CONV_ROWS, CONV_ROWS)
        acc = jnp.zeros((CONV_ROWS, D_CONV), F32)
        for j in range(CONV_WIDTH):
            s = j + (CONV_ROWS - CONV_PREFIX)
            r, a = s % SUBLANES, s // SUBLANES
            acc = acc + shift_ref[r, pl.ds(base + a * SUBLANES, CONV_ROWS), :] * conv_w_ref[j:j + 1, :]
        c_ref[pl.ds(base, CONV_ROWS), :] = acc
        return carry

    lax.fori_loop(0, tl // CONV_ROWS, conv_block, 0)
    shift_ref[0, 0:CONV_ROWS, :] = shift_ref[0, tl:tl + CONV_ROWS, :]
    c = _conv_ln_silu(c_ref[...], conv_b_ref[...], ln_g_ref[...], ln_b_ref[...])
    p_conv = _dot(c.astype(BF16), w_conv_out_ref[...])

    lb = _lower_bound(lb_logits_ref[...])
    q_dec, k_inv, k_end, b_last = _hgrn_gates(
        _dot(hb, w_in_ref[:, COL_HQ:COL_HF]), _dot(hb, w_in_ref[:, COL_HF:COL_HI]), lb, CHUNK)
    decay = jnp.exp(b_last)
    v = _dot(hb, w_in_ref[:, COL_HI:COL_HGATE])
    n_chunks = tl // CHUNK
    row = lax.broadcasted_iota(jnp.int32, (tl, tl), 0)
    col = lax.broadcasted_iota(jnp.int32, (tl, tl), 1)
    causal = ((row ^ col) < CHUNK) & (col <= row)
    o_heads = []
    for h in range(HG_HEADS):
        hs = slice(h * HG_DK, (h + 1) * HG_DK)
        qd = q_dec[:, hs].astype(BF16)
        ki = k_inv[:, hs].astype(BF16)
        ke = k_end[:, hs].astype(BF16)
        vh = v[:, hs].astype(BF16)
        scores = jnp.where(causal, _dot_nt(qd, ki), 0.0)
        o_intra = _dot(scores.astype(BF16), vh)
        st = state_t_ref[h]
        o_inter = []
        for n in range(n_chunks):
            rs = slice(n * CHUNK, (n + 1) * CHUNK)
            o_inter.append(_dot_nt(qd[rs], st.astype(BF16)))
            st = st * decay[n:n + 1, hs] + _dot_tn(vh[rs], ke[rs])
        state_t_ref[h] = st
        o_heads.append(o_intra + jnp.concatenate(o_inter, axis=0))

    @pl.when(t == last_t)
    def _():
        for h in range(HG_HEADS):
            new_hg_ref[h] = state_t_ref[h].T

    o = _head_rms_gate(jnp.concatenate(o_heads, axis=-1),
                       _dot(hb, w_in_ref[:, COL_HGATE:COL_MQ]), hg_g_ref[...])
    p_hg = _dot(o.astype(BF16), w_hg_out_ref[...])

    mq = _dot(hb, w_in_ref[:, COL_MQ:COL_GATES]).astype(BF16)
    om = []
    for h in range(MEM_HEADS):
        hs = slice(h * MEM_HEAD_DIM, (h + 1) * MEM_HEAD_DIM)
        s = _dot_nt(mq[:, hs], kb_ref[:, hs]) * (MEM_HEAD_DIM ** -0.5)
        e = jnp.exp(s - jnp.max(s, axis=-1, keepdims=True))
        p = e / jnp.sum(e, axis=-1, keepdims=True)
        om.append(_dot(p.astype(BF16), vb_ref[:, hs]))
    p_mem = _dot(jnp.concatenate(om, axis=-1).astype(BF16), w_mem_out_ref[...])

    y_ref[...] = _merge_out(hb, x, p_conv, p_hg, p_mem, w_in_ref, w_out_ref, post_g_ref[...])


def _const_spec(shape):
    zeros = (0,) * len(shape)
    return pl.BlockSpec(shape, lambda *_: zeros, pipeline_mode=pl.Buffered(1))


def _mixer_weight_specs(w_in, w_conv_out, w_hg_out, w_mem_out, w_out):
    return [_const_spec(w.shape) for w in (w_in, w_conv_out, w_hg_out, w_mem_out, w_out)]


def _vector_specs():
    return [
        _const_spec((1, D_MODEL)), _const_spec((1, D_MODEL)),
        _const_spec((CONV_WIDTH, D_CONV)), _const_spec((1, D_CONV)),
        _const_spec((1, D_CONV)), _const_spec((1, D_CONV)),
        _const_spec((2, D_HGRN)), _const_spec((1, D_HGRN)),
    ]


def _prompt_mixer(x, mk, mv, vectors, weights):
    batch, seq, _ = x.shape
    n_mem = mk.shape[1]
    tl = PROMPT_TILE
    return pl.pallas_call(
        _prompt_mixer_kernel,
        grid=(batch, seq // tl),
        in_specs=[
            pl.BlockSpec((None, tl, D_MODEL), lambda b, t: (b, t, 0)),
            pl.BlockSpec((None, n_mem, D_MEM), lambda b, t: (b, 0, 0)),
            pl.BlockSpec((None, n_mem, D_MEM), lambda b, t: (b, 0, 0)),
        ] + _vector_specs() + _mixer_weight_specs(*weights),
        out_specs=[
            pl.BlockSpec((None, tl, D_MODEL), lambda b, t: (b, t, 0)),
            pl.BlockSpec((None, CONV_PREFIX, D_CONV), lambda b, t: (b, 0, 0)),
            pl.BlockSpec((None, HG_HEADS, HG_DK, HG_DV), lambda b, t: (b, 0, 0, 0)),
        ],
        out_shape=[
            jax.ShapeDtypeStruct((batch, seq, D_MODEL), F32),
            jax.ShapeDtypeStruct((batch, CONV_PREFIX, D_CONV), F32),
            jax.ShapeDtypeStruct((batch, HG_HEADS, HG_DK, HG_DV), F32),
        ],
        scratch_shapes=[
            pltpu.VMEM((SUBLANES, tl + CONV_ROWS, D_CONV), F32),
            pltpu.VMEM((HG_HEADS, HG_DV, HG_DK), F32),
            pltpu.VMEM((tl, D_CONV), F32),
            pltpu.VMEM((n_mem, D_MEM), BF16),
            pltpu.VMEM((n_mem, D_MEM), BF16),
        ],
        compiler_params=pltpu.CompilerParams(
            dimension_semantics=("parallel", "arbitrary"), vmem_limit_bytes=VMEM_LIMIT),
        name="prompt_mixer",
    )(x, mk, mv, *vectors, *weights)


def _sample_mixer_kernel(x_ref, sc_ref, sh_ref, ck_ref, cv_ref, pre_g_ref, post_g_ref, conv_w_ref,
                         conv_b_ref, ln_g_ref, ln_b_ref, lb_logits_ref, hg_g_ref, w_in_ref,
                         w_conv_out_ref, w_hg_out_ref, w_mem_out_ref, w_out_ref,
                         y_ref, new_conv_ref, new_hg_ref,
                         ext_ref, c_ref, o_ref, om_ref):
    n_seq, dec = x_ref.shape[0], x_ref.shape[1]
    rows = n_seq * dec
    x = x_ref[...].reshape(rows, D_MODEL)
    hb = _rms(x, pre_g_ref[...]).astype(BF16)

    u = _dot(hb, w_in_ref[:, COL_CA:COL_CB]) * _sigmoid(_dot(hb, w_in_ref[:, COL_CB:COL_HQ]))
    for g in range(n_seq):
        ext_ref[g, 0:CONV_PREFIX, :] = sc_ref[g]
        ext_ref[g, CONV_PREFIX:CONV_PREFIX + dec, :] = u[g * dec:(g + 1) * dec]
        new_conv_ref[g] = ext_ref[g, dec:dec + CONV_PREFIX, :]
        acc = jnp.zeros((dec, D_CONV), F32)
        for j in range(CONV_WIDTH):
            acc = acc + ext_ref[g, j:j + dec, :] * conv_w_ref[j:j + 1, :]
        c_ref[g * dec:(g + 1) * dec, :] = acc
    c = _conv_ln_silu(c_ref[...], conv_b_ref[...], ln_g_ref[...], ln_b_ref[...])
    p_conv = _dot(c.astype(BF16), w_conv_out_ref[...])

    lb = _lower_bound(lb_logits_ref[...])
    q_dec, k_inv, k_end, b_last = _hgrn_gates(
        _dot(hb, w_in_ref[:, COL_HQ:COL_HF]), _dot(hb, w_in_ref[:, COL_HF:COL_HI]), lb, dec)
    decay = jnp.exp(b_last)
    v = _dot(hb, w_in_ref[:, COL_HI:COL_HGATE])
    row = lax.broadcasted_iota(jnp.int32, (dec, dec), 0)
    col = lax.broadcasted_iota(jnp.int32, (dec, dec), 1)
    causal = col <= row
    mq = _dot(hb, w_in_ref[:, COL_MQ:COL_GATES]).astype(BF16)
    for g in range(n_seq):
        rs = slice(g * dec, (g + 1) * dec)
        for h in range(HG_HEADS):
            hs = slice(h * HG_DK, (h + 1) * HG_DK)
            qd = q_dec[rs, hs].astype(BF16)
            ki = k_inv[rs, hs].astype(BF16)
            ke = k_end[rs, hs].astype(BF16)
            vh = v[rs, hs].astype(BF16)
            s0 = sh_ref[g, h]
            scores = jnp.where(causal, _dot_nt(qd, ki), 0.0)
            o_ref[rs, hs] = _dot(scores.astype(BF16), vh) + _dot(qd, s0.astype(BF16))
            decay_col = jnp.broadcast_to(decay[g:g + 1, hs], (HG_DV, HG_DK)).T
            new_hg_ref[g, h] = decay_col * s0 + _dot_tn(ke, vh)
        for h in range(MEM_HEADS):
            hs = slice(h * MEM_HEAD_DIM, (h + 1) * MEM_HEAD_DIM)
            s = _dot_nt(mq[rs, hs], ck_ref[g, :, hs].astype(BF16)) * (MEM_HEAD_DIM ** -0.5)
            e = jnp.exp(s - jnp.max(s, axis=-1, keepdims=True))
            p = e / jnp.sum(e, axis=-1, keepdims=True)
            om_ref[rs, hs] = _dot(p.astype(BF16), cv_ref[g, :, hs].astype(BF16))

    o = _head_rms_gate(o_ref[...], _dot(hb, w_in_ref[:, COL_HGATE:COL_MQ]), hg_g_ref[...])
    p_hg = _dot(o.astype(BF16), w_hg_out_ref[...])
    p_mem = _dot(om_ref[...].astype(BF16), w_mem_out_ref[...])
    y = _merge_out(hb, x, p_conv, p_hg, p_mem, w_in_ref, w_out_ref, post_g_ref[...])
    y_ref[...] = y.reshape(n_seq, dec, D_MODEL)


def _sample_mixer(x, state_conv, state_hgrn, cache_k, cache_v, vectors, weights):
    n, dec, _ = x.shape
    n_mem = cache_k.shape[1]
    g = SAMPLE_GROUP
    return pl.pallas_call(
        _sample_mixer_kernel,
        grid=(n // g,),
        in_specs=[
            pl.BlockSpec((g, dec, D_MODEL), lambda i: (i, 0, 0)),
            pl.BlockSpec((g, CONV_PREFIX, D_CONV), lambda i: (i, 0, 0)),
            pl.BlockSpec((g, HG_HEADS, HG_DK, HG_DV), lambda i: (i, 0, 0, 0)),
            pl.BlockSpec((g, n_mem, D_MEM), lambda i: (i, 0, 0)),
            pl.BlockSpec((g, n_mem, D_MEM), lambda i: (i, 0, 0)),
        ] + _vector_specs() + _mixer_weight_specs(*weights),
        out_specs=[
            pl.BlockSpec((g, dec, D_MODEL), lambda i: (i, 0, 0)),
            pl.BlockSpec((g, CONV_PREFIX, D_CONV), lambda i: (i, 0, 0)),
            pl.BlockSpec((g, HG_HEADS, HG_DK, HG_DV), lambda i: (i, 0, 0, 0)),
        ],
        out_shape=[
            jax.ShapeDtypeStruct((n, dec, D_MODEL), F32),
            jax.ShapeDtypeStruct((n, CONV_PREFIX, D_CONV), F32),
            jax.ShapeDtypeStruct((n, HG_HEADS, HG_DK, HG_DV), F32),
        ],
        scratch_shapes=[
            pltpu.VMEM((g, CONV_PREFIX + dec + 2, D_CONV), F32),
            pltpu.VMEM((g * dec, D_CONV), F32),
            pltpu.VMEM((g * dec, D_HGRN), F32),
            pltpu.VMEM((g * dec, D_MEM), F32),
        ],
        compiler_params=pltpu.CompilerParams(
            dimension_semantics=("parallel",), vmem_limit_bytes=VMEM_LIMIT),
        name="sample_mixer",
    )(x, state_conv, state_hgrn, cache_k, cache_v, *vectors, *weights)


def _ffn_kernel(x_ref, pre_g_ref, post_g_ref, w_gate_ref, w_up_ref, w_down_ref, y_ref):
    x = x_ref[...]
    hb = _rms(x, pre_g_ref[...]).astype(BF16)
    f = _silu(_dot(hb, w_gate_ref[...])) * _dot(hb, w_up_ref[...])
    d = _dot(f.astype(BF16), w_down_ref[...])
    y_ref[...] = x + _rms(d, post_g_ref[...])


def _ffn(x2d, pre_g, post_g, w_gate, w_up, w_down):
    rows = x2d.shape[0]
    tile = min(FFN_TILE, rows)
    return pl.pallas_call(
        _ffn_kernel,
        grid=(rows // tile,),
        in_specs=[
            pl.BlockSpec((tile, D_MODEL), lambda i: (i, 0)),
            _const_spec((1, D_MODEL)), _const_spec((1, D_MODEL)),
            _const_spec(w_gate.shape), _const_spec(w_up.shape), _const_spec(w_down.shape),
        ],
        out_specs=pl.BlockSpec((tile, D_MODEL), lambda i: (i, 0)),
        out_shape=jax.ShapeDtypeStruct((rows, D_MODEL), F32),
        compiler_params=pltpu.CompilerParams(
            dimension_semantics=("parallel",), vmem_limit_bytes=VMEM_LIMIT),
        name="ffn",
    )(x2d, pre_g, post_g, w_gate, w_up, w_down)


def kernel(x_prompt, x_sample, mem_prompt, state_conv, state_hgrn, cache_mem_k, cache_mem_v, norm_pre_mix, norm_post_mix, norm_pre_ffn, norm_post_ffn, w_in, conv_w, conv_b, conv_ln_g, conv_ln_b, w_conv_out, hg_lb_logits, hg_norm_g, w_hg_out, mem_norm_g, w_mem_kv, w_mem_out, w_out, w_ffn_gate, w_ffn_up, w_ffn_down):
    depth = w_in.shape[0]
    assert depth == 1 and hg_lb_logits.shape[0] == 2, "single-layer step"
    batch, seq, _ = x_prompt.shape
    n_dec, dec, _ = x_sample.shape
    n_mem = mem_prompt.shape[1]
    assert seq % PROMPT_TILE == 0 and n_dec % SAMPLE_GROUP == 0 and dec == SUBLANES

    vectors = (norm_pre_mix, norm_post_mix, conv_w[0], conv_b, conv_ln_g, conv_ln_b,
               hg_lb_logits, jnp.tile(hg_norm_g, (1, HG_HEADS)))
    weights = tuple(w[0].astype(BF16) for w in (w_in, w_conv_out, w_hg_out, w_mem_out, w_out))
    ffn_weights = tuple(w[0].astype(BF16) for w in (w_ffn_gate, w_ffn_up, w_ffn_down))

    mk, mv = _memory_kv(mem_prompt.reshape(batch * n_mem, D_MODEL), mem_norm_g,
                        w_mem_kv[0].astype(BF16), n_mem)
    mk = mk.reshape(batch, n_mem, D_MEM)
    mv = mv.reshape(batch, n_mem, D_MEM)

    xp, conv_p, hg_p = _prompt_mixer(x_prompt, mk, mv, vectors, weights)
    xs, conv_s, hg_s = _sample_mixer(
        x_sample, state_conv[0], state_hgrn[0],
        cache_mem_k[0].reshape(n_dec, n_mem, D_MEM), cache_mem_v[0].reshape(n_dec, n_mem, D_MEM),
        vectors, weights)

    yp = _ffn(xp.reshape(batch * seq, D_MODEL), norm_pre_ffn, norm_post_ffn, *ffn_weights)
    ys = _ffn(xs.reshape(n_dec * dec, D_MODEL), norm_pre_ffn, norm_post_ffn, *ffn_weights)

    kv_shape = (1, batch, n_mem, MEM_HEADS, MEM_HEAD_DIM)
    return (yp.reshape(batch, seq, D_MODEL), ys.reshape(n_dec, dec, D_MODEL),
            conv_p[None], hg_p[None], mk.reshape(kv_shape), mv.reshape(kv_shape),
            conv_s[None], hg_s[None])
```

```python
from typing import NamedTuple

import jax
import jax.numpy as jnp
from jax import lax
from jax.experimental import pallas as pl
from jax.experimental.pallas import tpu as pltpu

D_MODEL = 1024
D_CONV = 512
CONV_WIDTH = 31
CONV_PREFIX = CONV_WIDTH - 1
HG_HEADS = 4
HG_DK = 128
HG_DV = 128
D_HGRN = HG_HEADS * HG_DK
MEM_HEADS = 4
MEM_HEAD_DIM = 128
D_MEM = MEM_HEADS * MEM_HEAD_DIM
CHUNK = 32
EPS = 1e-6

COL_CA = 0
COL_CB = COL_CA + D_CONV
COL_HQ = COL_CB + D_CONV
COL_HF = COL_HQ + D_HGRN
COL_HI = COL_HF + D_HGRN
COL_HGATE = COL_HI + D_HGRN
COL_MQ = COL_HGATE + D_HGRN
COL_GATES = COL_MQ + D_MEM

SUBLANES = 8
VMEM_LIMIT = 56 * 1024 * 1024

PROMPT_TILE = 256
SAMPLE_GROUP = 8
FFN_TILE = 512
CONV_ROWS = 32

BF16 = jnp.bfloat16
F32 = jnp.float32


def _dot(a, b):
    return jnp.dot(a, b, preferred_element_type=F32)


def _dot_nt(a, b):
    return lax.dot_general(a, b, (((1,), (1,)), ((), ())), preferred_element_type=F32)


def _dot_tn(a, b):
    return lax.dot_general(a, b, (((0,), (0,)), ((), ())), preferred_element_type=F32)


def _rms(x, gain):
    return x * lax.rsqrt(jnp.mean(x * x, axis=-1, keepdims=True) + EPS) * gain


def _sigmoid(x):
    return 1.0 / (1.0 + jnp.exp(-x))


def _silu(x):
    return x * _sigmoid(x)


def _lower_bound(lb_logits):
    m = jnp.max(lb_logits, axis=0, keepdims=True)
    e = jnp.exp(lb_logits - m)
    return e[0:1] / jnp.sum(e, axis=0, keepdims=True)


def _segment_cumsum(x, seg):
    pos = lax.broadcasted_iota(jnp.int32, x.shape, 0) & (seg - 1)
    s = 1
    while s < seg:
        x = x + jnp.where(pos >= s, pltpu.roll(x, s, axis=0), 0.0)
        s *= 2
    return x


def _conv_ln_silu(c, conv_b, ln_g, ln_b):
    c = c + conv_b
    mu = jnp.mean(c, axis=-1, keepdims=True)
    d = c - mu
    var = jnp.mean(d * d, axis=-1, keepdims=True)
    return _silu(d * lax.rsqrt(var + EPS) * ln_g + ln_b)


def _hgrn_gates(hq, hf, lb, seg):
    q = _silu(hq)
    logf = jnp.log(lb + (1.0 - lb) * _sigmoid(hf))
    k = (1.0 - lb) * _sigmoid(-hf)
    b = _segment_cumsum(logf, seg)
    rows = b.shape[0]
    b3 = b.reshape(rows // seg, seg, D_HGRN)
    b_last3 = b3[:, seg - 1:seg, :]
    rest = (b_last3 - b3).reshape(rows, D_HGRN)
    q_dec = q * jnp.exp(b)
    k_inv = k * jnp.exp(-b)
    k_end = k * jnp.exp(rest)
    return q_dec, k_inv, k_end, b_last3.reshape(rows // seg, D_HGRN)


class _TileLevels(NamedTuple):
    q_mid: jax.Array
    k_mid: jax.Array
    q_64: jax.Array
    k_64: jax.Array
    q_128: jax.Array
    k_128: jax.Array
    q_tile: jax.Array
    k_tile: jax.Array
    decay: jax.Array


def _hgrn_tile_levels(hq, hf, lb):
    tl = hq.shape[0]
    n_blocks = 4
    block = 2 * CHUNK
    assert tl == n_blocks * block
    q = _silu(hq)
    logf = jnp.log(lb + (1.0 - lb) * _sigmoid(hf))
    k = (1.0 - lb) * _sigmoid(-hf)
    b = _segment_cumsum(logf, CHUNK)
    chunks = [b[n * CHUNK:(n + 1) * CHUNK] for n in range(2 * n_blocks)]
    totals = [c[CHUNK - 1:CHUNK] for c in chunks]
    e = jnp.concatenate([c - totals[n] if n % 2 == 0 else c for n, c in enumerate(chunks)], axis=0)
    q_mid = q * jnp.exp(e)
    k_mid = k * jnp.exp(-e)
    first = [totals[2 * j] for j in range(n_blocks)]
    second = [totals[2 * j + 1] for j in range(n_blocks)]
    both = [first[j] + second[j] for j in range(n_blocks)]

    def scaled(x, blocks, log_scales):
        return jnp.concatenate(
            [x[j * block:(j + 1) * block] * jnp.exp(s) for j, s in zip(blocks, log_scales)], axis=0)

    every = range(n_blocks)
    return _TileLevels(
        q_mid=q_mid, k_mid=k_mid,
        q_64=scaled(q_mid, every, first),
        k_64=scaled(k_mid, every, second),
        q_128=scaled(q_mid, (2, 3), (first[2], first[3] + both[2])),
        k_128=scaled(k_mid, (0, 1), (second[0] + both[1], second[1])),
        q_tile=scaled(q_mid, every, [first[j] + sum(both[:j], 0.0) for j in every]),
        k_tile=scaled(k_mid, every, [second[j] + sum(both[j + 1:], 0.0) for j in every]),
        decay=jnp.exp(sum(both[1:], both[0])))


def _head_rms_gate(o, hgate, gain4):
    parts = []
    for h in range(HG_HEADS):
        oh = o[:, h * HG_DV:(h + 1) * HG_DV]
        parts.append(oh * lax.rsqrt(jnp.mean(oh * oh, axis=-1, keepdims=True) + EPS))
    return jnp.concatenate(parts, axis=-1) * gain4 * _silu(hgate)


def _merge_out(hb, x, p_conv, p_hg, p_mem, w_in_ref, w_out_ref, post_g):
    acc = None
    for i, p in enumerate((p_conv, p_hg, p_mem)):
        c0 = COL_GATES + i * D_MODEL
        g = _sigmoid(_dot(hb, w_in_ref[:, c0:c0 + D_MODEL]))
        acc = g * p if acc is None else acc + g * p
    m = _dot(acc.astype(BF16), w_out_ref[...])
    return x + _rms(m, post_g)


def _memory_kv_kernel(mem_ref, g_ref, w_ref, k_ref, v_ref):
    m = _rms(mem_ref[...], g_ref[...]).astype(BF16)
    k_ref[...] = _dot(m, w_ref[:, :D_MEM])
    v_ref[...] = _dot(m, w_ref[:, D_MEM:])


def _memory_kv(mem2d, gain, w_bf16, tile):
    rows = mem2d.shape[0]
    return pl.pallas_call(
        _memory_kv_kernel,
        grid=(rows // tile,),
        in_specs=[
            pl.BlockSpec((tile, D_MODEL), lambda i: (i, 0)),
            pl.BlockSpec((1, D_MODEL), lambda i: (0, 0)),
            pl.BlockSpec((D_MODEL, 2 * D_MEM), lambda i: (0, 0)),
        ],
        out_specs=[
            pl.BlockSpec((tile, D_MEM), lambda i: (i, 0)),
            pl.BlockSpec((tile, D_MEM), lambda i: (i, 0)),
        ],
        out_shape=[jax.ShapeDtypeStruct((rows, D_MEM), F32)] * 2,
        compiler_params=pltpu.CompilerParams(dimension_semantics=("parallel",)),
        name="memory_kv",
    )(mem2d, gain, w_bf16)


def _prompt_mixer_kernel(x_ref, mk_ref, mv_ref, pre_g_ref, post_g_ref, conv_w_ref, conv_b_ref,
                         ln_g_ref, ln_b_ref, lb_logits_ref, hg_g_ref, w_in_ref, w_conv_out_ref,
                         w_hg_out_ref, w_mem_out_ref, w_out_ref,
                         y_ref, new_conv_ref, new_hg_ref,
                         shift_ref, state_t_ref, c_ref, kb_ref, vb_ref):
    t = pl.program_id(1)
    last_t = pl.num_programs(1) - 1
    tl = x_ref.shape[0]

    @pl.when(t == 0)
    def _():
        shift_ref[0, 0:CONV_ROWS, :] = jnp.zeros((CONV_ROWS, D_CONV), F32)
        state_t_ref[...] = jnp.zeros_like(state_t_ref)
        kb_ref[...] = mk_ref[...].astype(BF16)
        vb_ref[...] = mv_ref[...].astype(BF16)

    x = x_ref[...]
    hb = _rms(x, pre_g_ref[...]).astype(BF16)

    u = _dot(hb, w_in_ref[:, COL_CA:COL_CB]) * _sigmoid(_dot(hb, w_in_ref[:, COL_CB:COL_HQ]))
    shift_ref[0, CONV_ROWS:CONV_ROWS + tl, :] = u
    for r in range(1, SUBLANES):
        shift_ref[r, 0:tl + CONV_ROWS - SUBLANES, :] = shift_ref[0, r:r + tl + CONV_ROWS - SUBLANES, :]

    @pl.when(t == last_t)
    def _():
        new_conv_ref[...] = shift_ref[0, tl + CONV_ROWS - CONV_PREFIX:tl + CONV_ROWS, :]

    def conv_block(i, carry):
        base = pl.multiple_of(i * CONV_ROWS, CONV_ROWS)
        acc = jnp.zeros((CONV_ROWS, D_CONV), F32)
        for j in range(CONV_WIDTH):
            s = j + (CONV_ROWS - CONV_PREFIX)
            r, a = s % SUBLANES, s // SUBLANES
            acc = acc + shift_ref[r, pl.ds(base + a * SUBLANES, CONV_ROWS), :] * conv_w_ref[j:j + 1, :]
        c_ref[pl.ds(base, CONV_ROWS), :] = acc
        return carry

    lax.fori_loop(0, tl // CONV_ROWS, conv_block, 0)
    shift_ref[0, 0:CONV_ROWS, :] = shift_ref[0, tl:tl + CONV_ROWS, :]
    c = _conv_ln_silu(c_ref[...], conv_b_ref[...], ln_g_ref[...], ln_b_ref[...])
    p_conv = _dot(c.astype(BF16), w_conv_out_ref[...])

    lb = _lower_bound(lb_logits_ref[...])
    lv = _hgrn_tile_levels(_dot(hb, w_in_ref[:, COL_HQ:COL_HF]),
                           _dot(hb, w_in_ref[:, COL_HF:COL_HI]), lb)
    v = _dot(hb, w_in_ref[:, COL_HI:COL_HGATE])
    half = tl // 2
    row = lax.broadcasted_iota(jnp.int32, (tl, tl), 0)
    col = lax.broadcasted_iota(jnp.int32, (tl, tl), 1)
    span = row ^ col
    near = (span < 2 * CHUNK) & (col <= row)
    mid = (span < 4 * CHUNK) & (col < row)
    o_heads = []
    for h in range(HG_HEADS):
        hs = slice(h * HG_DK, (h + 1) * HG_DK)
        vh = v[:, hs].astype(BF16)
        s_near = _dot_nt(lv.q_mid[:, hs].astype(BF16), lv.k_mid[:, hs].astype(BF16))
        s_mid = _dot_nt(lv.q_64[:, hs].astype(BF16), lv.k_64[:, hs].astype(BF16))
        s_far = _dot_nt(lv.q_128[:, hs].astype(BF16), lv.k_128[:, hs].astype(BF16))
        base = jnp.where(near, s_near, jnp.where(mid, s_mid, 0.0))
        scores = jnp.concatenate(
            [base[:half], jnp.concatenate([s_far, base[half:, half:]], axis=1)], axis=0)
        st = state_t_ref[h]
        o_heads.append(_dot(scores.astype(BF16), vh)
                       + _dot_nt(lv.q_tile[:, hs].astype(BF16), st.astype(BF16)))
        state_t_ref[h] = st * lv.decay[:, hs] + _dot_tn(vh, lv.k_tile[:, hs].astype(BF16))

    @pl.when(t == last_t)
    def _():
        for h in range(HG_HEADS):
            new_hg_ref[h] = state_t_ref[h].T

    o = _head_rms_gate(jnp.concatenate(o_heads, axis=-1),
                       _dot(hb, w_in_ref[:, COL_HGATE:COL_MQ]), hg_g_ref[...])
    p_hg = _dot(o.astype(BF16), w_hg_out_ref[...])

    mq = _dot(hb, w_in_ref[:, COL_MQ:COL_GATES]).astype(BF16)
    om = []
    for h in range(MEM_HEADS):
        hs = slice(h * MEM_HEAD_DIM, (h + 1) * MEM_HEAD_DIM)
        s = _dot_nt(mq[:, hs], kb_ref[:, hs]) * (MEM_HEAD_DIM ** -0.5)
        e = jnp.exp(s - jnp.max(s, axis=-1, keepdims=True))
        p = e / jnp.sum(e, axis=-1, keepdims=True)
        om.append(_dot(p.astype(BF16), vb_ref[:, hs]))
    p_mem = _dot(jnp.concatenate(om, axis=-1).astype(BF16), w_mem_out_ref[...])

    y_ref[...] = _merge_out(hb, x, p_conv, p_hg, p_mem, w_in_ref, w_out_ref, post_g_ref[...])


def _const_spec(shape):
    zeros = (0,) * len(shape)
    return pl.BlockSpec(shape, lambda *_: zeros, pipeline_mode=pl.Buffered(1))


def _mixer_weight_specs(w_in, w_conv_out, w_hg_out, w_mem_out, w_out):
    return [_const_spec(w.shape) for w in (w_in, w_conv_out, w_hg_out, w_mem_out, w_out)]


def _vector_specs():
    return [
        _const_spec((1, D_MODEL)), _const_spec((1, D_MODEL)),
        _const_spec((CONV_WIDTH, D_CONV)), _const_spec((1, D_CONV)),
        _const_spec((1, D_CONV)), _const_spec((1, D_CONV)),
        _const_spec((2, D_HGRN)), _const_spec((1, D_HGRN)),
    ]


def _prompt_mixer(x, mk, mv, vectors, weights):
    batch, seq, _ = x.shape
    n_mem = mk.shape[1]
    tl = PROMPT_TILE
    return pl.pallas_call(
        _prompt_mixer_kernel,
        grid=(batch, seq // tl),
        in_specs=[
            pl.BlockSpec((None, tl, D_MODEL), lambda b, t: (b, t, 0)),
            pl.BlockSpec((None, n_mem, D_MEM), lambda b, t: (b, 0, 0)),
            pl.BlockSpec((None, n_mem, D_MEM), lambda b, t: (b, 0, 0)),
        ] + _vector_specs() + _mixer_weight_specs(*weights),
        out_specs=[
            pl.BlockSpec((None, tl, D_MODEL), lambda b, t: (b, t, 0)),
            pl.BlockSpec((None, CONV_PREFIX, D_CONV), lambda b, t: (b, 0, 0)),
            pl.BlockSpec((None, HG_HEADS, HG_DK, HG_DV), lambda b, t: (b, 0, 0, 0)),
        ],
        out_shape=[
            jax.ShapeDtypeStruct((batch, seq, D_MODEL), F32),
            jax.ShapeDtypeStruct((batch, CONV_PREFIX, D_CONV), F32),
            jax.ShapeDtypeStruct((batch, HG_HEADS, HG_DK, HG_DV), F32),
        ],
        scratch_shapes=[
            pltpu.VMEM((SUBLANES, tl + CONV_ROWS, D_CONV), F32),
            pltpu.VMEM((HG_HEADS, HG_DV, HG_DK), F32),
            pltpu.VMEM((tl, D_CONV), F32),
            pltpu.VMEM((n_mem, D_MEM), BF16),
            pltpu.VMEM((n_mem, D_MEM), BF16),
        ],
        compiler_params=pltpu.CompilerParams(
            dimension_semantics=("parallel", "arbitrary"), vmem_limit_bytes=VMEM_LIMIT),
        name="prompt_mixer",
    )(x, mk, mv, *vectors, *weights)


def _sample_mixer_kernel(x_ref, sc_ref, sh_ref, ck_ref, cv_ref, pre_g_ref, post_g_ref, conv_w_ref,
                         conv_b_ref, ln_g_ref, ln_b_ref, lb_logits_ref, hg_g_ref, w_in_ref,
                         w_conv_out_ref, w_hg_out_ref, w_mem_out_ref, w_out_ref,
                         y_ref, new_conv_ref, new_hg_ref,
                         ext_ref, c_ref, o_ref, om_ref):
    n_seq, dec = x_ref.shape[0], x_ref.shape[1]
    rows = n_seq * dec
    x = x_ref[...].reshape(rows, D_MODEL)
    hb = _rms(x, pre_g_ref[...]).astype(BF16)

    u = _dot(hb, w_in_ref[:, COL_CA:COL_CB]) * _sigmoid(_dot(hb, w_in_ref[:, COL_CB:COL_HQ]))
    for g in range(n_seq):
        ext_ref[g, 0:CONV_PREFIX, :] = sc_ref[g]
        ext_ref[g, CONV_PREFIX:CONV_PREFIX + dec, :] = u[g * dec:(g + 1) * dec]
        new_conv_ref[g] = ext_ref[g, dec:dec + CONV_PREFIX, :]
        acc = jnp.zeros((dec, D_CONV), F32)
        for j in range(CONV_WIDTH):
            acc = acc + ext_ref[g, j:j + dec, :] * conv_w_ref[j:j + 1, :]
        c_ref[g * dec:(g + 1) * dec, :] = acc
    c = _conv_ln_silu(c_ref[...], conv_b_ref[...], ln_g_ref[...], ln_b_ref[...])
    p_conv = _dot(c.astype(BF16), w_conv_out_ref[...])

    lb = _lower_bound(lb_logits_ref[...])
    q_dec, k_inv, k_end, b_last = _hgrn_gates(
        _dot(hb, w_in_ref[:, COL_HQ:COL_HF]), _dot(hb, w_in_ref[:, COL_HF:COL_HI]), lb, dec)
    decay = jnp.exp(b_last)
    v = _dot(hb, w_in_ref[:, COL_HI:COL_HGATE])
    row = lax.broadcasted_iota(jnp.int32, (dec, dec), 0)
    col = lax.broadcasted_iota(jnp.int32, (dec, dec), 1)
    causal = col <= row
    mq = _dot(hb, w_in_ref[:, COL_MQ:COL_GATES]).astype(BF16)
    for g in range(n_seq):
        rs = slice(g * dec, (g + 1) * dec)
        for h in range(HG_HEADS):
            hs = slice(h * HG_DK, (h + 1) * HG_DK)
            qd = q_dec[rs, hs].astype(BF16)
            ki = k_inv[rs, hs].astype(BF16)
            ke = k_end[rs, hs].astype(BF16)
            vh = v[rs, hs].astype(BF16)
            s0 = sh_ref[g, h]
            scores = jnp.where(causal, _dot_nt(qd, ki), 0.0)
            o_ref[rs, hs] = _dot(scores.astype(BF16), vh) + _dot(qd, s0.astype(BF16))
            decay_col = jnp.broadcast_to(decay[g:g + 1, hs], (HG_DV, HG_DK)).T
            new_hg_ref[g, h] = decay_col * s0 + _dot_tn(ke, vh)
        for h in range(MEM_HEADS):
            hs = slice(h * MEM_HEAD_DIM, (h + 1) * MEM_HEAD_DIM)
            s = _dot_nt(mq[rs, hs], ck_ref[g, :, hs].astype(BF16)) * (MEM_HEAD_DIM ** -0.5)
            e = jnp.exp(s - jnp.max(s, axis=-1, keepdims=True))
            p = e / jnp.sum(e, axis=-1, keepdims=True)
            om_ref[rs, hs] = _dot(p.astype(BF16), cv_ref[g, :, hs].astype(BF16))

    o = _head_rms_gate(o_ref[...], _dot(hb, w_in_ref[:, COL_HGATE:COL_MQ]), hg_g_ref[...])
    p_hg = _dot(o.astype(BF16), w_hg_out_ref[...])
    p_mem = _dot(om_ref[...].astype(BF16), w_mem_out_ref[...])
    y = _merge_out(hb, x, p_conv, p_hg, p_mem, w_in_ref, w_out_ref, post_g_ref[...])
    y_ref[...] = y.reshape(n_seq, dec, D_MODEL)


def _sample_mixer(x, state_conv, state_hgrn, cache_k, cache_v, vectors, weights):
    n, dec, _ = x.shape
    n_mem = cache_k.shape[1]
    g = SAMPLE_GROUP
    return pl.pallas_call(
        _sample_mixer_kernel,
        grid=(n // g,),
        in_specs=[
            pl.BlockSpec((g, dec, D_MODEL), lambda i: (i, 0, 0)),
            pl.BlockSpec((g, CONV_PREFIX, D_CONV), lambda i: (i, 0, 0)),
            pl.BlockSpec((g, HG_HEADS, HG_DK, HG_DV), lambda i: (i, 0, 0, 0)),
            pl.BlockSpec((g, n_mem, D_MEM), lambda i: (i, 0, 0)),
            pl.BlockSpec((g, n_mem, D_MEM), lambda i: (i, 0, 0)),
        ] + _vector_specs() + _mixer_weight_specs(*weights),
        out_specs=[
            pl.BlockSpec((g, dec, D_MODEL), lambda i: (i, 0, 0)),
            pl.BlockSpec((g, CONV_PREFIX, D_CONV), lambda i: (i, 0, 0)),
            pl.BlockSpec((g, HG_HEADS, HG_DK, HG_DV), lambda i: (i, 0, 0, 0)),
        ],
        out_shape=[
            jax.ShapeDtypeStruct((n, dec, D_MODEL), F32),
            jax.ShapeDtypeStruct((n, CONV_PREFIX, D_CONV), F32),
            jax.ShapeDtypeStruct((n, HG_HEADS, HG_DK, HG_DV), F32),
        ],
        scratch_shapes=[
            pltpu.VMEM((g, CONV_PREFIX + dec + 2, D_CONV), F32),
            pltpu.VMEM((g * dec, D_CONV), F32),
            pltpu.VMEM((g * dec, D_HGRN), F32),
            pltpu.VMEM((g * dec, D_MEM), F32),
        ],
        compiler_params=pltpu.CompilerParams(
            dimension_semantics=("parallel",), vmem_limit_bytes=VMEM_LIMIT),
        name="sample_mixer",
    )(x, state_conv, state_hgrn, cache_k, cache_v, *vectors, *weights)


def _ffn_kernel(x_ref, pre_g_ref, post_g_ref, w_gate_ref, w_up_ref, w_down_ref, y_ref):
    x = x_ref[...]
    hb = _rms(x, pre_g_ref[...]).astype(BF16)
    f = _silu(_dot(hb, w_gate_ref[...])) * _dot(hb, w_up_ref[...])
    d = _dot(f.astype(BF16), w_down_ref[...])
    y_ref[...] = x + _rms(d, post_g_ref[...])


def _ffn(x2d, pre_g, post_g, w_gate, w_up, w_down):
    rows = x2d.shape[0]
    tile = min(FFN_TILE, rows)
    return pl.pallas_call(
        _ffn_kernel,
        grid=(rows // tile,),
        in_specs=[
            pl.BlockSpec((tile, D_MODEL), lambda i: (i, 0)),
            _const_spec((1, D_MODEL)), _const_spec((1, D_MODEL)),
            _const_spec(w_gate.shape), _const_spec(w_up.shape), _const_spec(w_down.shape),
        ],
        out_specs=pl.BlockSpec((tile, D_MODEL), lambda i: (i, 0)),
        out_shape=jax.ShapeDtypeStruct((rows, D_MODEL), F32),
        compiler_params=pltpu.CompilerParams(
            dimension_semantics=("parallel",), vmem_limit_bytes=VMEM_LIMIT),
        name="ffn",
    )(x2d, pre_g, post_g, w_gate, w_up, w_down)


def kernel(x_prompt, x_sample, mem_prompt, state_conv, state_hgrn, cache_mem_k, cache_mem_v, norm_pre_mix, norm_post_mix, norm_pre_ffn, norm_post_ffn, w_in, conv_w, conv_b, conv_ln_g, conv_ln_b, w_conv_out, hg_lb_logits, hg_norm_g, w_hg_out, mem_norm_g, w_mem_kv, w_mem_out, w_out, w_ffn_gate, w_ffn_up, w_ffn_down):
    depth = w_in.shape[0]
    assert depth == 1 and hg_lb_logits.shape[0] == 2, "single-layer step"
    batch, seq, _ = x_prompt.shape
    n_dec, dec, _ = x_sample.shape
    n_mem = mem_prompt.shape[1]
    assert seq % PROMPT_TILE == 0 and n_dec % SAMPLE_GROUP == 0 and dec == SUBLANES

    vectors = (norm_pre_mix, norm_post_mix, conv_w[0], conv_b, conv_ln_g, conv_ln_b,
               hg_lb_logits, jnp.tile(hg_norm_g, (1, HG_HEADS)))
    weights = tuple(w[0].astype(BF16) for w in (w_in, w_conv_out, w_hg_out, w_mem_out, w_out))
    ffn_weights = tuple(w[0].astype(BF16) for w in (w_ffn_gate, w_ffn_up, w_ffn_down))

    mk, mv = _memory_kv(mem_prompt.reshape(batch * n_mem, D_MODEL), mem_norm_g,
                        w_mem_kv[0].astype(BF16), n_mem)
    mk = mk.reshape(batch, n_mem, D_MEM)
    mv = mv.reshape(batch, n_mem, D_MEM)

    xp, conv_p, hg_p = _prompt_mixer(x_prompt, mk, mv, vectors, weights)
    xs, conv_s, hg_s = _sample_mixer(
        x_sample, state_conv[0], state_hgrn[0],
        cache_mem_k[0].reshape(n_dec, n_mem, D_MEM), cache_mem_v[0].reshape(n_dec, n_mem, D_MEM),
        vectors, weights)

    yp = _ffn(xp.reshape(batch * seq, D_MODEL), norm_pre_ffn, norm_post_ffn, *ffn_weights)
    ys = _ffn(xs.reshape(n_dec * dec, D_MODEL), norm_pre_ffn, norm_post_ffn, *ffn_weights)

    kv_shape = (1, batch, n_mem, MEM_HEADS, MEM_HEAD_DIM)
    return (yp.reshape(batch, seq, D_MODEL), ys.reshape(n_dec, dec, D_MODEL),
            conv_p[None], hg_p[None], mk.reshape(kv_shape), mv.reshape(kv_shape),
            conv_s[None], hg_s[None])
```

```python
from typing import NamedTuple

import jax
import jax.numpy as jnp
from jax import lax
from jax.experimental import pallas as pl
from jax.experimental.pallas import tpu as pltpu

D_MODEL = 1024
D_CONV = 512
CONV_WIDTH = 31
CONV_PREFIX = CONV_WIDTH - 1
HG_HEADS = 4
HG_DK = 128
HG_DV = 128
D_HGRN = HG_HEADS * HG_DK
MEM_HEADS = 4
MEM_HEAD_DIM = 128
D_MEM = MEM_HEADS * MEM_HEAD_DIM
CHUNK = 32
EPS = 1e-6

COL_CA = 0
COL_CB = COL_CA + D_CONV
COL_HQ = COL_CB + D_CONV
COL_HF = COL_HQ + D_HGRN
COL_HI = COL_HF + D_HGRN
COL_HGATE = COL_HI + D_HGRN
COL_MQ = COL_HGATE + D_HGRN
COL_GATES = COL_MQ + D_MEM

SUBLANES = 8
VMEM_LIMIT = 56 * 1024 * 1024

PROMPT_TILE = 256
SAMPLE_GROUP = 8
SAMPLE_TILE = 256
FFN_TILE = 512
CONV_ROWS = 32

BF16 = jnp.bfloat16
F32 = jnp.float32


def _dot(a, b):
    return jnp.dot(a, b, preferred_element_type=F32)


def _dot_nt(a, b):
    return lax.dot_general(a, b, (((1,), (1,)), ((), ())), preferred_element_type=F32)


def _dot_tn(a, b):
    return lax.dot_general(a, b, (((0,), (0,)), ((), ())), preferred_element_type=F32)


def _rms(x, gain):
    return x * lax.rsqrt(jnp.mean(x * x, axis=-1, keepdims=True) + EPS) * gain


def _sigmoid(x):
    return 1.0 / (1.0 + jnp.exp(-x))


def _silu(x):
    return x * _sigmoid(x)


def _lower_bound(lb_logits):
    m = jnp.max(lb_logits, axis=0, keepdims=True)
    e = jnp.exp(lb_logits - m)
    return e[0:1] / jnp.sum(e, axis=0, keepdims=True)


def _segment_cumsum(x, seg):
    pos = lax.broadcasted_iota(jnp.int32, x.shape, 0) & (seg - 1)
    s = 1
    while s < seg:
        x = x + jnp.where(pos >= s, pltpu.roll(x, s, axis=0), 0.0)
        s *= 2
    return x


def _conv_ln_silu(c, conv_b, ln_g, ln_b):
    c = c + conv_b
    mu = jnp.mean(c, axis=-1, keepdims=True)
    d = c - mu
    var = jnp.mean(d * d, axis=-1, keepdims=True)
    return _silu(d * lax.rsqrt(var + EPS) * ln_g + ln_b)


def _hgrn_gates(hq, hf, lb, seg):
    q = _silu(hq)
    logf = jnp.log(lb + (1.0 - lb) * _sigmoid(hf))
    k = (1.0 - lb) * _sigmoid(-hf)
    b = _segment_cumsum(logf, seg)
    rows = b.shape[0]
    b3 = b.reshape(rows // seg, seg, D_HGRN)
    b_last3 = b3[:, seg - 1:seg, :]
    rest = (b_last3 - b3).reshape(rows, D_HGRN)
    q_dec = q * jnp.exp(b)
    k_inv = k * jnp.exp(-b)
    k_end = k * jnp.exp(rest)
    return q_dec, k_inv, k_end, b_last3.reshape(rows // seg, D_HGRN)


class _TileLevels(NamedTuple):
    q_mid: jax.Array
    k_mid: jax.Array
    q_64: jax.Array
    k_64: jax.Array
    q_128: jax.Array
    k_128: jax.Array
    q_tile: jax.Array
    k_tile: jax.Array
    decay: jax.Array


def _hgrn_tile_levels(hq, hf, lb):
    tl = hq.shape[0]
    n_blocks = 4
    block = 2 * CHUNK
    assert tl == n_blocks * block
    q = _silu(hq)
    logf = jnp.log(lb + (1.0 - lb) * _sigmoid(hf))
    k = (1.0 - lb) * _sigmoid(-hf)
    b = _segment_cumsum(logf, CHUNK)
    chunks = [b[n * CHUNK:(n + 1) * CHUNK] for n in range(2 * n_blocks)]
    totals = [c[CHUNK - 1:CHUNK] for c in chunks]
    e = jnp.concatenate([c - totals[n] if n % 2 == 0 else c for n, c in enumerate(chunks)], axis=0)
    q_mid = q * jnp.exp(e)
    k_mid = k * jnp.exp(-e)
    first = [totals[2 * j] for j in range(n_blocks)]
    second = [totals[2 * j + 1] for j in range(n_blocks)]
    both = [first[j] + second[j] for j in range(n_blocks)]

    def scaled(x, blocks, log_scales):
        return jnp.concatenate(
            [x[j * block:(j + 1) * block] * jnp.exp(s) for j, s in zip(blocks, log_scales)], axis=0)

    every = range(n_blocks)
    return _TileLevels(
        q_mid=q_mid, k_mid=k_mid,
        q_64=scaled(q_mid, every, first),
        k_64=scaled(k_mid, every, second),
        q_128=scaled(q_mid, (2, 3), (first[2], first[3] + both[2])),
        k_128=scaled(k_mid, (0, 1), (second[0] + both[1], second[1])),
        q_tile=scaled(q_mid, every, [first[j] + sum(both[:j], 0.0) for j in every]),
        k_tile=scaled(k_mid, every, [second[j] + sum(both[j + 1:], 0.0) for j in every]),
        decay=jnp.exp(sum(both[1:], both[0])))


def _head_rms(o, gain4):
    parts = []
    for h in range(HG_HEADS):
        oh = o[:, h * HG_DV:(h + 1) * HG_DV]
        parts.append(oh * lax.rsqrt(jnp.mean(oh * oh, axis=-1, keepdims=True) + EPS))
    return jnp.concatenate(parts, axis=-1) * gain4


def _branch_gate(hb, w_in_ref, i):
    c0 = COL_GATES + i * D_MODEL
    return _sigmoid(_dot(hb, w_in_ref[:, c0:c0 + D_MODEL]))


def _merge_out(gate, x, branches, w_out_ref, post_g):
    acc = None
    for i, p in enumerate(branches):
        acc = gate(i) * p if acc is None else acc + gate(i) * p
    m = _dot(acc.astype(BF16), w_out_ref[...])
    return x + _rms(m, post_g)


def _ffn_block(x, pre_g, post_g, w_gate_ref, w_up_ref, w_down_ref):
    hb = _rms(x, pre_g).astype(BF16)
    f = _silu(_dot(hb, w_gate_ref[...])) * _dot(hb, w_up_ref[...])
    d = _dot(f.astype(BF16), w_down_ref[...])
    return x + _rms(d, post_g)


def _memory_kv_kernel(mem_ref, g_ref, w_ref, k_ref, v_ref):
    m = _rms(mem_ref[...], g_ref[...]).astype(BF16)
    k_ref[...] = _dot(m, w_ref[:, :D_MEM])
    v_ref[...] = _dot(m, w_ref[:, D_MEM:])


def _memory_kv(mem2d, gain, w_bf16, tile):
    rows = mem2d.shape[0]
    return pl.pallas_call(
        _memory_kv_kernel,
        grid=(rows // tile,),
        in_specs=[
            pl.BlockSpec((tile, D_MODEL), lambda i: (i, 0)),
            pl.BlockSpec((1, D_MODEL), lambda i: (0, 0)),
            pl.BlockSpec((D_MODEL, 2 * D_MEM), lambda i: (0, 0)),
        ],
        out_specs=[
            pl.BlockSpec((tile, D_MEM), lambda i: (i, 0)),
            pl.BlockSpec((tile, D_MEM), lambda i: (i, 0)),
        ],
        out_shape=[jax.ShapeDtypeStruct((rows, D_MEM), F32)] * 2,
        compiler_params=pltpu.CompilerParams(dimension_semantics=("parallel",)),
        name="memory_kv",
    )(mem2d, gain, w_bf16)


def _prompt_mixer_kernel(x_ref, mk_ref, mv_ref, pre_g_ref, post_g_ref, conv_w_ref, conv_b_ref,
                         ln_g_ref, ln_b_ref, lb_logits_ref, hg_g_ref, w_in_ref, w_conv_out_ref,
                         w_hg_out_ref, w_mem_out_ref, w_out_ref,
                         y_ref, new_conv_ref, new_hg_ref,
                         shift_ref, state_t_ref, c_ref, kb_ref, vb_ref):
    t = pl.program_id(1)
    last_t = pl.num_programs(1) - 1
    tl = x_ref.shape[0]

    @pl.when(t == 0)
    def _():
        shift_ref[0, 0:CONV_ROWS, :] = jnp.zeros((CONV_ROWS, D_CONV), F32)
        state_t_ref[...] = jnp.zeros_like(state_t_ref)
        kb_ref[...] = mk_ref[...].astype(BF16)
        vb_ref[...] = mv_ref[...].astype(BF16)

    x = x_ref[...]
    hb = _rms(x, pre_g_ref[...]).astype(BF16)

    u = _dot(hb, w_in_ref[:, COL_CA:COL_CB]) * _sigmoid(_dot(hb, w_in_ref[:, COL_CB:COL_HQ]))
    shift_ref[0, CONV_ROWS:CONV_ROWS + tl, :] = u
    for r in range(1, SUBLANES):
        shift_ref[r, 0:tl + CONV_ROWS - SUBLANES, :] = shift_ref[0, r:r + tl + CONV_ROWS - SUBLANES, :]

    @pl.when(t == last_t)
    def _():
        new_conv_ref[...] = shift_ref[0, tl + CONV_ROWS - CONV_PREFIX:tl + CONV_ROWS, :]

    def conv_block(i, carry):
        base = pl.multiple_of(i * CONV_ROWS, CONV_ROWS)
        acc = jnp.zeros((CONV_ROWS, D_CONV), F32)
        for j in range(CONV_WIDTH):
            s = j + (CONV_ROWS - CONV_PREFIX)
            r, a = s % SUBLANES, s // SUBLANES
            acc = acc + shift_ref[r, pl.ds(base + a * SUBLANES, CONV_ROWS), :] * conv_w_ref[j:j + 1, :]
        c_ref[pl.ds(base, CONV_ROWS), :] = acc
        return carry

    lax.fori_loop(0, tl // CONV_ROWS, conv_block, 0)
    shift_ref[0, 0:CONV_ROWS, :] = shift_ref[0, tl:tl + CONV_ROWS, :]
    c = _conv_ln_silu(c_ref[...], conv_b_ref[...], ln_g_ref[...], ln_b_ref[...])
    p_conv = _dot(c.astype(BF16), w_conv_out_ref[...])

    lb = _lower_bound(lb_logits_ref[...])
    lv = _hgrn_tile_levels(_dot(hb, w_in_ref[:, COL_HQ:COL_HF]),
                           _dot(hb, w_in_ref[:, COL_HF:COL_HI]), lb)
    v = _dot(hb, w_in_ref[:, COL_HI:COL_HGATE])
    half = tl // 2
    row = lax.broadcasted_iota(jnp.int32, (tl, tl), 0)
    col = lax.broadcasted_iota(jnp.int32, (tl, tl), 1)
    span = row ^ col
    near = (span < 2 * CHUNK) & (col <= row)
    mid = (span < 4 * CHUNK) & (col < row)
    o_heads = []
    for h in range(HG_HEADS):
        hs = slice(h * HG_DK, (h + 1) * HG_DK)
        vh = v[:, hs].astype(BF16)
        s_near = _dot_nt(lv.q_mid[:, hs].astype(BF16), lv.k_mid[:, hs].astype(BF16))
        s_mid = _dot_nt(lv.q_64[:, hs].astype(BF16), lv.k_64[:, hs].astype(BF16))
        s_far = _dot_nt(lv.q_128[:, hs].astype(BF16), lv.k_128[:, hs].astype(BF16))
        base = jnp.where(near, s_near, jnp.where(mid, s_mid, 0.0))
        scores = jnp.concatenate(
            [base[:half], jnp.concatenate([s_far, base[half:, half:]], axis=1)], axis=0)
        st = state_t_ref[h]
        o_heads.append(_dot(scores.astype(BF16), vh)
                       + _dot_nt(lv.q_tile[:, hs].astype(BF16), st.astype(BF16)))
        state_t_ref[h] = st * lv.decay[:, hs] + _dot_tn(vh, lv.k_tile[:, hs].astype(BF16))

    @pl.when(t == last_t)
    def _():
        for h in range(HG_HEADS):
            new_hg_ref[h] = state_t_ref[h].T

    o = (_head_rms(jnp.concatenate(o_heads, axis=-1), hg_g_ref[...])
         * _silu(_dot(hb, w_in_ref[:, COL_HGATE:COL_MQ])))
    p_hg = _dot(o.astype(BF16), w_hg_out_ref[...])

    mq = _dot(hb, w_in_ref[:, COL_MQ:COL_GATES]).astype(BF16)
    om = []
    for h in range(MEM_HEADS):
        hs = slice(h * MEM_HEAD_DIM, (h + 1) * MEM_HEAD_DIM)
        s = _dot_nt(mq[:, hs], kb_ref[:, hs]) * (MEM_HEAD_DIM ** -0.5)
        e = jnp.exp(s - jnp.max(s, axis=-1, keepdims=True))
        p = e / jnp.sum(e, axis=-1, keepdims=True)
        om.append(_dot(p.astype(BF16), vb_ref[:, hs]))
    p_mem = _dot(jnp.concatenate(om, axis=-1).astype(BF16), w_mem_out_ref[...])

    y_ref[...] = _merge_out(lambda i: _branch_gate(hb, w_in_ref, i), x, (p_conv, p_hg, p_mem),
                            w_out_ref, post_g_ref[...])


def _const_spec(shape):
    zeros = (0,) * len(shape)
    return pl.BlockSpec(shape, lambda *_: zeros, pipeline_mode=pl.Buffered(1))


def _mixer_weight_specs(w_in, w_conv_out, w_hg_out, w_mem_out, w_out):
    return [_const_spec(w.shape) for w in (w_in, w_conv_out, w_hg_out, w_mem_out, w_out)]


def _vector_specs():
    return [
        _const_spec((1, D_MODEL)), _const_spec((1, D_MODEL)),
        _const_spec((CONV_WIDTH, D_CONV)), _const_spec((1, D_CONV)),
        _const_spec((1, D_CONV)), _const_spec((1, D_CONV)),
        _const_spec((2, D_HGRN)), _const_spec((1, D_HGRN)),
    ]


def _prompt_mixer(x, mk, mv, vectors, weights):
    batch, seq, _ = x.shape
    n_mem = mk.shape[1]
    tl = PROMPT_TILE
    return pl.pallas_call(
        _prompt_mixer_kernel,
        grid=(batch, seq // tl),
        in_specs=[
            pl.BlockSpec((None, tl, D_MODEL), lambda b, t: (b, t, 0)),
            pl.BlockSpec((None, n_mem, D_MEM), lambda b, t: (b, 0, 0)),
            pl.BlockSpec((None, n_mem, D_MEM), lambda b, t: (b, 0, 0)),
        ] + _vector_specs() + _mixer_weight_specs(*weights),
        out_specs=[
            pl.BlockSpec((None, tl, D_MODEL), lambda b, t: (b, t, 0)),
            pl.BlockSpec((None, CONV_PREFIX, D_CONV), lambda b, t: (b, 0, 0)),
            pl.BlockSpec((None, HG_HEADS, HG_DK, HG_DV), lambda b, t: (b, 0, 0, 0)),
        ],
        out_shape=[
            jax.ShapeDtypeStruct((batch, seq, D_MODEL), F32),
            jax.ShapeDtypeStruct((batch, CONV_PREFIX, D_CONV), F32),
            jax.ShapeDtypeStruct((batch, HG_HEADS, HG_DK, HG_DV), F32),
        ],
        scratch_shapes=[
            pltpu.VMEM((SUBLANES, tl + CONV_ROWS, D_CONV), F32),
            pltpu.VMEM((HG_HEADS, HG_DV, HG_DK), F32),
            pltpu.VMEM((tl, D_CONV), F32),
            pltpu.VMEM((n_mem, D_MEM), BF16),
            pltpu.VMEM((n_mem, D_MEM), BF16),
        ],
        compiler_params=pltpu.CompilerParams(
            dimension_semantics=("parallel", "arbitrary"), vmem_limit_bytes=VMEM_LIMIT),
        name="prompt_mixer",
    )(x, mk, mv, *vectors, *weights)


def _sample_in_kernel(x_ref, pre_g_ref, lb_logits_ref, w_in_ref,
                      u_ref, qd_ref, ki_ref, ke_ref, v_ref, blast_ref, q4_ref, hgate_ref, gates_ref):
    n_seq = x_ref.shape[0] // SUBLANES
    hb = _rms(x_ref[...], pre_g_ref[...]).astype(BF16)
    u_ref[...] = _dot(hb, w_in_ref[:, COL_CA:COL_CB]) * _sigmoid(_dot(hb, w_in_ref[:, COL_CB:COL_HQ]))
    lb = _lower_bound(lb_logits_ref[...])
    q_dec, k_inv, k_end, b_last = _hgrn_gates(
        _dot(hb, w_in_ref[:, COL_HQ:COL_HF]), _dot(hb, w_in_ref[:, COL_HF:COL_HI]), lb, SUBLANES)
    qd_ref[...] = q_dec
    ki_ref[...] = k_inv
    ke_ref[...] = k_end
    blast_ref[...] = b_last
    v_ref[...] = _dot(hb, w_in_ref[:, COL_HI:COL_HGATE])
    hgate_ref[...] = _silu(_dot(hb, w_in_ref[:, COL_HGATE:COL_MQ]))
    mq = _dot(hb, w_in_ref[:, COL_MQ:COL_GATES])
    for h in range(MEM_HEADS):
        q4_ref[:, h * SUBLANES:(h + 1) * SUBLANES, :] = (
            mq[:, h * MEM_HEAD_DIM:(h + 1) * MEM_HEAD_DIM].reshape(n_seq, SUBLANES, MEM_HEAD_DIM))
    for i in range(3):
        gates_ref[:, i * D_MODEL:(i + 1) * D_MODEL] = _branch_gate(hb, w_in_ref, i)


def _sample_seq_kernel(u_ref, qd_ref, ki_ref, ke_ref, v_ref, blast_ref, q4_ref, sc_ref, sh_ref,
                       k2_ref, v2_ref, conv_w_ref,
                       c_ref, o_ref, om_ref, new_conv_ref, new_hg_ref, ext_ref):
    n_seq, dec = sc_ref.shape[0], SUBLANES
    rows = n_seq * dec

    for g in range(n_seq):
        rs = slice(g * dec, (g + 1) * dec)
        ext_ref[g, 0:CONV_PREFIX, :] = sc_ref[g]
        ext_ref[g, CONV_PREFIX:CONV_PREFIX + dec, :] = u_ref[rs, :]
        new_conv_ref[g] = ext_ref[g, dec:dec + CONV_PREFIX, :]
        acc = jnp.zeros((dec, D_CONV), F32)
        for j in range(CONV_WIDTH):
            acc = acc + ext_ref[g, j:j + dec, :] * conv_w_ref[j:j + 1, :]
        c_ref[rs, :] = acc

    row = lax.broadcasted_iota(jnp.int32, (rows, rows), 0)
    col = lax.broadcasted_iota(jnp.int32, (rows, rows), 1)
    causal = ((row ^ col) < dec) & (col <= row)
    decay = jnp.exp(blast_ref[...])
    for h in range(HG_HEADS):
        hs = slice(h * HG_DK, (h + 1) * HG_DK)
        scores = jnp.where(causal, _dot_nt(qd_ref[:, hs].astype(BF16), ki_ref[:, hs].astype(BF16)), 0.0)
        o_intra = _dot(scores.astype(BF16), v_ref[:, hs].astype(BF16))
        for g in range(n_seq):
            rs = slice(g * dec, (g + 1) * dec)
            s0 = sh_ref[g, h]
            o_ref[rs, hs] = o_intra[rs] + _dot(qd_ref[rs, hs].astype(BF16), s0.astype(BF16))
            decay_col = jnp.broadcast_to(decay[g:g + 1, hs], (HG_DV, HG_DK)).T
            new_hg_ref[g, h] = decay_col * s0 + _dot_tn(ke_ref[rs, hs].astype(BF16),
                                                        v_ref[rs, hs].astype(BF16))

    assert dec == 8 and MEM_HEADS == 4
    q_head = lax.broadcasted_iota(jnp.int32, (MEM_HEADS * dec, k2_ref.shape[1]), 0) >> 3
    k_head = lax.broadcasted_iota(jnp.int32, (MEM_HEADS * dec, k2_ref.shape[1]), 1) & 3
    own_head = q_head == k_head
    for g in range(n_seq):
        rs = slice(g * dec, (g + 1) * dec)
        s = _dot_nt(q4_ref[g].astype(BF16), k2_ref[g].astype(BF16)) * (MEM_HEAD_DIM ** -0.5)
        s = jnp.where(own_head, s, -jnp.inf)
        e = jnp.exp(s - jnp.max(s, axis=-1, keepdims=True))
        p = e / jnp.sum(e, axis=-1, keepdims=True)
        om = _dot(p.astype(BF16), v2_ref[g].astype(BF16))
        for h in range(MEM_HEADS):
            om_ref[rs, h * MEM_HEAD_DIM:(h + 1) * MEM_HEAD_DIM] = om[h * dec:(h + 1) * dec]


def _sample_out_kernel(x_ref, c_ref, o_ref, om_ref, hgate_ref, gates_ref, post_g_ref, conv_b_ref,
                       ln_g_ref, ln_b_ref, hg_g_ref, w_conv_out_ref, w_hg_out_ref, w_mem_out_ref,
                       w_out_ref, ffn_pre_g_ref, ffn_post_g_ref, w_gate_ref, w_up_ref, w_down_ref, y_ref):
    c = _conv_ln_silu(c_ref[...], conv_b_ref[...], ln_g_ref[...], ln_b_ref[...])
    p_conv = _dot(c.astype(BF16), w_conv_out_ref[...])
    o = _head_rms(o_ref[...], hg_g_ref[...]) * hgate_ref[...]
    p_hg = _dot(o.astype(BF16), w_hg_out_ref[...])
    p_mem = _dot(om_ref[...].astype(BF16), w_mem_out_ref[...])
    x1 = _merge_out(lambda i: gates_ref[:, i * D_MODEL:(i + 1) * D_MODEL], x_ref[...],
                    (p_conv, p_hg, p_mem), w_out_ref, post_g_ref[...])
    y_ref[...] = _ffn_block(x1, ffn_pre_g_ref[...], ffn_post_g_ref[...], w_gate_ref, w_up_ref, w_down_ref)


def _row_spec(tile, width):
    return pl.BlockSpec((tile, width), lambda i: (i, 0))


def _sample_in(x2d, pre_g, lb_logits, w_in):
    rows = x2d.shape[0]
    tile = SAMPLE_TILE
    seqs = tile // SUBLANES
    wide = [D_CONV, D_HGRN, D_HGRN, D_HGRN, D_HGRN]
    return pl.pallas_call(
        _sample_in_kernel,
        grid=(rows // tile,),
        in_specs=[_row_spec(tile, D_MODEL), _const_spec((1, D_MODEL)), _const_spec((2, D_HGRN)),
                  _const_spec(w_in.shape)],
        out_specs=[_row_spec(tile, w) for w in wide] + [
            _row_spec(seqs, D_HGRN),
            pl.BlockSpec((seqs, MEM_HEADS * SUBLANES, MEM_HEAD_DIM), lambda i: (i, 0, 0)),
            _row_spec(tile, D_HGRN),
            _row_spec(tile, 3 * D_MODEL),
        ],
        out_shape=[jax.ShapeDtypeStruct((rows, w), F32) for w in wide] + [
            jax.ShapeDtypeStruct((rows // SUBLANES, D_HGRN), F32),
            jax.ShapeDtypeStruct((rows // SUBLANES, MEM_HEADS * SUBLANES, MEM_HEAD_DIM), F32),
            jax.ShapeDtypeStruct((rows, D_HGRN), F32),
            jax.ShapeDtypeStruct((rows, 3 * D_MODEL), F32),
        ],
        compiler_params=pltpu.CompilerParams(
            dimension_semantics=("parallel",), vmem_limit_bytes=VMEM_LIMIT),
        name="sample_in",
    )(x2d, pre_g, lb_logits, w_in)


def _sample_seq(u, qd, ki, ke, v, blast, q4, state_conv, state_hgrn, k2, v2, conv_w):
    n = state_conv.shape[0]
    g = SAMPLE_GROUP
    rows = g * SUBLANES
    kv_rows = k2.shape[1]
    seq_block = lambda *tail: pl.BlockSpec((g,) + tail, lambda i: (i,) + (0,) * len(tail))
    return pl.pallas_call(
        _sample_seq_kernel,
        grid=(n // g,),
        in_specs=[_row_spec(rows, D_CONV)] + [_row_spec(rows, D_HGRN)] * 4 + [
            _row_spec(g, D_HGRN),
            seq_block(MEM_HEADS * SUBLANES, MEM_HEAD_DIM),
            seq_block(CONV_PREFIX, D_CONV),
            seq_block(HG_HEADS, HG_DK, HG_DV),
            seq_block(kv_rows, MEM_HEAD_DIM),
            seq_block(kv_rows, MEM_HEAD_DIM),
            _const_spec((CONV_WIDTH, D_CONV)),
        ],
        out_specs=[_row_spec(rows, D_CONV), _row_spec(rows, D_HGRN), _row_spec(rows, D_MEM),
                   seq_block(CONV_PREFIX, D_CONV), seq_block(HG_HEADS, HG_DK, HG_DV)],
        out_shape=[
            jax.ShapeDtypeStruct((n * SUBLANES, D_CONV), F32),
            jax.ShapeDtypeStruct((n * SUBLANES, D_HGRN), F32),
            jax.ShapeDtypeStruct((n * SUBLANES, D_MEM), F32),
            jax.ShapeDtypeStruct((n, CONV_PREFIX, D_CONV), F32),
            jax.ShapeDtypeStruct((n, HG_HEADS, HG_DK, HG_DV), F32),
        ],
        scratch_shapes=[pltpu.VMEM((g, CONV_PREFIX + SUBLANES + 2, D_CONV), F32)],
        compiler_params=pltpu.CompilerParams(
            dimension_semantics=("parallel",), vmem_limit_bytes=VMEM_LIMIT),
        name="sample_seq",
    )(u, qd, ki, ke, v, blast, q4, state_conv, state_hgrn, k2, v2, conv_w)


def _sample_out(x2d, c, o, om, hgate, gates, vectors, weights, ffn_vectors, ffn_weights):
    rows = x2d.shape[0]
    tile = SAMPLE_TILE
    acts = (x2d, c, o, om, hgate, gates)
    consts = tuple(vectors) + tuple(weights) + tuple(ffn_vectors) + tuple(ffn_weights)
    return pl.pallas_call(
        _sample_out_kernel,
        grid=(rows // tile,),
        in_specs=[_row_spec(tile, a.shape[1]) for a in acts] + [_const_spec(a.shape) for a in consts],
        out_specs=_row_spec(tile, D_MODEL),
        out_shape=jax.ShapeDtypeStruct((rows, D_MODEL), F32),
        compiler_params=pltpu.CompilerParams(
            dimension_semantics=("parallel",), vmem_limit_bytes=VMEM_LIMIT),
        name="sample_out",
    )(*acts, *consts)


def _ffn_kernel(x_ref, pre_g_ref, post_g_ref, w_gate_ref, w_up_ref, w_down_ref, y_ref):
    y_ref[...] = _ffn_block(x_ref[...], pre_g_ref[...], post_g_ref[...], w_gate_ref, w_up_ref, w_down_ref)


def _ffn(x2d, pre_g, post_g, w_gate, w_up, w_down):
    rows = x2d.shape[0]
    tile = min(FFN_TILE, rows)
    return pl.pallas_call(
        _ffn_kernel,
        grid=(rows // tile,),
        in_specs=[
            pl.BlockSpec((tile, D_MODEL), lambda i: (i, 0)),
            _const_spec((1, D_MODEL)), _const_spec((1, D_MODEL)),
            _const_spec(w_gate.shape), _const_spec(w_up.shape), _const_spec(w_down.shape),
        ],
        out_specs=pl.BlockSpec((tile, D_MODEL), lambda i: (i, 0)),
        out_shape=jax.ShapeDtypeStruct((rows, D_MODEL), F32),
        compiler_params=pltpu.CompilerParams(
            dimension_semantics=("parallel",), vmem_limit_bytes=VMEM_LIMIT),
        name="ffn",
    )(x2d, pre_g, post_g, w_gate, w_up, w_down)


def kernel(x_prompt, x_sample, mem_prompt, state_conv, state_hgrn, cache_mem_k, cache_mem_v, norm_pre_mix, norm_post_mix, norm_pre_ffn, norm_post_ffn, w_in, conv_w, conv_b, conv_ln_g, conv_ln_b, w_conv_out, hg_lb_logits, hg_norm_g, w_hg_out, mem_norm_g, w_mem_kv, w_mem_out, w_out, w_ffn_gate, w_ffn_up, w_ffn_down):
    depth = w_in.shape[0]
    assert depth == 1 and hg_lb_logits.shape[0] == 2, "single-layer step"
    batch, seq, _ = x_prompt.shape
    n_dec, dec, _ = x_sample.shape
    n_mem = mem_prompt.shape[1]
    assert seq % PROMPT_TILE == 0 and n_dec % SAMPLE_GROUP == 0 and dec == SUBLANES
    assert (n_dec * dec) % SAMPLE_TILE == 0

    vectors = (norm_pre_mix, norm_post_mix, conv_w[0], conv_b, conv_ln_g, conv_ln_b,
               hg_lb_logits, jnp.tile(hg_norm_g, (1, HG_HEADS)))
    weights = tuple(w[0].astype(BF16) for w in (w_in, w_conv_out, w_hg_out, w_mem_out, w_out))
    ffn_weights = tuple(w[0].astype(BF16) for w in (w_ffn_gate, w_ffn_up, w_ffn_down))

    mk, mv = _memory_kv(mem_prompt.reshape(batch * n_mem, D_MODEL), mem_norm_g,
                        w_mem_kv[0].astype(BF16), n_mem)
    mk = mk.reshape(batch, n_mem, D_MEM)
    mv = mv.reshape(batch, n_mem, D_MEM)

    xp, conv_p, hg_p = _prompt_mixer(x_prompt, mk, mv, vectors, weights)
    yp = _ffn(xp.reshape(batch * seq, D_MODEL), norm_pre_ffn, norm_post_ffn, *ffn_weights)

    xs2d = x_sample.reshape(n_dec * dec, D_MODEL)
    u, qd, ki, ke, v, blast, q4, hgate, gates = _sample_in(xs2d, norm_pre_mix, hg_lb_logits, weights[0])
    k2 = cache_mem_k[0].reshape(n_dec, n_mem * MEM_HEADS, MEM_HEAD_DIM)
    v2 = cache_mem_v[0].reshape(n_dec, n_mem * MEM_HEADS, MEM_HEAD_DIM)
    c, o, om, conv_s, hg_s = _sample_seq(u, qd, ki, ke, v, blast, q4, state_conv[0], state_hgrn[0],
                                         k2, v2, conv_w[0])
    ys = _sample_out(xs2d, c, o, om, hgate, gates,
                     (norm_post_mix, conv_b, conv_ln_g, conv_ln_b, vectors[-1]), weights[1:],
                     (norm_pre_ffn, norm_post_ffn), ffn_weights)

    kv_shape = (1, batch, n_mem, MEM_HEADS, MEM_HEAD_DIM)
    return (yp.reshape(batch, seq, D_MODEL), ys.reshape(n_dec, dec, D_MODEL),
            conv_p[None], hg_p[None], mk.reshape(kv_shape), mv.reshape(kv_shape),
            conv_s[None], hg_s[None])
```

```python
from typing import NamedTuple

import jax
import jax.numpy as jnp
from jax import lax
from jax.experimental import pallas as pl
from jax.experimental.pallas import tpu as pltpu

D_MODEL = 1024
D_CONV = 512
CONV_WIDTH = 31
CONV_PREFIX = CONV_WIDTH - 1
HG_HEADS = 4
HG_DK = 128
HG_DV = 128
D_HGRN = HG_HEADS * HG_DK
MEM_HEADS = 4
MEM_HEAD_DIM = 128
D_MEM = MEM_HEADS * MEM_HEAD_DIM
CHUNK = 32
EPS = 1e-6

COL_CA = 0
COL_CB = COL_CA + D_CONV
COL_HQ = COL_CB + D_CONV
COL_HF = COL_HQ + D_HGRN
COL_HI = COL_HF + D_HGRN
COL_HGATE = COL_HI + D_HGRN
COL_MQ = COL_HGATE + D_HGRN
COL_GATES = COL_MQ + D_MEM

SUBLANES = 8
VMEM_LIMIT = 56 * 1024 * 1024

PROMPT_TILE = 256
SAMPLE_GROUP = 8
SAMPLE_TILE = 256
FFN_TILE = 512
CONV_ROWS = 32

BF16 = jnp.bfloat16
F32 = jnp.float32


def _dot(a, b):
    return jnp.dot(a, b, preferred_element_type=F32)


def _dot_nt(a, b):
    return lax.dot_general(a, b, (((1,), (1,)), ((), ())), preferred_element_type=F32)


def _dot_tn(a, b):
    return lax.dot_general(a, b, (((0,), (0,)), ((), ())), preferred_element_type=F32)


def _rms(x, gain):
    return x * lax.rsqrt(jnp.mean(x * x, axis=-1, keepdims=True) + EPS) * gain


def _sigmoid(x):
    return 1.0 / (1.0 + jnp.exp(-x))


def _silu(x):
    return x * _sigmoid(x)


def _lower_bound(lb_logits):
    m = jnp.max(lb_logits, axis=0, keepdims=True)
    e = jnp.exp(lb_logits - m)
    return e[0:1] / jnp.sum(e, axis=0, keepdims=True)


def _segment_cumsum(x, seg):
    pos = lax.broadcasted_iota(jnp.int32, x.shape, 0) & (seg - 1)
    s = 1
    while s < seg:
        x = x + jnp.where(pos >= s, pltpu.roll(x, s, axis=0), 0.0)
        s *= 2
    return x


def _conv_ln_silu(c, conv_b, ln_g, ln_b):
    c = c + conv_b
    mu = jnp.mean(c, axis=-1, keepdims=True)
    d = c - mu
    var = jnp.mean(d * d, axis=-1, keepdims=True)
    return _silu(d * lax.rsqrt(var + EPS) * ln_g + ln_b)


def _hgrn_gates(hq, hf, lb, seg):
    q = _silu(hq)
    logf = jnp.log(lb + (1.0 - lb) * _sigmoid(hf))
    k = (1.0 - lb) * _sigmoid(-hf)
    b = _segment_cumsum(logf, seg)
    rows = b.shape[0]
    b3 = b.reshape(rows // seg, seg, D_HGRN)
    b_last3 = b3[:, seg - 1:seg, :]
    rest = (b_last3 - b3).reshape(rows, D_HGRN)
    q_dec = q * jnp.exp(b)
    k_inv = k * jnp.exp(-b)
    k_end = k * jnp.exp(rest)
    return q_dec, k_inv, k_end, b_last3.reshape(rows // seg, D_HGRN)


class _TileLevels(NamedTuple):
    q_mid: jax.Array
    k_mid: jax.Array
    q_64: jax.Array
    k_64: jax.Array
    q_128: jax.Array
    k_128: jax.Array
    q_tile: jax.Array
    k_tile: jax.Array
    decay: jax.Array


def _hgrn_tile_levels(hq, hf, lb):
    tl = hq.shape[0]
    n_blocks = 4
    block = 2 * CHUNK
    assert tl == n_blocks * block
    q = _silu(hq)
    logf = jnp.log(lb + (1.0 - lb) * _sigmoid(hf))
    k = (1.0 - lb) * _sigmoid(-hf)
    b = _segment_cumsum(logf, CHUNK)
    chunks = [b[n * CHUNK:(n + 1) * CHUNK] for n in range(2 * n_blocks)]
    totals = [c[CHUNK - 1:CHUNK] for c in chunks]
    e = jnp.concatenate([c - totals[n] if n % 2 == 0 else c for n, c in enumerate(chunks)], axis=0)
    q_mid = q * jnp.exp(e)
    k_mid = k * jnp.exp(-e)
    first = [totals[2 * j] for j in range(n_blocks)]
    second = [totals[2 * j + 1] for j in range(n_blocks)]
    both = [first[j] + second[j] for j in range(n_blocks)]

    def scaled(x, blocks, log_scales):
        return jnp.concatenate(
            [x[j * block:(j + 1) * block] * jnp.exp(s) for j, s in zip(blocks, log_scales)], axis=0)

    every = range(n_blocks)
    return _TileLevels(
        q_mid=q_mid, k_mid=k_mid,
        q_64=scaled(q_mid, every, first),
        k_64=scaled(k_mid, every, second),
        q_128=scaled(q_mid, (2, 3), (first[2], first[3] + both[2])),
        k_128=scaled(k_mid, (0, 1), (second[0] + both[1], second[1])),
        q_tile=scaled(q_mid, every, [first[j] + sum(both[:j], 0.0) for j in every]),
        k_tile=scaled(k_mid, every, [second[j] + sum(both[j + 1:], 0.0) for j in every]),
        decay=jnp.exp(sum(both[1:], both[0])))


def _head_rms(o, gain4):
    parts = []
    for h in range(HG_HEADS):
        oh = o[:, h * HG_DV:(h + 1) * HG_DV]
        parts.append(oh * lax.rsqrt(jnp.mean(oh * oh, axis=-1, keepdims=True) + EPS))
    return jnp.concatenate(parts, axis=-1) * gain4


def _branch_gate(hb, w_in_ref, i):
    c0 = COL_GATES + i * D_MODEL
    return _sigmoid(_dot(hb, w_in_ref[:, c0:c0 + D_MODEL]))


def _merge_out(gate, x, branches, w_out_ref, post_g):
    acc = None
    for i, p in enumerate(branches):
        acc = gate(i) * p if acc is None else acc + gate(i) * p
    m = _dot(acc.astype(BF16), w_out_ref[...])
    return x + _rms(m, post_g)


def _ffn_block(x, pre_g, post_g, w_gate_ref, w_up_ref, w_down_ref):
    hb = _rms(x, pre_g).astype(BF16)
    f = _silu(_dot(hb, w_gate_ref[...])) * _dot(hb, w_up_ref[...])
    d = _dot(f.astype(BF16), w_down_ref[...])
    return x + _rms(d, post_g)


def _memory_kv_kernel(mem_ref, g_ref, w_ref, k_ref, v_ref):
    m = _rms(mem_ref[...], g_ref[...]).astype(BF16)
    k_ref[...] = _dot(m, w_ref[:, :D_MEM])
    v_ref[...] = _dot(m, w_ref[:, D_MEM:])


def _memory_kv(mem2d, gain, w_bf16, tile):
    rows = mem2d.shape[0]
    return pl.pallas_call(
        _memory_kv_kernel,
        grid=(rows // tile,),
        in_specs=[
            pl.BlockSpec((tile, D_MODEL), lambda i: (i, 0)),
            pl.BlockSpec((1, D_MODEL), lambda i: (0, 0)),
            pl.BlockSpec((D_MODEL, 2 * D_MEM), lambda i: (0, 0)),
        ],
        out_specs=[
            pl.BlockSpec((tile, D_MEM), lambda i: (i, 0)),
            pl.BlockSpec((tile, D_MEM), lambda i: (i, 0)),
        ],
        out_shape=[jax.ShapeDtypeStruct((rows, D_MEM), F32)] * 2,
        compiler_params=pltpu.CompilerParams(dimension_semantics=("parallel",)),
        name="memory_kv",
    )(mem2d, gain, w_bf16)


def _prompt_mixer_kernel(x_ref, mk_ref, mv_ref, pre_g_ref, post_g_ref, conv_w_ref, conv_b_ref,
                         ln_g_ref, ln_b_ref, lb_logits_ref, hg_g_ref, w_in_ref, w_conv_out_ref,
                         w_hg_out_ref, w_mem_out_ref, w_out_ref,
                         y_ref, new_conv_ref, new_hg_ref,
                         shift_ref, state_t_ref, kb_ref, vb_ref):
    t = pl.program_id(1)
    last_t = pl.num_programs(1) - 1
    tl = x_ref.shape[0]

    @pl.when(t == 0)
    def _():
        shift_ref[0, 0:CONV_ROWS, :] = jnp.zeros((CONV_ROWS, D_CONV), F32)
        state_t_ref[...] = jnp.zeros_like(state_t_ref)
        kb_ref[...] = mk_ref[...].astype(BF16)
        vb_ref[...] = mv_ref[...].astype(BF16)

    x = x_ref[...]
    hb = _rms(x, pre_g_ref[...]).astype(BF16)

    def proj(c0, c1):
        return _dot(hb, w_in_ref[:, c0:c1])

    u = proj(COL_CA, COL_CB) * _sigmoid(proj(COL_CB, COL_HQ))
    shift_ref[0, CONV_ROWS:CONV_ROWS + tl, :] = u
    for r in range(1, SUBLANES):
        shift_ref[r, 0:tl + CONV_ROWS - SUBLANES, :] = shift_ref[0, r:r + tl + CONV_ROWS - SUBLANES, :]
    hq = proj(COL_HQ, COL_HF)
    hf = proj(COL_HF, COL_HI)
    v = proj(COL_HI, COL_HGATE)
    mq = proj(COL_MQ, COL_GATES).astype(BF16)
    hgate = _silu(proj(COL_HGATE, COL_MQ))
    gates = [_branch_gate(hb, w_in_ref, i) for i in range(3)]

    conv_blocks = []
    for base in range(0, tl, CONV_ROWS):
        acc = None
        for j in range(CONV_WIDTH):
            s = j + (CONV_ROWS - CONV_PREFIX)
            r, a = s % SUBLANES, s // SUBLANES
            lo = base + a * SUBLANES
            term = shift_ref[r, lo:lo + CONV_ROWS, :] * conv_w_ref[j:j + 1, :]
            acc = term if acc is None else acc + term
        conv_blocks.append(acc)
    c = _conv_ln_silu(jnp.concatenate(conv_blocks, axis=0),
                      conv_b_ref[...], ln_g_ref[...], ln_b_ref[...])

    om = []
    for h in range(MEM_HEADS):
        hs = slice(h * MEM_HEAD_DIM, (h + 1) * MEM_HEAD_DIM)
        s = _dot_nt(mq[:, hs], kb_ref[:, hs]) * (MEM_HEAD_DIM ** -0.5)
        e = jnp.exp(s - jnp.max(s, axis=-1, keepdims=True))
        p = e / jnp.sum(e, axis=-1, keepdims=True)
        om.append(_dot(p.astype(BF16), vb_ref[:, hs]))

    lv = _hgrn_tile_levels(hq, hf, _lower_bound(lb_logits_ref[...]))
    half = tl // 2
    row = lax.broadcasted_iota(jnp.int32, (tl, tl), 0)
    col = lax.broadcasted_iota(jnp.int32, (tl, tl), 1)
    span = row ^ col
    near = (span < 2 * CHUNK) & (col <= row)
    mid = (span < 4 * CHUNK) & (col < row)
    o_heads = []
    for h in range(HG_HEADS):
        hs = slice(h * HG_DK, (h + 1) * HG_DK)
        vh = v[:, hs].astype(BF16)
        s_near = _dot_nt(lv.q_mid[:, hs].astype(BF16), lv.k_mid[:, hs].astype(BF16))
        s_mid = _dot_nt(lv.q_64[:, hs].astype(BF16), lv.k_64[:, hs].astype(BF16))
        s_far = _dot_nt(lv.q_128[:, hs].astype(BF16), lv.k_128[:, hs].astype(BF16))
        base = jnp.where(near, s_near, jnp.where(mid, s_mid, 0.0))
        scores = jnp.concatenate(
            [base[:half], jnp.concatenate([s_far, base[half:, half:]], axis=1)], axis=0)
        st = state_t_ref[h]
        o_heads.append(_dot(scores.astype(BF16), vh)
                       + _dot_nt(lv.q_tile[:, hs].astype(BF16), st.astype(BF16)))
        state_t_ref[h] = st * lv.decay[:, hs] + _dot_tn(vh, lv.k_tile[:, hs].astype(BF16))

    o = _head_rms(jnp.concatenate(o_heads, axis=-1), hg_g_ref[...]) * hgate

    p_conv = _dot(c.astype(BF16), w_conv_out_ref[...])
    p_mem = _dot(jnp.concatenate(om, axis=-1).astype(BF16), w_mem_out_ref[...])
    p_hg = _dot(o.astype(BF16), w_hg_out_ref[...])
    y_ref[...] = _merge_out(lambda i: gates[i], x, (p_conv, p_hg, p_mem), w_out_ref, post_g_ref[...])

    @pl.when(t == last_t)
    def _():
        new_conv_ref[...] = shift_ref[0, tl + CONV_ROWS - CONV_PREFIX:tl + CONV_ROWS, :]
        for h in range(HG_HEADS):
            new_hg_ref[h] = state_t_ref[h].T

    shift_ref[0, 0:CONV_ROWS, :] = shift_ref[0, tl:tl + CONV_ROWS, :]


def _const_spec(shape):
    zeros = (0,) * len(shape)
    return pl.BlockSpec(shape, lambda *_: zeros, pipeline_mode=pl.Buffered(1))


def _mixer_weight_specs(w_in, w_conv_out, w_hg_out, w_mem_out, w_out):
    return [_const_spec(w.shape) for w in (w_in, w_conv_out, w_hg_out, w_mem_out, w_out)]


def _vector_specs():
    return [
        _const_spec((1, D_MODEL)), _const_spec((1, D_MODEL)),
        _const_spec((CONV_WIDTH, D_CONV)), _const_spec((1, D_CONV)),
        _const_spec((1, D_CONV)), _const_spec((1, D_CONV)),
        _const_spec((2, D_HGRN)), _const_spec((1, D_HGRN)),
    ]


def _prompt_mixer(x, mk, mv, vectors, weights):
    batch, seq, _ = x.shape
    n_mem = mk.shape[1]
    tl = PROMPT_TILE
    return pl.pallas_call(
        _prompt_mixer_kernel,
        grid=(batch, seq // tl),
        in_specs=[
            pl.BlockSpec((None, tl, D_MODEL), lambda b, t: (b, t, 0)),
            pl.BlockSpec((None, n_mem, D_MEM), lambda b, t: (b, 0, 0)),
            pl.BlockSpec((None, n_mem, D_MEM), lambda b, t: (b, 0, 0)),
        ] + _vector_specs() + _mixer_weight_specs(*weights),
        out_specs=[
            pl.BlockSpec((None, tl, D_MODEL), lambda b, t: (b, t, 0)),
            pl.BlockSpec((None, CONV_PREFIX, D_CONV), lambda b, t: (b, 0, 0)),
            pl.BlockSpec((None, HG_HEADS, HG_DK, HG_DV), lambda b, t: (b, 0, 0, 0)),
        ],
        out_shape=[
            jax.ShapeDtypeStruct((batch, seq, D_MODEL), F32),
            jax.ShapeDtypeStruct((batch, CONV_PREFIX, D_CONV), F32),
            jax.ShapeDtypeStruct((batch, HG_HEADS, HG_DK, HG_DV), F32),
        ],
        scratch_shapes=[
            pltpu.VMEM((SUBLANES, tl + CONV_ROWS, D_CONV), F32),
            pltpu.VMEM((HG_HEADS, HG_DV, HG_DK), F32),
            pltpu.VMEM((n_mem, D_MEM), BF16),
            pltpu.VMEM((n_mem, D_MEM), BF16),
        ],
        compiler_params=pltpu.CompilerParams(
            dimension_semantics=("parallel", "arbitrary"), vmem_limit_bytes=VMEM_LIMIT),
        name="prompt_mixer",
    )(x, mk, mv, *vectors, *weights)


def _sample_in_kernel(x_ref, pre_g_ref, lb_logits_ref, w_in_ref,
                      u_ref, qd_ref, ki_ref, ke_ref, v_ref, blast_ref, q4_ref, hgate_ref, gates_ref):
    n_seq = x_ref.shape[0] // SUBLANES
    hb = _rms(x_ref[...], pre_g_ref[...]).astype(BF16)
    u_ref[...] = _dot(hb, w_in_ref[:, COL_CA:COL_CB]) * _sigmoid(_dot(hb, w_in_ref[:, COL_CB:COL_HQ]))
    lb = _lower_bound(lb_logits_ref[...])
    q_dec, k_inv, k_end, b_last = _hgrn_gates(
        _dot(hb, w_in_ref[:, COL_HQ:COL_HF]), _dot(hb, w_in_ref[:, COL_HF:COL_HI]), lb, SUBLANES)
    qd_ref[...] = q_dec
    ki_ref[...] = k_inv
    ke_ref[...] = k_end
    blast_ref[...] = b_last
    v_ref[...] = _dot(hb, w_in_ref[:, COL_HI:COL_HGATE])
    hgate_ref[...] = _silu(_dot(hb, w_in_ref[:, COL_HGATE:COL_MQ]))
    mq = _dot(hb, w_in_ref[:, COL_MQ:COL_GATES])
    for h in range(MEM_HEADS):
        q4_ref[:, h * SUBLANES:(h + 1) * SUBLANES, :] = (
            mq[:, h * MEM_HEAD_DIM:(h + 1) * MEM_HEAD_DIM].reshape(n_seq, SUBLANES, MEM_HEAD_DIM))
    for i in range(3):
        gates_ref[:, i * D_MODEL:(i + 1) * D_MODEL] = _branch_gate(hb, w_in_ref, i)


def _sample_seq_kernel(u_ref, qd_ref, ki_ref, ke_ref, v_ref, blast_ref, q4_ref, sc_ref, sh_ref,
                       k2_ref, v2_ref, conv_w_ref,
                       c_ref, o_ref, om_ref, new_conv_ref, new_hg_ref, ext_ref):
    n_seq, dec = sc_ref.shape[0], SUBLANES
    rows = n_seq * dec

    for g in range(n_seq):
        rs = slice(g * dec, (g + 1) * dec)
        ext_ref[g, 0:CONV_PREFIX, :] = sc_ref[g]
        ext_ref[g, CONV_PREFIX:CONV_PREFIX + dec, :] = u_ref[rs, :]
        new_conv_ref[g] = ext_ref[g, dec:dec + CONV_PREFIX, :]
        acc = jnp.zeros((dec, D_CONV), F32)
        for j in range(CONV_WIDTH):
            acc = acc + ext_ref[g, j:j + dec, :] * conv_w_ref[j:j + 1, :]
        c_ref[rs, :] = acc

    row = lax.broadcasted_iota(jnp.int32, (rows, rows), 0)
    col = lax.broadcasted_iota(jnp.int32, (rows, rows), 1)
    causal = ((row ^ col) < dec) & (col <= row)
    decay = jnp.exp(blast_ref[...])
    for h in range(HG_HEADS):
        hs = slice(h * HG_DK, (h + 1) * HG_DK)
        scores = jnp.where(causal, _dot_nt(qd_ref[:, hs].astype(BF16), ki_ref[:, hs].astype(BF16)), 0.0)
        o_intra = _dot(scores.astype(BF16), v_ref[:, hs].astype(BF16))
        for g in range(n_seq):
            rs = slice(g * dec, (g + 1) * dec)
            s0 = sh_ref[g, h]
            o_ref[rs, hs] = o_intra[rs] + _dot(qd_ref[rs, hs].astype(BF16), s0.astype(BF16))
            decay_col = jnp.broadcast_to(decay[g:g + 1, hs], (HG_DV, HG_DK)).T
            new_hg_ref[g, h] = decay_col * s0 + _dot_tn(ke_ref[rs, hs].astype(BF16),
                                                        v_ref[rs, hs].astype(BF16))

    assert dec == 8 and MEM_HEADS == 4
    q_head = lax.broadcasted_iota(jnp.int32, (MEM_HEADS * dec, k2_ref.shape[1]), 0) >> 3
    k_head = lax.broadcasted_iota(jnp.int32, (MEM_HEADS * dec, k2_ref.shape[1]), 1) & 3
    own_head = q_head == k_head
    for g in range(n_seq):
        rs = slice(g * dec, (g + 1) * dec)
        s = _dot_nt(q4_ref[g].astype(BF16), k2_ref[g].astype(BF16)) * (MEM_HEAD_DIM ** -0.5)
        s = jnp.where(own_head, s, -jnp.inf)
        e = jnp.exp(s - jnp.max(s, axis=-1, keepdims=True))
        p = e / jnp.sum(e, axis=-1, keepdims=True)
        om = _dot(p.astype(BF16), v2_ref[g].astype(BF16))
        for h in range(MEM_HEADS):
            om_ref[rs, h * MEM_HEAD_DIM:(h + 1) * MEM_HEAD_DIM] = om[h * dec:(h + 1) * dec]


def _sample_out_kernel(x_ref, c_ref, o_ref, om_ref, hgate_ref, gates_ref, post_g_ref, conv_b_ref,
                       ln_g_ref, ln_b_ref, hg_g_ref, w_conv_out_ref, w_hg_out_ref, w_mem_out_ref,
                       w_out_ref, ffn_pre_g_ref, ffn_post_g_ref, w_gate_ref, w_up_ref, w_down_ref, y_ref):
    c = _conv_ln_silu(c_ref[...], conv_b_ref[...], ln_g_ref[...], ln_b_ref[...])
    p_conv = _dot(c.astype(BF16), w_conv_out_ref[...])
    o = _head_rms(o_ref[...], hg_g_ref[...]) * hgate_ref[...]
    p_hg = _dot(o.astype(BF16), w_hg_out_ref[...])
    p_mem = _dot(om_ref[...].astype(BF16), w_mem_out_ref[...])
    x1 = _merge_out(lambda i: gates_ref[:, i * D_MODEL:(i + 1) * D_MODEL], x_ref[...],
                    (p_conv, p_hg, p_mem), w_out_ref, post_g_ref[...])
    y_ref[...] = _ffn_block(x1, ffn_pre_g_ref[...], ffn_post_g_ref[...], w_gate_ref, w_up_ref, w_down_ref)


def _row_spec(tile, width):
    return pl.BlockSpec((tile, width), lambda i: (i, 0))


def _sample_in(x2d, pre_g, lb_logits, w_in):
    rows = x2d.shape[0]
    tile = SAMPLE_TILE
    seqs = tile // SUBLANES
    wide = [D_CONV, D_HGRN, D_HGRN, D_HGRN, D_HGRN]
    return pl.pallas_call(
        _sample_in_kernel,
        grid=(rows // tile,),
        in_specs=[_row_spec(tile, D_MODEL), _const_spec((1, D_MODEL)), _const_spec((2, D_HGRN)),
                  _const_spec(w_in.shape)],
        out_specs=[_row_spec(tile, w) for w in wide] + [
            _row_spec(seqs, D_HGRN),
            pl.BlockSpec((seqs, MEM_HEADS * SUBLANES, MEM_HEAD_DIM), lambda i: (i, 0, 0)),
            _row_spec(tile, D_HGRN),
            _row_spec(tile, 3 * D_MODEL),
        ],
        out_shape=[jax.ShapeDtypeStruct((rows, w), F32) for w in wide] + [
            jax.ShapeDtypeStruct((rows // SUBLANES, D_HGRN), F32),
            jax.ShapeDtypeStruct((rows // SUBLANES, MEM_HEADS * SUBLANES, MEM_HEAD_DIM), F32),
            jax.ShapeDtypeStruct((rows, D_HGRN), F32),
            jax.ShapeDtypeStruct((rows, 3 * D_MODEL), F32),
        ],
        compiler_params=pltpu.CompilerParams(
            dimension_semantics=("parallel",), vmem_limit_bytes=VMEM_LIMIT),
        name="sample_in",
    )(x2d, pre_g, lb_logits, w_in)


def _sample_seq(u, qd, ki, ke, v, blast, q4, state_conv, state_hgrn, k2, v2, conv_w):
    n = state_conv.shape[0]
    g = SAMPLE_GROUP
    rows = g * SUBLANES
    kv_rows = k2.shape[1]
    seq_block = lambda *tail: pl.BlockSpec((g,) + tail, lambda i: (i,) + (0,) * len(tail))
    return pl.pallas_call(
        _sample_seq_kernel,
        grid=(n // g,),
        in_specs=[_row_spec(rows, D_CONV)] + [_row_spec(rows, D_HGRN)] * 4 + [
            _row_spec(g, D_HGRN),
            seq_block(MEM_HEADS * SUBLANES, MEM_HEAD_DIM),
            seq_block(CONV_PREFIX, D_CONV),
            seq_block(HG_HEADS, HG_DK, HG_DV),
            seq_block(kv_rows, MEM_HEAD_DIM),
            seq_block(kv_rows, MEM_HEAD_DIM),
            _const_spec((CONV_WIDTH, D_CONV)),
        ],
        out_specs=[_row_spec(rows, D_CONV), _row_spec(rows, D_HGRN), _row_spec(rows, D_MEM),
                   seq_block(CONV_PREFIX, D_CONV), seq_block(HG_HEADS, HG_DK, HG_DV)],
        out_shape=[
            jax.ShapeDtypeStruct((n * SUBLANES, D_CONV), F32),
            jax.ShapeDtypeStruct((n * SUBLANES, D_HGRN), F32),
            jax.ShapeDtypeStruct((n * SUBLANES, D_MEM), F32),
            jax.ShapeDtypeStruct((n, CONV_PREFIX, D_CONV), F32),
            jax.ShapeDtypeStruct((n, HG_HEADS, HG_DK, HG_DV), F32),
        ],
        scratch_shapes=[pltpu.VMEM((g, CONV_PREFIX + SUBLANES + 2, D_CONV), F32)],
        compiler_params=pltpu.CompilerParams(
            dimension_semantics=("parallel",), vmem_limit_bytes=VMEM_LIMIT),
        name="sample_seq",
    )(u, qd, ki, ke, v, blast, q4, state_conv, state_hgrn, k2, v2, conv_w)


def _sample_out(x2d, c, o, om, hgate, gates, vectors, weights, ffn_vectors, ffn_weights):
    rows = x2d.shape[0]
    tile = SAMPLE_TILE
    acts = (x2d, c, o, om, hgate, gates)
    consts = tuple(vectors) + tuple(weights) + tuple(ffn_vectors) + tuple(ffn_weights)
    return pl.pallas_call(
        _sample_out_kernel,
        grid=(rows // tile,),
        in_specs=[_row_spec(tile, a.shape[1]) for a in acts] + [_const_spec(a.shape) for a in consts],
        out_specs=_row_spec(tile, D_MODEL),
        out_shape=jax.ShapeDtypeStruct((rows, D_MODEL), F32),
        compiler_params=pltpu.CompilerParams(
            dimension_semantics=("parallel",), vmem_limit_bytes=VMEM_LIMIT),
        name="sample_out",
    )(*acts, *consts)


def _ffn_kernel(x_ref, pre_g_ref, post_g_ref, w_gate_ref, w_up_ref, w_down_ref, y_ref):
    y_ref[...] = _ffn_block(x_ref[...], pre_g_ref[...], post_g_ref[...], w_gate_ref, w_up_ref, w_down_ref)


def _ffn(x2d, pre_g, post_g, w_gate, w_up, w_down):
    rows = x2d.shape[0]
    tile = min(FFN_TILE, rows)
    return pl.pallas_call(
        _ffn_kernel,
        grid=(rows // tile,),
        in_specs=[
            pl.BlockSpec((tile, D_MODEL), lambda i: (i, 0)),
            _const_spec((1, D_MODEL)), _const_spec((1, D_MODEL)),
            _const_spec(w_gate.shape), _const_spec(w_up.shape), _const_spec(w_down.shape),
        ],
        out_specs=pl.BlockSpec((tile, D_MODEL), lambda i: (i, 0)),
        out_shape=jax.ShapeDtypeStruct((rows, D_MODEL), F32),
        compiler_params=pltpu.CompilerParams(
            dimension_semantics=("parallel",), vmem_limit_bytes=VMEM_LIMIT),
        name="ffn",
    )(x2d, pre_g, post_g, w_gate, w_up, w_down)


def kernel(x_prompt, x_sample, mem_prompt, state_conv, state_hgrn, cache_mem_k, cache_mem_v, norm_pre_mix, norm_post_mix, norm_pre_ffn, norm_post_ffn, w_in, conv_w, conv_b, conv_ln_g, conv_ln_b, w_conv_out, hg_lb_logits, hg_norm_g, w_hg_out, mem_norm_g, w_mem_kv, w_mem_out, w_out, w_ffn_gate, w_ffn_up, w_ffn_down):
    depth = w_in.shape[0]
    assert depth == 1 and hg_lb_logits.shape[0] == 2, "single-layer step"
    batch, seq, _ = x_prompt.shape
    n_dec, dec, _ = x_sample.shape
    n_mem = mem_prompt.shape[1]
    assert seq % PROMPT_TILE == 0 and n_dec % SAMPLE_GROUP == 0 and dec == SUBLANES
    assert (n_dec * dec) % SAMPLE_TILE == 0

    vectors = (norm_pre_mix, norm_post_mix, conv_w[0], conv_b, conv_ln_g, conv_ln_b,
               hg_lb_logits, jnp.tile(hg_norm_g, (1, HG_HEADS)))
    weights = tuple(w[0].astype(BF16) for w in (w_in, w_conv_out, w_hg_out, w_mem_out, w_out))
    ffn_weights = tuple(w[0].astype(BF16) for w in (w_ffn_gate, w_ffn_up, w_ffn_down))

    mk, mv = _memory_kv(mem_prompt.reshape(batch * n_mem, D_MODEL), mem_norm_g,
                        w_mem_kv[0].astype(BF16), n_mem)
    mk = mk.reshape(batch, n_mem, D_MEM)
    mv = mv.reshape(batch, n_mem, D_MEM)

    xp, conv_p, hg_p = _prompt_mixer(x_prompt, mk, mv, vectors, weights)
    yp = _ffn(xp.reshape(batch * seq, D_MODEL), norm_pre_ffn, norm_post_ffn, *ffn_weights)

    xs2d = x_sample.reshape(n_dec * dec, D_MODEL)
    u, qd, ki, ke, v, blast, q4, hgate, gates = _sample_in(xs2d, norm_pre_mix, hg_lb_logits, weights[0])
    k2 = cache_mem_k[0].reshape(n_dec, n_mem * MEM_HEADS, MEM_HEAD_DIM)
    v2 = cache_mem_v[0].reshape(n_dec, n_mem * MEM_HEADS, MEM_HEAD_DIM)
    c, o, om, conv_s, hg_s = _sample_seq(u, qd, ki, ke, v, blast, q4, state_conv[0], state_hgrn[0],
                                         k2, v2, conv_w[0])
    ys = _sample_out(xs2d, c, o, om, hgate, gates,
                     (norm_post_mix, conv_b, conv_ln_g, conv_ln_b, vectors[-1]), weights[1:],
                     (norm_pre_ffn, norm_post_ffn), ffn_weights)

    kv_shape = (1, batch, n_mem, MEM_HEADS, MEM_HEAD_DIM)
    return (yp.reshape(batch, seq, D_MODEL), ys.reshape(n_dec, dec, D_MODEL),
            conv_p[None], hg_p[None], mk.reshape(kv_shape), mv.reshape(kv_shape),
            conv_s[None], hg_s[None])
```

```python
from typing import NamedTuple

import jax
import jax.numpy as jnp
from jax import lax
from jax.experimental import pallas as pl
from jax.experimental.pallas import tpu as pltpu

D_MODEL = 1024
D_CONV = 512
CONV_WIDTH = 31
CONV_PREFIX = CONV_WIDTH - 1
HG_HEADS = 4
HG_DK = 128
HG_DV = 128
D_HGRN = HG_HEADS * HG_DK
MEM_HEADS = 4
MEM_HEAD_DIM = 128
D_MEM = MEM_HEADS * MEM_HEAD_DIM
CHUNK = 32
EPS = 1e-6
NEG_LOG2E = -1.4426950408889634

COL_CA = 0
COL_CB = COL_CA + D_CONV
COL_HQ = COL_CB + D_CONV
COL_HF = COL_HQ + D_HGRN
COL_HI = COL_HF + D_HGRN
COL_HGATE = COL_HI + D_HGRN
COL_MQ = COL_HGATE + D_HGRN
COL_GATES = COL_MQ + D_MEM

SUBLANES = 8
VMEM_LIMIT = 56 * 1024 * 1024

PROMPT_TILE = 256
PROMPT_STEP_TILES = 1
SAMPLE_GROUP = 8
SAMPLE_TILE = 256
FFN_TILE = 512
CONV_ROWS = 32

BF16 = jnp.bfloat16
F32 = jnp.float32


def _dot(a, b):
    return jnp.dot(a, b, preferred_element_type=F32)


def _dot_nt(a, b):
    return lax.dot_general(a, b, (((1,), (1,)), ((), ())), preferred_element_type=F32)


def _dot_tn(a, b):
    return lax.dot_general(a, b, (((0,), (0,)), ((), ())), preferred_element_type=F32)


def _rms(x, gain):
    return x * lax.rsqrt(jnp.mean(x * x, axis=-1, keepdims=True) + EPS) * gain


def _sigmoid(x):
    return 1.0 / (1.0 + jnp.exp2(x * NEG_LOG2E))


def _silu(x):
    return x * _sigmoid(x)


def _forget_terms(hf, lb):
    sig = _sigmoid(hf)
    return jnp.log(lb + (1.0 - lb) * sig), (1.0 - lb) * (1.0 - sig)


def _lower_bound(lb_logits):
    m = jnp.max(lb_logits, axis=0, keepdims=True)
    e = jnp.exp(lb_logits - m)
    return e[0:1] / jnp.sum(e, axis=0, keepdims=True)


def _segment_cumsum(x, seg):
    pos = lax.broadcasted_iota(jnp.int32, x.shape, 0) & (seg - 1)
    s = 1
    while s < seg:
        x = x + jnp.where(pos >= s, pltpu.roll(x, s, axis=0), 0.0)
        s *= 2
    return x


def _conv_ln_silu(c, conv_b, ln_g, ln_b):
    c = c + conv_b
    mu = jnp.mean(c, axis=-1, keepdims=True)
    d = c - mu
    var = jnp.mean(d * d, axis=-1, keepdims=True)
    return _silu(d * lax.rsqrt(var + EPS) * ln_g + ln_b)


def _hgrn_gates(hq, hf, lb, seg):
    q = _silu(hq)
    logf, k = _forget_terms(hf, lb)
    b = _segment_cumsum(logf, seg)
    rows = b.shape[0]
    b3 = b.reshape(rows // seg, seg, D_HGRN)
    b_last3 = b3[:, seg - 1:seg, :]
    rest = (b_last3 - b3).reshape(rows, D_HGRN)
    q_dec = q * jnp.exp(b)
    k_inv = k * jnp.exp(-b)
    k_end = k * jnp.exp(rest)
    return q_dec, k_inv, k_end, b_last3.reshape(rows // seg, D_HGRN)


class _TileLevels(NamedTuple):
    q_mid: jax.Array
    k_mid: jax.Array
    q_64: jax.Array
    k_64: jax.Array
    q_128: jax.Array
    k_128: jax.Array
    q_tile: jax.Array
    k_tile: jax.Array
    decay: jax.Array


def _hgrn_tile_levels(hq, hf, lb):
    tl = hq.shape[0]
    n_blocks = 4
    block = 2 * CHUNK
    assert tl == n_blocks * block
    q = _silu(hq)
    logf, k = _forget_terms(hf, lb)
    b = _segment_cumsum(logf, CHUNK)
    chunks = [b[n * CHUNK:(n + 1) * CHUNK] for n in range(2 * n_blocks)]
    totals = [c[CHUNK - 1:CHUNK] for c in chunks]
    e = jnp.concatenate([c - totals[n] if n % 2 == 0 else c for n, c in enumerate(chunks)], axis=0)
    q_mid = q * jnp.exp(e)
    k_mid = k * jnp.exp(-e)
    first = [totals[2 * j] for j in range(n_blocks)]
    second = [totals[2 * j + 1] for j in range(n_blocks)]
    both = [first[j] + second[j] for j in range(n_blocks)]

    def scaled(x, blocks, log_scales):
        return jnp.concatenate(
            [x[j * block:(j + 1) * block] * jnp.exp(s) for j, s in zip(blocks, log_scales)], axis=0)

    every = range(n_blocks)
    return _TileLevels(
        q_mid=q_mid, k_mid=k_mid,
        q_64=scaled(q_mid, every, first),
        k_64=scaled(k_mid, every, second),
        q_128=scaled(q_mid, (2, 3), (first[2], first[3] + both[2])),
        k_128=scaled(k_mid, (0, 1), (second[0] + both[1], second[1])),
        q_tile=scaled(q_mid, every, [first[j] + sum(both[:j], 0.0) for j in every]),
        k_tile=scaled(k_mid, every, [second[j] + sum(both[j + 1:], 0.0) for j in every]),
        decay=jnp.exp(sum(both[1:], both[0])))


def _head_rms(o, gain4):
    parts = []
    for h in range(HG_HEADS):
        oh = o[:, h * HG_DV:(h + 1) * HG_DV]
        parts.append(oh * lax.rsqrt(jnp.mean(oh * oh, axis=-1, keepdims=True) + EPS))
    return jnp.concatenate(parts, axis=-1) * gain4


def _branch_gate(hb, w_in_ref, i):
    c0 = COL_GATES + i * D_MODEL
    return _sigmoid(_dot(hb, w_in_ref[:, c0:c0 + D_MODEL]))


def _merge_out(gate, x, branches, w_out_ref, post_g):
    acc = None
    for i, p in enumerate(branches):
        acc = gate(i) * p if acc is None else acc + gate(i) * p
    m = _dot(acc.astype(BF16), w_out_ref[...])
    return x + _rms(m, post_g)


def _ffn_block(x, pre_g, post_g, w_gate_ref, w_up_ref, w_down_ref):
    hb = _rms(x, pre_g).astype(BF16)
    f = _silu(_dot(hb, w_gate_ref[...])) * _dot(hb, w_up_ref[...])
    d = _dot(f.astype(BF16), w_down_ref[...])
    return x + _rms(d, post_g)


def _memory_kv_kernel(mem_ref, g_ref, w_ref, k_ref, v_ref):
    m = _rms(mem_ref[...], g_ref[...]).astype(BF16)
    k_ref[...] = _dot(m, w_ref[:, :D_MEM])
    v_ref[...] = _dot(m, w_ref[:, D_MEM:])


def _memory_kv(mem2d, gain, w_bf16, tile):
    rows = mem2d.shape[0]
    return pl.pallas_call(
        _memory_kv_kernel,
        grid=(rows // tile,),
        in_specs=[
            pl.BlockSpec((tile, D_MODEL), lambda i: (i, 0)),
            pl.BlockSpec((1, D_MODEL), lambda i: (0, 0)),
            pl.BlockSpec((D_MODEL, 2 * D_MEM), lambda i: (0, 0)),
        ],
        out_specs=[
            pl.BlockSpec((tile, D_MEM), lambda i: (i, 0)),
            pl.BlockSpec((tile, D_MEM), lambda i: (i, 0)),
        ],
        out_shape=[jax.ShapeDtypeStruct((rows, D_MEM), F32)] * 2,
        compiler_params=pltpu.CompilerParams(dimension_semantics=("parallel",)),
        name="memory_kv",
    )(mem2d, gain, w_bf16)


def _prompt_mixer_kernel(x_ref, mk_ref, mv_ref, pre_g_ref, post_g_ref, conv_w_ref, conv_b_ref,
                         ln_g_ref, ln_b_ref, lb_logits_ref, hg_g_ref, w_in_ref, w_conv_out_ref,
                         w_hg_out_ref, w_mem_out_ref, w_out_ref,
                         y_ref, new_conv_ref, new_hg_ref,
                         shift_ref, tail_ref, state_t_ref, kb_ref, vb_ref):
    t = pl.program_id(1)
    last_t = pl.num_programs(1) - 1
    tl = PROMPT_TILE
    n_tiles = x_ref.shape[0] // tl

    @pl.when(t == 0)
    def _():
        tail_ref[...] = jnp.zeros_like(tail_ref)
        state_t_ref[...] = jnp.zeros_like(state_t_ref)
        kb_ref[...] = mk_ref[...].astype(BF16)
        vb_ref[...] = mv_ref[...].astype(BF16)

    lb = _lower_bound(lb_logits_ref[...])
    half = tl // 2
    row = lax.broadcasted_iota(jnp.int32, (tl, tl), 0)
    col = lax.broadcasted_iota(jnp.int32, (tl, tl), 1)
    span = row ^ col
    near = (span < 2 * CHUNK) & (col <= row)
    mid = (span < 4 * CHUNK) & (col < row)

    def tile(i, u_tail):
        rows = slice(i * tl, (i + 1) * tl)
        shift = shift_ref.at[i]
        x = x_ref[rows, :]
        hb = _rms(x, pre_g_ref[...]).astype(BF16)

        def proj(c0, c1):
            return _dot(hb, w_in_ref[:, c0:c1])

        u = proj(COL_CA, COL_CB) * _sigmoid(proj(COL_CB, COL_HQ))
        shift[0, 0:CONV_ROWS, :] = u_tail
        shift[0, CONV_ROWS:CONV_ROWS + tl, :] = u
        for r in range(1, SUBLANES):
            shift[r, 0:tl + CONV_ROWS - SUBLANES, :] = shift[0, r:r + tl + CONV_ROWS - SUBLANES, :]

        conv_blocks = []
        for base in range(0, tl, CONV_ROWS):
            acc = None
            for j in range(CONV_WIDTH):
                s = j + (CONV_ROWS - CONV_PREFIX)
                r, a = s % SUBLANES, s // SUBLANES
                lo = base + a * SUBLANES
                term = shift[r, lo:lo + CONV_ROWS, :] * conv_w_ref[j:j + 1, :]
                acc = term if acc is None else acc + term
            conv_blocks.append(acc)
        c = _conv_ln_silu(jnp.concatenate(conv_blocks, axis=0),
                          conv_b_ref[...], ln_g_ref[...], ln_b_ref[...]).astype(BF16)

        lv = _hgrn_tile_levels(proj(COL_HQ, COL_HF), proj(COL_HF, COL_HI), lb)
        lvb = _TileLevels(*[a.astype(BF16) for a in lv[:-1]], lv.decay)
        vb16 = proj(COL_HI, COL_HGATE).astype(BF16)
        hgate = _silu(proj(COL_HGATE, COL_MQ))
        mq = proj(COL_MQ, COL_GATES).astype(BF16)
        gates = [_branch_gate(hb, w_in_ref, g) for g in range(3)]

        om = []
        for h in range(MEM_HEADS):
            hs = slice(h * MEM_HEAD_DIM, (h + 1) * MEM_HEAD_DIM)
            s = _dot_nt(mq[:, hs], kb_ref[:, hs]) * (MEM_HEAD_DIM ** -0.5)
            e = jnp.exp(s - jnp.max(s, axis=-1, keepdims=True))
            p = e / jnp.sum(e, axis=-1, keepdims=True)
            om.append(_dot(p.astype(BF16), vb_ref[:, hs]))

        o_heads = []
        for h in range(HG_HEADS):
            hs = slice(h * HG_DK, (h + 1) * HG_DK)
            vh = vb16[:, hs]
            s_near = _dot_nt(lvb.q_mid[:, hs], lvb.k_mid[:, hs])
            s_mid = _dot_nt(lvb.q_64[:, hs], lvb.k_64[:, hs])
            s_far = _dot_nt(lvb.q_128[:, hs], lvb.k_128[:, hs])
            base = jnp.where(near, s_near, jnp.where(mid, s_mid, 0.0))
            scores = jnp.concatenate(
                [base[:half], jnp.concatenate([s_far, base[half:, half:]], axis=1)], axis=0)
            st = state_t_ref[h]
            o_heads.append(_dot(scores.astype(BF16), vh) + _dot_nt(lvb.q_tile[:, hs], st.astype(BF16)))
            state_t_ref[h] = st * lvb.decay[:, hs] + _dot_tn(vh, lvb.k_tile[:, hs])
        o = _head_rms(jnp.concatenate(o_heads, axis=-1), hg_g_ref[...]) * hgate

        p_conv = _dot(c, w_conv_out_ref[...])
        p_mem = _dot(jnp.concatenate(om, axis=-1).astype(BF16), w_mem_out_ref[...])
        p_hg = _dot(o.astype(BF16), w_hg_out_ref[...])
        y_ref[rows, :] = _merge_out(lambda g: gates[g], x, (p_conv, p_hg, p_mem), w_out_ref,
                                    post_g_ref[...])

        return u[tl - CONV_ROWS:]

    u_tail = tail_ref[...]
    for i in range(n_tiles):
        u_tail = tile(i, u_tail)
    tail_ref[...] = u_tail

    @pl.when(t == last_t)
    def _():
        new_conv_ref[...] = tail_ref[CONV_ROWS - CONV_PREFIX:CONV_ROWS, :]
        for h in range(HG_HEADS):
            new_hg_ref[h] = state_t_ref[h].T


def _const_spec(shape):
    zeros = (0,) * len(shape)
    return pl.BlockSpec(shape, lambda *_: zeros, pipeline_mode=pl.Buffered(1))


def _mixer_weight_specs(w_in, w_conv_out, w_hg_out, w_mem_out, w_out):
    return [_const_spec(w.shape) for w in (w_in, w_conv_out, w_hg_out, w_mem_out, w_out)]


def _vector_specs():
    return [
        _const_spec((1, D_MODEL)), _const_spec((1, D_MODEL)),
        _const_spec((CONV_WIDTH, D_CONV)), _const_spec((1, D_CONV)),
        _const_spec((1, D_CONV)), _const_spec((1, D_CONV)),
        _const_spec((2, D_HGRN)), _const_spec((1, D_HGRN)),
    ]


def _prompt_mixer(x, mk, mv, vectors, weights):
    batch, seq, _ = x.shape
    n_mem = mk.shape[1]
    tl = PROMPT_TILE * PROMPT_STEP_TILES
    return pl.pallas_call(
        _prompt_mixer_kernel,
        grid=(batch, seq // tl),
        in_specs=[
            pl.BlockSpec((None, tl, D_MODEL), lambda b, t: (b, t, 0)),
            pl.BlockSpec((None, n_mem, D_MEM), lambda b, t: (b, 0, 0)),
            pl.BlockSpec((None, n_mem, D_MEM), lambda b, t: (b, 0, 0)),
        ] + _vector_specs() + _mixer_weight_specs(*weights),
        out_specs=[
            pl.BlockSpec((None, tl, D_MODEL), lambda b, t: (b, t, 0)),
            pl.BlockSpec((None, CONV_PREFIX, D_CONV), lambda b, t: (b, 0, 0)),
            pl.BlockSpec((None, HG_HEADS, HG_DK, HG_DV), lambda b, t: (b, 0, 0, 0)),
        ],
        out_shape=[
            jax.ShapeDtypeStruct((batch, seq, D_MODEL), F32),
            jax.ShapeDtypeStruct((batch, CONV_PREFIX, D_CONV), F32),
            jax.ShapeDtypeStruct((batch, HG_HEADS, HG_DK, HG_DV), F32),
        ],
        scratch_shapes=[
            pltpu.VMEM((PROMPT_STEP_TILES, SUBLANES, PROMPT_TILE + CONV_ROWS, D_CONV), F32),
            pltpu.VMEM((CONV_ROWS, D_CONV), F32),
            pltpu.VMEM((HG_HEADS, HG_DV, HG_DK), F32),
            pltpu.VMEM((n_mem, D_MEM), BF16),
            pltpu.VMEM((n_mem, D_MEM), BF16),
        ],
        compiler_params=pltpu.CompilerParams(
            dimension_semantics=("parallel", "arbitrary"), vmem_limit_bytes=VMEM_LIMIT),
        name="prompt_mixer",
    )(x, mk, mv, *vectors, *weights)


def _sample_in_kernel(x_ref, pre_g_ref, lb_logits_ref, w_in_ref,
                      u_ref, qd_ref, ki_ref, ke_ref, v_ref, blast_ref, q4_ref, hgate_ref, gates_ref):
    n_seq = x_ref.shape[0] // SUBLANES
    hb = _rms(x_ref[...], pre_g_ref[...]).astype(BF16)
    u_ref[...] = _dot(hb, w_in_ref[:, COL_CA:COL_CB]) * _sigmoid(_dot(hb, w_in_ref[:, COL_CB:COL_HQ]))
    lb = _lower_bound(lb_logits_ref[...])
    q_dec, k_inv, k_end, b_last = _hgrn_gates(
        _dot(hb, w_in_ref[:, COL_HQ:COL_HF]), _dot(hb, w_in_ref[:, COL_HF:COL_HI]), lb, SUBLANES)
    qd_ref[...] = q_dec
    ki_ref[...] = k_inv
    ke_ref[...] = k_end
    blast_ref[...] = b_last
    v_ref[...] = _dot(hb, w_in_ref[:, COL_HI:COL_HGATE])
    hgate_ref[...] = _silu(_dot(hb, w_in_ref[:, COL_HGATE:COL_MQ]))
    mq = _dot(hb, w_in_ref[:, COL_MQ:COL_GATES])
    for h in range(MEM_HEADS):
        q4_ref[:, h * SUBLANES:(h + 1) * SUBLANES, :] = (
            mq[:, h * MEM_HEAD_DIM:(h + 1) * MEM_HEAD_DIM].reshape(n_seq, SUBLANES, MEM_HEAD_DIM))
    for i in range(3):
        gates_ref[:, i * D_MODEL:(i + 1) * D_MODEL] = _branch_gate(hb, w_in_ref, i)


def _sample_seq_kernel(u_ref, qd_ref, ki_ref, ke_ref, v_ref, blast_ref, q4_ref, sc_ref, sh_ref,
                       k2_ref, v2_ref, conv_w_ref,
                       c_ref, o_ref, om_ref, new_conv_ref, new_hg_ref, ext_ref):
    n_seq, dec = sc_ref.shape[0], SUBLANES
    rows = n_seq * dec

    for g in range(n_seq):
        rs = slice(g * dec, (g + 1) * dec)
        ext_ref[g, 0:CONV_PREFIX, :] = sc_ref[g]
        ext_ref[g, CONV_PREFIX:CONV_PREFIX + dec, :] = u_ref[rs, :]
        new_conv_ref[g] = ext_ref[g, dec:dec + CONV_PREFIX, :]
        acc = jnp.zeros((dec, D_CONV), F32)
        for j in range(CONV_WIDTH):
            acc = acc + ext_ref[g, j:j + dec, :] * conv_w_ref[j:j + 1, :]
        c_ref[rs, :] = acc

    row = lax.broadcasted_iota(jnp.int32, (rows, rows), 0)
    col = lax.broadcasted_iota(jnp.int32, (rows, rows), 1)
    causal = ((row ^ col) < dec) & (col <= row)
    decay = jnp.exp(blast_ref[...])
    for h in range(HG_HEADS):
        hs = slice(h * HG_DK, (h + 1) * HG_DK)
        scores = jnp.where(causal, _dot_nt(qd_ref[:, hs].astype(BF16), ki_ref[:, hs].astype(BF16)), 0.0)
        o_intra = _dot(scores.astype(BF16), v_ref[:, hs].astype(BF16))
        for g in range(n_seq):
            rs = slice(g * dec, (g + 1) * dec)
            s0 = sh_ref[g, h]
            o_ref[rs, hs] = o_intra[rs] + _dot(qd_ref[rs, hs].astype(BF16), s0.astype(BF16))
            decay_col = jnp.broadcast_to(decay[g:g + 1, hs], (HG_DV, HG_DK)).T
            new_hg_ref[g, h] = decay_col * s0 + _dot_tn(ke_ref[rs, hs].astype(BF16),
                                                        v_ref[rs, hs].astype(BF16))

    assert dec == 8 and MEM_HEADS == 4
    q_head = lax.broadcasted_iota(jnp.int32, (MEM_HEADS * dec, k2_ref.shape[1]), 0) >> 3
    k_head = lax.broadcasted_iota(jnp.int32, (MEM_HEADS * dec, k2_ref.shape[1]), 1) & 3
    own_head = q_head == k_head
    for g in range(n_seq):
        rs = slice(g * dec, (g + 1) * dec)
        s = _dot_nt(q4_ref[g].astype(BF16), k2_ref[g].astype(BF16)) * (MEM_HEAD_DIM ** -0.5)
        s = jnp.where(own_head, s, -jnp.inf)
        e = jnp.exp(s - jnp.max(s, axis=-1, keepdims=True))
        p = e / jnp.sum(e, axis=-1, keepdims=True)
        om = _dot(p.astype(BF16), v2_ref[g].astype(BF16))
        for h in range(MEM_HEADS):
            om_ref[rs, h * MEM_HEAD_DIM:(h + 1) * MEM_HEAD_DIM] = om[h * dec:(h + 1) * dec]


def _sample_out_kernel(x_ref, c_ref, o_ref, om_ref, hgate_ref, gates_ref, post_g_ref, conv_b_ref,
                       ln_g_ref, ln_b_ref, hg_g_ref, w_conv_out_ref, w_hg_out_ref, w_mem_out_ref,
                       w_out_ref, ffn_pre_g_ref, ffn_post_g_ref, w_gate_ref, w_up_ref, w_down_ref, y_ref):
    c = _conv_ln_silu(c_ref[...], conv_b_ref[...], ln_g_ref[...], ln_b_ref[...])
    p_conv = _dot(c.astype(BF16), w_conv_out_ref[...])
    o = _head_rms(o_ref[...], hg_g_ref[...]) * hgate_ref[...]
    p_hg = _dot(o.astype(BF16), w_hg_out_ref[...])
    p_mem = _dot(om_ref[...].astype(BF16), w_mem_out_ref[...])
    x1 = _merge_out(lambda i: gates_ref[:, i * D_MODEL:(i + 1) * D_MODEL], x_ref[...],
                    (p_conv, p_hg, p_mem), w_out_ref, post_g_ref[...])
    y_ref[...] = _ffn_block(x1, ffn_pre_g_ref[...], ffn_post_g_ref[...], w_gate_ref, w_up_ref, w_down_ref)


def _row_spec(tile, width):
    return pl.BlockSpec((tile, width), lambda i: (i, 0))


def _sample_in(x2d, pre_g, lb_logits, w_in):
    rows = x2d.shape[0]
    tile = SAMPLE_TILE
    seqs = tile // SUBLANES
    wide = [D_CONV, D_HGRN, D_HGRN, D_HGRN, D_HGRN]
    return pl.pallas_call(
        _sample_in_kernel,
        grid=(rows // tile,),
        in_specs=[_row_spec(tile, D_MODEL), _const_spec((1, D_MODEL)), _const_spec((2, D_HGRN)),
                  _const_spec(w_in.shape)],
        out_specs=[_row_spec(tile, w) for w in wide] + [
            _row_spec(seqs, D_HGRN),
            pl.BlockSpec((seqs, MEM_HEADS * SUBLANES, MEM_HEAD_DIM), lambda i: (i, 0, 0)),
            _row_spec(tile, D_HGRN),
            _row_spec(tile, 3 * D_MODEL),
        ],
        out_shape=[jax.ShapeDtypeStruct((rows, w), F32) for w in wide] + [
            jax.ShapeDtypeStruct((rows // SUBLANES, D_HGRN), F32),
            jax.ShapeDtypeStruct((rows // SUBLANES, MEM_HEADS * SUBLANES, MEM_HEAD_DIM), F32),
            jax.ShapeDtypeStruct((rows, D_HGRN), F32),
            jax.ShapeDtypeStruct((rows, 3 * D_MODEL), F32),
        ],
        compiler_params=pltpu.CompilerParams(
            dimension_semantics=("parallel",), vmem_limit_bytes=VMEM_LIMIT),
        name="sample_in",
    )(x2d, pre_g, lb_logits, w_in)


def _sample_seq(u, qd, ki, ke, v, blast, q4, state_conv, state_hgrn, k2, v2, conv_w):
    n = state_conv.shape[0]
    g = SAMPLE_GROUP
    rows = g * SUBLANES
    kv_rows = k2.shape[1]
    seq_block = lambda *tail: pl.BlockSpec((g,) + tail, lambda i: (i,) + (0,) * len(tail))
    return pl.pallas_call(
        _sample_seq_kernel,
        grid=(n // g,),
        in_specs=[_row_spec(rows, D_CONV)] + [_row_spec(rows, D_HGRN)] * 4 + [
            _row_spec(g, D_HGRN),
            seq_block(MEM_HEADS * SUBLANES, MEM_HEAD_DIM),
            seq_block(CONV_PREFIX, D_CONV),
            seq_block(HG_HEADS, HG_DK, HG_DV),
            seq_block(kv_rows, MEM_HEAD_DIM),
            seq_block(kv_rows, MEM_HEAD_DIM),
            _const_spec((CONV_WIDTH, D_CONV)),
        ],
        out_specs=[_row_spec(rows, D_CONV), _row_spec(rows, D_HGRN), _row_spec(rows, D_MEM),
                   seq_block(CONV_PREFIX, D_CONV), seq_block(HG_HEADS, HG_DK, HG_DV)],
        out_shape=[
            jax.ShapeDtypeStruct((n * SUBLANES, D_CONV), F32),
            jax.ShapeDtypeStruct((n * SUBLANES, D_HGRN), F32),
            jax.ShapeDtypeStruct((n * SUBLANES, D_MEM), F32),
            jax.ShapeDtypeStruct((n, CONV_PREFIX, D_CONV), F32),
            jax.ShapeDtypeStruct((n, HG_HEADS, HG_DK, HG_DV), F32),
        ],
        scratch_shapes=[pltpu.VMEM((g, CONV_PREFIX + SUBLANES + 2, D_CONV), F32)],
        compiler_params=pltpu.CompilerParams(
            dimension_semantics=("parallel",), vmem_limit_bytes=VMEM_LIMIT),
        name="sample_seq",
    )(u, qd, ki, ke, v, blast, q4, state_conv, state_hgrn, k2, v2, conv_w)


def _sample_out(x2d, c, o, om, hgate, gates, vectors, weights, ffn_vectors, ffn_weights):
    rows = x2d.shape[0]
    tile = SAMPLE_TILE
    acts = (x2d, c, o, om, hgate, gates)
    consts = tuple(vectors) + tuple(weights) + tuple(ffn_vectors) + tuple(ffn_weights)
    return pl.pallas_call(
        _sample_out_kernel,
        grid=(rows // tile,),
        in_specs=[_row_spec(tile, a.shape[1]) for a in acts] + [_const_spec(a.shape) for a in consts],
        out_specs=_row_spec(tile, D_MODEL),
        out_shape=jax.ShapeDtypeStruct((rows, D_MODEL), F32),
        compiler_params=pltpu.CompilerParams(
            dimension_semantics=("parallel",), vmem_limit_bytes=VMEM_LIMIT),
        name="sample_out",
    )(*acts, *consts)


def _ffn_kernel(x_ref, pre_g_ref, post_g_ref, w_gate_ref, w_up_ref, w_down_ref, y_ref):
    y_ref[...] = _ffn_block(x_ref[...], pre_g_ref[...], post_g_ref[...], w_gate_ref, w_up_ref, w_down_ref)


def _ffn(x2d, pre_g, post_g, w_gate, w_up, w_down):
    rows = x2d.shape[0]
    tile = min(FFN_TILE, rows)
    return pl.pallas_call(
        _ffn_kernel,
        grid=(rows // tile,),
        in_specs=[
            pl.BlockSpec((tile, D_MODEL), lambda i: (i, 0)),
            _const_spec((1, D_MODEL)), _const_spec((1, D_MODEL)),
            _const_spec(w_gate.shape), _const_spec(w_up.shape), _const_spec(w_down.shape),
        ],
        out_specs=pl.BlockSpec((tile, D_MODEL), lambda i: (i, 0)),
        out_shape=jax.ShapeDtypeStruct((rows, D_MODEL), F32),
        compiler_params=pltpu.CompilerParams(
            dimension_semantics=("parallel",), vmem_limit_bytes=VMEM_LIMIT),
        name="ffn",
    )(x2d, pre_g, post_g, w_gate, w_up, w_down)


def kernel(x_prompt, x_sample, mem_prompt, state_conv, state_hgrn, cache_mem_k, cache_mem_v, norm_pre_mix, norm_post_mix, norm_pre_ffn, norm_post_ffn, w_in, conv_w, conv_b, conv_ln_g, conv_ln_b, w_conv_out, hg_lb_logits, hg_norm_g, w_hg_out, mem_norm_g, w_mem_kv, w_mem_out, w_out, w_ffn_gate, w_ffn_up, w_ffn_down):
    depth = w_in.shape[0]
    assert depth == 1 and hg_lb_logits.shape[0] == 2, "single-layer step"
    batch, seq, _ = x_prompt.shape
    n_dec, dec, _ = x_sample.shape
    n_mem = mem_prompt.shape[1]
    assert seq % PROMPT_TILE == 0 and n_dec % SAMPLE_GROUP == 0 and dec == SUBLANES
    assert (n_dec * dec) % SAMPLE_TILE == 0

    vectors = (norm_pre_mix, norm_post_mix, conv_w[0], conv_b, conv_ln_g, conv_ln_b,
               hg_lb_logits, jnp.tile(hg_norm_g, (1, HG_HEADS)))
    weights = tuple(w[0].astype(BF16) for w in (w_in, w_conv_out, w_hg_out, w_mem_out, w_out))
    ffn_weights = tuple(w[0].astype(BF16) for w in (w_ffn_gate, w_ffn_up, w_ffn_down))

    mk, mv = _memory_kv(mem_prompt.reshape(batch * n_mem, D_MODEL), mem_norm_g,
                        w_mem_kv[0].astype(BF16), n_mem)
    mk = mk.reshape(batch, n_mem, D_MEM)
    mv = mv.reshape(batch, n_mem, D_MEM)

    xp, conv_p, hg_p = _prompt_mixer(x_prompt, mk, mv, vectors, weights)
    yp = _ffn(xp.reshape(batch * seq, D_MODEL), norm_pre_ffn, norm_post_ffn, *ffn_weights)

    xs2d = x_sample.reshape(n_dec * dec, D_MODEL)
    u, qd, ki, ke, v, blast, q4, hgate, gates = _sample_in(xs2d, norm_pre_mix, hg_lb_logits, weights[0])
    k2 = cache_mem_k[0].reshape(n_dec, n_mem * MEM_HEADS, MEM_HEAD_DIM)
    v2 = cache_mem_v[0].reshape(n_dec, n_mem * MEM_HEADS, MEM_HEAD_DIM)
    c, o, om, conv_s, hg_s = _sample_seq(u, qd, ki, ke, v, blast, q4, state_conv[0], state_hgrn[0],
                                         k2, v2, conv_w[0])
    ys = _sample_out(xs2d, c, o, om, hgate, gates,
                     (norm_post_mix, conv_b, conv_ln_g, conv_ln_b, vectors[-1]), weights[1:],
                     (norm_pre_ffn, norm_post_ffn), ffn_weights)

    kv_shape = (1, batch, n_mem, MEM_HEADS, MEM_HEAD_DIM)
    return (yp.reshape(batch, seq, D_MODEL), ys.reshape(n_dec, dec, D_MODEL),
            conv_p[None], hg_p[None], mk.reshape(kv_shape), mv.reshape(kv_shape),
            conv_s[None], hg_s[None])
```

```python
from typing import NamedTuple

import jax
import jax.numpy as jnp
from jax import lax
from jax.experimental import pallas as pl
from jax.experimental.pallas import tpu as pltpu

D_MODEL = 1024
D_CONV = 512
CONV_WIDTH = 31
CONV_PREFIX = CONV_WIDTH - 1
HG_HEADS = 4
HG_DK = 128
HG_DV = 128
D_HGRN = HG_HEADS * HG_DK
MEM_HEADS = 4
MEM_HEAD_DIM = 128
D_MEM = MEM_HEADS * MEM_HEAD_DIM
CHUNK = 32
EPS = 1e-6
NEG_LOG2E = -1.4426950408889634

COL_CA = 0
COL_CB = COL_CA + D_CONV
COL_HQ = COL_CB + D_CONV
COL_HF = COL_HQ + D_HGRN
COL_HI = COL_HF + D_HGRN
COL_HGATE = COL_HI + D_HGRN
COL_MQ = COL_HGATE + D_HGRN
COL_GATES = COL_MQ + D_MEM

SUBLANES = 8
LANES = 128
VMEM_LIMIT = 56 * 1024 * 1024

PROMPT_TILE = 256
PROMPT_STEP_TILES = 1
SAMPLE_GROUP = 8
SAMPLE_TILE = 256
FFN_TILE = 512
CONV_ROWS = 32

BF16 = jnp.bfloat16
F32 = jnp.float32


def _dot(a, b):
    return jnp.dot(a, b, preferred_element_type=F32)


def _dot_nt(a, b):
    return lax.dot_general(a, b, (((1,), (1,)), ((), ())), preferred_element_type=F32)


def _dot_tn(a, b):
    return lax.dot_general(a, b, (((0,), (0,)), ((), ())), preferred_element_type=F32)


def _rms(x, gain):
    return x * lax.rsqrt(jnp.mean(x * x, axis=-1, keepdims=True) + EPS) * gain


def _sigmoid(x):
    return 1.0 / (1.0 + jnp.exp2(x * NEG_LOG2E))


def _silu(x):
    return x * _sigmoid(x)


def _forget_terms(hf, lb):
    sig = _sigmoid(hf)
    return jnp.log(lb + (1.0 - lb) * sig), (1.0 - lb) * (1.0 - sig)


def _lower_bound(lb_logits):
    m = jnp.max(lb_logits, axis=0, keepdims=True)
    e = jnp.exp(lb_logits - m)
    return e[0:1] / jnp.sum(e, axis=0, keepdims=True)


def _segment_cumsum(x, seg):
    pos = lax.broadcasted_iota(jnp.int32, x.shape, 0) & (seg - 1)
    s = 1
    while s < seg:
        x = x + jnp.where(pos >= s, pltpu.roll(x, s, axis=0), 0.0)
        s *= 2
    return x


def _conv_ln_silu(c, conv_b, ln_g, ln_b):
    c = c + conv_b
    mu = jnp.mean(c, axis=-1, keepdims=True)
    d = c - mu
    var = jnp.mean(d * d, axis=-1, keepdims=True)
    return _silu(d * lax.rsqrt(var + EPS) * ln_g + ln_b)


def _hgrn_gates(hq, hf, lb, seg):
    q = _silu(hq)
    logf, k = _forget_terms(hf, lb)
    b = _segment_cumsum(logf, seg)
    rows = b.shape[0]
    b3 = b.reshape(rows // seg, seg, D_HGRN)
    b_last3 = b3[:, seg - 1:seg, :]
    rest = (b_last3 - b3).reshape(rows, D_HGRN)
    q_dec = q * jnp.exp(b)
    k_inv = k * jnp.exp(-b)
    k_end = k * jnp.exp(rest)
    return q_dec, k_inv, k_end, b_last3.reshape(rows // seg, D_HGRN)


class _TileLevels(NamedTuple):
    q_mid: jax.Array
    k_mid: jax.Array
    q_64: jax.Array
    k_64: jax.Array
    q_128: jax.Array
    k_128: jax.Array
    q_tile: jax.Array
    k_tile: jax.Array
    decay: jax.Array


def _hgrn_tile_levels(hq, hf, lb):
    tl = hq.shape[0]
    n_blocks = 4
    block = 2 * CHUNK
    assert tl == n_blocks * block
    q = _silu(hq)
    logf, k = _forget_terms(hf, lb)
    b = _segment_cumsum(logf, CHUNK)
    chunks = [b[n * CHUNK:(n + 1) * CHUNK] for n in range(2 * n_blocks)]
    totals = [c[CHUNK - 1:CHUNK] for c in chunks]
    e = jnp.concatenate([c - totals[n] if n % 2 == 0 else c for n, c in enumerate(chunks)], axis=0)
    q_mid = q * jnp.exp(e)
    k_mid = k * jnp.exp(-e)
    first = [totals[2 * j] for j in range(n_blocks)]
    second = [totals[2 * j + 1] for j in range(n_blocks)]
    both = [first[j] + second[j] for j in range(n_blocks)]

    def scaled(x, blocks, log_scales):
        return jnp.concatenate(
            [x[j * block:(j + 1) * block] * jnp.exp(s) for j, s in zip(blocks, log_scales)], axis=0)

    every = range(n_blocks)
    return _TileLevels(
        q_mid=q_mid, k_mid=k_mid,
        q_64=scaled(q_mid, every, first),
        k_64=scaled(k_mid, every, second),
        q_128=scaled(q_mid, (2, 3), (first[2], first[3] + both[2])),
        k_128=scaled(k_mid, (0, 1), (second[0] + both[1], second[1])),
        q_tile=scaled(q_mid, every, [first[j] + sum(both[:j], 0.0) for j in every]),
        k_tile=scaled(k_mid, every, [second[j] + sum(both[j + 1:], 0.0) for j in every]),
        decay=jnp.exp(sum(both[1:], both[0])))


def _head_rms(o, gain4):
    parts = []
    for h in range(HG_HEADS):
        oh = o[:, h * HG_DV:(h + 1) * HG_DV]
        parts.append(oh * lax.rsqrt(jnp.mean(oh * oh, axis=-1, keepdims=True) + EPS))
    return jnp.concatenate(parts, axis=-1) * gain4


def _branch_gate(hb, w_in_ref, i):
    c0 = COL_GATES + i * D_MODEL
    return _sigmoid(_dot(hb, w_in_ref[:, c0:c0 + D_MODEL]))


def _merge_out(gate, x, branches, w_out_ref, post_g):
    acc = None
    for i, p in enumerate(branches):
        acc = gate(i) * p if acc is None else acc + gate(i) * p
    m = _dot(acc.astype(BF16), w_out_ref[...])
    return x + _rms(m, post_g)


def _ffn_block(x, pre_g, post_g, w_gate_ref, w_up_ref, w_down_ref):
    hb = _rms(x, pre_g).astype(BF16)
    f = _silu(_dot(hb, w_gate_ref[...])) * _dot(hb, w_up_ref[...])
    d = _dot(f.astype(BF16), w_down_ref[...])
    return x + _rms(d, post_g)


def _memory_kv_kernel(mem_ref, g_ref, w_ref, k_rows_ref, v_rows_ref, k_ref, v_ref):
    m = _rms(mem_ref[...], g_ref[...]).astype(BF16)
    tokens = mem_ref.shape[0]
    for full, rows_ref, op_ref in ((_dot(m, w_ref[:, :D_MEM]), k_rows_ref, k_ref),
                                   (_dot(m, w_ref[:, D_MEM:]), v_rows_ref, v_ref)):
        op_ref[...] = full.astype(BF16)
        for h in range(MEM_HEADS):
            rows_ref[pl.ds(h, tokens, stride=MEM_HEADS), :] = full[:, h * MEM_HEAD_DIM:(h + 1) * MEM_HEAD_DIM]


def _memory_kv(mem2d, gain, w_bf16, tile):
    rows = mem2d.shape[0]
    return pl.pallas_call(
        _memory_kv_kernel,
        grid=(rows // tile,),
        in_specs=[
            pl.BlockSpec((tile, D_MODEL), lambda i: (i, 0)),
            pl.BlockSpec((1, D_MODEL), lambda i: (0, 0)),
            pl.BlockSpec((D_MODEL, 2 * D_MEM), lambda i: (0, 0)),
        ],
        out_specs=[pl.BlockSpec((tile * MEM_HEADS, MEM_HEAD_DIM), lambda i: (i, 0))] * 2
        + [pl.BlockSpec((tile, D_MEM), lambda i: (i, 0))] * 2,
        out_shape=[jax.ShapeDtypeStruct((rows * MEM_HEADS, MEM_HEAD_DIM), F32)] * 2
        + [jax.ShapeDtypeStruct((rows, D_MEM), BF16)] * 2,
        compiler_params=pltpu.CompilerParams(dimension_semantics=("parallel",)),
        name="memory_kv",
    )(mem2d, gain, w_bf16)


def _prompt_mixer_kernel(x_ref, kb_ref, vb_ref, pre_g_ref, post_g_ref, conv_w_ref, conv_b_ref,
                         ln_g_ref, ln_b_ref, lb_logits_ref, hg_g_ref, w_in_ref, w_conv_out_ref,
                         w_hg_out_ref, w_mem_out_ref, w_out_ref,
                         y_ref, new_conv_ref, new_hg_ref,
                         shift_ref, tail_ref, state_t_ref):
    t = pl.program_id(1)
    last_t = pl.num_programs(1) - 1
    tl = PROMPT_TILE
    n_tiles = x_ref.shape[0] // tl

    @pl.when(t == 0)
    def _():
        tail_ref[...] = jnp.zeros_like(tail_ref)
        state_t_ref[...] = jnp.zeros_like(state_t_ref)

    lb = _lower_bound(lb_logits_ref[...])
    half = tl // 2
    row = lax.broadcasted_iota(jnp.int32, (tl, tl), 0)
    col = lax.broadcasted_iota(jnp.int32, (tl, tl), 1)
    span = row ^ col
    near = (span < 2 * CHUNK) & (col <= row)
    mid = (span < 4 * CHUNK) & (col < row)

    def tile(i, u_tail):
        rows = slice(i * tl, (i + 1) * tl)
        shift = shift_ref.at[i]
        x = x_ref[rows, :]
        hb = _rms(x, pre_g_ref[...]).astype(BF16)

        def proj(c0, c1):
            return _dot(hb, w_in_ref[:, c0:c1])

        u = proj(COL_CA, COL_CB) * _sigmoid(proj(COL_CB, COL_HQ))
        shift[0, 0:CONV_ROWS, :] = u_tail
        shift[0, CONV_ROWS:CONV_ROWS + tl, :] = u
        for r in range(1, SUBLANES):
            shift[r, 0:tl + CONV_ROWS - SUBLANES, :] = shift[0, r:r + tl + CONV_ROWS - SUBLANES, :]

        conv_blocks = []
        for base in range(0, tl, CONV_ROWS):
            acc = None
            for j in range(CONV_WIDTH):
                s = j + (CONV_ROWS - CONV_PREFIX)
                r, a = s % SUBLANES, s // SUBLANES
                lo = base + a * SUBLANES
                term = shift[r, lo:lo + CONV_ROWS, :] * conv_w_ref[j:j + 1, :]
                acc = term if acc is None else acc + term
            conv_blocks.append(acc)
        c = _conv_ln_silu(jnp.concatenate(conv_blocks, axis=0),
                          conv_b_ref[...], ln_g_ref[...], ln_b_ref[...]).astype(BF16)

        lv = _hgrn_tile_levels(proj(COL_HQ, COL_HF), proj(COL_HF, COL_HI), lb)
        lvb = _TileLevels(*[a.astype(BF16) for a in lv[:-1]], lv.decay)
        vb16 = proj(COL_HI, COL_HGATE).astype(BF16)
        hgate = _silu(proj(COL_HGATE, COL_MQ))
        mq = proj(COL_MQ, COL_GATES).astype(BF16)
        gates = [_branch_gate(hb, w_in_ref, g) for g in range(3)]

        om = []
        for h in range(MEM_HEADS):
            hs = slice(h * MEM_HEAD_DIM, (h + 1) * MEM_HEAD_DIM)
            s = _dot_nt(mq[:, hs], kb_ref[:, hs]) * (MEM_HEAD_DIM ** -0.5)
            e = jnp.exp(s - jnp.max(s, axis=-1, keepdims=True))
            p = e / jnp.sum(e, axis=-1, keepdims=True)
            om.append(_dot(p.astype(BF16), vb_ref[:, hs]))

        o_heads = []
        for h in range(HG_HEADS):
            hs = slice(h * HG_DK, (h + 1) * HG_DK)
            vh = vb16[:, hs]
            s_near = _dot_nt(lvb.q_mid[:, hs], lvb.k_mid[:, hs])
            s_mid = _dot_nt(lvb.q_64[:, hs], lvb.k_64[:, hs])
            s_far = _dot_nt(lvb.q_128[:, hs], lvb.k_128[:, hs])
            base = jnp.where(near, s_near, jnp.where(mid, s_mid, 0.0))
            scores = jnp.concatenate(
                [base[:half], jnp.concatenate([s_far, base[half:, half:]], axis=1)], axis=0)
            st = state_t_ref[h]
            o_heads.append(_dot(scores.astype(BF16), vh) + _dot_nt(lvb.q_tile[:, hs], st.astype(BF16)))
            state_t_ref[h] = st * lvb.decay[:, hs] + _dot_tn(vh, lvb.k_tile[:, hs])
        o = _head_rms(jnp.concatenate(o_heads, axis=-1), hg_g_ref[...]) * hgate

        p_conv = _dot(c, w_conv_out_ref[...])
        p_mem = _dot(jnp.concatenate(om, axis=-1).astype(BF16), w_mem_out_ref[...])
        p_hg = _dot(o.astype(BF16), w_hg_out_ref[...])
        y_ref[rows, :] = _merge_out(lambda g: gates[g], x, (p_conv, p_hg, p_mem), w_out_ref,
                                    post_g_ref[...])

        return u[tl - CONV_ROWS:]

    u_tail = tail_ref[...]
    for i in range(n_tiles):
        u_tail = tile(i, u_tail)
    tail_ref[...] = u_tail

    @pl.when(t == last_t)
    def _():
        new_conv_ref[...] = tail_ref[CONV_ROWS - CONV_PREFIX:CONV_ROWS, :]
        for h in range(HG_HEADS):
            new_hg_ref[h] = state_t_ref[h].T


def _const_spec(shape):
    zeros = (0,) * len(shape)
    return pl.BlockSpec(shape, lambda *_: zeros, pipeline_mode=pl.Buffered(1))


def _mixer_weight_specs(w_in, w_conv_out, w_hg_out, w_mem_out, w_out):
    return [_const_spec(w.shape) for w in (w_in, w_conv_out, w_hg_out, w_mem_out, w_out)]


def _vector_specs():
    return [
        _const_spec((1, D_MODEL)), _const_spec((1, D_MODEL)),
        _const_spec((CONV_WIDTH, D_CONV)), _const_spec((1, D_CONV)),
        _const_spec((1, D_CONV)), _const_spec((1, D_CONV)),
        _const_spec((2, D_HGRN)), _const_spec((1, D_HGRN)),
    ]


def _prompt_mixer(x, mk, mv, vectors, weights):
    batch, seq, _ = x.shape
    n_mem = mk.shape[1]
    tl = PROMPT_TILE * PROMPT_STEP_TILES
    return pl.pallas_call(
        _prompt_mixer_kernel,
        grid=(batch, seq // tl),
        in_specs=[
            pl.BlockSpec((None, tl, D_MODEL), lambda b, t: (b, t, 0)),
            pl.BlockSpec((None, n_mem, D_MEM), lambda b, t: (b, 0, 0)),
            pl.BlockSpec((None, n_mem, D_MEM), lambda b, t: (b, 0, 0)),
        ] + _vector_specs() + _mixer_weight_specs(*weights),
        out_specs=[
            pl.BlockSpec((None, tl, D_MODEL), lambda b, t: (b, t, 0)),
            pl.BlockSpec((None, CONV_PREFIX, D_CONV), lambda b, t: (b, 0, 0)),
            pl.BlockSpec((None, HG_HEADS, HG_DK, HG_DV), lambda b, t: (b, 0, 0, 0)),
        ],
        out_shape=[
            jax.ShapeDtypeStruct((batch, seq, D_MODEL), F32),
            jax.ShapeDtypeStruct((batch, CONV_PREFIX, D_CONV), F32),
            jax.ShapeDtypeStruct((batch, HG_HEADS, HG_DK, HG_DV), F32),
        ],
        scratch_shapes=[
            pltpu.VMEM((PROMPT_STEP_TILES, SUBLANES, PROMPT_TILE + CONV_ROWS, D_CONV), F32),
            pltpu.VMEM((CONV_ROWS, D_CONV), F32),
            pltpu.VMEM((HG_HEADS, HG_DV, HG_DK), F32),
        ],
        compiler_params=pltpu.CompilerParams(
            dimension_semantics=("parallel", "arbitrary"), vmem_limit_bytes=VMEM_LIMIT),
        name="prompt_mixer",
    )(x, mk, mv, *vectors, *weights)


def _store_lane_chunks(ref, x):
    for i in range(ref.shape[0]):
        ref[i] = x[:, i * LANES:(i + 1) * LANES]


def _load_lane_chunks(ref, rows=slice(None)):
    return jnp.concatenate([ref[i, rows, :] for i in range(ref.shape[0])], axis=1)


def _lane_chunk_spec(tile, width):
    return pl.BlockSpec((width // LANES, tile, LANES), lambda i: (0, i, 0))


def _sample_in_kernel(x_ref, pre_g_ref, lb_logits_ref, w_in_ref,
                      u_ref, qd_ref, ki_ref, ke_ref, v_ref, blast_ref, q4_ref, hgate_ref, gates_ref):
    n_seq = x_ref.shape[0] // SUBLANES
    hb = _rms(x_ref[...], pre_g_ref[...]).astype(BF16)
    _store_lane_chunks(
        u_ref, _dot(hb, w_in_ref[:, COL_CA:COL_CB]) * _sigmoid(_dot(hb, w_in_ref[:, COL_CB:COL_HQ])))
    lb = _lower_bound(lb_logits_ref[...])
    q_dec, k_inv, k_end, b_last = _hgrn_gates(
        _dot(hb, w_in_ref[:, COL_HQ:COL_HF]), _dot(hb, w_in_ref[:, COL_HF:COL_HI]), lb, SUBLANES)
    qd_ref[...] = q_dec
    ki_ref[...] = k_inv
    ke_ref[...] = k_end
    blast_ref[...] = b_last
    v_ref[...] = _dot(hb, w_in_ref[:, COL_HI:COL_HGATE])
    hgate_ref[...] = _silu(_dot(hb, w_in_ref[:, COL_HGATE:COL_MQ]))
    mq = _dot(hb, w_in_ref[:, COL_MQ:COL_GATES])
    for h in range(MEM_HEADS):
        q4_ref[:, h * SUBLANES:(h + 1) * SUBLANES, :] = (
            mq[:, h * MEM_HEAD_DIM:(h + 1) * MEM_HEAD_DIM].reshape(n_seq, SUBLANES, MEM_HEAD_DIM))
    for i in range(3):
        gates_ref[:, i * D_MODEL:(i + 1) * D_MODEL] = _branch_gate(hb, w_in_ref, i)


def _sample_seq_kernel(u_ref, qd_ref, ki_ref, ke_ref, v_ref, blast_ref, q4_ref, sc_ref, sh_ref,
                       k2_ref, v2_ref, conv_w_ref,
                       c_ref, o_ref, om_ref, new_conv_ref, new_hg_ref):
    n_seq, dec = sc_ref.shape[1], SUBLANES
    rows = n_seq * dec

    assert dec == 8 and MEM_HEADS == 4
    q_head = lax.broadcasted_iota(jnp.int32, (MEM_HEADS * dec, k2_ref.shape[1]), 0) >> 3
    k_head = lax.broadcasted_iota(jnp.int32, (MEM_HEADS * dec, k2_ref.shape[1]), 1) & 3
    own_head = q_head == k_head
    om = []
    for g in range(n_seq):
        s = _dot_nt(q4_ref[g].astype(BF16), k2_ref[g].astype(BF16)) * (MEM_HEAD_DIM ** -0.5)
        s = jnp.where(own_head, s, -jnp.inf)
        e = jnp.exp(s - jnp.max(s, axis=-1, keepdims=True))
        p = e / jnp.sum(e, axis=-1, keepdims=True)
        om_g = _dot(p.astype(BF16), v2_ref[g].astype(BF16))
        om.extend(om_g[h * dec:(h + 1) * dec] for h in range(MEM_HEADS))

    u_slabs = [_load_lane_chunks(u_ref, pl.ds(step, n_seq, stride=dec)) for step in range(dec)]
    conv = [None] * dec
    for s in range(CONV_PREFIX + dec):
        slab = sc_ref[s] if s < CONV_PREFIX else u_slabs[s - CONV_PREFIX]
        if s >= dec:
            new_conv_ref[s - dec] = slab
        for step in range(max(0, s - CONV_PREFIX), min(dec, s + 1)):
            term = slab * conv_w_ref[s - step:s - step + 1, :]
            conv[step] = term if conv[step] is None else conv[step] + term

    row = lax.broadcasted_iota(jnp.int32, (rows, rows), 0)
    col = lax.broadcasted_iota(jnp.int32, (rows, rows), 1)
    causal = ((row ^ col) < dec) & (col <= row)
    decay = jnp.exp(blast_ref[...])
    for h in range(HG_HEADS):
        hs = slice(h * HG_DK, (h + 1) * HG_DK)
        scores = jnp.where(causal, _dot_nt(qd_ref[:, hs].astype(BF16), ki_ref[:, hs].astype(BF16)), 0.0)
        o_intra = _dot(scores.astype(BF16), v_ref[:, hs].astype(BF16))
        for g in range(n_seq):
            rs = slice(g * dec, (g + 1) * dec)
            s0 = sh_ref[g, h]
            o_ref[rs, hs] = o_intra[rs] + _dot(qd_ref[rs, hs].astype(BF16), s0.astype(BF16))
            decay_col = jnp.broadcast_to(decay[g:g + 1, hs], (HG_DV, HG_DK)).T
            new_hg_ref[g, h] = decay_col * s0 + _dot_tn(ke_ref[rs, hs].astype(BF16),
                                                        v_ref[rs, hs].astype(BF16))

    for step in range(dec):
        for i in range(c_ref.shape[0]):
            c_ref[i, pl.ds(step, n_seq, stride=dec), :] = conv[step][:, i * LANES:(i + 1) * LANES]
    for g in range(n_seq):
        om_ref[g * dec:(g + 1) * dec, :] = jnp.concatenate(om[g * MEM_HEADS:(g + 1) * MEM_HEADS], axis=1)


def _sample_out_kernel(x_ref, c_ref, o_ref, om_ref, hgate_ref, gates_ref, post_g_ref, conv_b_ref,
                       ln_g_ref, ln_b_ref, hg_g_ref, w_conv_out_ref, w_hg_out_ref, w_mem_out_ref,
                       w_out_ref, ffn_pre_g_ref, ffn_post_g_ref, w_gate_ref, w_up_ref, w_down_ref, y_ref):
    c = _conv_ln_silu(_load_lane_chunks(c_ref), conv_b_ref[...], ln_g_ref[...], ln_b_ref[...])
    p_conv = _dot(c.astype(BF16), w_conv_out_ref[...])
    o = _head_rms(o_ref[...], hg_g_ref[...]) * hgate_ref[...]
    p_hg = _dot(o.astype(BF16), w_hg_out_ref[...])
    p_mem = _dot(om_ref[...].astype(BF16), w_mem_out_ref[...])
    x1 = _merge_out(lambda i: gates_ref[:, i * D_MODEL:(i + 1) * D_MODEL], x_ref[...],
                    (p_conv, p_hg, p_mem), w_out_ref, post_g_ref[...])
    y_ref[...] = _ffn_block(x1, ffn_pre_g_ref[...], ffn_post_g_ref[...], w_gate_ref, w_up_ref, w_down_ref)


def _row_spec(tile, width):
    return pl.BlockSpec((tile, width), lambda i: (i, 0))


def _sample_in(x2d, pre_g, lb_logits, w_in):
    rows = x2d.shape[0]
    tile = SAMPLE_TILE
    seqs = tile // SUBLANES
    wide = [D_HGRN, D_HGRN, D_HGRN, D_HGRN]
    return pl.pallas_call(
        _sample_in_kernel,
        grid=(rows // tile,),
        in_specs=[_row_spec(tile, D_MODEL), _const_spec((1, D_MODEL)), _const_spec((2, D_HGRN)),
                  _const_spec(w_in.shape)],
        out_specs=[_lane_chunk_spec(tile, D_CONV)] + [_row_spec(tile, w) for w in wide] + [
            _row_spec(seqs, D_HGRN),
            pl.BlockSpec((seqs, MEM_HEADS * SUBLANES, MEM_HEAD_DIM), lambda i: (i, 0, 0)),
            _row_spec(tile, D_HGRN),
            _row_spec(tile, 3 * D_MODEL),
        ],
        out_shape=[jax.ShapeDtypeStruct((D_CONV // LANES, rows, LANES), F32)]
        + [jax.ShapeDtypeStruct((rows, w), F32) for w in wide] + [
            jax.ShapeDtypeStruct((rows // SUBLANES, D_HGRN), F32),
            jax.ShapeDtypeStruct((rows // SUBLANES, MEM_HEADS * SUBLANES, MEM_HEAD_DIM), F32),
            jax.ShapeDtypeStruct((rows, D_HGRN), F32),
            jax.ShapeDtypeStruct((rows, 3 * D_MODEL), F32),
        ],
        compiler_params=pltpu.CompilerParams(
            dimension_semantics=("parallel",), vmem_limit_bytes=VMEM_LIMIT),
        name="sample_in",
    )(x2d, pre_g, lb_logits, w_in)


def _sample_seq(u, qd, ki, ke, v, blast, q4, conv_rows, state_hgrn, k2, v2, conv_w):
    n = conv_rows.shape[1]
    g = SAMPLE_GROUP
    rows = g * SUBLANES
    kv_rows = k2.shape[1]
    seq_block = lambda *tail: pl.BlockSpec((g,) + tail, lambda i: (i,) + (0,) * len(tail))
    conv_block = pl.BlockSpec((CONV_PREFIX, g, D_CONV), lambda i: (0, i, 0))
    return pl.pallas_call(
        _sample_seq_kernel,
        grid=(n // g,),
        in_specs=[_lane_chunk_spec(rows, D_CONV)] + [_row_spec(rows, D_HGRN)] * 4 + [
            _row_spec(g, D_HGRN),
            seq_block(MEM_HEADS * SUBLANES, MEM_HEAD_DIM),
            conv_block,
            seq_block(HG_HEADS, HG_DK, HG_DV),
            seq_block(kv_rows, MEM_HEAD_DIM),
            seq_block(kv_rows, MEM_HEAD_DIM),
            _const_spec((CONV_WIDTH, D_CONV)),
        ],
        out_specs=[_lane_chunk_spec(rows, D_CONV), _row_spec(rows, D_HGRN), _row_spec(rows, D_MEM),
                   conv_block, seq_block(HG_HEADS, HG_DK, HG_DV)],
        out_shape=[
            jax.ShapeDtypeStruct((D_CONV // LANES, n * SUBLANES, LANES), F32),
            jax.ShapeDtypeStruct((n * SUBLANES, D_HGRN), F32),
            jax.ShapeDtypeStruct((n * SUBLANES, D_MEM), F32),
            jax.ShapeDtypeStruct((CONV_PREFIX, n, D_CONV), F32),
            jax.ShapeDtypeStruct((n, HG_HEADS, HG_DK, HG_DV), F32),
        ],
        compiler_params=pltpu.CompilerParams(
            dimension_semantics=("parallel",), vmem_limit_bytes=VMEM_LIMIT),
        name="sample_seq",
    )(u, qd, ki, ke, v, blast, q4, conv_rows, state_hgrn, k2, v2, conv_w)


def _sample_out(x2d, c, o, om, hgate, gates, vectors, weights, ffn_vectors, ffn_weights):
    rows = x2d.shape[0]
    tile = SAMPLE_TILE
    acts = (x2d, c, o, om, hgate, gates)
    consts = tuple(vectors) + tuple(weights) + tuple(ffn_vectors) + tuple(ffn_weights)
    return pl.pallas_call(
        _sample_out_kernel,
        grid=(rows // tile,),
        in_specs=[_lane_chunk_spec(tile, D_CONV) if a is c else _row_spec(tile, a.shape[1]) for a in acts]
        + [_const_spec(a.shape) for a in consts],
        out_specs=_row_spec(tile, D_MODEL),
        out_shape=jax.ShapeDtypeStruct((rows, D_MODEL), F32),
        compiler_params=pltpu.CompilerParams(
            dimension_semantics=("parallel",), vmem_limit_bytes=VMEM_LIMIT),
        name="sample_out",
    )(*acts, *consts)


def _ffn_kernel(x_ref, pre_g_ref, post_g_ref, w_gate_ref, w_up_ref, w_down_ref, y_ref):
    y_ref[...] = _ffn_block(x_ref[...], pre_g_ref[...], post_g_ref[...], w_gate_ref, w_up_ref, w_down_ref)


def _ffn(x2d, pre_g, post_g, w_gate, w_up, w_down):
    rows = x2d.shape[0]
    tile = min(FFN_TILE, rows)
    return pl.pallas_call(
        _ffn_kernel,
        grid=(rows // tile,),
        in_specs=[
            pl.BlockSpec((tile, D_MODEL), lambda i: (i, 0)),
            _const_spec((1, D_MODEL)), _const_spec((1, D_MODEL)),
            _const_spec(w_gate.shape), _const_spec(w_up.shape), _const_spec(w_down.shape),
        ],
        out_specs=pl.BlockSpec((tile, D_MODEL), lambda i: (i, 0)),
        out_shape=jax.ShapeDtypeStruct((rows, D_MODEL), F32),
        compiler_params=pltpu.CompilerParams(
            dimension_semantics=("parallel",), vmem_limit_bytes=VMEM_LIMIT),
        name="ffn",
    )(x2d, pre_g, post_g, w_gate, w_up, w_down)


def kernel(x_prompt, x_sample, mem_prompt, state_conv, state_hgrn, cache_mem_k, cache_mem_v, norm_pre_mix, norm_post_mix, norm_pre_ffn, norm_post_ffn, w_in, conv_w, conv_b, conv_ln_g, conv_ln_b, w_conv_out, hg_lb_logits, hg_norm_g, w_hg_out, mem_norm_g, w_mem_kv, w_mem_out, w_out, w_ffn_gate, w_ffn_up, w_ffn_down):
    depth = w_in.shape[0]
    assert depth == 1 and hg_lb_logits.shape[0] == 2, "single-layer step"
    batch, seq, _ = x_prompt.shape
    n_dec, dec, _ = x_sample.shape
    n_mem = mem_prompt.shape[1]
    assert seq % PROMPT_TILE == 0 and n_dec % SAMPLE_GROUP == 0 and dec == SUBLANES
    assert (n_dec * dec) % SAMPLE_TILE == 0

    vectors = (norm_pre_mix, norm_post_mix, conv_w[0], conv_b, conv_ln_g, conv_ln_b,
               hg_lb_logits, jnp.tile(hg_norm_g, (1, HG_HEADS)))
    weights = tuple(w[0].astype(BF16) for w in (w_in, w_conv_out, w_hg_out, w_mem_out, w_out))
    ffn_weights = tuple(w[0].astype(BF16) for w in (w_ffn_gate, w_ffn_up, w_ffn_down))

    mk, mv, kb, vb = _memory_kv(mem_prompt.reshape(batch * n_mem, D_MODEL), mem_norm_g,
                                w_mem_kv[0].astype(BF16), n_mem)

    xp, conv_p, hg_p = _prompt_mixer(x_prompt, kb.reshape(batch, n_mem, D_MEM),
                                     vb.reshape(batch, n_mem, D_MEM), vectors, weights)
    yp = _ffn(xp.reshape(batch * seq, D_MODEL), norm_pre_ffn, norm_post_ffn, *ffn_weights)

    xs2d = x_sample.reshape(n_dec * dec, D_MODEL)
    u, qd, ki, ke, v, blast, q4, hgate, gates = _sample_in(xs2d, norm_pre_mix, hg_lb_logits, weights[0])
    k2 = cache_mem_k[0].reshape(n_dec, n_mem * MEM_HEADS, MEM_HEAD_DIM)
    v2 = cache_mem_v[0].reshape(n_dec, n_mem * MEM_HEADS, MEM_HEAD_DIM)
    c, o, om, conv_rows, hg_s = _sample_seq(u, qd, ki, ke, v, blast, q4,
                                            jnp.transpose(state_conv[0], (1, 0, 2)), state_hgrn[0],
                                            k2, v2, conv_w[0])
    conv_s = jnp.transpose(conv_rows, (1, 0, 2))
    ys = _sample_out(xs2d, c, o, om, hgate, gates,
                     (norm_post_mix, conv_b, conv_ln_g, conv_ln_b, vectors[-1]), weights[1:],
                     (norm_pre_ffn, norm_post_ffn), ffn_weights)

    kv_shape = (1, batch, n_mem, MEM_HEADS, MEM_HEAD_DIM)
    return (yp.reshape(batch, seq, D_MODEL), ys.reshape(n_dec, dec, D_MODEL),
            conv_p[None], hg_p[None], mk.reshape(kv_shape), mv.reshape(kv_shape),
            conv_s[None], hg_s[None])
```

```python
from typing import NamedTuple

import jax
import jax.numpy as jnp
from jax import lax
from jax.experimental import pallas as pl
from jax.experimental.pallas import tpu as pltpu

D_MODEL = 1024
D_CONV = 512
CONV_WIDTH = 31
CONV_PREFIX = CONV_WIDTH - 1
HG_HEADS = 4
HG_DK = 128
HG_DV = 128
D_HGRN = HG_HEADS * HG_DK
MEM_HEADS = 4
MEM_HEAD_DIM = 128
D_MEM = MEM_HEADS * MEM_HEAD_DIM
CHUNK = 32
EPS = 1e-6
NEG_LOG2E = -1.4426950408889634

COL_CA = 0
COL_CB = COL_CA + D_CONV
COL_HQ = COL_CB + D_CONV
COL_HF = COL_HQ + D_HGRN
COL_HI = COL_HF + D_HGRN
COL_HGATE = COL_HI + D_HGRN
COL_MQ = COL_HGATE + D_HGRN
COL_GATES = COL_MQ + D_MEM

SUBLANES = 8
LANES = 128
VMEM_LIMIT = 56 * 1024 * 1024

PROMPT_TILE = 256
PROMPT_STEP_TILES = 1
SAMPLE_GROUP = 8
SAMPLE_TILE = 256
FFN_TILE = 512
CONV_ROWS = 32

BF16 = jnp.bfloat16
F32 = jnp.float32


def _dot(a, b):
    return jnp.dot(a, b, preferred_element_type=F32)


def _dot_nt(a, b):
    return lax.dot_general(a, b, (((1,), (1,)), ((), ())), preferred_element_type=F32)


def _dot_tn(a, b):
    return lax.dot_general(a, b, (((0,), (0,)), ((), ())), preferred_element_type=F32)


def _rms(x, gain):
    return x * lax.rsqrt(jnp.mean(x * x, axis=-1, keepdims=True) + EPS) * gain


def _sigmoid(x):
    return 1.0 / (1.0 + jnp.exp2(x * NEG_LOG2E))


def _silu(x):
    return x * _sigmoid(x)


def _forget_terms(hf, lb):
    sig = _sigmoid(hf)
    return jnp.log(lb + (1.0 - lb) * sig), (1.0 - lb) * (1.0 - sig)


def _lower_bound(lb_logits):
    m = jnp.max(lb_logits, axis=0, keepdims=True)
    e = jnp.exp(lb_logits - m)
    return e[0:1] / jnp.sum(e, axis=0, keepdims=True)


def _segment_cumsum(x, seg):
    pos = lax.broadcasted_iota(jnp.int32, x.shape, 0) & (seg - 1)
    s = 1
    while s < seg:
        x = x + jnp.where(pos >= s, pltpu.roll(x, s, axis=0), 0.0)
        s *= 2
    return x


def _conv_ln_silu(c, conv_b, ln_g, ln_b):
    c = c + conv_b
    mu = jnp.mean(c, axis=-1, keepdims=True)
    d = c - mu
    var = jnp.mean(d * d, axis=-1, keepdims=True)
    return _silu(d * lax.rsqrt(var + EPS) * ln_g + ln_b)


def _hgrn_gates(hq, hf, lb, seg):
    q = _silu(hq)
    logf, k = _forget_terms(hf, lb)
    b = _segment_cumsum(logf, seg)
    rows = b.shape[0]
    b3 = b.reshape(rows // seg, seg, D_HGRN)
    b_last3 = b3[:, seg - 1:seg, :]
    rest = (b_last3 - b3).reshape(rows, D_HGRN)
    q_dec = q * jnp.exp(b)
    k_inv = k * jnp.exp(-b)
    k_end = k * jnp.exp(rest)
    return q_dec, k_inv, k_end, b_last3.reshape(rows // seg, D_HGRN)


class _TileLevels(NamedTuple):
    q_mid: jax.Array
    k_mid: jax.Array
    q_64: jax.Array
    k_64: jax.Array
    q_128: jax.Array
    k_128: jax.Array
    q_tile: jax.Array
    k_tile: jax.Array
    decay: jax.Array


def _hgrn_tile_levels(hq, hf, lb):
    tl = hq.shape[0]
    n_blocks = 4
    block = 2 * CHUNK
    assert tl == n_blocks * block
    q = _silu(hq)
    logf, k = _forget_terms(hf, lb)
    b = _segment_cumsum(logf, CHUNK)
    chunks = [b[n * CHUNK:(n + 1) * CHUNK] for n in range(2 * n_blocks)]
    totals = [c[CHUNK - 1:CHUNK] for c in chunks]
    e = jnp.concatenate([c - totals[n] if n % 2 == 0 else c for n, c in enumerate(chunks)], axis=0)
    q_mid = q * jnp.exp(e)
    k_mid = k * jnp.exp(-e)
    first = [totals[2 * j] for j in range(n_blocks)]
    second = [totals[2 * j + 1] for j in range(n_blocks)]
    both = [first[j] + second[j] for j in range(n_blocks)]

    def scaled(x, blocks, log_scales):
        return jnp.concatenate(
            [x[j * block:(j + 1) * block] * jnp.exp(s) for j, s in zip(blocks, log_scales)], axis=0)

    every = range(n_blocks)
    return _TileLevels(
        q_mid=q_mid, k_mid=k_mid,
        q_64=scaled(q_mid, every, first),
        k_64=scaled(k_mid, every, second),
        q_128=scaled(q_mid, (2, 3), (first[2], first[3] + both[2])),
        k_128=scaled(k_mid, (0, 1), (second[0] + both[1], second[1])),
        q_tile=scaled(q_mid, every, [first[j] + sum(both[:j], 0.0) for j in every]),
        k_tile=scaled(k_mid, every, [second[j] + sum(both[j + 1:], 0.0) for j in every]),
        decay=jnp.exp(sum(both[1:], both[0])))


def _head_rms(o, gain4):
    parts = []
    for h in range(HG_HEADS):
        oh = o[:, h * HG_DV:(h + 1) * HG_DV]
        parts.append(oh * lax.rsqrt(jnp.mean(oh * oh, axis=-1, keepdims=True) + EPS))
    return jnp.concatenate(parts, axis=-1) * gain4


def _branch_gate(hb, w_in_ref, i):
    c0 = COL_GATES + i * D_MODEL
    return _sigmoid(_dot(hb, w_in_ref[:, c0:c0 + D_MODEL]))


def _merge_out(gate, x, branches, w_out_ref, post_g):
    acc = None
    for i, p in enumerate(branches):
        acc = gate(i) * p if acc is None else acc + gate(i) * p
    m = _dot(acc.astype(BF16), w_out_ref[...])
    return x + _rms(m, post_g)


def _ffn_block(x, pre_g, post_g, w_gate_ref, w_up_ref, w_down_ref):
    hb = _rms(x, pre_g).astype(BF16)
    f = _silu(_dot(hb, w_gate_ref[...])) * _dot(hb, w_up_ref[...])
    d = _dot(f.astype(BF16), w_down_ref[...])
    return x + _rms(d, post_g)


def _memory_kv_kernel(mem_ref, g_ref, w_ref, k_rows_ref, v_rows_ref, k_ref, v_ref):
    m = _rms(mem_ref[...], g_ref[...]).astype(BF16)
    tokens = mem_ref.shape[0]
    for full, rows_ref, op_ref in ((_dot(m, w_ref[:, :D_MEM]), k_rows_ref, k_ref),
                                   (_dot(m, w_ref[:, D_MEM:]), v_rows_ref, v_ref)):
        op_ref[...] = full.astype(BF16)
        for h in range(MEM_HEADS):
            rows_ref[pl.ds(h, tokens, stride=MEM_HEADS), :] = full[:, h * MEM_HEAD_DIM:(h + 1) * MEM_HEAD_DIM]


def _memory_kv(mem2d, gain, w_bf16, tile):
    rows = mem2d.shape[0]
    return pl.pallas_call(
        _memory_kv_kernel,
        grid=(rows // tile,),
        in_specs=[
            pl.BlockSpec((tile, D_MODEL), lambda i: (i, 0)),
            pl.BlockSpec((1, D_MODEL), lambda i: (0, 0)),
            pl.BlockSpec((D_MODEL, 2 * D_MEM), lambda i: (0, 0)),
        ],
        out_specs=[pl.BlockSpec((tile * MEM_HEADS, MEM_HEAD_DIM), lambda i: (i, 0))] * 2
        + [pl.BlockSpec((tile, D_MEM), lambda i: (i, 0))] * 2,
        out_shape=[jax.ShapeDtypeStruct((rows * MEM_HEADS, MEM_HEAD_DIM), F32)] * 2
        + [jax.ShapeDtypeStruct((rows, D_MEM), BF16)] * 2,
        compiler_params=pltpu.CompilerParams(dimension_semantics=("parallel",)),
        name="memory_kv",
    )(mem2d, gain, w_bf16)


def _prompt_mixer_kernel(x_ref, kb_ref, vb_ref, pre_g_ref, post_g_ref, conv_w_ref, conv_b_ref,
                         ln_g_ref, ln_b_ref, lb_logits_ref, hg_g_ref, w_in_ref, w_conv_out_ref,
                         w_hg_out_ref, w_mem_out_ref, w_out_ref,
                         y_ref, new_conv_ref, new_hg_ref,
                         shift_ref, tail_ref, state_t_ref):
    t = pl.program_id(1)
    last_t = pl.num_programs(1) - 1
    tl = PROMPT_TILE
    n_tiles = x_ref.shape[0] // tl

    @pl.when(t == 0)
    def _():
        tail_ref[...] = jnp.zeros_like(tail_ref)
        state_t_ref[...] = jnp.zeros_like(state_t_ref)

    lb = _lower_bound(lb_logits_ref[...])
    half = tl // 2
    row = lax.broadcasted_iota(jnp.int32, (tl, tl), 0)
    col = lax.broadcasted_iota(jnp.int32, (tl, tl), 1)
    span = row ^ col
    near = (span < 2 * CHUNK) & (col <= row)
    mid = (span < 4 * CHUNK) & (col < row)

    def tile(i, u_tail):
        rows = slice(i * tl, (i + 1) * tl)
        shift = shift_ref.at[i]
        x = x_ref[rows, :]
        hb = _rms(x, pre_g_ref[...]).astype(BF16)

        def proj(c0, c1):
            return _dot(hb, w_in_ref[:, c0:c1])

        u = proj(COL_CA, COL_CB) * _sigmoid(proj(COL_CB, COL_HQ))
        shift[0, 0:CONV_ROWS, :] = u_tail
        shift[0, CONV_ROWS:CONV_ROWS + tl, :] = u
        for r in range(1, SUBLANES):
            shift[r, 0:tl + CONV_ROWS - SUBLANES, :] = shift[0, r:r + tl + CONV_ROWS - SUBLANES, :]

        conv_blocks = []
        for base in range(0, tl, CONV_ROWS):
            acc = None
            for j in range(CONV_WIDTH):
                s = j + (CONV_ROWS - CONV_PREFIX)
                r, a = s % SUBLANES, s // SUBLANES
                lo = base + a * SUBLANES
                term = shift[r, lo:lo + CONV_ROWS, :] * conv_w_ref[j:j + 1, :]
                acc = term if acc is None else acc + term
            conv_blocks.append(acc)
        c = _conv_ln_silu(jnp.concatenate(conv_blocks, axis=0),
                          conv_b_ref[...], ln_g_ref[...], ln_b_ref[...]).astype(BF16)

        lv = _hgrn_tile_levels(proj(COL_HQ, COL_HF), proj(COL_HF, COL_HI), lb)
        lvb = _TileLevels(*[a.astype(BF16) for a in lv[:-1]], lv.decay)
        vb16 = proj(COL_HI, COL_HGATE).astype(BF16)
        hgate = _silu(proj(COL_HGATE, COL_MQ))
        mq = proj(COL_MQ, COL_GATES).astype(BF16)
        gates = [_branch_gate(hb, w_in_ref, g) for g in range(3)]

        om = []
        for h in range(MEM_HEADS):
            hs = slice(h * MEM_HEAD_DIM, (h + 1) * MEM_HEAD_DIM)
            s = _dot_nt(mq[:, hs], kb_ref[:, hs]) * (MEM_HEAD_DIM ** -0.5)
            e = jnp.exp(s - jnp.max(s, axis=-1, keepdims=True))
            p = e / jnp.sum(e, axis=-1, keepdims=True)
            om.append(_dot(p.astype(BF16), vb_ref[:, hs]))

        o_heads = []
        for h in range(HG_HEADS):
            hs = slice(h * HG_DK, (h + 1) * HG_DK)
            vh = vb16[:, hs]
            s_near = _dot_nt(lvb.q_mid[:, hs], lvb.k_mid[:, hs])
            s_mid = _dot_nt(lvb.q_64[:, hs], lvb.k_64[:, hs])
            s_far = _dot_nt(lvb.q_128[:, hs], lvb.k_128[:, hs])
            base = jnp.where(near, s_near, jnp.where(mid, s_mid, 0.0))
            scores = jnp.concatenate(
                [base[:half], jnp.concatenate([s_far, base[half:, half:]], axis=1)], axis=0)
            st = state_t_ref[h]
            o_heads.append(_dot(scores.astype(BF16), vh) + _dot_nt(lvb.q_tile[:, hs], st.astype(BF16)))
            state_t_ref[h] = st * lvb.decay[:, hs] + _dot_tn(vh, lvb.k_tile[:, hs])
        o = _head_rms(jnp.concatenate(o_heads, axis=-1), hg_g_ref[...]) * hgate

        p_conv = _dot(c, w_conv_out_ref[...])
        p_mem = _dot(jnp.concatenate(om, axis=-1).astype(BF16), w_mem_out_ref[...])
        p_hg = _dot(o.astype(BF16), w_hg_out_ref[...])
        y_ref[rows, :] = _merge_out(lambda g: gates[g], x, (p_conv, p_hg, p_mem), w_out_ref,
                                    post_g_ref[...])

        return u[tl - CONV_ROWS:]

    u_tail = tail_ref[...]
    for i in range(n_tiles):
        u_tail = tile(i, u_tail)
    tail_ref[...] = u_tail

    @pl.when(t == last_t)
    def _():
        new_conv_ref[...] = tail_ref[CONV_ROWS - CONV_PREFIX:CONV_ROWS, :]
        for h in range(HG_HEADS):
            new_hg_ref[h] = state_t_ref[h].T


def _const_spec(shape):
    zeros = (0,) * len(shape)
    return pl.BlockSpec(shape, lambda *_: zeros, pipeline_mode=pl.Buffered(1))


def _mixer_weight_specs(w_in, w_conv_out, w_hg_out, w_mem_out, w_out):
    return [_const_spec(w.shape) for w in (w_in, w_conv_out, w_hg_out, w_mem_out, w_out)]


def _vector_specs():
    return [
        _const_spec((1, D_MODEL)), _const_spec((1, D_MODEL)),
        _const_spec((CONV_WIDTH, D_CONV)), _const_spec((1, D_CONV)),
        _const_spec((1, D_CONV)), _const_spec((1, D_CONV)),
        _const_spec((2, D_HGRN)), _const_spec((1, D_HGRN)),
    ]


def _prompt_mixer(x, mk, mv, vectors, weights):
    batch, seq, _ = x.shape
    n_mem = mk.shape[1]
    tl = PROMPT_TILE * PROMPT_STEP_TILES
    return pl.pallas_call(
        _prompt_mixer_kernel,
        grid=(batch, seq // tl),
        in_specs=[
            pl.BlockSpec((None, tl, D_MODEL), lambda b, t: (b, t, 0)),
            pl.BlockSpec((None, n_mem, D_MEM), lambda b, t: (b, 0, 0)),
            pl.BlockSpec((None, n_mem, D_MEM), lambda b, t: (b, 0, 0)),
        ] + _vector_specs() + _mixer_weight_specs(*weights),
        out_specs=[
            pl.BlockSpec((None, tl, D_MODEL), lambda b, t: (b, t, 0)),
            pl.BlockSpec((None, CONV_PREFIX, D_CONV), lambda b, t: (b, 0, 0)),
            pl.BlockSpec((None, HG_HEADS, HG_DK, HG_DV), lambda b, t: (b, 0, 0, 0)),
        ],
        out_shape=[
            jax.ShapeDtypeStruct((batch, seq, D_MODEL), F32),
            jax.ShapeDtypeStruct((batch, CONV_PREFIX, D_CONV), F32),
            jax.ShapeDtypeStruct((batch, HG_HEADS, HG_DK, HG_DV), F32),
        ],
        scratch_shapes=[
            pltpu.VMEM((PROMPT_STEP_TILES, SUBLANES, PROMPT_TILE + CONV_ROWS, D_CONV), F32),
            pltpu.VMEM((CONV_ROWS, D_CONV), F32),
            pltpu.VMEM((HG_HEADS, HG_DV, HG_DK), F32),
        ],
        compiler_params=pltpu.CompilerParams(
            dimension_semantics=("parallel", "arbitrary"), vmem_limit_bytes=VMEM_LIMIT),
        name="prompt_mixer",
    )(x, mk, mv, *vectors, *weights)


def _store_lane_chunks(ref, x):
    for i in range(ref.shape[0]):
        ref[i] = x[:, i * LANES:(i + 1) * LANES]


def _load_lane_chunks(ref, rows=slice(None)):
    return jnp.concatenate([ref[i, rows, :] for i in range(ref.shape[0])], axis=1)


def _lane_chunk_spec(tile, width):
    return pl.BlockSpec((width // LANES, tile, LANES), lambda i: (0, i, 0))


def _sample_in_kernel(x_ref, pre_g_ref, lb_logits_ref, w_in_ref,
                      u_ref, qd_ref, ki_ref, ke_ref, v_ref, blast_ref, q4_ref, hgate_ref, gates_ref):
    n_seq = x_ref.shape[0] // SUBLANES
    hb = _rms(x_ref[...], pre_g_ref[...]).astype(BF16)
    _store_lane_chunks(
        u_ref, _dot(hb, w_in_ref[:, COL_CA:COL_CB]) * _sigmoid(_dot(hb, w_in_ref[:, COL_CB:COL_HQ])))
    lb = _lower_bound(lb_logits_ref[...])
    q_dec, k_inv, k_end, b_last = _hgrn_gates(
        _dot(hb, w_in_ref[:, COL_HQ:COL_HF]), _dot(hb, w_in_ref[:, COL_HF:COL_HI]), lb, SUBLANES)
    qd_ref[...] = q_dec
    ki_ref[...] = k_inv
    ke_ref[...] = k_end
    blast_ref[...] = b_last
    v_ref[...] = _dot(hb, w_in_ref[:, COL_HI:COL_HGATE])
    hgate_ref[...] = _silu(_dot(hb, w_in_ref[:, COL_HGATE:COL_MQ]))
    mq = _dot(hb, w_in_ref[:, COL_MQ:COL_GATES])
    for h in range(MEM_HEADS):
        q4_ref[:, h * SUBLANES:(h + 1) * SUBLANES, :] = (
            mq[:, h * MEM_HEAD_DIM:(h + 1) * MEM_HEAD_DIM].reshape(n_seq, SUBLANES, MEM_HEAD_DIM))
    for i in range(3):
        gates_ref[:, i * D_MODEL:(i + 1) * D_MODEL] = _branch_gate(hb, w_in_ref, i)


def _sample_seq_kernel(u_ref, qd_ref, ki_ref, ke_ref, v_ref, blast_ref, q4_ref, sc_ref, sh_ref,
                       k2_ref, v2_ref, conv_w_ref,
                       c_ref, o_ref, om_ref, new_conv_ref, new_hg_ref):
    n_seq, dec = sc_ref.shape[1], SUBLANES
    rows = n_seq * dec

    assert dec == 8 and MEM_HEADS == 4
    q_rows = MEM_HEADS * dec
    s = jnp.concatenate([_dot_nt(q4_ref[g].astype(BF16), k2_ref[g].astype(BF16)) for g in range(n_seq)],
                        axis=0) * (MEM_HEAD_DIM ** -0.5)
    q_head = (lax.broadcasted_iota(jnp.int32, s.shape, 0) >> 3) & 3
    k_head = lax.broadcasted_iota(jnp.int32, s.shape, 1) & 3
    s = jnp.where(q_head == k_head, s, -jnp.inf)
    e = jnp.exp(s - jnp.max(s, axis=-1, keepdims=True))
    p = (e / jnp.sum(e, axis=-1, keepdims=True)).astype(BF16)
    om = []
    for g in range(n_seq):
        om_g = _dot(p[g * q_rows:(g + 1) * q_rows], v2_ref[g].astype(BF16))
        om.extend(om_g[h * dec:(h + 1) * dec] for h in range(MEM_HEADS))

    u_slabs = [_load_lane_chunks(u_ref, pl.ds(step, n_seq, stride=dec)) for step in range(dec)]
    conv = [None] * dec
    for s in range(CONV_PREFIX + dec):
        slab = sc_ref[s] if s < CONV_PREFIX else u_slabs[s - CONV_PREFIX]
        if s >= dec:
            new_conv_ref[s - dec] = slab
        for step in range(max(0, s - CONV_PREFIX), min(dec, s + 1)):
            term = slab * conv_w_ref[s - step:s - step + 1, :]
            conv[step] = term if conv[step] is None else conv[step] + term

    row = lax.broadcasted_iota(jnp.int32, (rows, rows), 0)
    col = lax.broadcasted_iota(jnp.int32, (rows, rows), 1)
    causal = ((row ^ col) < dec) & (col <= row)
    decay = jnp.exp(blast_ref[...])
    for h in range(HG_HEADS):
        hs = slice(h * HG_DK, (h + 1) * HG_DK)
        scores = jnp.where(causal, _dot_nt(qd_ref[:, hs].astype(BF16), ki_ref[:, hs].astype(BF16)), 0.0)
        o_intra = _dot(scores.astype(BF16), v_ref[:, hs].astype(BF16))
        for g in range(n_seq):
            rs = slice(g * dec, (g + 1) * dec)
            s0 = sh_ref[g, h]
            o_ref[rs, hs] = o_intra[rs] + _dot(qd_ref[rs, hs].astype(BF16), s0.astype(BF16))
            decay_col = jnp.broadcast_to(decay[g:g + 1, hs], (HG_DV, HG_DK)).T
            new_hg_ref[g, h] = decay_col * s0 + _dot_tn(ke_ref[rs, hs].astype(BF16),
                                                        v_ref[rs, hs].astype(BF16))

    for step in range(dec):
        for i in range(c_ref.shape[0]):
            c_ref[i, pl.ds(step, n_seq, stride=dec), :] = conv[step][:, i * LANES:(i + 1) * LANES]
    for g in range(n_seq):
        om_ref[g * dec:(g + 1) * dec, :] = jnp.concatenate(om[g * MEM_HEADS:(g + 1) * MEM_HEADS], axis=1)


def _sample_out_kernel(x_ref, c_ref, o_ref, om_ref, hgate_ref, gates_ref, post_g_ref, conv_b_ref,
                       ln_g_ref, ln_b_ref, hg_g_ref, w_conv_out_ref, w_hg_out_ref, w_mem_out_ref,
                       w_out_ref, ffn_pre_g_ref, ffn_post_g_ref, w_gate_ref, w_up_ref, w_down_ref, y_ref):
    c = _conv_ln_silu(_load_lane_chunks(c_ref), conv_b_ref[...], ln_g_ref[...], ln_b_ref[...])
    p_conv = _dot(c.astype(BF16), w_conv_out_ref[...])
    o = _head_rms(o_ref[...], hg_g_ref[...]) * hgate_ref[...]
    p_hg = _dot(o.astype(BF16), w_hg_out_ref[...])
    p_mem = _dot(om_ref[...].astype(BF16), w_mem_out_ref[...])
    x1 = _merge_out(lambda i: gates_ref[:, i * D_MODEL:(i + 1) * D_MODEL], x_ref[...],
                    (p_conv, p_hg, p_mem), w_out_ref, post_g_ref[...])
    y_ref[...] = _ffn_block(x1, ffn_pre_g_ref[...], ffn_post_g_ref[...], w_gate_ref, w_up_ref, w_down_ref)


def _row_spec(tile, width):
    return pl.BlockSpec((tile, width), lambda i: (i, 0))


def _sample_in(x2d, pre_g, lb_logits, w_in):
    rows = x2d.shape[0]
    tile = SAMPLE_TILE
    seqs = tile // SUBLANES
    wide = [D_HGRN, D_HGRN, D_HGRN, D_HGRN]
    return pl.pallas_call(
        _sample_in_kernel,
        grid=(rows // tile,),
        in_specs=[_row_spec(tile, D_MODEL), _const_spec((1, D_MODEL)), _const_spec((2, D_HGRN)),
                  _const_spec(w_in.shape)],
        out_specs=[_lane_chunk_spec(tile, D_CONV)] + [_row_spec(tile, w) for w in wide] + [
            _row_spec(seqs, D_HGRN),
            pl.BlockSpec((seqs, MEM_HEADS * SUBLANES, MEM_HEAD_DIM), lambda i: (i, 0, 0)),
            _row_spec(tile, D_HGRN),
            _row_spec(tile, 3 * D_MODEL),
        ],
        out_shape=[jax.ShapeDtypeStruct((D_CONV // LANES, rows, LANES), F32)]
        + [jax.ShapeDtypeStruct((rows, w), F32) for w in wide] + [
            jax.ShapeDtypeStruct((rows // SUBLANES, D_HGRN), F32),
            jax.ShapeDtypeStruct((rows // SUBLANES, MEM_HEADS * SUBLANES, MEM_HEAD_DIM), F32),
            jax.ShapeDtypeStruct((rows, D_HGRN), F32),
            jax.ShapeDtypeStruct((rows, 3 * D_MODEL), F32),
        ],
        compiler_params=pltpu.CompilerParams(
            dimension_semantics=("parallel",), vmem_limit_bytes=VMEM_LIMIT),
        name="sample_in",
    )(x2d, pre_g, lb_logits, w_in)


def _sample_seq(u, qd, ki, ke, v, blast, q4, conv_rows, state_hgrn, k2, v2, conv_w):
    n = conv_rows.shape[1]
    g = SAMPLE_GROUP
    rows = g * SUBLANES
    kv_rows = k2.shape[1]
    seq_block = lambda *tail: pl.BlockSpec((g,) + tail, lambda i: (i,) + (0,) * len(tail))
    conv_block = pl.BlockSpec((CONV_PREFIX, g, D_CONV), lambda i: (0, i, 0))
    return pl.pallas_call(
        _sample_seq_kernel,
        grid=(n // g,),
        in_specs=[_lane_chunk_spec(rows, D_CONV)] + [_row_spec(rows, D_HGRN)] * 4 + [
            _row_spec(g, D_HGRN),
            seq_block(MEM_HEADS * SUBLANES, MEM_HEAD_DIM),
            conv_block,
            seq_block(HG_HEADS, HG_DK, HG_DV),
            seq_block(kv_rows, MEM_HEAD_DIM),
            seq_block(kv_rows, MEM_HEAD_DIM),
            _const_spec((CONV_WIDTH, D_CONV)),
        ],
        out_specs=[_lane_chunk_spec(rows, D_CONV), _row_spec(rows, D_HGRN), _row_spec(rows, D_MEM),
                   conv_block, seq_block(HG_HEADS, HG_DK, HG_DV)],
        out_shape=[
            jax.ShapeDtypeStruct((D_CONV // LANES, n * SUBLANES, LANES), F32),
            jax.ShapeDtypeStruct((n * SUBLANES, D_HGRN), F32),
            jax.ShapeDtypeStruct((n * SUBLANES, D_MEM), F32),
            jax.ShapeDtypeStruct((CONV_PREFIX, n, D_CONV), F32),
            jax.ShapeDtypeStruct((n, HG_HEADS, HG_DK, HG_DV), F32),
        ],
        compiler_params=pltpu.CompilerParams(
            dimension_semantics=("parallel",), vmem_limit_bytes=VMEM_LIMIT),
        name="sample_seq",
    )(u, qd, ki, ke, v, blast, q4, conv_rows, state_hgrn, k2, v2, conv_w)


def _sample_out(x2d, c, o, om, hgate, gates, vectors, weights, ffn_vectors, ffn_weights):
    rows = x2d.shape[0]
    tile = SAMPLE_TILE
    acts = (x2d, c, o, om, hgate, gates)
    consts = tuple(vectors) + tuple(weights) + tuple(ffn_vectors) + tuple(ffn_weights)
    return pl.pallas_call(
        _sample_out_kernel,
        grid=(rows // tile,),
        in_specs=[_lane_chunk_spec(tile, D_CONV) if a is c else _row_spec(tile, a.shape[1]) for a in acts]
        + [_const_spec(a.shape) for a in consts],
        out_specs=_row_spec(tile, D_MODEL),
        out_shape=jax.ShapeDtypeStruct((rows, D_MODEL), F32),
        compiler_params=pltpu.CompilerParams(
            dimension_semantics=("parallel",), vmem_limit_bytes=VMEM_LIMIT),
        name="sample_out",
    )(*acts, *consts)


def _ffn_kernel(x_ref, pre_g_ref, post_g_ref, w_gate_ref, w_up_ref, w_down_ref, y_ref):
    y_ref[...] = _ffn_block(x_ref[...], pre_g_ref[...], post_g_ref[...], w_gate_ref, w_up_ref, w_down_ref)


def _ffn(x2d, pre_g, post_g, w_gate, w_up, w_down):
    rows = x2d.shape[0]
    tile = min(FFN_TILE, rows)
    return pl.pallas_call(
        _ffn_kernel,
        grid=(rows // tile,),
        in_specs=[
            pl.BlockSpec((tile, D_MODEL), lambda i: (i, 0)),
            _const_spec((1, D_MODEL)), _const_spec((1, D_MODEL)),
            _const_spec(w_gate.shape), _const_spec(w_up.shape), _const_spec(w_down.shape),
        ],
        out_specs=pl.BlockSpec((tile, D_MODEL), lambda i: (i, 0)),
        out_shape=jax.ShapeDtypeStruct((rows, D_MODEL), F32),
        compiler_params=pltpu.CompilerParams(
            dimension_semantics=("parallel",), vmem_limit_bytes=VMEM_LIMIT),
        name="ffn",
    )(x2d, pre_g, post_g, w_gate, w_up, w_down)


def kernel(x_prompt, x_sample, mem_prompt, state_conv, state_hgrn, cache_mem_k, cache_mem_v, norm_pre_mix, norm_post_mix, norm_pre_ffn, norm_post_ffn, w_in, conv_w, conv_b, conv_ln_g, conv_ln_b, w_conv_out, hg_lb_logits, hg_norm_g, w_hg_out, mem_norm_g, w_mem_kv, w_mem_out, w_out, w_ffn_gate, w_ffn_up, w_ffn_down):
    depth = w_in.shape[0]
    assert depth == 1 and hg_lb_logits.shape[0] == 2, "single-layer step"
    batch, seq, _ = x_prompt.shape
    n_dec, dec, _ = x_sample.shape
    n_mem = mem_prompt.shape[1]
    assert seq % PROMPT_TILE == 0 and n_dec % SAMPLE_GROUP == 0 and dec == SUBLANES
    assert (n_dec * dec) % SAMPLE_TILE == 0

    vectors = (norm_pre_mix, norm_post_mix, conv_w[0], conv_b, conv_ln_g, conv_ln_b,
               hg_lb_logits, jnp.tile(hg_norm_g, (1, HG_HEADS)))
    weights = tuple(w[0].astype(BF16) for w in (w_in, w_conv_out, w_hg_out, w_mem_out, w_out))
    ffn_weights = tuple(w[0].astype(BF16) for w in (w_ffn_gate, w_ffn_up, w_ffn_down))

    mk, mv, kb, vb = _memory_kv(mem_prompt.reshape(batch * n_mem, D_MODEL), mem_norm_g,
                                w_mem_kv[0].astype(BF16), n_mem)

    xp, conv_p, hg_p = _prompt_mixer(x_prompt, kb.reshape(batch, n_mem, D_MEM),
                                     vb.reshape(batch, n_mem, D_MEM), vectors, weights)
    yp = _ffn(xp.reshape(batch * seq, D_MODEL), norm_pre_ffn, norm_post_ffn, *ffn_weights)

    xs2d = x_sample.reshape(n_dec * dec, D_MODEL)
    u, qd, ki, ke, v, blast, q4, hgate, gates = _sample_in(xs2d, norm_pre_mix, hg_lb_logits, weights[0])
    k2 = cache_mem_k[0].reshape(n_dec, n_mem * MEM_HEADS, MEM_HEAD_DIM)
    v2 = cache_mem_v[0].reshape(n_dec, n_mem * MEM_HEADS, MEM_HEAD_DIM)
    c, o, om, conv_rows, hg_s = _sample_seq(u, qd, ki, ke, v, blast, q4,
                                            jnp.transpose(state_conv[0], (1, 0, 2)), state_hgrn[0],
                                            k2, v2, conv_w[0])
    conv_s = jnp.transpose(conv_rows, (1, 0, 2))
    ys = _sample_out(xs2d, c, o, om, hgate, gates,
                     (norm_post_mix, conv_b, conv_ln_g, conv_ln_b, vectors[-1]), weights[1:],
                     (norm_pre_ffn, norm_post_ffn), ffn_weights)

    kv_shape = (1, batch, n_mem, MEM_HEADS, MEM_HEAD_DIM)
    return (yp.reshape(batch, seq, D_MODEL), ys.reshape(n_dec, dec, D_MODEL),
            conv_p[None], hg_p[None], mk.reshape(kv_shape), mv.reshape(kv_shape),
            conv_s[None], hg_s[None])
```

```python
from typing import NamedTuple

import jax
import jax.numpy as jnp
from jax import lax
from jax.experimental import pallas as pl
from jax.experimental.pallas import tpu as pltpu

D_MODEL = 1024
D_CONV = 512
CONV_WIDTH = 31
CONV_PREFIX = CONV_WIDTH - 1
HG_HEADS = 4
HG_DK = 128
HG_DV = 128
D_HGRN = HG_HEADS * HG_DK
MEM_HEADS = 4
MEM_HEAD_DIM = 128
D_MEM = MEM_HEADS * MEM_HEAD_DIM
CHUNK = 32
EPS = 1e-6
NEG_LOG2E = -1.4426950408889634

COL_CA = 0
COL_CB = COL_CA + D_CONV
COL_HQ = COL_CB + D_CONV
COL_HF = COL_HQ + D_HGRN
COL_HI = COL_HF + D_HGRN
COL_HGATE = COL_HI + D_HGRN
COL_MQ = COL_HGATE + D_HGRN
COL_GATES = COL_MQ + D_MEM

SUBLANES = 8
LANES = 128
VMEM_LIMIT = 56 * 1024 * 1024

PROMPT_TILE = 256
PROMPT_STEP_TILES = 1
SAMPLE_GROUP = 8
SAMPLE_TILE = 256
FFN_TILE = 512
CONV_ROWS = 32

BF16 = jnp.bfloat16
F32 = jnp.float32


def _dot(a, b):
    return jnp.dot(a, b, preferred_element_type=F32)


def _dot_nt(a, b):
    return lax.dot_general(a, b, (((1,), (1,)), ((), ())), preferred_element_type=F32)


def _dot_tn(a, b):
    return lax.dot_general(a, b, (((0,), (0,)), ((), ())), preferred_element_type=F32)


def _rms(x, gain):
    return x * lax.rsqrt(jnp.mean(x * x, axis=-1, keepdims=True) + EPS) * gain


def _sigmoid(x):
    return 1.0 / (1.0 + jnp.exp2(x * NEG_LOG2E))


def _silu(x):
    return x * _sigmoid(x)


def _forget_terms(hf, lb):
    sig = _sigmoid(hf)
    return jnp.log(lb + (1.0 - lb) * sig), (1.0 - lb) * (1.0 - sig)


def _lower_bound(lb_logits):
    m = jnp.max(lb_logits, axis=0, keepdims=True)
    e = jnp.exp(lb_logits - m)
    return e[0:1] / jnp.sum(e, axis=0, keepdims=True)


def _segment_cumsum(x, seg):
    pos = lax.broadcasted_iota(jnp.int32, x.shape, 0) & (seg - 1)
    s = 1
    while s < seg:
        x = x + jnp.where(pos >= s, pltpu.roll(x, s, axis=0), 0.0)
        s *= 2
    return x


def _conv_ln_silu(c, conv_b, ln_g, ln_b):
    c = c + conv_b
    mu = jnp.mean(c, axis=-1, keepdims=True)
    d = c - mu
    var = jnp.mean(d * d, axis=-1, keepdims=True)
    return _silu(d * lax.rsqrt(var + EPS) * ln_g + ln_b)


def _hgrn_gates(hq, hf, lb, seg):
    q = _silu(hq)
    logf, k = _forget_terms(hf, lb)
    b = _segment_cumsum(logf, seg)
    rows = b.shape[0]
    b3 = b.reshape(rows // seg, seg, D_HGRN)
    b_last3 = b3[:, seg - 1:seg, :]
    rest = (b_last3 - b3).reshape(rows, D_HGRN)
    q_dec = q * jnp.exp(b)
    k_inv = k * jnp.exp(-b)
    k_end = k * jnp.exp(rest)
    return q_dec, k_inv, k_end, b_last3.reshape(rows // seg, D_HGRN)


class _TileLevels(NamedTuple):
    q_mid: jax.Array
    k_mid: jax.Array
    q_64: jax.Array
    k_64: jax.Array
    q_128: jax.Array
    k_128: jax.Array
    q_tile: jax.Array
    k_tile: jax.Array
    decay: jax.Array


def _hgrn_tile_levels(hq, hf, lb):
    tl = hq.shape[0]
    n_blocks = 4
    block = 2 * CHUNK
    assert tl == n_blocks * block
    q = _silu(hq)
    logf, k = _forget_terms(hf, lb)
    b = _segment_cumsum(logf, CHUNK)
    chunks = [b[n * CHUNK:(n + 1) * CHUNK] for n in range(2 * n_blocks)]
    totals = [c[CHUNK - 1:CHUNK] for c in chunks]
    e = jnp.concatenate([c - totals[n] if n % 2 == 0 else c for n, c in enumerate(chunks)], axis=0)
    q_mid = q * jnp.exp(e)
    k_mid = k * jnp.exp(-e)
    first = [totals[2 * j] for j in range(n_blocks)]
    second = [totals[2 * j + 1] for j in range(n_blocks)]
    both = [first[j] + second[j] for j in range(n_blocks)]

    def scaled(x, blocks, log_scales):
        return jnp.concatenate(
            [x[j * block:(j + 1) * block] * jnp.exp(s) for j, s in zip(blocks, log_scales)], axis=0)

    every = range(n_blocks)
    return _TileLevels(
        q_mid=q_mid, k_mid=k_mid,
        q_64=scaled(q_mid, every, first),
        k_64=scaled(k_mid, every, second),
        q_128=scaled(q_mid, (2, 3), (first[2], first[3] + both[2])),
        k_128=scaled(k_mid, (0, 1), (second[0] + both[1], second[1])),
        q_tile=scaled(q_mid, every, [first[j] + sum(both[:j], 0.0) for j in every]),
        k_tile=scaled(k_mid, every, [second[j] + sum(both[j + 1:], 0.0) for j in every]),
        decay=jnp.exp(sum(both[1:], both[0])))


def _head_rms(o, gain4):
    parts = []
    for h in range(HG_HEADS):
        oh = o[:, h * HG_DV:(h + 1) * HG_DV]
        parts.append(oh * lax.rsqrt(jnp.mean(oh * oh, axis=-1, keepdims=True) + EPS))
    return jnp.concatenate(parts, axis=-1) * gain4


def _branch_gate(hb, w_in_ref, i):
    c0 = COL_GATES + i * D_MODEL
    return _sigmoid(_dot(hb, w_in_ref[:, c0:c0 + D_MODEL]))


def _merge_out(gate, x, branches, w_out_ref, post_g):
    acc = None
    for i, p in enumerate(branches):
        acc = gate(i) * p if acc is None else acc + gate(i) * p
    m = _dot(acc.astype(BF16), w_out_ref[...])
    return x + _rms(m, post_g)


def _ffn_block(x, pre_g, post_g, w_gate_ref, w_up_ref, w_down_ref):
    hb = _rms(x, pre_g).astype(BF16)
    f = _silu(_dot(hb, w_gate_ref[...])) * _dot(hb, w_up_ref[...])
    d = _dot(f.astype(BF16), w_down_ref[...])
    return x + _rms(d, post_g)


def _memory_kv_kernel(mem_ref, g_ref, w_ref, k_rows_ref, v_rows_ref, k_ref, v_ref):
    m = _rms(mem_ref[...], g_ref[...]).astype(BF16)
    tokens = mem_ref.shape[0]
    for full, rows_ref, op_ref in ((_dot(m, w_ref[:, :D_MEM]), k_rows_ref, k_ref),
                                   (_dot(m, w_ref[:, D_MEM:]), v_rows_ref, v_ref)):
        op_ref[...] = full.astype(BF16)
        for h in range(MEM_HEADS):
            rows_ref[pl.ds(h, tokens, stride=MEM_HEADS), :] = full[:, h * MEM_HEAD_DIM:(h + 1) * MEM_HEAD_DIM]


def _memory_kv(mem2d, gain, w_bf16, tile):
    rows = mem2d.shape[0]
    return pl.pallas_call(
        _memory_kv_kernel,
        grid=(rows // tile,),
        in_specs=[
            pl.BlockSpec((tile, D_MODEL), lambda i: (i, 0)),
            pl.BlockSpec((1, D_MODEL), lambda i: (0, 0)),
            pl.BlockSpec((D_MODEL, 2 * D_MEM), lambda i: (0, 0)),
        ],
        out_specs=[pl.BlockSpec((tile * MEM_HEADS, MEM_HEAD_DIM), lambda i: (i, 0))] * 2
        + [pl.BlockSpec((tile, D_MEM), lambda i: (i, 0))] * 2,
        out_shape=[jax.ShapeDtypeStruct((rows * MEM_HEADS, MEM_HEAD_DIM), F32)] * 2
        + [jax.ShapeDtypeStruct((rows, D_MEM), BF16)] * 2,
        compiler_params=pltpu.CompilerParams(dimension_semantics=("parallel",)),
        name="memory_kv",
    )(mem2d, gain, w_bf16)


def _prompt_mixer_kernel(x_ref, kb_ref, vb_ref, pre_g_ref, post_g_ref, conv_w_ref, conv_b_ref,
                         ln_g_ref, ln_b_ref, lb_logits_ref, hg_g_ref, w_in_ref, w_conv_out_ref,
                         w_hg_out_ref, w_mem_out_ref, w_out_ref, *rest):
    n_cast = (len(rest) - 6) // 2
    cast_src = rest[:n_cast]
    y_ref, new_conv_ref, new_hg_ref = rest[n_cast:n_cast + 3]
    cast_dst = rest[n_cast + 3:2 * n_cast + 3]
    shift_ref, tail_ref, state_t_ref = rest[2 * n_cast + 3:]
    for src, dst in zip(cast_src, cast_dst):
        dst[...] = src[...].astype(BF16)
    t = pl.program_id(1)
    last_t = pl.num_programs(1) - 1
    tl = PROMPT_TILE
    n_tiles = x_ref.shape[0] // tl

    @pl.when(t == 0)
    def _():
        tail_ref[...] = jnp.zeros_like(tail_ref)
        state_t_ref[...] = jnp.zeros_like(state_t_ref)

    lb = _lower_bound(lb_logits_ref[...])
    half = tl // 2
    row = lax.broadcasted_iota(jnp.int32, (tl, tl), 0)
    col = lax.broadcasted_iota(jnp.int32, (tl, tl), 1)
    span = row ^ col
    near = (span < 2 * CHUNK) & (col <= row)
    mid = (span < 4 * CHUNK) & (col < row)

    def tile(i, u_tail):
        rows = slice(i * tl, (i + 1) * tl)
        shift = shift_ref.at[i]
        x = x_ref[rows, :]
        hb = _rms(x, pre_g_ref[...]).astype(BF16)

        def proj(c0, c1):
            return _dot(hb, w_in_ref[:, c0:c1])

        u = proj(COL_CA, COL_CB) * _sigmoid(proj(COL_CB, COL_HQ))
        shift[0, 0:CONV_ROWS, :] = u_tail
        shift[0, CONV_ROWS:CONV_ROWS + tl, :] = u
        for r in range(1, SUBLANES):
            shift[r, 0:tl + CONV_ROWS - SUBLANES, :] = shift[0, r:r + tl + CONV_ROWS - SUBLANES, :]

        conv_blocks = []
        for base in range(0, tl, CONV_ROWS):
            acc = None
            for j in range(CONV_WIDTH):
                s = j + (CONV_ROWS - CONV_PREFIX)
                r, a = s % SUBLANES, s // SUBLANES
                lo = base + a * SUBLANES
                term = shift[r, lo:lo + CONV_ROWS, :] * conv_w_ref[j:j + 1, :]
                acc = term if acc is None else acc + term
            conv_blocks.append(acc)
        c = _conv_ln_silu(jnp.concatenate(conv_blocks, axis=0),
                          conv_b_ref[...], ln_g_ref[...], ln_b_ref[...]).astype(BF16)

        lv = _hgrn_tile_levels(proj(COL_HQ, COL_HF), proj(COL_HF, COL_HI), lb)
        lvb = _TileLevels(*[a.astype(BF16) for a in lv[:-1]], lv.decay)
        vb16 = proj(COL_HI, COL_HGATE).astype(BF16)
        hgate = _silu(proj(COL_HGATE, COL_MQ))
        mq = proj(COL_MQ, COL_GATES).astype(BF16)
        gates = [_branch_gate(hb, w_in_ref, g) for g in range(3)]

        head = lambda h: slice(h * MEM_HEAD_DIM, (h + 1) * MEM_HEAD_DIM)
        s = jnp.concatenate([_dot_nt(mq[:, head(h)], kb_ref[:, head(h)]) for h in range(MEM_HEADS)],
                            axis=0) * (MEM_HEAD_DIM ** -0.5)
        e = jnp.exp(s - jnp.max(s, axis=-1, keepdims=True))
        p = (e / jnp.sum(e, axis=-1, keepdims=True)).astype(BF16)
        om = [_dot(p[h * tl:(h + 1) * tl], vb_ref[:, head(h)]) for h in range(MEM_HEADS)]

        o_heads = []
        for h in range(HG_HEADS):
            hs = slice(h * HG_DK, (h + 1) * HG_DK)
            vh = vb16[:, hs]
            s_near = _dot_nt(lvb.q_mid[:, hs], lvb.k_mid[:, hs])
            s_mid = _dot_nt(lvb.q_64[:, hs], lvb.k_64[:, hs])
            s_far = _dot_nt(lvb.q_128[:, hs], lvb.k_128[:, hs])
            base = jnp.where(near, s_near, jnp.where(mid, s_mid, 0.0))
            scores = jnp.concatenate(
                [base[:half], jnp.concatenate([s_far, base[half:, half:]], axis=1)], axis=0)
            st = state_t_ref[h]
            o_heads.append(_dot(scores.astype(BF16), vh) + _dot_nt(lvb.q_tile[:, hs], st.astype(BF16)))
            state_t_ref[h] = st * lvb.decay[:, hs] + _dot_tn(vh, lvb.k_tile[:, hs])
        o = _head_rms(jnp.concatenate(o_heads, axis=-1), hg_g_ref[...]) * hgate

        p_conv = _dot(c, w_conv_out_ref[...])
        p_mem = _dot(jnp.concatenate(om, axis=-1).astype(BF16), w_mem_out_ref[...])
        p_hg = _dot(o.astype(BF16), w_hg_out_ref[...])
        y_ref[rows, :] = _merge_out(lambda g: gates[g], x, (p_conv, p_hg, p_mem), w_out_ref,
                                    post_g_ref[...])

        return u[tl - CONV_ROWS:]

    u_tail = tail_ref[...]
    for i in range(n_tiles):
        u_tail = tile(i, u_tail)
    tail_ref[...] = u_tail

    @pl.when(t == last_t)
    def _():
        new_conv_ref[...] = tail_ref[CONV_ROWS - CONV_PREFIX:CONV_ROWS, :]
        for h in range(HG_HEADS):
            new_hg_ref[h] = state_t_ref[h].T


def _const_spec(shape):
    zeros = (0,) * len(shape)
    return pl.BlockSpec(shape, lambda *_: zeros, pipeline_mode=pl.Buffered(1))


def _mixer_weight_specs(w_in, w_conv_out, w_hg_out, w_mem_out, w_out):
    return [_const_spec(w.shape) for w in (w_in, w_conv_out, w_hg_out, w_mem_out, w_out)]


def _vector_specs():
    return [
        _const_spec((1, D_MODEL)), _const_spec((1, D_MODEL)),
        _const_spec((CONV_WIDTH, D_CONV)), _const_spec((1, D_CONV)),
        _const_spec((1, D_CONV)), _const_spec((1, D_CONV)),
        _const_spec((2, D_HGRN)), _const_spec((1, D_HGRN)),
    ]


def _cast_block_rows(rows, n_steps):
    bf16_sublanes = 2 * SUBLANES
    for n_blocks in range(min(n_steps, rows // bf16_sublanes), 0, -1):
        if rows % n_blocks == 0 and (rows // n_blocks) % bf16_sublanes == 0:
            return rows // n_blocks
    raise ValueError(f"no bf16-tileable split of {rows} rows")


def _prompt_mixer(x, mk, mv, vectors, weights, later_weights):
    batch, seq, _ = x.shape
    n_mem = mk.shape[1]
    tl = PROMPT_TILE * PROMPT_STEP_TILES
    n_t = seq // tl
    cast_specs = []
    for w in later_weights:
        rows_blk = _cast_block_rows(w.shape[0], batch * n_t)
        last = w.shape[0] // rows_blk - 1
        cast_specs.append(pl.BlockSpec(
            (rows_blk, w.shape[1]), lambda b, t, last=last: (jnp.minimum(b * n_t + t, last), 0)))
    return pl.pallas_call(
        _prompt_mixer_kernel,
        grid=(batch, n_t),
        in_specs=[
            pl.BlockSpec((None, tl, D_MODEL), lambda b, t: (b, t, 0)),
            pl.BlockSpec((None, n_mem, D_MEM), lambda b, t: (b, 0, 0)),
            pl.BlockSpec((None, n_mem, D_MEM), lambda b, t: (b, 0, 0)),
        ] + _vector_specs() + _mixer_weight_specs(*weights) + cast_specs,
        out_specs=[
            pl.BlockSpec((None, tl, D_MODEL), lambda b, t: (b, t, 0)),
            pl.BlockSpec((None, CONV_PREFIX, D_CONV), lambda b, t: (b, 0, 0)),
            pl.BlockSpec((None, HG_HEADS, HG_DK, HG_DV), lambda b, t: (b, 0, 0, 0)),
        ] + cast_specs,
        out_shape=[
            jax.ShapeDtypeStruct((batch, seq, D_MODEL), F32),
            jax.ShapeDtypeStruct((batch, CONV_PREFIX, D_CONV), F32),
            jax.ShapeDtypeStruct((batch, HG_HEADS, HG_DK, HG_DV), F32),
        ] + [jax.ShapeDtypeStruct(w.shape, BF16) for w in later_weights],
        scratch_shapes=[
            pltpu.VMEM((PROMPT_STEP_TILES, SUBLANES, PROMPT_TILE + CONV_ROWS, D_CONV), F32),
            pltpu.VMEM((CONV_ROWS, D_CONV), F32),
            pltpu.VMEM((HG_HEADS, HG_DV, HG_DK), F32),
        ],
        compiler_params=pltpu.CompilerParams(
            dimension_semantics=("arbitrary", "arbitrary"), vmem_limit_bytes=VMEM_LIMIT),
        name="prompt_mixer",
    )(x, mk, mv, *vectors, *weights, *later_weights)


def _store_lane_chunks(ref, x):
    for i in range(ref.shape[0]):
        ref[i] = x[:, i * LANES:(i + 1) * LANES]


def _load_lane_chunks(ref, rows=slice(None)):
    return jnp.concatenate([ref[i, rows, :] for i in range(ref.shape[0])], axis=1)


def _lane_chunk_spec(tile, width):
    return pl.BlockSpec((width // LANES, tile, LANES), lambda i: (0, i, 0))


def _sample_in_kernel(x_ref, pre_g_ref, lb_logits_ref, w_in_ref,
                      u_ref, qd_ref, ki_ref, ke_ref, v_ref, blast_ref, q4_ref, hgate_ref, gates_ref):
    n_seq = x_ref.shape[0] // SUBLANES
    hb = _rms(x_ref[...], pre_g_ref[...]).astype(BF16)
    _store_lane_chunks(
        u_ref, _dot(hb, w_in_ref[:, COL_CA:COL_CB]) * _sigmoid(_dot(hb, w_in_ref[:, COL_CB:COL_HQ])))
    lb = _lower_bound(lb_logits_ref[...])
    q_dec, k_inv, k_end, b_last = _hgrn_gates(
        _dot(hb, w_in_ref[:, COL_HQ:COL_HF]), _dot(hb, w_in_ref[:, COL_HF:COL_HI]), lb, SUBLANES)
    qd_ref[...] = q_dec
    ki_ref[...] = k_inv
    ke_ref[...] = k_end
    blast_ref[...] = b_last
    v_ref[...] = _dot(hb, w_in_ref[:, COL_HI:COL_HGATE])
    hgate_ref[...] = _silu(_dot(hb, w_in_ref[:, COL_HGATE:COL_MQ]))
    mq = _dot(hb, w_in_ref[:, COL_MQ:COL_GATES])
    for h in range(MEM_HEADS):
        q4_ref[:, h * SUBLANES:(h + 1) * SUBLANES, :] = (
            mq[:, h * MEM_HEAD_DIM:(h + 1) * MEM_HEAD_DIM].reshape(n_seq, SUBLANES, MEM_HEAD_DIM))
    for i in range(3):
        gates_ref[:, i * D_MODEL:(i + 1) * D_MODEL] = _branch_gate(hb, w_in_ref, i)


def _sample_seq_kernel(u_ref, qd_ref, ki_ref, ke_ref, v_ref, blast_ref, q4_ref, sc_ref, sh_ref,
                       k2_ref, v2_ref, conv_w_ref,
                       c_ref, o_ref, om_ref, new_conv_ref, new_hg_ref):
    n_seq, dec = sc_ref.shape[1], SUBLANES
    rows = n_seq * dec

    assert dec == 8 and MEM_HEADS == 4
    q_rows = MEM_HEADS * dec
    s = jnp.concatenate([_dot_nt(q4_ref[g].astype(BF16), k2_ref[g].astype(BF16)) for g in range(n_seq)],
                        axis=0) * (MEM_HEAD_DIM ** -0.5)
    q_head = (lax.broadcasted_iota(jnp.int32, s.shape, 0) >> 3) & 3
    k_head = lax.broadcasted_iota(jnp.int32, s.shape, 1) & 3
    s = jnp.where(q_head == k_head, s, -jnp.inf)
    e = jnp.exp(s - jnp.max(s, axis=-1, keepdims=True))
    p = (e / jnp.sum(e, axis=-1, keepdims=True)).astype(BF16)
    om = []
    for g in range(n_seq):
        om_g = _dot(p[g * q_rows:(g + 1) * q_rows], v2_ref[g].astype(BF16))
        om.extend(om_g[h * dec:(h + 1) * dec] for h in range(MEM_HEADS))

    u_slabs = [_load_lane_chunks(u_ref, pl.ds(step, n_seq, stride=dec)) for step in range(dec)]
    conv = [None] * dec
    for s in range(CONV_PREFIX + dec):
        slab = sc_ref[s] if s < CONV_PREFIX else u_slabs[s - CONV_PREFIX]
        if s >= dec:
            new_conv_ref[s - dec] = slab
        for step in range(max(0, s - CONV_PREFIX), min(dec, s + 1)):
            term = slab * conv_w_ref[s - step:s - step + 1, :]
            conv[step] = term if conv[step] is None else conv[step] + term

    row = lax.broadcasted_iota(jnp.int32, (rows, rows), 0)
    col = lax.broadcasted_iota(jnp.int32, (rows, rows), 1)
    causal = ((row ^ col) < dec) & (col <= row)
    decay = jnp.exp(blast_ref[...])
    for h in range(HG_HEADS):
        hs = slice(h * HG_DK, (h + 1) * HG_DK)
        scores = jnp.where(causal, _dot_nt(qd_ref[:, hs].astype(BF16), ki_ref[:, hs].astype(BF16)), 0.0)
        o_intra = _dot(scores.astype(BF16), v_ref[:, hs].astype(BF16))
        for g in range(n_seq):
            rs = slice(g * dec, (g + 1) * dec)
            s0 = sh_ref[g, h]
            o_ref[rs, hs] = o_intra[rs] + _dot(qd_ref[rs, hs].astype(BF16), s0.astype(BF16))
            decay_col = jnp.broadcast_to(decay[g:g + 1, hs], (HG_DV, HG_DK)).T
            new_hg_ref[g, h] = decay_col * s0 + _dot_tn(ke_ref[rs, hs].astype(BF16),
                                                        v_ref[rs, hs].astype(BF16))

    for step in range(dec):
        for i in range(c_ref.shape[0]):
            c_ref[i, pl.ds(step, n_seq, stride=dec), :] = conv[step][:, i * LANES:(i + 1) * LANES]
    for g in range(n_seq):
        om_ref[g * dec:(g + 1) * dec, :] = jnp.concatenate(om[g * MEM_HEADS:(g + 1) * MEM_HEADS], axis=1)


def _sample_out_kernel(x_ref, c_ref, o_ref, om_ref, hgate_ref, gates_ref, post_g_ref, conv_b_ref,
                       ln_g_ref, ln_b_ref, hg_g_ref, w_conv_out_ref, w_hg_out_ref, w_mem_out_ref,
                       w_out_ref, ffn_pre_g_ref, ffn_post_g_ref, w_gate_ref, w_up_ref, w_down_ref, y_ref):
    c = _conv_ln_silu(_load_lane_chunks(c_ref), conv_b_ref[...], ln_g_ref[...], ln_b_ref[...])
    p_conv = _dot(c.astype(BF16), w_conv_out_ref[...])
    o = _head_rms(o_ref[...], hg_g_ref[...]) * hgate_ref[...]
    p_hg = _dot(o.astype(BF16), w_hg_out_ref[...])
    p_mem = _dot(om_ref[...].astype(BF16), w_mem_out_ref[...])
    x1 = _merge_out(lambda i: gates_ref[:, i * D_MODEL:(i + 1) * D_MODEL], x_ref[...],
                    (p_conv, p_hg, p_mem), w_out_ref, post_g_ref[...])
    y_ref[...] = _ffn_block(x1, ffn_pre_g_ref[...], ffn_post_g_ref[...], w_gate_ref, w_up_ref, w_down_ref)


def _row_spec(tile, width):
    return pl.BlockSpec((tile, width), lambda i: (i, 0))


def _sample_in(x2d, pre_g, lb_logits, w_in):
    rows = x2d.shape[0]
    tile = SAMPLE_TILE
    seqs = tile // SUBLANES
    wide = [D_HGRN, D_HGRN, D_HGRN, D_HGRN]
    return pl.pallas_call(
        _sample_in_kernel,
        grid=(rows // tile,),
        in_specs=[_row_spec(tile, D_MODEL), _const_spec((1, D_MODEL)), _const_spec((2, D_HGRN)),
                  _const_spec(w_in.shape)],
        out_specs=[_lane_chunk_spec(tile, D_CONV)] + [_row_spec(tile, w) for w in wide] + [
            _row_spec(seqs, D_HGRN),
            pl.BlockSpec((seqs, MEM_HEADS * SUBLANES, MEM_HEAD_DIM), lambda i: (i, 0, 0)),
            _row_spec(tile, D_HGRN),
            _row_spec(tile, 3 * D_MODEL),
        ],
        out_shape=[jax.ShapeDtypeStruct((D_CONV // LANES, rows, LANES), F32)]
        + [jax.ShapeDtypeStruct((rows, w), F32) for w in wide] + [
            jax.ShapeDtypeStruct((rows // SUBLANES, D_HGRN), F32),
            jax.ShapeDtypeStruct((rows // SUBLANES, MEM_HEADS * SUBLANES, MEM_HEAD_DIM), F32),
            jax.ShapeDtypeStruct((rows, D_HGRN), F32),
            jax.ShapeDtypeStruct((rows, 3 * D_MODEL), F32),
        ],
        compiler_params=pltpu.CompilerParams(
            dimension_semantics=("parallel",), vmem_limit_bytes=VMEM_LIMIT),
        name="sample_in",
    )(x2d, pre_g, lb_logits, w_in)


def _sample_seq(u, qd, ki, ke, v, blast, q4, conv_rows, state_hgrn, k2, v2, conv_w):
    n = conv_rows.shape[1]
    g = SAMPLE_GROUP
    rows = g * SUBLANES
    kv_rows = k2.shape[1]
    seq_block = lambda *tail: pl.BlockSpec((g,) + tail, lambda i: (i,) + (0,) * len(tail))
    conv_block = pl.BlockSpec((CONV_PREFIX, g, D_CONV), lambda i: (0, i, 0))
    return pl.pallas_call(
        _sample_seq_kernel,
        grid=(n // g,),
        in_specs=[_lane_chunk_spec(rows, D_CONV)] + [_row_spec(rows, D_HGRN)] * 4 + [
            _row_spec(g, D_HGRN),
            seq_block(MEM_HEADS * SUBLANES, MEM_HEAD_DIM),
            conv_block,
            seq_block(HG_HEADS, HG_DK, HG_DV),
            seq_block(kv_rows, MEM_HEAD_DIM),
            seq_block(kv_rows, MEM_HEAD_DIM),
            _const_spec((CONV_WIDTH, D_CONV)),
        ],
        out_specs=[_lane_chunk_spec(rows, D_CONV), _row_spec(rows, D_HGRN), _row_spec(rows, D_MEM),
                   conv_block, seq_block(HG_HEADS, HG_DK, HG_DV)],
        out_shape=[
            jax.ShapeDtypeStruct((D_CONV // LANES, n * SUBLANES, LANES), F32),
            jax.ShapeDtypeStruct((n * SUBLANES, D_HGRN), F32),
            jax.ShapeDtypeStruct((n * SUBLANES, D_MEM), F32),
            jax.ShapeDtypeStruct((CONV_PREFIX, n, D_CONV), F32),
            jax.ShapeDtypeStruct((n, HG_HEADS, HG_DK, HG_DV), F32),
        ],
        compiler_params=pltpu.CompilerParams(
            dimension_semantics=("parallel",), vmem_limit_bytes=VMEM_LIMIT),
        name="sample_seq",
    )(u, qd, ki, ke, v, blast, q4, conv_rows, state_hgrn, k2, v2, conv_w)


def _sample_out(x2d, c, o, om, hgate, gates, vectors, weights, ffn_vectors, ffn_weights):
    rows = x2d.shape[0]
    tile = SAMPLE_TILE
    acts = (x2d, c, o, om, hgate, gates)
    consts = tuple(vectors) + tuple(weights) + tuple(ffn_vectors) + tuple(ffn_weights)
    return pl.pallas_call(
        _sample_out_kernel,
        grid=(rows // tile,),
        in_specs=[_lane_chunk_spec(tile, D_CONV) if a is c else _row_spec(tile, a.shape[1]) for a in acts]
        + [_const_spec(a.shape) for a in consts],
        out_specs=_row_spec(tile, D_MODEL),
        out_shape=jax.ShapeDtypeStruct((rows, D_MODEL), F32),
        compiler_params=pltpu.CompilerParams(
            dimension_semantics=("parallel",), vmem_limit_bytes=VMEM_LIMIT),
        name="sample_out",
    )(*acts, *consts)


def _ffn_kernel(x_ref, pre_g_ref, post_g_ref, w_gate_ref, w_up_ref, w_down_ref, y_ref):
    y_ref[...] = _ffn_block(x_ref[...], pre_g_ref[...], post_g_ref[...], w_gate_ref, w_up_ref, w_down_ref)


def _ffn(x2d, pre_g, post_g, w_gate, w_up, w_down):
    rows = x2d.shape[0]
    tile = min(FFN_TILE, rows)
    return pl.pallas_call(
        _ffn_kernel,
        grid=(rows // tile,),
        in_specs=[
            pl.BlockSpec((tile, D_MODEL), lambda i: (i, 0)),
            _const_spec((1, D_MODEL)), _const_spec((1, D_MODEL)),
            _const_spec(w_gate.shape), _const_spec(w_up.shape), _const_spec(w_down.shape),
        ],
        out_specs=pl.BlockSpec((tile, D_MODEL), lambda i: (i, 0)),
        out_shape=jax.ShapeDtypeStruct((rows, D_MODEL), F32),
        compiler_params=pltpu.CompilerParams(
            dimension_semantics=("parallel",), vmem_limit_bytes=VMEM_LIMIT),
        name="ffn",
    )(x2d, pre_g, post_g, w_gate, w_up, w_down)


def kernel(x_prompt, x_sample, mem_prompt, state_conv, state_hgrn, cache_mem_k, cache_mem_v, norm_pre_mix, norm_post_mix, norm_pre_ffn, norm_post_ffn, w_in, conv_w, conv_b, conv_ln_g, conv_ln_b, w_conv_out, hg_lb_logits, hg_norm_g, w_hg_out, mem_norm_g, w_mem_kv, w_mem_out, w_out, w_ffn_gate, w_ffn_up, w_ffn_down):
    depth = w_in.shape[0]
    assert depth == 1 and hg_lb_logits.shape[0] == 2, "single-layer step"
    batch, seq, _ = x_prompt.shape
    n_dec, dec, _ = x_sample.shape
    n_mem = mem_prompt.shape[1]
    assert seq % PROMPT_TILE == 0 and n_dec % SAMPLE_GROUP == 0 and dec == SUBLANES
    assert (n_dec * dec) % SAMPLE_TILE == 0

    vectors = (norm_pre_mix, norm_post_mix, conv_w[0], conv_b, conv_ln_g, conv_ln_b,
               hg_lb_logits, jnp.tile(hg_norm_g, (1, HG_HEADS)))
    weights = tuple(w[0].astype(BF16) for w in (w_in, w_conv_out, w_hg_out, w_mem_out, w_out))

    mk, mv, kb, vb = _memory_kv(mem_prompt.reshape(batch * n_mem, D_MODEL), mem_norm_g,
                                w_mem_kv[0].astype(BF16), n_mem)

    xp, conv_p, hg_p, *ffn_weights = _prompt_mixer(
        x_prompt, kb.reshape(batch, n_mem, D_MEM), vb.reshape(batch, n_mem, D_MEM), vectors, weights,
        (w_ffn_gate[0], w_ffn_up[0], w_ffn_down[0]))
    yp = _ffn(xp.reshape(batch * seq, D_MODEL), norm_pre_ffn, norm_post_ffn, *ffn_weights)

    xs2d = x_sample.reshape(n_dec * dec, D_MODEL)
    u, qd, ki, ke, v, blast, q4, hgate, gates = _sample_in(xs2d, norm_pre_mix, hg_lb_logits, weights[0])
    k2 = cache_mem_k[0].reshape(n_dec, n_mem * MEM_HEADS, MEM_HEAD_DIM)
    v2 = cache_mem_v[0].reshape(n_dec, n_mem * MEM_HEADS, MEM_HEAD_DIM)
    c, o, om, conv_rows, hg_s = _sample_seq(u, qd, ki, ke, v, blast, q4,
                                            jnp.transpose(state_conv[0], (1, 0, 2)), state_hgrn[0],
                                            k2, v2, conv_w[0])
    conv_s = jnp.transpose(conv_rows, (1, 0, 2))
    ys = _sample_out(xs2d, c, o, om, hgate, gates,
                     (norm_post_mix, conv_b, conv_ln_g, conv_ln_b, vectors[-1]), weights[1:],
                     (norm_pre_ffn, norm_post_ffn), ffn_weights)

    kv_shape = (1, batch, n_mem, MEM_HEADS, MEM_HEAD_DIM)
    return (yp.reshape(batch, seq, D_MODEL), ys.reshape(n_dec, dec, D_MODEL),
            conv_p[None], hg_p[None], mk.reshape(kv_shape), mv.reshape(kv_shape),
            conv_s[None], hg_s[None])
```

```python
from typing import NamedTuple

import jax
import jax.numpy as jnp
from jax import lax
from jax.experimental import pallas as pl
from jax.experimental.pallas import tpu as pltpu

D_MODEL = 1024
D_CONV = 512
CONV_WIDTH = 31
CONV_PREFIX = CONV_WIDTH - 1
HG_HEADS = 4
HG_DK = 128
HG_DV = 128
D_HGRN = HG_HEADS * HG_DK
MEM_HEADS = 4
MEM_HEAD_DIM = 128
D_MEM = MEM_HEADS * MEM_HEAD_DIM
CHUNK = 32
EPS = 1e-6
NEG_LOG2E = -1.4426950408889634

COL_CA = 0
COL_CB = COL_CA + D_CONV
COL_HQ = COL_CB + D_CONV
COL_HF = COL_HQ + D_HGRN
COL_HI = COL_HF + D_HGRN
COL_HGATE = COL_HI + D_HGRN
COL_MQ = COL_HGATE + D_HGRN
COL_GATES = COL_MQ + D_MEM

SUBLANES = 8
LANES = 128
VMEM_LIMIT = 56 * 1024 * 1024

PROMPT_TILE = 256
PROMPT_STEP_TILES = 1
SAMPLE_GROUP = 8
SAMPLE_TILE = 256
FFN_TILE = 512
CONV_ROWS = 32

BF16 = jnp.bfloat16
F32 = jnp.float32


def _dot(a, b):
    return jnp.dot(a, b, preferred_element_type=F32)


def _dot_nt(a, b):
    return lax.dot_general(a, b, (((1,), (1,)), ((), ())), preferred_element_type=F32)


def _dot_tn(a, b):
    return lax.dot_general(a, b, (((0,), (0,)), ((), ())), preferred_element_type=F32)


def _rms(x, gain):
    return x * lax.rsqrt(jnp.mean(x * x, axis=-1, keepdims=True) + EPS) * gain


def _sigmoid(x):
    return 1.0 / (1.0 + jnp.exp2(x * NEG_LOG2E))


def _silu(x):
    return x * _sigmoid(x)


def _forget_terms(hf, lb):
    sig = _sigmoid(hf)
    return jnp.log(lb + (1.0 - lb) * sig), (1.0 - lb) * (1.0 - sig)


def _lower_bound(lb_logits):
    m = jnp.max(lb_logits, axis=0, keepdims=True)
    e = jnp.exp(lb_logits - m)
    return e[0:1] / jnp.sum(e, axis=0, keepdims=True)


def _segment_cumsum(x, seg):
    pos = lax.broadcasted_iota(jnp.int32, x.shape, 0) & (seg - 1)
    s = 1
    while s < seg:
        x = x + jnp.where(pos >= s, pltpu.roll(x, s, axis=0), 0.0)
        s *= 2
    return x


def _conv_ln_silu(c, conv_b, ln_g, ln_b):
    c = c + conv_b
    mu = jnp.mean(c, axis=-1, keepdims=True)
    d = c - mu
    var = jnp.mean(d * d, axis=-1, keepdims=True)
    return _silu(d * lax.rsqrt(var + EPS) * ln_g + ln_b)


def _hgrn_gates(hq, hf, lb, seg):
    q = _silu(hq)
    logf, k = _forget_terms(hf, lb)
    b = _segment_cumsum(logf, seg)
    rows = b.shape[0]
    b3 = b.reshape(rows // seg, seg, D_HGRN)
    b_last3 = b3[:, seg - 1:seg, :]
    rest = (b_last3 - b3).reshape(rows, D_HGRN)
    q_dec = q * jnp.exp(b)
    k_inv = k * jnp.exp(-b)
    k_end = k * jnp.exp(rest)
    return q_dec, k_inv, k_end, b_last3.reshape(rows // seg, D_HGRN)


class _TileLevels(NamedTuple):
    q_mid: jax.Array
    k_mid: jax.Array
    q_64: jax.Array
    k_64: jax.Array
    q_128: jax.Array
    k_128: jax.Array
    q_tile: jax.Array
    k_tile: jax.Array
    decay: jax.Array


def _hgrn_tile_levels(hq, hf, lb):
    tl = hq.shape[0]
    n_blocks = 4
    block = 2 * CHUNK
    assert tl == n_blocks * block
    q = _silu(hq)
    logf, k = _forget_terms(hf, lb)
    b = _segment_cumsum(logf, CHUNK)
    chunks = [b[n * CHUNK:(n + 1) * CHUNK] for n in range(2 * n_blocks)]
    totals = [c[CHUNK - 1:CHUNK] for c in chunks]
    e = jnp.concatenate([c - totals[n] if n % 2 == 0 else c for n, c in enumerate(chunks)], axis=0)
    q_mid = q * jnp.exp(e)
    k_mid = k * jnp.exp(-e)
    first = [totals[2 * j] for j in range(n_blocks)]
    second = [totals[2 * j + 1] for j in range(n_blocks)]
    both = [first[j] + second[j] for j in range(n_blocks)]

    def scaled(x, blocks, log_scales):
        return jnp.concatenate(
            [x[j * block:(j + 1) * block] * jnp.exp(s) for j, s in zip(blocks, log_scales)], axis=0)

    every = range(n_blocks)
    return _TileLevels(
        q_mid=q_mid, k_mid=k_mid,
        q_64=scaled(q_mid, every, first),
        k_64=scaled(k_mid, every, second),
        q_128=scaled(q_mid, (2, 3), (first[2], first[3] + both[2])),
        k_128=scaled(k_mid, (0, 1), (second[0] + both[1], second[1])),
        q_tile=scaled(q_mid, every, [first[j] + sum(both[:j], 0.0) for j in every]),
        k_tile=scaled(k_mid, every, [second[j] + sum(both[j + 1:], 0.0) for j in every]),
        decay=jnp.exp(sum(both[1:], both[0])))


def _head_rms(o, gain4):
    parts = []
    for h in range(HG_HEADS):
        oh = o[:, h * HG_DV:(h + 1) * HG_DV]
        parts.append(oh * lax.rsqrt(jnp.mean(oh * oh, axis=-1, keepdims=True) + EPS))
    return jnp.concatenate(parts, axis=-1) * gain4


def _branch_gate(hb, w_in_ref, i):
    c0 = COL_GATES + i * D_MODEL
    return _sigmoid(_dot(hb, w_in_ref[:, c0:c0 + D_MODEL]))


def _merge_out(gate, x, branches, w_out_ref, post_g):
    acc = None
    for i, p in enumerate(branches):
        acc = gate(i) * p if acc is None else acc + gate(i) * p
    m = _dot(acc.astype(BF16), w_out_ref[...])
    return x + _rms(m, post_g)


def _ffn_block(x, pre_g, post_g, w_gate_ref, w_up_ref, w_down_ref):
    hb = _rms(x, pre_g).astype(BF16)
    f = _silu(_dot(hb, w_gate_ref[...])) * _dot(hb, w_up_ref[...])
    d = _dot(f.astype(BF16), w_down_ref[...])
    return x + _rms(d, post_g)


def _cast_block_rows(rows, n_steps):
    bf16_sublanes = 2 * SUBLANES
    for n_blocks in range(min(n_steps, rows // bf16_sublanes), 0, -1):
        if rows % n_blocks == 0 and (rows // n_blocks) % bf16_sublanes == 0:
            return rows // n_blocks
    raise ValueError(f"no bf16-tileable split of {rows} rows")


def _cast_specs(weights, n_steps, step_of):
    specs = []
    for w in weights:
        rows_blk = _cast_block_rows(w.shape[0], n_steps)
        last = w.shape[0] // rows_blk - 1
        specs.append(pl.BlockSpec((rows_blk, w.shape[1]),
                                  lambda *idx, last=last: (jnp.minimum(step_of(*idx), last), 0)))
    return specs


def _cast_blocks(srcs, dsts):
    for src, dst in zip(srcs, dsts):
        dst[...] = src[...].astype(BF16)


def _memory_kv_kernel(mem_ref, g_ref, w_ref, *rest):
    n_cast = (len(rest) - 5) // 2
    k_rows_ref, v_rows_ref, k_ref, v_ref = rest[n_cast:n_cast + 4]
    wb_ref = rest[-1]
    _cast_blocks(rest[:n_cast], rest[n_cast + 4:-1])

    @pl.when(pl.program_id(0) == 0)
    def _():
        wb_ref[...] = w_ref[...].astype(BF16)

    m = _rms(mem_ref[...], g_ref[...]).astype(BF16)
    tokens = mem_ref.shape[0]
    for full, rows_ref, op_ref in ((_dot(m, wb_ref[:, :D_MEM]), k_rows_ref, k_ref),
                                   (_dot(m, wb_ref[:, D_MEM:]), v_rows_ref, v_ref)):
        op_ref[...] = full.astype(BF16)
        for h in range(MEM_HEADS):
            rows_ref[pl.ds(h, tokens, stride=MEM_HEADS), :] = full[:, h * MEM_HEAD_DIM:(h + 1) * MEM_HEAD_DIM]


def _memory_kv(mem2d, gain, w_kv, tile, later_weights):
    rows = mem2d.shape[0]
    n_steps = rows // tile
    cast_specs = _cast_specs(later_weights, n_steps, lambda i: i)
    return pl.pallas_call(
        _memory_kv_kernel,
        grid=(n_steps,),
        in_specs=[
            pl.BlockSpec((tile, D_MODEL), lambda i: (i, 0)),
            pl.BlockSpec((1, D_MODEL), lambda i: (0, 0)),
            _const_spec(w_kv.shape),
        ] + cast_specs,
        out_specs=[pl.BlockSpec((tile * MEM_HEADS, MEM_HEAD_DIM), lambda i: (i, 0))] * 2
        + [pl.BlockSpec((tile, D_MEM), lambda i: (i, 0))] * 2 + cast_specs,
        out_shape=[jax.ShapeDtypeStruct((rows * MEM_HEADS, MEM_HEAD_DIM), F32)] * 2
        + [jax.ShapeDtypeStruct((rows, D_MEM), BF16)] * 2
        + [jax.ShapeDtypeStruct(w.shape, BF16) for w in later_weights],
        scratch_shapes=[pltpu.VMEM(w_kv.shape, BF16)],
        compiler_params=pltpu.CompilerParams(
            dimension_semantics=("arbitrary",), vmem_limit_bytes=VMEM_LIMIT),
        name="memory_kv",
    )(mem2d, gain, w_kv, *later_weights)


def _prompt_mixer_kernel(x_ref, kb_ref, vb_ref, pre_g_ref, post_g_ref, conv_w_ref, conv_b_ref,
                         ln_g_ref, ln_b_ref, lb_logits_ref, hg_g_ref, w_in_ref, w_conv_out_ref,
                         w_hg_out_ref, w_mem_out_ref, w_out_ref, *rest):
    n_cast = (len(rest) - 6) // 2
    cast_src = rest[:n_cast]
    y_ref, new_conv_ref, new_hg_ref = rest[n_cast:n_cast + 3]
    cast_dst = rest[n_cast + 3:2 * n_cast + 3]
    shift_ref, tail_ref, state_t_ref = rest[2 * n_cast + 3:]
    _cast_blocks(cast_src, cast_dst)
    t = pl.program_id(1)
    last_t = pl.num_programs(1) - 1
    tl = PROMPT_TILE
    n_tiles = x_ref.shape[0] // tl

    @pl.when(t == 0)
    def _():
        tail_ref[...] = jnp.zeros_like(tail_ref)
        state_t_ref[...] = jnp.zeros_like(state_t_ref)

    lb = _lower_bound(lb_logits_ref[...])
    half = tl // 2
    row = lax.broadcasted_iota(jnp.int32, (tl, tl), 0)
    col = lax.broadcasted_iota(jnp.int32, (tl, tl), 1)
    span = row ^ col
    near = (span < 2 * CHUNK) & (col <= row)
    mid = (span < 4 * CHUNK) & (col < row)

    def tile(i, u_tail):
        rows = slice(i * tl, (i + 1) * tl)
        shift = shift_ref.at[i]
        x = x_ref[rows, :]
        hb = _rms(x, pre_g_ref[...]).astype(BF16)

        def proj(c0, c1):
            return _dot(hb, w_in_ref[:, c0:c1])

        u = proj(COL_CA, COL_CB) * _sigmoid(proj(COL_CB, COL_HQ))
        shift[0, 0:CONV_ROWS, :] = u_tail
        shift[0, CONV_ROWS:CONV_ROWS + tl, :] = u
        for r in range(1, SUBLANES):
            shift[r, 0:tl + CONV_ROWS - SUBLANES, :] = shift[0, r:r + tl + CONV_ROWS - SUBLANES, :]

        conv_blocks = []
        for base in range(0, tl, CONV_ROWS):
            acc = None
            for j in range(CONV_WIDTH):
                s = j + (CONV_ROWS - CONV_PREFIX)
                r, a = s % SUBLANES, s // SUBLANES
                lo = base + a * SUBLANES
                term = shift[r, lo:lo + CONV_ROWS, :] * conv_w_ref[j:j + 1, :]
                acc = term if acc is None else acc + term
            conv_blocks.append(acc)
        c = _conv_ln_silu(jnp.concatenate(conv_blocks, axis=0),
                          conv_b_ref[...], ln_g_ref[...], ln_b_ref[...]).astype(BF16)

        lv = _hgrn_tile_levels(proj(COL_HQ, COL_HF), proj(COL_HF, COL_HI), lb)
        lvb = _TileLevels(*[a.astype(BF16) for a in lv[:-1]], lv.decay)
        vb16 = proj(COL_HI, COL_HGATE).astype(BF16)
        hgate = _silu(proj(COL_HGATE, COL_MQ))
        mq = proj(COL_MQ, COL_GATES).astype(BF16)
        gates = [_branch_gate(hb, w_in_ref, g) for g in range(3)]

        head = lambda h: slice(h * MEM_HEAD_DIM, (h + 1) * MEM_HEAD_DIM)
        s = jnp.concatenate([_dot_nt(mq[:, head(h)], kb_ref[:, head(h)]) for h in range(MEM_HEADS)],
                            axis=0) * (MEM_HEAD_DIM ** -0.5)
        e = jnp.exp(s - jnp.max(s, axis=-1, keepdims=True))
        p = (e / jnp.sum(e, axis=-1, keepdims=True)).astype(BF16)
        om = [_dot(p[h * tl:(h + 1) * tl], vb_ref[:, head(h)]) for h in range(MEM_HEADS)]

        o_heads = []
        for h in range(HG_HEADS):
            hs = slice(h * HG_DK, (h + 1) * HG_DK)
            vh = vb16[:, hs]
            s_near = _dot_nt(lvb.q_mid[:, hs], lvb.k_mid[:, hs])
            s_mid = _dot_nt(lvb.q_64[:, hs], lvb.k_64[:, hs])
            s_far = _dot_nt(lvb.q_128[:, hs], lvb.k_128[:, hs])
            base = jnp.where(near, s_near, jnp.where(mid, s_mid, 0.0))
            scores = jnp.concatenate(
                [base[:half], jnp.concatenate([s_far, base[half:, half:]], axis=1)], axis=0)
            st = state_t_ref[h]
            o_heads.append(_dot(scores.astype(BF16), vh) + _dot_nt(lvb.q_tile[:, hs], st.astype(BF16)))
            state_t_ref[h] = st * lvb.decay[:, hs] + _dot_tn(vh, lvb.k_tile[:, hs])
        o = _head_rms(jnp.concatenate(o_heads, axis=-1), hg_g_ref[...]) * hgate

        p_conv = _dot(c, w_conv_out_ref[...])
        p_mem = _dot(jnp.concatenate(om, axis=-1).astype(BF16), w_mem_out_ref[...])
        p_hg = _dot(o.astype(BF16), w_hg_out_ref[...])
        y_ref[rows, :] = _merge_out(lambda g: gates[g], x, (p_conv, p_hg, p_mem), w_out_ref,
                                    post_g_ref[...])

        return u[tl - CONV_ROWS:]

    u_tail = tail_ref[...]
    for i in range(n_tiles):
        u_tail = tile(i, u_tail)
    tail_ref[...] = u_tail

    @pl.when(t == last_t)
    def _():
        new_conv_ref[...] = tail_ref[CONV_ROWS - CONV_PREFIX:CONV_ROWS, :]
        for h in range(HG_HEADS):
            new_hg_ref[h] = state_t_ref[h].T


def _const_spec(shape):
    zeros = (0,) * len(shape)
    return pl.BlockSpec(shape, lambda *_: zeros, pipeline_mode=pl.Buffered(1))


def _mixer_weight_specs(w_in, w_conv_out, w_hg_out, w_mem_out, w_out):
    return [_const_spec(w.shape) for w in (w_in, w_conv_out, w_hg_out, w_mem_out, w_out)]


def _vector_specs():
    return [
        _const_spec((1, D_MODEL)), _const_spec((1, D_MODEL)),
        _const_spec((CONV_WIDTH, D_CONV)), _const_spec((1, D_CONV)),
        _const_spec((1, D_CONV)), _const_spec((1, D_CONV)),
        _const_spec((2, D_HGRN)), _const_spec((1, D_HGRN)),
    ]


def _prompt_mixer(x, mk, mv, vectors, weights, later_weights):
    batch, seq, _ = x.shape
    n_mem = mk.shape[1]
    tl = PROMPT_TILE * PROMPT_STEP_TILES
    n_t = seq // tl
    cast_specs = _cast_specs(later_weights, batch * n_t, lambda b, t: b * n_t + t)
    return pl.pallas_call(
        _prompt_mixer_kernel,
        grid=(batch, n_t),
        in_specs=[
            pl.BlockSpec((None, tl, D_MODEL), lambda b, t: (b, t, 0)),
            pl.BlockSpec((None, n_mem, D_MEM), lambda b, t: (b, 0, 0)),
            pl.BlockSpec((None, n_mem, D_MEM), lambda b, t: (b, 0, 0)),
        ] + _vector_specs() + _mixer_weight_specs(*weights) + cast_specs,
        out_specs=[
            pl.BlockSpec((None, tl, D_MODEL), lambda b, t: (b, t, 0)),
            pl.BlockSpec((None, CONV_PREFIX, D_CONV), lambda b, t: (b, 0, 0)),
            pl.BlockSpec((None, HG_HEADS, HG_DK, HG_DV), lambda b, t: (b, 0, 0, 0)),
        ] + cast_specs,
        out_shape=[
            jax.ShapeDtypeStruct((batch, seq, D_MODEL), F32),
            jax.ShapeDtypeStruct((batch, CONV_PREFIX, D_CONV), F32),
            jax.ShapeDtypeStruct((batch, HG_HEADS, HG_DK, HG_DV), F32),
        ] + [jax.ShapeDtypeStruct(w.shape, BF16) for w in later_weights],
        scratch_shapes=[
            pltpu.VMEM((PROMPT_STEP_TILES, SUBLANES, PROMPT_TILE + CONV_ROWS, D_CONV), F32),
            pltpu.VMEM((CONV_ROWS, D_CONV), F32),
            pltpu.VMEM((HG_HEADS, HG_DV, HG_DK), F32),
        ],
        compiler_params=pltpu.CompilerParams(
            dimension_semantics=("arbitrary", "arbitrary"), vmem_limit_bytes=VMEM_LIMIT),
        name="prompt_mixer",
    )(x, mk, mv, *vectors, *weights, *later_weights)


def _store_lane_chunks(ref, x):
    for i in range(ref.shape[0]):
        ref[i] = x[:, i * LANES:(i + 1) * LANES]


def _load_lane_chunks(ref, rows=slice(None)):
    return jnp.concatenate([ref[i, rows, :] for i in range(ref.shape[0])], axis=1)


def _lane_chunk_spec(tile, width):
    return pl.BlockSpec((width // LANES, tile, LANES), lambda i: (0, i, 0))


def _sample_in_kernel(x_ref, pre_g_ref, lb_logits_ref, w_in_ref,
                      u_ref, qd_ref, ki_ref, ke_ref, v_ref, blast_ref, q4_ref, hgate_ref, gates_ref):
    n_seq = x_ref.shape[0] // SUBLANES
    hb = _rms(x_ref[...], pre_g_ref[...]).astype(BF16)
    _store_lane_chunks(
        u_ref, _dot(hb, w_in_ref[:, COL_CA:COL_CB]) * _sigmoid(_dot(hb, w_in_ref[:, COL_CB:COL_HQ])))
    lb = _lower_bound(lb_logits_ref[...])
    q_dec, k_inv, k_end, b_last = _hgrn_gates(
        _dot(hb, w_in_ref[:, COL_HQ:COL_HF]), _dot(hb, w_in_ref[:, COL_HF:COL_HI]), lb, SUBLANES)
    qd_ref[...] = q_dec
    ki_ref[...] = k_inv
    ke_ref[...] = k_end
    blast_ref[...] = b_last
    v_ref[...] = _dot(hb, w_in_ref[:, COL_HI:COL_HGATE])
    hgate_ref[...] = _silu(_dot(hb, w_in_ref[:, COL_HGATE:COL_MQ]))
    mq = _dot(hb, w_in_ref[:, COL_MQ:COL_GATES])
    for h in range(MEM_HEADS):
        q4_ref[:, h * SUBLANES:(h + 1) * SUBLANES, :] = (
            mq[:, h * MEM_HEAD_DIM:(h + 1) * MEM_HEAD_DIM].reshape(n_seq, SUBLANES, MEM_HEAD_DIM))
    for i in range(3):
        gates_ref[:, i * D_MODEL:(i + 1) * D_MODEL] = _branch_gate(hb, w_in_ref, i)


def _sample_seq_kernel(u_ref, qd_ref, ki_ref, ke_ref, v_ref, blast_ref, q4_ref, sc_ref, sh_ref,
                       k2_ref, v2_ref, conv_w_ref,
                       c_ref, o_ref, om_ref, new_conv_ref, new_hg_ref):
    n_seq, dec = sc_ref.shape[1], SUBLANES
    rows = n_seq * dec

    assert dec == 8 and MEM_HEADS == 4
    q_rows = MEM_HEADS * dec
    s = jnp.concatenate([_dot_nt(q4_ref[g].astype(BF16), k2_ref[g].astype(BF16)) for g in range(n_seq)],
                        axis=0) * (MEM_HEAD_DIM ** -0.5)
    q_head = (lax.broadcasted_iota(jnp.int32, s.shape, 0) >> 3) & 3
    k_head = lax.broadcasted_iota(jnp.int32, s.shape, 1) & 3
    s = jnp.where(q_head == k_head, s, -jnp.inf)
    e = jnp.exp(s - jnp.max(s, axis=-1, keepdims=True))
    p = (e / jnp.sum(e, axis=-1, keepdims=True)).astype(BF16)
    om = []
    for g in range(n_seq):
        om_g = _dot(p[g * q_rows:(g + 1) * q_rows], v2_ref[g].astype(BF16))
        om.extend(om_g[h * dec:(h + 1) * dec] for h in range(MEM_HEADS))

    u_slabs = [_load_lane_chunks(u_ref, pl.ds(step, n_seq, stride=dec)) for step in range(dec)]
    conv = [None] * dec
    for s in range(CONV_PREFIX + dec):
        slab = sc_ref[s] if s < CONV_PREFIX else u_slabs[s - CONV_PREFIX]
        if s >= dec:
            new_conv_ref[s - dec] = slab
        for step in range(max(0, s - CONV_PREFIX), min(dec, s + 1)):
            term = slab * conv_w_ref[s - step:s - step + 1, :]
            conv[step] = term if conv[step] is None else conv[step] + term

    row = lax.broadcasted_iota(jnp.int32, (rows, rows), 0)
    col = lax.broadcasted_iota(jnp.int32, (rows, rows), 1)
    causal = ((row ^ col) < dec) & (col <= row)
    decay = jnp.exp(blast_ref[...])
    for h in range(HG_HEADS):
        hs = slice(h * HG_DK, (h + 1) * HG_DK)
        scores = jnp.where(causal, _dot_nt(qd_ref[:, hs].astype(BF16), ki_ref[:, hs].astype(BF16)), 0.0)
        o_intra = _dot(scores.astype(BF16), v_ref[:, hs].astype(BF16))
        for g in range(n_seq):
            rs = slice(g * dec, (g + 1) * dec)
            s0 = sh_ref[g, h]
            o_ref[rs, hs] = o_intra[rs] + _dot(qd_ref[rs, hs].astype(BF16), s0.astype(BF16))
            decay_col = jnp.broadcast_to(decay[g:g + 1, hs], (HG_DV, HG_DK)).T
            new_hg_ref[g, h] = decay_col * s0 + _dot_tn(ke_ref[rs, hs].astype(BF16),
                                                        v_ref[rs, hs].astype(BF16))

    for step in range(dec):
        for i in range(c_ref.shape[0]):
            c_ref[i, pl.ds(step, n_seq, stride=dec), :] = conv[step][:, i * LANES:(i + 1) * LANES]
    for g in range(n_seq):
        om_ref[g * dec:(g + 1) * dec, :] = jnp.concatenate(om[g * MEM_HEADS:(g + 1) * MEM_HEADS], axis=1)


def _sample_out_kernel(x_ref, c_ref, o_ref, om_ref, hgate_ref, gates_ref, post_g_ref, conv_b_ref,
                       ln_g_ref, ln_b_ref, hg_g_ref, w_conv_out_ref, w_hg_out_ref, w_mem_out_ref,
                       w_out_ref, ffn_pre_g_ref, ffn_post_g_ref, w_gate_ref, w_up_ref, w_down_ref, y_ref):
    c = _conv_ln_silu(_load_lane_chunks(c_ref), conv_b_ref[...], ln_g_ref[...], ln_b_ref[...])
    p_conv = _dot(c.astype(BF16), w_conv_out_ref[...])
    o = _head_rms(o_ref[...], hg_g_ref[...]) * hgate_ref[...]
    p_hg = _dot(o.astype(BF16), w_hg_out_ref[...])
    p_mem = _dot(om_ref[...].astype(BF16), w_mem_out_ref[...])
    x1 = _merge_out(lambda i: gates_ref[:, i * D_MODEL:(i + 1) * D_MODEL], x_ref[...],
                    (p_conv, p_hg, p_mem), w_out_ref, post_g_ref[...])
    y_ref[...] = _ffn_block(x1, ffn_pre_g_ref[...], ffn_post_g_ref[...], w_gate_ref, w_up_ref, w_down_ref)


def _row_spec(tile, width):
    return pl.BlockSpec((tile, width), lambda i: (i, 0))


def _sample_in(x2d, pre_g, lb_logits, w_in):
    rows = x2d.shape[0]
    tile = SAMPLE_TILE
    seqs = tile // SUBLANES
    wide = [D_HGRN, D_HGRN, D_HGRN, D_HGRN]
    return pl.pallas_call(
        _sample_in_kernel,
        grid=(rows // tile,),
        in_specs=[_row_spec(tile, D_MODEL), _const_spec((1, D_MODEL)), _const_spec((2, D_HGRN)),
                  _const_spec(w_in.shape)],
        out_specs=[_lane_chunk_spec(tile, D_CONV)] + [_row_spec(tile, w) for w in wide] + [
            _row_spec(seqs, D_HGRN),
            pl.BlockSpec((seqs, MEM_HEADS * SUBLANES, MEM_HEAD_DIM), lambda i: (i, 0, 0)),
            _row_spec(tile, D_HGRN),
            _row_spec(tile, 3 * D_MODEL),
        ],
        out_shape=[jax.ShapeDtypeStruct((D_CONV // LANES, rows, LANES), F32)]
        + [jax.ShapeDtypeStruct((rows, w), F32) for w in wide] + [
            jax.ShapeDtypeStruct((rows // SUBLANES, D_HGRN), F32),
            jax.ShapeDtypeStruct((rows // SUBLANES, MEM_HEADS * SUBLANES, MEM_HEAD_DIM), F32),
            jax.ShapeDtypeStruct((rows, D_HGRN), F32),
            jax.ShapeDtypeStruct((rows, 3 * D_MODEL), F32),
        ],
        compiler_params=pltpu.CompilerParams(
            dimension_semantics=("parallel",), vmem_limit_bytes=VMEM_LIMIT),
        name="sample_in",
    )(x2d, pre_g, lb_logits, w_in)


def _sample_seq(u, qd, ki, ke, v, blast, q4, conv_rows, state_hgrn, k2, v2, conv_w):
    n = conv_rows.shape[1]
    g = SAMPLE_GROUP
    rows = g * SUBLANES
    kv_rows = k2.shape[1]
    seq_block = lambda *tail: pl.BlockSpec((g,) + tail, lambda i: (i,) + (0,) * len(tail))
    conv_block = pl.BlockSpec((CONV_PREFIX, g, D_CONV), lambda i: (0, i, 0))
    return pl.pallas_call(
        _sample_seq_kernel,
        grid=(n // g,),
        in_specs=[_lane_chunk_spec(rows, D_CONV)] + [_row_spec(rows, D_HGRN)] * 4 + [
            _row_spec(g, D_HGRN),
            seq_block(MEM_HEADS * SUBLANES, MEM_HEAD_DIM),
            conv_block,
            seq_block(HG_HEADS, HG_DK, HG_DV),
            seq_block(kv_rows, MEM_HEAD_DIM),
            seq_block(kv_rows, MEM_HEAD_DIM),
            _const_spec((CONV_WIDTH, D_CONV)),
        ],
        out_specs=[_lane_chunk_spec(rows, D_CONV), _row_spec(rows, D_HGRN), _row_spec(rows, D_MEM),
                   conv_block, seq_block(HG_HEADS, HG_DK, HG_DV)],
        out_shape=[
            jax.ShapeDtypeStruct((D_CONV // LANES, n * SUBLANES, LANES), F32),
            jax.ShapeDtypeStruct((n * SUBLANES, D_HGRN), F32),
            jax.ShapeDtypeStruct((n * SUBLANES, D_MEM), F32),
            jax.ShapeDtypeStruct((CONV_PREFIX, n, D_CONV), F32),
            jax.ShapeDtypeStruct((n, HG_HEADS, HG_DK, HG_DV), F32),
        ],
        compiler_params=pltpu.CompilerParams(
            dimension_semantics=("parallel",), vmem_limit_bytes=VMEM_LIMIT),
        name="sample_seq",
    )(u, qd, ki, ke, v, blast, q4, conv_rows, state_hgrn, k2, v2, conv_w)


def _sample_out(x2d, c, o, om, hgate, gates, vectors, weights, ffn_vectors, ffn_weights):
    rows = x2d.shape[0]
    tile = SAMPLE_TILE
    acts = (x2d, c, o, om, hgate, gates)
    consts = tuple(vectors) + tuple(weights) + tuple(ffn_vectors) + tuple(ffn_weights)
    return pl.pallas_call(
        _sample_out_kernel,
        grid=(rows // tile,),
        in_specs=[_lane_chunk_spec(tile, D_CONV) if a is c else _row_spec(tile, a.shape[1]) for a in acts]
        + [_const_spec(a.shape) for a in consts],
        out_specs=_row_spec(tile, D_MODEL),
        out_shape=jax.ShapeDtypeStruct((rows, D_MODEL), F32),
        compiler_params=pltpu.CompilerParams(
            dimension_semantics=("parallel",), vmem_limit_bytes=VMEM_LIMIT),
        name="sample_out",
    )(*acts, *consts)


def _ffn_kernel(x_ref, pre_g_ref, post_g_ref, w_gate_ref, w_up_ref, w_down_ref, y_ref):
    y_ref[...] = _ffn_block(x_ref[...], pre_g_ref[...], post_g_ref[...], w_gate_ref, w_up_ref, w_down_ref)


def _ffn(x2d, pre_g, post_g, w_gate, w_up, w_down):
    rows = x2d.shape[0]
    tile = min(FFN_TILE, rows)
    return pl.pallas_call(
        _ffn_kernel,
        grid=(rows // tile,),
        in_specs=[
            pl.BlockSpec((tile, D_MODEL), lambda i: (i, 0)),
            _const_spec((1, D_MODEL)), _const_spec((1, D_MODEL)),
            _const_spec(w_gate.shape), _const_spec(w_up.shape), _const_spec(w_down.shape),
        ],
        out_specs=pl.BlockSpec((tile, D_MODEL), lambda i: (i, 0)),
        out_shape=jax.ShapeDtypeStruct((rows, D_MODEL), F32),
        compiler_params=pltpu.CompilerParams(
            dimension_semantics=("parallel",), vmem_limit_bytes=VMEM_LIMIT),
        name="ffn",
    )(x2d, pre_g, post_g, w_gate, w_up, w_down)


def kernel(x_prompt, x_sample, mem_prompt, state_conv, state_hgrn, cache_mem_k, cache_mem_v, norm_pre_mix, norm_post_mix, norm_pre_ffn, norm_post_ffn, w_in, conv_w, conv_b, conv_ln_g, conv_ln_b, w_conv_out, hg_lb_logits, hg_norm_g, w_hg_out, mem_norm_g, w_mem_kv, w_mem_out, w_out, w_ffn_gate, w_ffn_up, w_ffn_down):
    depth = w_in.shape[0]
    assert depth == 1 and hg_lb_logits.shape[0] == 2, "single-layer step"
    batch, seq, _ = x_prompt.shape
    n_dec, dec, _ = x_sample.shape
    n_mem = mem_prompt.shape[1]
    assert seq % PROMPT_TILE == 0 and n_dec % SAMPLE_GROUP == 0 and dec == SUBLANES
    assert (n_dec * dec) % SAMPLE_TILE == 0

    vectors = (norm_pre_mix, norm_post_mix, conv_w[0], conv_b, conv_ln_g, conv_ln_b,
               hg_lb_logits, jnp.tile(hg_norm_g, (1, HG_HEADS)))

    mk, mv, kb, vb, *weights = _memory_kv(
        mem_prompt.reshape(batch * n_mem, D_MODEL), mem_norm_g, w_mem_kv[0], n_mem,
        tuple(w[0] for w in (w_in, w_conv_out, w_hg_out, w_mem_out, w_out)))

    xp, conv_p, hg_p, *ffn_weights = _prompt_mixer(
        x_prompt, kb.reshape(batch, n_mem, D_MEM), vb.reshape(batch, n_mem, D_MEM), vectors, weights,
        (w_ffn_gate[0], w_ffn_up[0], w_ffn_down[0]))
    yp = _ffn(xp.reshape(batch * seq, D_MODEL), norm_pre_ffn, norm_post_ffn, *ffn_weights)

    xs2d = x_sample.reshape(n_dec * dec, D_MODEL)
    u, qd, ki, ke, v, blast, q4, hgate, gates = _sample_in(xs2d, norm_pre_mix, hg_lb_logits, weights[0])
    k2 = cache_mem_k[0].reshape(n_dec, n_mem * MEM_HEADS, MEM_HEAD_DIM)
    v2 = cache_mem_v[0].reshape(n_dec, n_mem * MEM_HEADS, MEM_HEAD_DIM)
    c, o, om, conv_rows, hg_s = _sample_seq(u, qd, ki, ke, v, blast, q4,
                                            jnp.transpose(state_conv[0], (1, 0, 2)), state_hgrn[0],
                                            k2, v2, conv_w[0])
    conv_s = jnp.transpose(conv_rows, (1, 0, 2))
    ys = _sample_out(xs2d, c, o, om, hgate, gates,
                     (norm_post_mix, conv_b, conv_ln_g, conv_ln_b, vectors[-1]), weights[1:],
                     (norm_pre_ffn, norm_post_ffn), ffn_weights)

    kv_shape = (1, batch, n_mem, MEM_HEADS, MEM_HEAD_DIM)
    return (yp.reshape(batch, seq, D_MODEL), ys.reshape(n_dec, dec, D_MODEL),
            conv_p[None], hg_p[None], mk.reshape(kv_shape), mv.reshape(kv_shape),
            conv_s[None], hg_s[None])
```

```python
from typing import NamedTuple

import jax
import jax.numpy as jnp
from jax import lax
from jax.experimental import pallas as pl
from jax.experimental.pallas import tpu as pltpu

D_MODEL = 1024
D_CONV = 512
CONV_WIDTH = 31
CONV_PREFIX = CONV_WIDTH - 1
HG_HEADS = 4
HG_DK = 128
HG_DV = 128
D_HGRN = HG_HEADS * HG_DK
MEM_HEADS = 4
MEM_HEAD_DIM = 128
D_MEM = MEM_HEADS * MEM_HEAD_DIM
CHUNK = 32
EPS = 1e-6
NEG_LOG2E = -1.4426950408889634

COL_CA = 0
COL_CB = COL_CA + D_CONV
COL_HQ = COL_CB + D_CONV
COL_HF = COL_HQ + D_HGRN
COL_HI = COL_HF + D_HGRN
COL_HGATE = COL_HI + D_HGRN
COL_MQ = COL_HGATE + D_HGRN
COL_GATES = COL_MQ + D_MEM

SUBLANES = 8
LANES = 128
VMEM_LIMIT = 56 * 1024 * 1024

PROMPT_TILE = 256
PROMPT_STEP_TILES = 2
SAMPLE_GROUP = 8
SAMPLE_TILE = 256
FFN_TILE = 512
CONV_ROWS = 32

BF16 = jnp.bfloat16
F32 = jnp.float32


def _dot(a, b):
    return jnp.dot(a, b, preferred_element_type=F32)


def _dot_nt(a, b):
    return lax.dot_general(a, b, (((1,), (1,)), ((), ())), preferred_element_type=F32)


def _dot_tn(a, b):
    return lax.dot_general(a, b, (((0,), (0,)), ((), ())), preferred_element_type=F32)


def _rms(x, gain):
    return x * lax.rsqrt(jnp.mean(x * x, axis=-1, keepdims=True) + EPS) * gain


def _sigmoid(x):
    return 1.0 / (1.0 + jnp.exp2(x * NEG_LOG2E))


def _silu(x):
    return x * _sigmoid(x)


def _forget_terms(hf, lb):
    sig = _sigmoid(hf)
    return jnp.log(lb + (1.0 - lb) * sig), (1.0 - lb) * (1.0 - sig)


def _lower_bound(lb_logits):
    m = jnp.max(lb_logits, axis=0, keepdims=True)
    e = jnp.exp(lb_logits - m)
    return e[0:1] / jnp.sum(e, axis=0, keepdims=True)


def _segment_cumsum(x, seg):
    pos = lax.broadcasted_iota(jnp.int32, x.shape, 0) & (seg - 1)
    s = 1
    while s < seg:
        x = x + jnp.where(pos >= s, pltpu.roll(x, s, axis=0), 0.0)
        s *= 2
    return x


def _conv_ln_silu(c, conv_b, ln_g, ln_b):
    c = c + conv_b
    mu = jnp.mean(c, axis=-1, keepdims=True)
    d = c - mu
    var = jnp.mean(d * d, axis=-1, keepdims=True)
    return _silu(d * lax.rsqrt(var + EPS) * ln_g + ln_b)


def _hgrn_gates(hq, hf, lb, seg):
    q = _silu(hq)
    logf, k = _forget_terms(hf, lb)
    b = _segment_cumsum(logf, seg)
    rows = b.shape[0]
    b3 = b.reshape(rows // seg, seg, D_HGRN)
    b_last3 = b3[:, seg - 1:seg, :]
    rest = (b_last3 - b3).reshape(rows, D_HGRN)
    q_dec = q * jnp.exp(b)
    k_inv = k * jnp.exp(-b)
    k_end = k * jnp.exp(rest)
    return q_dec, k_inv, k_end, b_last3.reshape(rows // seg, D_HGRN)


class _TileLevels(NamedTuple):
    q_mid: jax.Array
    k_mid: jax.Array
    q_64: jax.Array
    k_64: jax.Array
    q_128: jax.Array
    k_128: jax.Array
    q_tile: jax.Array
    k_tile: jax.Array
    decay: jax.Array


def _hgrn_tile_levels(hq, hf, lb):
    tl = hq.shape[0]
    n_blocks = 4
    block = 2 * CHUNK
    assert tl == n_blocks * block
    q = _silu(hq)
    logf, k = _forget_terms(hf, lb)
    b = _segment_cumsum(logf, CHUNK)
    chunks = [b[n * CHUNK:(n + 1) * CHUNK] for n in range(2 * n_blocks)]
    totals = [c[CHUNK - 1:CHUNK] for c in chunks]
    e = jnp.concatenate([c - totals[n] if n % 2 == 0 else c for n, c in enumerate(chunks)], axis=0)
    q_mid = q * jnp.exp(e)
    k_mid = k * jnp.exp(-e)
    first = [totals[2 * j] for j in range(n_blocks)]
    second = [totals[2 * j + 1] for j in range(n_blocks)]
    both = [first[j] + second[j] for j in range(n_blocks)]

    def scaled(x, blocks, log_scales):
        return jnp.concatenate(
            [x[j * block:(j + 1) * block] * jnp.exp(s) for j, s in zip(blocks, log_scales)], axis=0)

    every = range(n_blocks)
    return _TileLevels(
        q_mid=q_mid, k_mid=k_mid,
        q_64=scaled(q_mid, every, first),
        k_64=scaled(k_mid, every, second),
        q_128=scaled(q_mid, (2, 3), (first[2], first[3] + both[2])),
        k_128=scaled(k_mid, (0, 1), (second[0] + both[1], second[1])),
        q_tile=scaled(q_mid, every, [first[j] + sum(both[:j], 0.0) for j in every]),
        k_tile=scaled(k_mid, every, [second[j] + sum(both[j + 1:], 0.0) for j in every]),
        decay=jnp.exp(sum(both[1:], both[0])))


def _head_rms(o, gain4):
    parts = []
    for h in range(HG_HEADS):
        oh = o[:, h * HG_DV:(h + 1) * HG_DV]
        parts.append(oh * lax.rsqrt(jnp.mean(oh * oh, axis=-1, keepdims=True) + EPS))
    return jnp.concatenate(parts, axis=-1) * gain4


def _branch_gate(hb, w_in_ref, i):
    c0 = COL_GATES + i * D_MODEL
    return _sigmoid(_dot(hb, w_in_ref[:, c0:c0 + D_MODEL]))


def _merge_out(gate, x, branches, w_out_ref, post_g):
    acc = None
    for i, p in enumerate(branches):
        acc = gate(i) * p if acc is None else acc + gate(i) * p
    m = _dot(acc.astype(BF16), w_out_ref[...])
    return x + _rms(m, post_g)


def _ffn_block(x, pre_g, post_g, w_gate_ref, w_up_ref, w_down_ref):
    hb = _rms(x, pre_g).astype(BF16)
    f = _silu(_dot(hb, w_gate_ref[...])) * _dot(hb, w_up_ref[...])
    d = _dot(f.astype(BF16), w_down_ref[...])
    return x + _rms(d, post_g)


def _cast_block_rows(rows, n_steps):
    bf16_sublanes = 2 * SUBLANES
    for n_blocks in range(min(n_steps, rows // bf16_sublanes), 0, -1):
        if rows % n_blocks == 0 and (rows // n_blocks) % bf16_sublanes == 0:
            return rows // n_blocks
    raise ValueError(f"no bf16-tileable split of {rows} rows")


def _cast_specs(weights, n_steps, step_of):
    specs = []
    for w in weights:
        rows_blk = _cast_block_rows(w.shape[0], n_steps)
        last = w.shape[0] // rows_blk - 1
        specs.append(pl.BlockSpec((rows_blk, w.shape[1]),
                                  lambda *idx, last=last: (jnp.minimum(step_of(*idx), last), 0)))
    return specs


def _cast_blocks(srcs, dsts):
    for src, dst in zip(srcs, dsts):
        dst[...] = src[...].astype(BF16)


def _memory_kv_kernel(mem_ref, g_ref, w_ref, *rest):
    n_cast = (len(rest) - 5) // 2
    k_rows_ref, v_rows_ref, k_ref, v_ref = rest[n_cast:n_cast + 4]
    wb_ref = rest[-1]
    _cast_blocks(rest[:n_cast], rest[n_cast + 4:-1])

    @pl.when(pl.program_id(0) == 0)
    def _():
        wb_ref[...] = w_ref[...].astype(BF16)

    m = _rms(mem_ref[...], g_ref[...]).astype(BF16)
    tokens = mem_ref.shape[0]
    for full, rows_ref, op_ref in ((_dot(m, wb_ref[:, :D_MEM]), k_rows_ref, k_ref),
                                   (_dot(m, wb_ref[:, D_MEM:]), v_rows_ref, v_ref)):
        op_ref[...] = full.astype(BF16)
        for h in range(MEM_HEADS):
            rows_ref[pl.ds(h, tokens, stride=MEM_HEADS), :] = full[:, h * MEM_HEAD_DIM:(h + 1) * MEM_HEAD_DIM]


def _memory_kv(mem2d, gain, w_kv, tile, later_weights):
    rows = mem2d.shape[0]
    n_steps = rows // tile
    cast_specs = _cast_specs(later_weights, n_steps, lambda i: i)
    return pl.pallas_call(
        _memory_kv_kernel,
        grid=(n_steps,),
        in_specs=[
            pl.BlockSpec((tile, D_MODEL), lambda i: (i, 0)),
            pl.BlockSpec((1, D_MODEL), lambda i: (0, 0)),
            _const_spec(w_kv.shape),
        ] + cast_specs,
        out_specs=[pl.BlockSpec((tile * MEM_HEADS, MEM_HEAD_DIM), lambda i: (i, 0))] * 2
        + [pl.BlockSpec((tile, D_MEM), lambda i: (i, 0))] * 2 + cast_specs,
        out_shape=[jax.ShapeDtypeStruct((rows * MEM_HEADS, MEM_HEAD_DIM), F32)] * 2
        + [jax.ShapeDtypeStruct((rows, D_MEM), BF16)] * 2
        + [jax.ShapeDtypeStruct(w.shape, BF16) for w in later_weights],
        scratch_shapes=[pltpu.VMEM(w_kv.shape, BF16)],
        compiler_params=pltpu.CompilerParams(
            dimension_semantics=("arbitrary",), vmem_limit_bytes=VMEM_LIMIT),
        name="memory_kv",
    )(mem2d, gain, w_kv, *later_weights)


def _prompt_mixer_kernel(x_ref, kb_ref, vb_ref, pre_g_ref, post_g_ref, conv_w_ref, conv_b_ref,
                         ln_g_ref, ln_b_ref, lb_logits_ref, hg_g_ref, w_in_ref, w_conv_out_ref,
                         w_hg_out_ref, w_mem_out_ref, w_out_ref, *rest):
    n_cast = (len(rest) - 6) // 2
    cast_src = rest[:n_cast]
    y_ref, new_conv_ref, new_hg_ref = rest[n_cast:n_cast + 3]
    cast_dst = rest[n_cast + 3:2 * n_cast + 3]
    shift_ref, tail_ref, state_t_ref = rest[2 * n_cast + 3:]
    _cast_blocks(cast_src, cast_dst)
    t = pl.program_id(1)
    last_t = pl.num_programs(1) - 1
    tl = PROMPT_TILE
    n_tiles = x_ref.shape[0] // tl

    @pl.when(t == 0)
    def _():
        tail_ref[...] = jnp.zeros_like(tail_ref)
        state_t_ref[...] = jnp.zeros_like(state_t_ref)

    lb = _lower_bound(lb_logits_ref[...])
    half = tl // 2
    row = lax.broadcasted_iota(jnp.int32, (tl, tl), 0)
    col = lax.broadcasted_iota(jnp.int32, (tl, tl), 1)
    span = row ^ col
    near = (span < 2 * CHUNK) & (col <= row)
    mid = (span < 4 * CHUNK) & (col < row)

    def mix_rows():
        n_rows = n_tiles * tl
        shift = shift_ref
        x = x_ref[...]
        hb = _rms(x, pre_g_ref[...]).astype(BF16)

        def proj(c0, c1):
            return _dot(hb, w_in_ref[:, c0:c1])

        u = proj(COL_CA, COL_CB) * _sigmoid(proj(COL_CB, COL_HQ))
        shift[0, 0:CONV_ROWS, :] = tail_ref[...]
        shift[0, CONV_ROWS:CONV_ROWS + n_rows, :] = u
        tail_ref[...] = u[n_rows - CONV_ROWS:]
        for r in range(1, SUBLANES):
            shift[r, 0:n_rows + CONV_ROWS - SUBLANES, :] = shift[0, r:r + n_rows + CONV_ROWS - SUBLANES, :]

        conv_blocks = []
        for base in range(0, n_rows, CONV_ROWS):
            acc = None
            for j in range(CONV_WIDTH):
                s = j + (CONV_ROWS - CONV_PREFIX)
                r, a = s % SUBLANES, s // SUBLANES
                lo = base + a * SUBLANES
                term = shift[r, lo:lo + CONV_ROWS, :] * conv_w_ref[j:j + 1, :]
                acc = term if acc is None else acc + term
            conv_blocks.append(acc)
        c = _conv_ln_silu(jnp.concatenate(conv_blocks, axis=0),
                          conv_b_ref[...], ln_g_ref[...], ln_b_ref[...]).astype(BF16)

        hq = proj(COL_HQ, COL_HF)
        hf = proj(COL_HF, COL_HI)
        levels = []
        for i in range(n_tiles):
            lv = _hgrn_tile_levels(hq[i * tl:(i + 1) * tl], hf[i * tl:(i + 1) * tl], lb)
            levels.append(_TileLevels(*[a.astype(BF16) for a in lv[:-1]], lv.decay))
        vb16 = proj(COL_HI, COL_HGATE).astype(BF16)
        hgate = _silu(proj(COL_HGATE, COL_MQ))
        mq = proj(COL_MQ, COL_GATES).astype(BF16)
        gates = [_branch_gate(hb, w_in_ref, g) for g in range(3)]

        head = lambda h: slice(h * MEM_HEAD_DIM, (h + 1) * MEM_HEAD_DIM)
        s = jnp.concatenate([_dot_nt(mq[:, head(h)], kb_ref[:, head(h)]) for h in range(MEM_HEADS)],
                            axis=0) * (MEM_HEAD_DIM ** -0.5)
        e = jnp.exp(s - jnp.max(s, axis=-1, keepdims=True))
        p = (e / jnp.sum(e, axis=-1, keepdims=True)).astype(BF16)
        om = [_dot(p[h * n_rows:(h + 1) * n_rows], vb_ref[:, head(h)]) for h in range(MEM_HEADS)]

        o_tiles = []
        for i, lvb in enumerate(levels):
            o_heads = []
            for h in range(HG_HEADS):
                hs = slice(h * HG_DK, (h + 1) * HG_DK)
                vh = vb16[i * tl:(i + 1) * tl, hs]
                s_near = _dot_nt(lvb.q_mid[:, hs], lvb.k_mid[:, hs])
                s_mid = _dot_nt(lvb.q_64[:, hs], lvb.k_64[:, hs])
                s_far = _dot_nt(lvb.q_128[:, hs], lvb.k_128[:, hs])
                base = jnp.where(near, s_near, jnp.where(mid, s_mid, 0.0))
                scores = jnp.concatenate(
                    [base[:half], jnp.concatenate([s_far, base[half:, half:]], axis=1)], axis=0)
                st = state_t_ref[h]
                o_heads.append(_dot(scores.astype(BF16), vh)
                               + _dot_nt(lvb.q_tile[:, hs], st.astype(BF16)))
                state_t_ref[h] = st * lvb.decay[:, hs] + _dot_tn(vh, lvb.k_tile[:, hs])
            o_tiles.append(jnp.concatenate(o_heads, axis=-1))
        o = _head_rms(jnp.concatenate(o_tiles, axis=0), hg_g_ref[...]) * hgate

        p_conv = _dot(c, w_conv_out_ref[...])
        p_mem = _dot(jnp.concatenate(om, axis=-1).astype(BF16), w_mem_out_ref[...])
        p_hg = _dot(o.astype(BF16), w_hg_out_ref[...])
        y_ref[...] = _merge_out(lambda g: gates[g], x, (p_conv, p_hg, p_mem), w_out_ref, post_g_ref[...])

    mix_rows()

    @pl.when(t == last_t)
    def _():
        new_conv_ref[...] = tail_ref[CONV_ROWS - CONV_PREFIX:CONV_ROWS, :]
        for h in range(HG_HEADS):
            new_hg_ref[h] = state_t_ref[h].T


def _const_spec(shape):
    zeros = (0,) * len(shape)
    return pl.BlockSpec(shape, lambda *_: zeros, pipeline_mode=pl.Buffered(1))


def _mixer_weight_specs(w_in, w_conv_out, w_hg_out, w_mem_out, w_out):
    return [_const_spec(w.shape) for w in (w_in, w_conv_out, w_hg_out, w_mem_out, w_out)]


def _vector_specs():
    return [
        _const_spec((1, D_MODEL)), _const_spec((1, D_MODEL)),
        _const_spec((CONV_WIDTH, D_CONV)), _const_spec((1, D_CONV)),
        _const_spec((1, D_CONV)), _const_spec((1, D_CONV)),
        _const_spec((2, D_HGRN)), _const_spec((1, D_HGRN)),
    ]


def _prompt_mixer(x, mk, mv, vectors, weights, later_weights):
    batch, seq, _ = x.shape
    n_mem = mk.shape[1]
    tl = PROMPT_TILE * PROMPT_STEP_TILES
    n_t = seq // tl
    cast_specs = _cast_specs(later_weights, batch * n_t, lambda b, t: b * n_t + t)
    return pl.pallas_call(
        _prompt_mixer_kernel,
        grid=(batch, n_t),
        in_specs=[
            pl.BlockSpec((None, tl, D_MODEL), lambda b, t: (b, t, 0)),
            pl.BlockSpec((None, n_mem, D_MEM), lambda b, t: (b, 0, 0)),
            pl.BlockSpec((None, n_mem, D_MEM), lambda b, t: (b, 0, 0)),
        ] + _vector_specs() + _mixer_weight_specs(*weights) + cast_specs,
        out_specs=[
            pl.BlockSpec((None, tl, D_MODEL), lambda b, t: (b, t, 0)),
            pl.BlockSpec((None, CONV_PREFIX, D_CONV), lambda b, t: (b, 0, 0)),
            pl.BlockSpec((None, HG_HEADS, HG_DK, HG_DV), lambda b, t: (b, 0, 0, 0)),
        ] + cast_specs,
        out_shape=[
            jax.ShapeDtypeStruct((batch, seq, D_MODEL), F32),
            jax.ShapeDtypeStruct((batch, CONV_PREFIX, D_CONV), F32),
            jax.ShapeDtypeStruct((batch, HG_HEADS, HG_DK, HG_DV), F32),
        ] + [jax.ShapeDtypeStruct(w.shape, BF16) for w in later_weights],
        scratch_shapes=[
            pltpu.VMEM((SUBLANES, tl + CONV_ROWS, D_CONV), F32),
            pltpu.VMEM((CONV_ROWS, D_CONV), F32),
            pltpu.VMEM((HG_HEADS, HG_DV, HG_DK), F32),
        ],
        compiler_params=pltpu.CompilerParams(
            dimension_semantics=("arbitrary", "arbitrary"), vmem_limit_bytes=VMEM_LIMIT),
        name="prompt_mixer",
    )(x, mk, mv, *vectors, *weights, *later_weights)


def _store_lane_chunks(ref, x):
    for i in range(ref.shape[0]):
        ref[i] = x[:, i * LANES:(i + 1) * LANES]


def _load_lane_chunks(ref, rows=slice(None)):
    return jnp.concatenate([ref[i, rows, :] for i in range(ref.shape[0])], axis=1)


def _lane_chunk_spec(tile, width):
    return pl.BlockSpec((width // LANES, tile, LANES), lambda i: (0, i, 0))


def _sample_in_kernel(x_ref, pre_g_ref, lb_logits_ref, w_in_ref,
                      u_ref, qd_ref, ki_ref, ke_ref, v_ref, blast_ref, q4_ref, hgate_ref, gates_ref):
    n_seq = x_ref.shape[0] // SUBLANES
    hb = _rms(x_ref[...], pre_g_ref[...]).astype(BF16)
    _store_lane_chunks(
        u_ref, _dot(hb, w_in_ref[:, COL_CA:COL_CB]) * _sigmoid(_dot(hb, w_in_ref[:, COL_CB:COL_HQ])))
    lb = _lower_bound(lb_logits_ref[...])
    q_dec, k_inv, k_end, b_last = _hgrn_gates(
        _dot(hb, w_in_ref[:, COL_HQ:COL_HF]), _dot(hb, w_in_ref[:, COL_HF:COL_HI]), lb, SUBLANES)
    qd_ref[...] = q_dec
    ki_ref[...] = k_inv
    ke_ref[...] = k_end
    blast_ref[...] = b_last
    v_ref[...] = _dot(hb, w_in_ref[:, COL_HI:COL_HGATE])
    hgate_ref[...] = _silu(_dot(hb, w_in_ref[:, COL_HGATE:COL_MQ]))
    mq = _dot(hb, w_in_ref[:, COL_MQ:COL_GATES])
    for h in range(MEM_HEADS):
        q4_ref[:, h * SUBLANES:(h + 1) * SUBLANES, :] = (
            mq[:, h * MEM_HEAD_DIM:(h + 1) * MEM_HEAD_DIM].reshape(n_seq, SUBLANES, MEM_HEAD_DIM))
    for i in range(3):
        gates_ref[:, i * D_MODEL:(i + 1) * D_MODEL] = _branch_gate(hb, w_in_ref, i)


def _sample_seq_kernel(u_ref, qd_ref, ki_ref, ke_ref, v_ref, blast_ref, q4_ref, sc_ref, sh_ref,
                       k2_ref, v2_ref, conv_w_ref,
                       c_ref, o_ref, om_ref, new_conv_ref, new_hg_ref):
    n_seq, dec = sc_ref.shape[1], SUBLANES
    rows = n_seq * dec

    assert dec == 8 and MEM_HEADS == 4
    q_rows = MEM_HEADS * dec
    s = jnp.concatenate([_dot_nt(q4_ref[g].astype(BF16), k2_ref[g].astype(BF16)) for g in range(n_seq)],
                        axis=0) * (MEM_HEAD_DIM ** -0.5)
    q_head = (lax.broadcasted_iota(jnp.int32, s.shape, 0) >> 3) & 3
    k_head = lax.broadcasted_iota(jnp.int32, s.shape, 1) & 3
    s = jnp.where(q_head == k_head, s, -jnp.inf)
    e = jnp.exp(s - jnp.max(s, axis=-1, keepdims=True))
    p = (e / jnp.sum(e, axis=-1, keepdims=True)).astype(BF16)
    om = []
    for g in range(n_seq):
        om_g = _dot(p[g * q_rows:(g + 1) * q_rows], v2_ref[g].astype(BF16))
        om.extend(om_g[h * dec:(h + 1) * dec] for h in range(MEM_HEADS))

    u_slabs = [_load_lane_chunks(u_ref, pl.ds(step, n_seq, stride=dec)) for step in range(dec)]
    conv = [None] * dec
    for s in range(CONV_PREFIX + dec):
        slab = sc_ref[s] if s < CONV_PREFIX else u_slabs[s - CONV_PREFIX]
        if s >= dec:
            new_conv_ref[s - dec] = slab
        for step in range(max(0, s - CONV_PREFIX), min(dec, s + 1)):
            term = slab * conv_w_ref[s - step:s - step + 1, :]
            conv[step] = term if conv[step] is None else conv[step] + term

    row = lax.broadcasted_iota(jnp.int32, (rows, rows), 0)
    col = lax.broadcasted_iota(jnp.int32, (rows, rows), 1)
    causal = ((row ^ col) < dec) & (col <= row)
    decay = jnp.exp(blast_ref[...])
    for h in range(HG_HEADS):
        hs = slice(h * HG_DK, (h + 1) * HG_DK)
        scores = jnp.where(causal, _dot_nt(qd_ref[:, hs].astype(BF16), ki_ref[:, hs].astype(BF16)), 0.0)
        o_intra = _dot(scores.astype(BF16), v_ref[:, hs].astype(BF16))
        for g in range(n_seq):
            rs = slice(g * dec, (g + 1) * dec)
            s0 = sh_ref[g, h]
            o_ref[rs, hs] = o_intra[rs] + _dot(qd_ref[rs, hs].astype(BF16), s0.astype(BF16))
            decay_col = jnp.broadcast_to(decay[g:g + 1, hs], (HG_DV, HG_DK)).T
            new_hg_ref[g, h] = decay_col * s0 + _dot_tn(ke_ref[rs, hs].astype(BF16),
                                                        v_ref[rs, hs].astype(BF16))

    for step in range(dec):
        for i in range(c_ref.shape[0]):
            c_ref[i, pl.ds(step, n_seq, stride=dec), :] = conv[step][:, i * LANES:(i + 1) * LANES]
    for g in range(n_seq):
        om_ref[g * dec:(g + 1) * dec, :] = jnp.concatenate(om[g * MEM_HEADS:(g + 1) * MEM_HEADS], axis=1)


def _sample_out_kernel(x_ref, c_ref, o_ref, om_ref, hgate_ref, gates_ref, post_g_ref, conv_b_ref,
                       ln_g_ref, ln_b_ref, hg_g_ref, w_conv_out_ref, w_hg_out_ref, w_mem_out_ref,
                       w_out_ref, ffn_pre_g_ref, ffn_post_g_ref, w_gate_ref, w_up_ref, w_down_ref, y_ref):
    c = _conv_ln_silu(_load_lane_chunks(c_ref), conv_b_ref[...], ln_g_ref[...], ln_b_ref[...])
    p_conv = _dot(c.astype(BF16), w_conv_out_ref[...])
    o = _head_rms(o_ref[...], hg_g_ref[...]) * hgate_ref[...]
    p_hg = _dot(o.astype(BF16), w_hg_out_ref[...])
    p_mem = _dot(om_ref[...].astype(BF16), w_mem_out_ref[...])
    x1 = _merge_out(lambda i: gates_ref[:, i * D_MODEL:(i + 1) * D_MODEL], x_ref[...],
                    (p_conv, p_hg, p_mem), w_out_ref, post_g_ref[...])
    y_ref[...] = _ffn_block(x1, ffn_pre_g_ref[...], ffn_post_g_ref[...], w_gate_ref, w_up_ref, w_down_ref)


def _row_spec(tile, width):
    return pl.BlockSpec((tile, width), lambda i: (i, 0))


def _sample_in(x2d, pre_g, lb_logits, w_in):
    rows = x2d.shape[0]
    tile = SAMPLE_TILE
    seqs = tile // SUBLANES
    wide = [D_HGRN, D_HGRN, D_HGRN, D_HGRN]
    return pl.pallas_call(
        _sample_in_kernel,
        grid=(rows // tile,),
        in_specs=[_row_spec(tile, D_MODEL), _const_spec((1, D_MODEL)), _const_spec((2, D_HGRN)),
                  _const_spec(w_in.shape)],
        out_specs=[_lane_chunk_spec(tile, D_CONV)] + [_row_spec(tile, w) for w in wide] + [
            _row_spec(seqs, D_HGRN),
            pl.BlockSpec((seqs, MEM_HEADS * SUBLANES, MEM_HEAD_DIM), lambda i: (i, 0, 0)),
            _row_spec(tile, D_HGRN),
            _row_spec(tile, 3 * D_MODEL),
        ],
        out_shape=[jax.ShapeDtypeStruct((D_CONV // LANES, rows, LANES), F32)]
        + [jax.ShapeDtypeStruct((rows, w), F32) for w in wide] + [
            jax.ShapeDtypeStruct((rows // SUBLANES, D_HGRN), F32),
            jax.ShapeDtypeStruct((rows // SUBLANES, MEM_HEADS * SUBLANES, MEM_HEAD_DIM), F32),
            jax.ShapeDtypeStruct((rows, D_HGRN), F32),
            jax.ShapeDtypeStruct((rows, 3 * D_MODEL), F32),
        ],
        compiler_params=pltpu.CompilerParams(
            dimension_semantics=("parallel",), vmem_limit_bytes=VMEM_LIMIT),
        name="sample_in",
    )(x2d, pre_g, lb_logits, w_in)


def _sample_seq(u, qd, ki, ke, v, blast, q4, conv_rows, state_hgrn, k2, v2, conv_w):
    n = conv_rows.shape[1]
    g = SAMPLE_GROUP
    rows = g * SUBLANES
    kv_rows = k2.shape[1]
    seq_block = lambda *tail: pl.BlockSpec((g,) + tail, lambda i: (i,) + (0,) * len(tail))
    conv_block = pl.BlockSpec((CONV_PREFIX, g, D_CONV), lambda i: (0, i, 0))
    return pl.pallas_call(
        _sample_seq_kernel,
        grid=(n // g,),
        in_specs=[_lane_chunk_spec(rows, D_CONV)] + [_row_spec(rows, D_HGRN)] * 4 + [
            _row_spec(g, D_HGRN),
            seq_block(MEM_HEADS * SUBLANES, MEM_HEAD_DIM),
            conv_block,
            seq_block(HG_HEADS, HG_DK, HG_DV),
            seq_block(kv_rows, MEM_HEAD_DIM),
            seq_block(kv_rows, MEM_HEAD_DIM),
            _const_spec((CONV_WIDTH, D_CONV)),
        ],
        out_specs=[_lane_chunk_spec(rows, D_CONV), _row_spec(rows, D_HGRN), _row_spec(rows, D_MEM),
                   conv_block, seq_block(HG_HEADS, HG_DK, HG_DV)],
        out_shape=[
            jax.ShapeDtypeStruct((D_CONV // LANES, n * SUBLANES, LANES), F32),
            jax.ShapeDtypeStruct((n * SUBLANES, D_HGRN), F32),
            jax.ShapeDtypeStruct((n * SUBLANES, D_MEM), F32),
            jax.ShapeDtypeStruct((CONV_PREFIX, n, D_CONV), F32),
            jax.ShapeDtypeStruct((n, HG_HEADS, HG_DK, HG_DV), F32),
        ],
        compiler_params=pltpu.CompilerParams(
            dimension_semantics=("parallel",), vmem_limit_bytes=VMEM_LIMIT),
        name="sample_seq",
    )(u, qd, ki, ke, v, blast, q4, conv_rows, state_hgrn, k2, v2, conv_w)


def _sample_out(x2d, c, o, om, hgate, gates, vectors, weights, ffn_vectors, ffn_weights):
    rows = x2d.shape[0]
    tile = SAMPLE_TILE
    acts = (x2d, c, o, om, hgate, gates)
    consts = tuple(vectors) + tuple(weights) + tuple(ffn_vectors) + tuple(ffn_weights)
    return pl.pallas_call(
        _sample_out_kernel,
        grid=(rows // tile,),
        in_specs=[_lane_chunk_spec(tile, D_CONV) if a is c else _row_spec(tile, a.shape[1]) for a in acts]
        + [_const_spec(a.shape) for a in consts],
        out_specs=_row_spec(tile, D_MODEL),
        out_shape=jax.ShapeDtypeStruct((rows, D_MODEL), F32),
        compiler_params=pltpu.CompilerParams(
            dimension_semantics=("parallel",), vmem_limit_bytes=VMEM_LIMIT),
        name="sample_out",
    )(*acts, *consts)


def _ffn_kernel(x_ref, pre_g_ref, post_g_ref, w_gate_ref, w_up_ref, w_down_ref, y_ref):
    y_ref[...] = _ffn_block(x_ref[...], pre_g_ref[...], post_g_ref[...], w_gate_ref, w_up_ref, w_down_ref)


def _ffn(x2d, pre_g, post_g, w_gate, w_up, w_down):
    rows = x2d.shape[0]
    tile = min(FFN_TILE, rows)
    return pl.pallas_call(
        _ffn_kernel,
        grid=(rows // tile,),
        in_specs=[
            pl.BlockSpec((tile, D_MODEL), lambda i: (i, 0)),
            _const_spec((1, D_MODEL)), _const_spec((1, D_MODEL)),
            _const_spec(w_gate.shape), _const_spec(w_up.shape), _const_spec(w_down.shape),
        ],
        out_specs=pl.BlockSpec((tile, D_MODEL), lambda i: (i, 0)),
        out_shape=jax.ShapeDtypeStruct((rows, D_MODEL), F32),
        compiler_params=pltpu.CompilerParams(
            dimension_semantics=("parallel",), vmem_limit_bytes=VMEM_LIMIT),
        name="ffn",
    )(x2d, pre_g, post_g, w_gate, w_up, w_down)


def kernel(x_prompt, x_sample, mem_prompt, state_conv, state_hgrn, cache_mem_k, cache_mem_v, norm_pre_mix, norm_post_mix, norm_pre_ffn, norm_post_ffn, w_in, conv_w, conv_b, conv_ln_g, conv_ln_b, w_conv_out, hg_lb_logits, hg_norm_g, w_hg_out, mem_norm_g, w_mem_kv, w_mem_out, w_out, w_ffn_gate, w_ffn_up, w_ffn_down):
    depth = w_in.shape[0]
    assert depth == 1 and hg_lb_logits.shape[0] == 2, "single-layer step"
    batch, seq, _ = x_prompt.shape
    n_dec, dec, _ = x_sample.shape
    n_mem = mem_prompt.shape[1]
    assert seq % PROMPT_TILE == 0 and n_dec % SAMPLE_GROUP == 0 and dec == SUBLANES
    assert (n_dec * dec) % SAMPLE_TILE == 0

    vectors = (norm_pre_mix, norm_post_mix, conv_w[0], conv_b, conv_ln_g, conv_ln_b,
               hg_lb_logits, jnp.tile(hg_norm_g, (1, HG_HEADS)))

    mk, mv, kb, vb, *weights = _memory_kv(
        mem_prompt.reshape(batch * n_mem, D_MODEL), mem_norm_g, w_mem_kv[0], n_mem,
        tuple(w[0] for w in (w_in, w_conv_out, w_hg_out, w_mem_out, w_out)))

    xp, conv_p, hg_p, *ffn_weights = _prompt_mixer(
        x_prompt, kb.reshape(batch, n_mem, D_MEM), vb.reshape(batch, n_mem, D_MEM), vectors, weights,
        (w_ffn_gate[0], w_ffn_up[0], w_ffn_down[0]))
    yp = _ffn(xp.reshape(batch * seq, D_MODEL), norm_pre_ffn, norm_post_ffn, *ffn_weights)

    xs2d = x_sample.reshape(n_dec * dec, D_MODEL)
    u, qd, ki, ke, v, blast, q4, hgate, gates = _sample_in(xs2d, norm_pre_mix, hg_lb_logits, weights[0])
    k2 = cache_mem_k[0].reshape(n_dec, n_mem * MEM_HEADS, MEM_HEAD_DIM)
    v2 = cache_mem_v[0].reshape(n_dec, n_mem * MEM_HEADS, MEM_HEAD_DIM)
    c, o, om, conv_rows, hg_s = _sample_seq(u, qd, ki, ke, v, blast, q4,
                                            jnp.transpose(state_conv[0], (1, 0, 2)), state_hgrn[0],
                                            k2, v2, conv_w[0])
    conv_s = jnp.transpose(conv_rows, (1, 0, 2))
    ys = _sample_out(xs2d, c, o, om, hgate, gates,
                     (norm_post_mix, conv_b, conv_ln_g, conv_ln_b, vectors[-1]), weights[1:],
                     (norm_pre_ffn, norm_post_ffn), ffn_weights)

    kv_shape = (1, batch, n_mem, MEM_HEADS, MEM_HEAD_DIM)
    return (yp.reshape(batch, seq, D_MODEL), ys.reshape(n_dec, dec, D_MODEL),
            conv_p[None], hg_p[None], mk.reshape(kv_shape), mv.reshape(kv_shape),
            conv_s[None], hg_s[None])
```

```python
from typing import NamedTuple

import jax
import jax.numpy as jnp
from jax import lax
from jax.experimental import pallas as pl
from jax.experimental.pallas import tpu as pltpu

D_MODEL = 1024
D_CONV = 512
CONV_WIDTH = 31
CONV_PREFIX = CONV_WIDTH - 1
HG_HEADS = 4
HG_DK = 128
HG_DV = 128
D_HGRN = HG_HEADS * HG_DK
MEM_HEADS = 4
MEM_HEAD_DIM = 128
D_MEM = MEM_HEADS * MEM_HEAD_DIM
CHUNK = 32
EPS = 1e-6
NEG_LOG2E = -1.4426950408889634

COL_CA = 0
COL_CB = COL_CA + D_CONV
COL_HQ = COL_CB + D_CONV
COL_HF = COL_HQ + D_HGRN
COL_HI = COL_HF + D_HGRN
COL_HGATE = COL_HI + D_HGRN
COL_MQ = COL_HGATE + D_HGRN
COL_GATES = COL_MQ + D_MEM

SUBLANES = 8
LANES = 128
VMEM_LIMIT = 56 * 1024 * 1024

PROMPT_TILE = 256
PROMPT_STEP_TILES = 2
SAMPLE_GROUP = 8
SAMPLE_TILE = 256
FFN_TILE = 1024
CONV_ROWS = 32

BF16 = jnp.bfloat16
F32 = jnp.float32


def _dot(a, b):
    return jnp.dot(a, b, preferred_element_type=F32)


def _dot_nt(a, b):
    return lax.dot_general(a, b, (((1,), (1,)), ((), ())), preferred_element_type=F32)


def _dot_tn(a, b):
    return lax.dot_general(a, b, (((0,), (0,)), ((), ())), preferred_element_type=F32)


def _rms(x, gain):
    return x * lax.rsqrt(jnp.mean(x * x, axis=-1, keepdims=True) + EPS) * gain


def _sigmoid(x):
    return 1.0 / (1.0 + jnp.exp2(x * NEG_LOG2E))


def _silu(x):
    return x * _sigmoid(x)


def _forget_terms(hf, lb):
    sig = _sigmoid(hf)
    return jnp.log(lb + (1.0 - lb) * sig), (1.0 - lb) * (1.0 - sig)


def _lower_bound(lb_logits):
    m = jnp.max(lb_logits, axis=0, keepdims=True)
    e = jnp.exp(lb_logits - m)
    return e[0:1] / jnp.sum(e, axis=0, keepdims=True)


def _segment_cumsum(x, seg):
    pos = lax.broadcasted_iota(jnp.int32, x.shape, 0) & (seg - 1)
    s = 1
    while s < seg:
        x = x + jnp.where(pos >= s, pltpu.roll(x, s, axis=0), 0.0)
        s *= 2
    return x


def _conv_ln_silu(c, conv_b, ln_g, ln_b):
    c = c + conv_b
    mu = jnp.mean(c, axis=-1, keepdims=True)
    d = c - mu
    var = jnp.mean(d * d, axis=-1, keepdims=True)
    return _silu(d * lax.rsqrt(var + EPS) * ln_g + ln_b)


def _hgrn_gates(hq, hf, lb, seg):
    q = _silu(hq)
    logf, k = _forget_terms(hf, lb)
    b = _segment_cumsum(logf, seg)
    rows = b.shape[0]
    b3 = b.reshape(rows // seg, seg, D_HGRN)
    b_last3 = b3[:, seg - 1:seg, :]
    rest = (b_last3 - b3).reshape(rows, D_HGRN)
    q_dec = q * jnp.exp(b)
    k_inv = k * jnp.exp(-b)
    k_end = k * jnp.exp(rest)
    return q_dec, k_inv, k_end, b_last3.reshape(rows // seg, D_HGRN)


class _TileLevels(NamedTuple):
    q_mid: jax.Array
    k_mid: jax.Array
    q_64: jax.Array
    k_64: jax.Array
    q_128: jax.Array
    k_128: jax.Array
    q_tile: jax.Array
    k_tile: jax.Array
    decay: jax.Array


def _hgrn_tile_levels(hq, hf, lb):
    tl = hq.shape[0]
    n_blocks = 4
    block = 2 * CHUNK
    assert tl == n_blocks * block
    q = _silu(hq)
    logf, k = _forget_terms(hf, lb)
    b = _segment_cumsum(logf, CHUNK)
    chunks = [b[n * CHUNK:(n + 1) * CHUNK] for n in range(2 * n_blocks)]
    totals = [c[CHUNK - 1:CHUNK] for c in chunks]
    e = jnp.concatenate([c - totals[n] if n % 2 == 0 else c for n, c in enumerate(chunks)], axis=0)
    q_mid = q * jnp.exp(e)
    k_mid = k * jnp.exp(-e)
    first = [totals[2 * j] for j in range(n_blocks)]
    second = [totals[2 * j + 1] for j in range(n_blocks)]
    both = [first[j] + second[j] for j in range(n_blocks)]

    def scaled(x, blocks, log_scales):
        return jnp.concatenate(
            [x[j * block:(j + 1) * block] * jnp.exp(s) for j, s in zip(blocks, log_scales)], axis=0)

    every = range(n_blocks)
    return _TileLevels(
        q_mid=q_mid, k_mid=k_mid,
        q_64=scaled(q_mid, every, first),
        k_64=scaled(k_mid, every, second),
        q_128=scaled(q_mid, (2, 3), (first[2], first[3] + both[2])),
        k_128=scaled(k_mid, (0, 1), (second[0] + both[1], second[1])),
        q_tile=scaled(q_mid, every, [first[j] + sum(both[:j], 0.0) for j in every]),
        k_tile=scaled(k_mid, every, [second[j] + sum(both[j + 1:], 0.0) for j in every]),
        decay=jnp.exp(sum(both[1:], both[0])))


def _head_rms(o, gain4):
    parts = []
    for h in range(HG_HEADS):
        oh = o[:, h * HG_DV:(h + 1) * HG_DV]
        parts.append(oh * lax.rsqrt(jnp.mean(oh * oh, axis=-1, keepdims=True) + EPS))
    return jnp.concatenate(parts, axis=-1) * gain4


def _branch_gate(hb, w_in_ref, i):
    c0 = COL_GATES + i * D_MODEL
    return _sigmoid(_dot(hb, w_in_ref[:, c0:c0 + D_MODEL]))


def _merge_out(gate, x, branches, w_out_ref, post_g):
    acc = None
    for i, p in enumerate(branches):
        acc = gate(i) * p if acc is None else acc + gate(i) * p
    m = _dot(acc.astype(BF16), w_out_ref[...])
    return x + _rms(m, post_g)


def _ffn_block(x, pre_g, post_g, w_gate_ref, w_up_ref, w_down_ref):
    hb = _rms(x, pre_g).astype(BF16)
    f = _silu(_dot(hb, w_gate_ref[...])) * _dot(hb, w_up_ref[...])
    d = _dot(f.astype(BF16), w_down_ref[...])
    return x + _rms(d, post_g)


def _cast_block_rows(rows, n_steps):
    bf16_sublanes = 2 * SUBLANES
    for n_blocks in range(min(n_steps, rows // bf16_sublanes), 0, -1):
        if rows % n_blocks == 0 and (rows // n_blocks) % bf16_sublanes == 0:
            return rows // n_blocks
    raise ValueError(f"no bf16-tileable split of {rows} rows")


def _cast_specs(weights, n_steps, step_of):
    specs = []
    for w in weights:
        rows_blk = _cast_block_rows(w.shape[0], n_steps)
        last = w.shape[0] // rows_blk - 1
        specs.append(pl.BlockSpec((rows_blk, w.shape[1]),
                                  lambda *idx, last=last: (jnp.minimum(step_of(*idx), last), 0)))
    return specs


def _cast_blocks(srcs, dsts):
    for src, dst in zip(srcs, dsts):
        dst[...] = src[...].astype(BF16)


def _memory_kv_kernel(mem_ref, g_ref, w_ref, *rest):
    n_cast = (len(rest) - 5) // 2
    k_rows_ref, v_rows_ref, k_ref, v_ref = rest[n_cast:n_cast + 4]
    wb_ref = rest[-1]
    _cast_blocks(rest[:n_cast], rest[n_cast + 4:-1])

    @pl.when(pl.program_id(0) == 0)
    def _():
        wb_ref[...] = w_ref[...].astype(BF16)

    m = _rms(mem_ref[...], g_ref[...]).astype(BF16)
    tokens = mem_ref.shape[0]
    for full, rows_ref, op_ref in ((_dot(m, wb_ref[:, :D_MEM]), k_rows_ref, k_ref),
                                   (_dot(m, wb_ref[:, D_MEM:]), v_rows_ref, v_ref)):
        op_ref[...] = full.astype(BF16)
        for h in range(MEM_HEADS):
            rows_ref[pl.ds(h, tokens, stride=MEM_HEADS), :] = full[:, h * MEM_HEAD_DIM:(h + 1) * MEM_HEAD_DIM]


def _memory_kv(mem2d, gain, w_kv, tile, later_weights):
    rows = mem2d.shape[0]
    n_steps = rows // tile
    cast_specs = _cast_specs(later_weights, n_steps, lambda i: i)
    return pl.pallas_call(
        _memory_kv_kernel,
        grid=(n_steps,),
        in_specs=[
            pl.BlockSpec((tile, D_MODEL), lambda i: (i, 0)),
            pl.BlockSpec((1, D_MODEL), lambda i: (0, 0)),
            _const_spec(w_kv.shape),
        ] + cast_specs,
        out_specs=[pl.BlockSpec((tile * MEM_HEADS, MEM_HEAD_DIM), lambda i: (i, 0))] * 2
        + [pl.BlockSpec((tile, D_MEM), lambda i: (i, 0))] * 2 + cast_specs,
        out_shape=[jax.ShapeDtypeStruct((rows * MEM_HEADS, MEM_HEAD_DIM), F32)] * 2
        + [jax.ShapeDtypeStruct((rows, D_MEM), BF16)] * 2
        + [jax.ShapeDtypeStruct(w.shape, BF16) for w in later_weights],
        scratch_shapes=[pltpu.VMEM(w_kv.shape, BF16)],
        compiler_params=pltpu.CompilerParams(
            dimension_semantics=("arbitrary",), vmem_limit_bytes=VMEM_LIMIT),
        name="memory_kv",
    )(mem2d, gain, w_kv, *later_weights)


def _prompt_mixer_kernel(x_ref, kb_ref, vb_ref, pre_g_ref, post_g_ref, conv_w_ref, conv_b_ref,
                         ln_g_ref, ln_b_ref, lb_logits_ref, hg_g_ref, w_in_ref, w_conv_out_ref,
                         w_hg_out_ref, w_mem_out_ref, w_out_ref, *rest):
    n_cast = (len(rest) - 6) // 2
    cast_src = rest[:n_cast]
    y_ref, new_conv_ref, new_hg_ref = rest[n_cast:n_cast + 3]
    cast_dst = rest[n_cast + 3:2 * n_cast + 3]
    shift_ref, tail_ref, state_t_ref = rest[2 * n_cast + 3:]
    _cast_blocks(cast_src, cast_dst)
    t = pl.program_id(1)
    last_t = pl.num_programs(1) - 1
    tl = PROMPT_TILE
    n_tiles = x_ref.shape[0] // tl

    @pl.when(t == 0)
    def _():
        tail_ref[...] = jnp.zeros_like(tail_ref)
        state_t_ref[...] = jnp.zeros_like(state_t_ref)

    lb = _lower_bound(lb_logits_ref[...])
    half = tl // 2
    row = lax.broadcasted_iota(jnp.int32, (tl, tl), 0)
    col = lax.broadcasted_iota(jnp.int32, (tl, tl), 1)
    span = row ^ col
    near = (span < 2 * CHUNK) & (col <= row)
    mid = (span < 4 * CHUNK) & (col < row)

    def mix_rows():
        n_rows = n_tiles * tl
        shift = shift_ref
        x = x_ref[...]
        hb = _rms(x, pre_g_ref[...]).astype(BF16)

        def proj(c0, c1):
            return _dot(hb, w_in_ref[:, c0:c1])

        u = proj(COL_CA, COL_CB) * _sigmoid(proj(COL_CB, COL_HQ))
        shift[0, 0:CONV_ROWS, :] = tail_ref[...]
        shift[0, CONV_ROWS:CONV_ROWS + n_rows, :] = u
        tail_ref[...] = u[n_rows - CONV_ROWS:]
        for r in range(1, SUBLANES):
            shift[r, 0:n_rows + CONV_ROWS - SUBLANES, :] = shift[0, r:r + n_rows + CONV_ROWS - SUBLANES, :]

        conv_blocks = []
        for base in range(0, n_rows, CONV_ROWS):
            acc = None
            for j in range(CONV_WIDTH):
                s = j + (CONV_ROWS - CONV_PREFIX)
                r, a = s % SUBLANES, s // SUBLANES
                lo = base + a * SUBLANES
                term = shift[r, lo:lo + CONV_ROWS, :] * conv_w_ref[j:j + 1, :]
                acc = term if acc is None else acc + term
            conv_blocks.append(acc)
        c = _conv_ln_silu(jnp.concatenate(conv_blocks, axis=0),
                          conv_b_ref[...], ln_g_ref[...], ln_b_ref[...]).astype(BF16)

        hq = proj(COL_HQ, COL_HF)
        hf = proj(COL_HF, COL_HI)
        levels = []
        for i in range(n_tiles):
            lv = _hgrn_tile_levels(hq[i * tl:(i + 1) * tl], hf[i * tl:(i + 1) * tl], lb)
            levels.append(_TileLevels(*[a.astype(BF16) for a in lv[:-1]], lv.decay))
        vb16 = proj(COL_HI, COL_HGATE).astype(BF16)
        hgate = _silu(proj(COL_HGATE, COL_MQ))
        mq = proj(COL_MQ, COL_GATES).astype(BF16)
        gates = [_branch_gate(hb, w_in_ref, g) for g in range(3)]

        head = lambda h: slice(h * MEM_HEAD_DIM, (h + 1) * MEM_HEAD_DIM)
        s = jnp.concatenate([_dot_nt(mq[:, head(h)], kb_ref[:, head(h)]) for h in range(MEM_HEADS)],
                            axis=0) * (MEM_HEAD_DIM ** -0.5)
        e = jnp.exp(s - jnp.max(s, axis=-1, keepdims=True))
        p = (e / jnp.sum(e, axis=-1, keepdims=True)).astype(BF16)
        om = [_dot(p[h * n_rows:(h + 1) * n_rows], vb_ref[:, head(h)]) for h in range(MEM_HEADS)]

        o_tiles = []
        for i, lvb in enumerate(levels):
            o_heads = []
            for h in range(HG_HEADS):
                hs = slice(h * HG_DK, (h + 1) * HG_DK)
                vh = vb16[i * tl:(i + 1) * tl, hs]
                s_near = _dot_nt(lvb.q_mid[:, hs], lvb.k_mid[:, hs])
                s_mid = _dot_nt(lvb.q_64[:, hs], lvb.k_64[:, hs])
                s_far = _dot_nt(lvb.q_128[:, hs], lvb.k_128[:, hs])
                base = jnp.where(near, s_near, jnp.where(mid, s_mid, 0.0))
                scores = jnp.concatenate(
                    [base[:half], jnp.concatenate([s_far, base[half:, half:]], axis=1)], axis=0)
                st = state_t_ref[h]
                o_heads.append(_dot(scores.astype(BF16), vh)
                               + _dot_nt(lvb.q_tile[:, hs], st.astype(BF16)))
                state_t_ref[h] = st * lvb.decay[:, hs] + _dot_tn(vh, lvb.k_tile[:, hs])
            o_tiles.append(jnp.concatenate(o_heads, axis=-1))
        o = _head_rms(jnp.concatenate(o_tiles, axis=0), hg_g_ref[...]) * hgate

        p_conv = _dot(c, w_conv_out_ref[...])
        p_mem = _dot(jnp.concatenate(om, axis=-1).astype(BF16), w_mem_out_ref[...])
        p_hg = _dot(o.astype(BF16), w_hg_out_ref[...])
        y_ref[...] = _merge_out(lambda g: gates[g], x, (p_conv, p_hg, p_mem), w_out_ref, post_g_ref[...])

    mix_rows()

    @pl.when(t == last_t)
    def _():
        new_conv_ref[...] = tail_ref[CONV_ROWS - CONV_PREFIX:CONV_ROWS, :]
        for h in range(HG_HEADS):
            new_hg_ref[h] = state_t_ref[h].T


def _const_spec(shape):
    zeros = (0,) * len(shape)
    return pl.BlockSpec(shape, lambda *_: zeros, pipeline_mode=pl.Buffered(1))


def _mixer_weight_specs(w_in, w_conv_out, w_hg_out, w_mem_out, w_out):
    return [_const_spec(w.shape) for w in (w_in, w_conv_out, w_hg_out, w_mem_out, w_out)]


def _vector_specs():
    return [
        _const_spec((1, D_MODEL)), _const_spec((1, D_MODEL)),
        _const_spec((CONV_WIDTH, D_CONV)), _const_spec((1, D_CONV)),
        _const_spec((1, D_CONV)), _const_spec((1, D_CONV)),
        _const_spec((2, D_HGRN)), _const_spec((1, D_HGRN)),
    ]


def _prompt_mixer(x, mk, mv, vectors, weights, later_weights):
    batch, seq, _ = x.shape
    n_mem = mk.shape[1]
    tl = PROMPT_TILE * PROMPT_STEP_TILES
    n_t = seq // tl
    cast_specs = _cast_specs(later_weights, batch * n_t, lambda b, t: b * n_t + t)
    return pl.pallas_call(
        _prompt_mixer_kernel,
        grid=(batch, n_t),
        in_specs=[
            pl.BlockSpec((None, tl, D_MODEL), lambda b, t: (b, t, 0)),
            pl.BlockSpec((None, n_mem, D_MEM), lambda b, t: (b, 0, 0)),
            pl.BlockSpec((None, n_mem, D_MEM), lambda b, t: (b, 0, 0)),
        ] + _vector_specs() + _mixer_weight_specs(*weights) + cast_specs,
        out_specs=[
            pl.BlockSpec((None, tl, D_MODEL), lambda b, t: (b, t, 0)),
            pl.BlockSpec((None, CONV_PREFIX, D_CONV), lambda b, t: (b, 0, 0)),
            pl.BlockSpec((None, HG_HEADS, HG_DK, HG_DV), lambda b, t: (b, 0, 0, 0)),
        ] + cast_specs,
        out_shape=[
            jax.ShapeDtypeStruct((batch, seq, D_MODEL), F32),
            jax.ShapeDtypeStruct((batch, CONV_PREFIX, D_CONV), F32),
            jax.ShapeDtypeStruct((batch, HG_HEADS, HG_DK, HG_DV), F32),
        ] + [jax.ShapeDtypeStruct(w.shape, BF16) for w in later_weights],
        scratch_shapes=[
            pltpu.VMEM((SUBLANES, tl + CONV_ROWS, D_CONV), F32),
            pltpu.VMEM((CONV_ROWS, D_CONV), F32),
            pltpu.VMEM((HG_HEADS, HG_DV, HG_DK), F32),
        ],
        compiler_params=pltpu.CompilerParams(
            dimension_semantics=("arbitrary", "arbitrary"), vmem_limit_bytes=VMEM_LIMIT),
        name="prompt_mixer",
    )(x, mk, mv, *vectors, *weights, *later_weights)


def _store_lane_chunks(ref, x):
    for i in range(ref.shape[0]):
        ref[i] = x[:, i * LANES:(i + 1) * LANES]


def _load_lane_chunks(ref, rows=slice(None)):
    return jnp.concatenate([ref[i, rows, :] for i in range(ref.shape[0])], axis=1)


def _lane_chunk_spec(tile, width):
    return pl.BlockSpec((width // LANES, tile, LANES), lambda i: (0, i, 0))


def _sample_in_kernel(x_ref, pre_g_ref, lb_logits_ref, w_in_ref,
                      u_ref, qd_ref, ki_ref, ke_ref, v_ref, blast_ref, q4_ref, hgate_ref, gates_ref):
    n_seq = x_ref.shape[0] // SUBLANES
    hb = _rms(x_ref[...], pre_g_ref[...]).astype(BF16)
    _store_lane_chunks(
        u_ref, _dot(hb, w_in_ref[:, COL_CA:COL_CB]) * _sigmoid(_dot(hb, w_in_ref[:, COL_CB:COL_HQ])))
    lb = _lower_bound(lb_logits_ref[...])
    q_dec, k_inv, k_end, b_last = _hgrn_gates(
        _dot(hb, w_in_ref[:, COL_HQ:COL_HF]), _dot(hb, w_in_ref[:, COL_HF:COL_HI]), lb, SUBLANES)
    qd_ref[...] = q_dec
    ki_ref[...] = k_inv
    ke_ref[...] = k_end
    blast_ref[...] = b_last
    v_ref[...] = _dot(hb, w_in_ref[:, COL_HI:COL_HGATE])
    hgate_ref[...] = _silu(_dot(hb, w_in_ref[:, COL_HGATE:COL_MQ]))
    mq = _dot(hb, w_in_ref[:, COL_MQ:COL_GATES])
    for h in range(MEM_HEADS):
        q4_ref[:, h * SUBLANES:(h + 1) * SUBLANES, :] = (
            mq[:, h * MEM_HEAD_DIM:(h + 1) * MEM_HEAD_DIM].reshape(n_seq, SUBLANES, MEM_HEAD_DIM))
    for i in range(3):
        gates_ref[:, i * D_MODEL:(i + 1) * D_MODEL] = _branch_gate(hb, w_in_ref, i)


def _sample_seq_kernel(u_ref, qd_ref, ki_ref, ke_ref, v_ref, blast_ref, q4_ref, sc_ref, sh_ref,
                       k2_ref, v2_ref, conv_w_ref,
                       c_ref, o_ref, om_ref, new_conv_ref, new_hg_ref):
    n_seq, dec = sc_ref.shape[1], SUBLANES
    rows = n_seq * dec

    assert dec == 8 and MEM_HEADS == 4
    q_rows = MEM_HEADS * dec
    s = jnp.concatenate([_dot_nt(q4_ref[g].astype(BF16), k2_ref[g].astype(BF16)) for g in range(n_seq)],
                        axis=0) * (MEM_HEAD_DIM ** -0.5)
    q_head = (lax.broadcasted_iota(jnp.int32, s.shape, 0) >> 3) & 3
    k_head = lax.broadcasted_iota(jnp.int32, s.shape, 1) & 3
    s = jnp.where(q_head == k_head, s, -jnp.inf)
    e = jnp.exp(s - jnp.max(s, axis=-1, keepdims=True))
    p = (e / jnp.sum(e, axis=-1, keepdims=True)).astype(BF16)
    om = []
    for g in range(n_seq):
        om_g = _dot(p[g * q_rows:(g + 1) * q_rows], v2_ref[g].astype(BF16))
        om.extend(om_g[h * dec:(h + 1) * dec] for h in range(MEM_HEADS))

    u_slabs = [_load_lane_chunks(u_ref, pl.ds(step, n_seq, stride=dec)) for step in range(dec)]
    conv = [None] * dec
    for s in range(CONV_PREFIX + dec):
        slab = sc_ref[s] if s < CONV_PREFIX else u_slabs[s - CONV_PREFIX]
        if s >= dec:
            new_conv_ref[s - dec] = slab
        for step in range(max(0, s - CONV_PREFIX), min(dec, s + 1)):
            term = slab * conv_w_ref[s - step:s - step + 1, :]
            conv[step] = term if conv[step] is None else conv[step] + term

    row = lax.broadcasted_iota(jnp.int32, (rows, rows), 0)
    col = lax.broadcasted_iota(jnp.int32, (rows, rows), 1)
    causal = ((row ^ col) < dec) & (col <= row)
    decay = jnp.exp(blast_ref[...])
    for h in range(HG_HEADS):
        hs = slice(h * HG_DK, (h + 1) * HG_DK)
        scores = jnp.where(causal, _dot_nt(qd_ref[:, hs].astype(BF16), ki_ref[:, hs].astype(BF16)), 0.0)
        o_intra = _dot(scores.astype(BF16), v_ref[:, hs].astype(BF16))
        for g in range(n_seq):
            rs = slice(g * dec, (g + 1) * dec)
            s0 = sh_ref[g, h]
            o_ref[rs, hs] = o_intra[rs] + _dot(qd_ref[rs, hs].astype(BF16), s0.astype(BF16))
            decay_col = jnp.broadcast_to(decay[g:g + 1, hs], (HG_DV, HG_DK)).T
            new_hg_ref[g, h] = decay_col * s0 + _dot_tn(ke_ref[rs, hs].astype(BF16),
                                                        v_ref[rs, hs].astype(BF16))

    for step in range(dec):
        for i in range(c_ref.shape[0]):
            c_ref[i, pl.ds(step, n_seq, stride=dec), :] = conv[step][:, i * LANES:(i + 1) * LANES]
    for g in range(n_seq):
        om_ref[g * dec:(g + 1) * dec, :] = jnp.concatenate(om[g * MEM_HEADS:(g + 1) * MEM_HEADS], axis=1)


def _sample_out_kernel(x_ref, c_ref, o_ref, om_ref, hgate_ref, gates_ref, post_g_ref, conv_b_ref,
                       ln_g_ref, ln_b_ref, hg_g_ref, w_conv_out_ref, w_hg_out_ref, w_mem_out_ref,
                       w_out_ref, ffn_pre_g_ref, ffn_post_g_ref, w_gate_ref, w_up_ref, w_down_ref, y_ref):
    c = _conv_ln_silu(_load_lane_chunks(c_ref), conv_b_ref[...], ln_g_ref[...], ln_b_ref[...])
    p_conv = _dot(c.astype(BF16), w_conv_out_ref[...])
    o = _head_rms(o_ref[...], hg_g_ref[...]) * hgate_ref[...]
    p_hg = _dot(o.astype(BF16), w_hg_out_ref[...])
    p_mem = _dot(om_ref[...].astype(BF16), w_mem_out_ref[...])
    x1 = _merge_out(lambda i: gates_ref[:, i * D_MODEL:(i + 1) * D_MODEL], x_ref[...],
                    (p_conv, p_hg, p_mem), w_out_ref, post_g_ref[...])
    y_ref[...] = _ffn_block(x1, ffn_pre_g_ref[...], ffn_post_g_ref[...], w_gate_ref, w_up_ref, w_down_ref)


def _row_spec(tile, width):
    return pl.BlockSpec((tile, width), lambda i: (i, 0))


def _sample_in(x2d, pre_g, lb_logits, w_in):
    rows = x2d.shape[0]
    tile = SAMPLE_TILE
    seqs = tile // SUBLANES
    wide = [D_HGRN, D_HGRN, D_HGRN, D_HGRN]
    return pl.pallas_call(
        _sample_in_kernel,
        grid=(rows // tile,),
        in_specs=[_row_spec(tile, D_MODEL), _const_spec((1, D_MODEL)), _const_spec((2, D_HGRN)),
                  _const_spec(w_in.shape)],
        out_specs=[_lane_chunk_spec(tile, D_CONV)] + [_row_spec(tile, w) for w in wide] + [
            _row_spec(seqs, D_HGRN),
            pl.BlockSpec((seqs, MEM_HEADS * SUBLANES, MEM_HEAD_DIM), lambda i: (i, 0, 0)),
            _row_spec(tile, D_HGRN),
            _row_spec(tile, 3 * D_MODEL),
        ],
        out_shape=[jax.ShapeDtypeStruct((D_CONV // LANES, rows, LANES), F32)]
        + [jax.ShapeDtypeStruct((rows, w), F32) for w in wide] + [
            jax.ShapeDtypeStruct((rows // SUBLANES, D_HGRN), F32),
            jax.ShapeDtypeStruct((rows // SUBLANES, MEM_HEADS * SUBLANES, MEM_HEAD_DIM), F32),
            jax.ShapeDtypeStruct((rows, D_HGRN), F32),
            jax.ShapeDtypeStruct((rows, 3 * D_MODEL), F32),
        ],
        compiler_params=pltpu.CompilerParams(
            dimension_semantics=("parallel",), vmem_limit_bytes=VMEM_LIMIT),
        name="sample_in",
    )(x2d, pre_g, lb_logits, w_in)


def _sample_seq(u, qd, ki, ke, v, blast, q4, conv_rows, state_hgrn, k2, v2, conv_w):
    n = conv_rows.shape[1]
    g = SAMPLE_GROUP
    rows = g * SUBLANES
    kv_rows = k2.shape[1]
    seq_block = lambda *tail: pl.BlockSpec((g,) + tail, lambda i: (i,) + (0,) * len(tail))
    conv_block = pl.BlockSpec((CONV_PREFIX, g, D_CONV), lambda i: (0, i, 0))
    return pl.pallas_call(
        _sample_seq_kernel,
        grid=(n // g,),
        in_specs=[_lane_chunk_spec(rows, D_CONV)] + [_row_spec(rows, D_HGRN)] * 4 + [
            _row_spec(g, D_HGRN),
            seq_block(MEM_HEADS * SUBLANES, MEM_HEAD_DIM),
            conv_block,
            seq_block(HG_HEADS, HG_DK, HG_DV),
            seq_block(kv_rows, MEM_HEAD_DIM),
            seq_block(kv_rows, MEM_HEAD_DIM),
            _const_spec((CONV_WIDTH, D_CONV)),
        ],
        out_specs=[_lane_chunk_spec(rows, D_CONV), _row_spec(rows, D_HGRN), _row_spec(rows, D_MEM),
                   conv_block, seq_block(HG_HEADS, HG_DK, HG_DV)],
        out_shape=[
            jax.ShapeDtypeStruct((D_CONV // LANES, n * SUBLANES, LANES), F32),
            jax.ShapeDtypeStruct((n * SUBLANES, D_HGRN), F32),
            jax.ShapeDtypeStruct((n * SUBLANES, D_MEM), F32),
            jax.ShapeDtypeStruct((CONV_PREFIX, n, D_CONV), F32),
            jax.ShapeDtypeStruct((n, HG_HEADS, HG_DK, HG_DV), F32),
        ],
        compiler_params=pltpu.CompilerParams(
            dimension_semantics=("parallel",), vmem_limit_bytes=VMEM_LIMIT),
        name="sample_seq",
    )(u, qd, ki, ke, v, blast, q4, conv_rows, state_hgrn, k2, v2, conv_w)


def _sample_out(x2d, c, o, om, hgate, gates, vectors, weights, ffn_vectors, ffn_weights):
    rows = x2d.shape[0]
    tile = SAMPLE_TILE
    acts = (x2d, c, o, om, hgate, gates)
    consts = tuple(vectors) + tuple(weights) + tuple(ffn_vectors) + tuple(ffn_weights)
    return pl.pallas_call(
        _sample_out_kernel,
        grid=(rows // tile,),
        in_specs=[_lane_chunk_spec(tile, D_CONV) if a is c else _row_spec(tile, a.shape[1]) for a in acts]
        + [_const_spec(a.shape) for a in consts],
        out_specs=_row_spec(tile, D_MODEL),
        out_shape=jax.ShapeDtypeStruct((rows, D_MODEL), F32),
        compiler_params=pltpu.CompilerParams(
            dimension_semantics=("parallel",), vmem_limit_bytes=VMEM_LIMIT),
        name="sample_out",
    )(*acts, *consts)


def _ffn_kernel(x_ref, pre_g_ref, post_g_ref, w_gate_ref, w_up_ref, w_down_ref, y_ref):
    y_ref[...] = _ffn_block(x_ref[...], pre_g_ref[...], post_g_ref[...], w_gate_ref, w_up_ref, w_down_ref)


def _ffn(x2d, pre_g, post_g, w_gate, w_up, w_down):
    rows = x2d.shape[0]
    tile = min(FFN_TILE, rows)
    return pl.pallas_call(
        _ffn_kernel,
        grid=(rows // tile,),
        in_specs=[
            pl.BlockSpec((tile, D_MODEL), lambda i: (i, 0)),
            _const_spec((1, D_MODEL)), _const_spec((1, D_MODEL)),
            _const_spec(w_gate.shape), _const_spec(w_up.shape), _const_spec(w_down.shape),
        ],
        out_specs=pl.BlockSpec((tile, D_MODEL), lambda i: (i, 0)),
        out_shape=jax.ShapeDtypeStruct((rows, D_MODEL), F32),
        compiler_params=pltpu.CompilerParams(
            dimension_semantics=("parallel",), vmem_limit_bytes=VMEM_LIMIT),
        name="ffn",
    )(x2d, pre_g, post_g, w_gate, w_up, w_down)


def kernel(x_prompt, x_sample, mem_prompt, state_conv, state_hgrn, cache_mem_k, cache_mem_v, norm_pre_mix, norm_post_mix, norm_pre_ffn, norm_post_ffn, w_in, conv_w, conv_b, conv_ln_g, conv_ln_b, w_conv_out, hg_lb_logits, hg_norm_g, w_hg_out, mem_norm_g, w_mem_kv, w_mem_out, w_out, w_ffn_gate, w_ffn_up, w_ffn_down):
    depth = w_in.shape[0]
    assert depth == 1 and hg_lb_logits.shape[0] == 2, "single-layer step"
    batch, seq, _ = x_prompt.shape
    n_dec, dec, _ = x_sample.shape
    n_mem = mem_prompt.shape[1]
    assert seq % PROMPT_TILE == 0 and n_dec % SAMPLE_GROUP == 0 and dec == SUBLANES
    assert (n_dec * dec) % SAMPLE_TILE == 0

    vectors = (norm_pre_mix, norm_post_mix, conv_w[0], conv_b, conv_ln_g, conv_ln_b,
               hg_lb_logits, jnp.tile(hg_norm_g, (1, HG_HEADS)))

    mk, mv, kb, vb, *weights = _memory_kv(
        mem_prompt.reshape(batch * n_mem, D_MODEL), mem_norm_g, w_mem_kv[0], n_mem,
        tuple(w[0] for w in (w_in, w_conv_out, w_hg_out, w_mem_out, w_out)))

    xp, conv_p, hg_p, *ffn_weights = _prompt_mixer(
        x_prompt, kb.reshape(batch, n_mem, D_MEM), vb.reshape(batch, n_mem, D_MEM), vectors, weights,
        (w_ffn_gate[0], w_ffn_up[0], w_ffn_down[0]))
    yp = _ffn(xp.reshape(batch * seq, D_MODEL), norm_pre_ffn, norm_post_ffn, *ffn_weights)

    xs2d = x_sample.reshape(n_dec * dec, D_MODEL)
    u, qd, ki, ke, v, blast, q4, hgate, gates = _sample_in(xs2d, norm_pre_mix, hg_lb_logits, weights[0])
    k2 = cache_mem_k[0].reshape(n_dec, n_mem * MEM_HEADS, MEM_HEAD_DIM)
    v2 = cache_mem_v[0].reshape(n_dec, n_mem * MEM_HEADS, MEM_HEAD_DIM)
    c, o, om, conv_rows, hg_s = _sample_seq(u, qd, ki, ke, v, blast, q4,
                                            jnp.transpose(state_conv[0], (1, 0, 2)), state_hgrn[0],
                                            k2, v2, conv_w[0])
    conv_s = jnp.transpose(conv_rows, (1, 0, 2))
    ys = _sample_out(xs2d, c, o, om, hgate, gates,
                     (norm_post_mix, conv_b, conv_ln_g, conv_ln_b, vectors[-1]), weights[1:],
                     (norm_pre_ffn, norm_post_ffn), ffn_weights)

    kv_shape = (1, batch, n_mem, MEM_HEADS, MEM_HEAD_DIM)
    return (yp.reshape(batch, seq, D_MODEL), ys.reshape(n_dec, dec, D_MODEL),
            conv_p[None], hg_p[None], mk.reshape(kv_shape), mv.reshape(kv_shape),
            conv_s[None], hg_s[None])
```

```python
from typing import NamedTuple

import jax
import jax.numpy as jnp
from jax import lax
from jax.experimental import pallas as pl
from jax.experimental.pallas import tpu as pltpu

D_MODEL = 1024
D_CONV = 512
CONV_WIDTH = 31
CONV_PREFIX = CONV_WIDTH - 1
HG_HEADS = 4
HG_DK = 128
HG_DV = 128
D_HGRN = HG_HEADS * HG_DK
MEM_HEADS = 4
MEM_HEAD_DIM = 128
D_MEM = MEM_HEADS * MEM_HEAD_DIM
CHUNK = 32
EPS = 1e-6
LOG2E = 1.4426950408889634

COL_CA = 0
COL_CB = COL_CA + D_CONV
COL_HQ = COL_CB + D_CONV
COL_HF = COL_HQ + D_HGRN
COL_HI = COL_HF + D_HGRN
COL_HGATE = COL_HI + D_HGRN
COL_MQ = COL_HGATE + D_HGRN
COL_GATES = COL_MQ + D_MEM

SUBLANES = 8
LANES = 128
VMEM_LIMIT = 56 * 1024 * 1024

PROMPT_TILE = 256
PROMPT_STEP_TILES = 2
SAMPLE_GROUP = 8
SAMPLE_TILE = 256
FFN_TILE = 1024
FFN_BLOCK = 256
CONV_ROWS = 32

BF16 = jnp.bfloat16
F32 = jnp.float32


def _dot(a, b):
    return jnp.dot(a, b, preferred_element_type=F32)


def _dot_nt(a, b):
    return lax.dot_general(a, b, (((1,), (1,)), ((), ())), preferred_element_type=F32)


def _dot_tn(a, b):
    return lax.dot_general(a, b, (((0,), (0,)), ((), ())), preferred_element_type=F32)


def _rms(x, gain):
    return x * lax.rsqrt(jnp.mean(x * x, axis=-1, keepdims=True) + EPS) * gain


def _sigmoid(x):
    return 1.0 / (1.0 + jnp.exp2(x * -LOG2E))


def _silu(x):
    return x * _sigmoid(x)


def _attention_softmax(scores, keep=None):
    t = scores * (MEM_HEAD_DIM ** -0.5 * LOG2E)
    if keep is not None:
        t = jnp.where(keep, t, -jnp.inf)
    e = jnp.exp2(t - jnp.max(t, axis=-1, keepdims=True))
    return e / jnp.sum(e, axis=-1, keepdims=True)


def _forget_terms(hf, lb):
    sig = _sigmoid(hf)
    return jnp.log(lb + (1.0 - lb) * sig), (1.0 - lb) * (1.0 - sig)


def _lower_bound(lb_logits):
    m = jnp.max(lb_logits, axis=0, keepdims=True)
    e = jnp.exp(lb_logits - m)
    return e[0:1] / jnp.sum(e, axis=0, keepdims=True)


def _segment_cumsum(x, seg):
    pos = lax.broadcasted_iota(jnp.int32, x.shape, 0) & (seg - 1)
    s = 1
    while s < seg:
        x = x + jnp.where(pos >= s, pltpu.roll(x, s, axis=0), 0.0)
        s *= 2
    return x


def _conv_ln_silu(c, conv_b, ln_g, ln_b):
    c = c + conv_b
    mu = jnp.mean(c, axis=-1, keepdims=True)
    d = c - mu
    var = jnp.mean(d * d, axis=-1, keepdims=True)
    return _silu(d * lax.rsqrt(var + EPS) * ln_g + ln_b)


def _hgrn_gates(hq, hf, lb, seg):
    q = _silu(hq)
    logf, k = _forget_terms(hf, lb)
    b = _segment_cumsum(logf, seg)
    rows = b.shape[0]
    b3 = b.reshape(rows // seg, seg, D_HGRN)
    b_last3 = b3[:, seg - 1:seg, :]
    rest = (b_last3 - b3).reshape(rows, D_HGRN)
    q_dec = q * jnp.exp(b)
    k_inv = k * jnp.exp(-b)
    k_end = k * jnp.exp(rest)
    return q_dec, k_inv, k_end, b_last3.reshape(rows // seg, D_HGRN)


class _TileLevels(NamedTuple):
    q_mid: jax.Array
    k_mid: jax.Array
    q_64: jax.Array
    k_64: jax.Array
    q_128: jax.Array
    k_128: jax.Array
    q_tile: jax.Array
    k_tile: jax.Array
    decay: jax.Array


def _hgrn_tile_levels(hq, hf, lb):
    tl = hq.shape[0]
    n_blocks = 4
    block = 2 * CHUNK
    assert tl == n_blocks * block
    q = _silu(hq)
    logf, k = _forget_terms(hf, lb)
    b = _segment_cumsum(logf, CHUNK)
    chunks = [b[n * CHUNK:(n + 1) * CHUNK] for n in range(2 * n_blocks)]
    totals = [c[CHUNK - 1:CHUNK] for c in chunks]
    e = jnp.concatenate([c - totals[n] if n % 2 == 0 else c for n, c in enumerate(chunks)], axis=0)
    q_mid = q * jnp.exp(e)
    k_mid = k * jnp.exp(-e)
    first = [totals[2 * j] for j in range(n_blocks)]
    second = [totals[2 * j + 1] for j in range(n_blocks)]
    both = [first[j] + second[j] for j in range(n_blocks)]

    def scaled(x, blocks, log_scales):
        return jnp.concatenate(
            [x[j * block:(j + 1) * block] * jnp.exp(s) for j, s in zip(blocks, log_scales)], axis=0)

    every = range(n_blocks)
    return _TileLevels(
        q_mid=q_mid, k_mid=k_mid,
        q_64=scaled(q_mid, every, first),
        k_64=scaled(k_mid, every, second),
        q_128=scaled(q_mid, (2, 3), (first[2], first[3] + both[2])),
        k_128=scaled(k_mid, (0, 1), (second[0] + both[1], second[1])),
        q_tile=scaled(q_mid, every, [first[j] + sum(both[:j], 0.0) for j in every]),
        k_tile=scaled(k_mid, every, [second[j] + sum(both[j + 1:], 0.0) for j in every]),
        decay=jnp.exp(sum(both[1:], both[0])))


def _head_rms(o, gain4):
    parts = []
    for h in range(HG_HEADS):
        oh = o[:, h * HG_DV:(h + 1) * HG_DV]
        parts.append(oh * lax.rsqrt(jnp.mean(oh * oh, axis=-1, keepdims=True) + EPS))
    return jnp.concatenate(parts, axis=-1) * gain4


def _branch_gate(hb, w_in_ref, i):
    c0 = COL_GATES + i * D_MODEL
    return _sigmoid(_dot(hb, w_in_ref[:, c0:c0 + D_MODEL]))


def _merge_out(gate, x, branches, w_out_ref, post_g):
    acc = None
    for i, p in enumerate(branches):
        acc = gate(i) * p if acc is None else acc + gate(i) * p
    m = _dot(acc.astype(BF16), w_out_ref[...])
    return x + _rms(m, post_g)


def _ffn_block(x, pre_g, post_g, w_gate_ref, w_up_ref, w_down_ref):
    hb = _rms(x, pre_g).astype(BF16)
    f = _silu(_dot(hb, w_gate_ref[...])) * _dot(hb, w_up_ref[...])
    d = _dot(f.astype(BF16), w_down_ref[...])
    return x + _rms(d, post_g)


def _cast_block_rows(rows, n_steps):
    bf16_sublanes = 2 * SUBLANES
    for n_blocks in range(min(n_steps, rows // bf16_sublanes), 0, -1):
        if rows % n_blocks == 0 and (rows // n_blocks) % bf16_sublanes == 0:
            return rows // n_blocks
    raise ValueError(f"no bf16-tileable split of {rows} rows")


def _cast_specs(weights, n_steps, step_of):
    specs = []
    for w in weights:
        rows_blk = _cast_block_rows(w.shape[0], n_steps)
        last = w.shape[0] // rows_blk - 1
        specs.append(pl.BlockSpec((rows_blk, w.shape[1]),
                                  lambda *idx, last=last: (jnp.minimum(step_of(*idx), last), 0)))
    return specs


def _cast_blocks(srcs, dsts):
    for src, dst in zip(srcs, dsts):
        dst[...] = src[...].astype(BF16)


def _memory_kv_kernel(mem_ref, g_ref, w_ref, *rest):
    n_cast = (len(rest) - 5) // 2
    k_rows_ref, v_rows_ref, k_ref, v_ref = rest[n_cast:n_cast + 4]
    wb_ref = rest[-1]
    _cast_blocks(rest[:n_cast], rest[n_cast + 4:-1])

    @pl.when(pl.program_id(0) == 0)
    def _():
        wb_ref[...] = w_ref[...].astype(BF16)

    m = _rms(mem_ref[...], g_ref[...]).astype(BF16)
    tokens = mem_ref.shape[0]
    for full, rows_ref, op_ref in ((_dot(m, wb_ref[:, :D_MEM]), k_rows_ref, k_ref),
                                   (_dot(m, wb_ref[:, D_MEM:]), v_rows_ref, v_ref)):
        op_ref[...] = full.astype(BF16)
        for h in range(MEM_HEADS):
            rows_ref[pl.ds(h, tokens, stride=MEM_HEADS), :] = full[:, h * MEM_HEAD_DIM:(h + 1) * MEM_HEAD_DIM]


def _memory_kv(mem2d, gain, w_kv, tile, later_weights):
    rows = mem2d.shape[0]
    n_steps = rows // tile
    cast_specs = _cast_specs(later_weights, n_steps, lambda i: i)
    return pl.pallas_call(
        _memory_kv_kernel,
        grid=(n_steps,),
        in_specs=[
            pl.BlockSpec((tile, D_MODEL), lambda i: (i, 0)),
            pl.BlockSpec((1, D_MODEL), lambda i: (0, 0)),
            _const_spec(w_kv.shape),
        ] + cast_specs,
        out_specs=[pl.BlockSpec((tile * MEM_HEADS, MEM_HEAD_DIM), lambda i: (i, 0))] * 2
        + [pl.BlockSpec((tile, D_MEM), lambda i: (i, 0))] * 2 + cast_specs,
        out_shape=[jax.ShapeDtypeStruct((rows * MEM_HEADS, MEM_HEAD_DIM), F32)] * 2
        + [jax.ShapeDtypeStruct((rows, D_MEM), BF16)] * 2
        + [jax.ShapeDtypeStruct(w.shape, BF16) for w in later_weights],
        scratch_shapes=[pltpu.VMEM(w_kv.shape, BF16)],
        compiler_params=pltpu.CompilerParams(
            dimension_semantics=("arbitrary",), vmem_limit_bytes=VMEM_LIMIT),
        name="memory_kv",
    )(mem2d, gain, w_kv, *later_weights)


def _prompt_mixer_kernel(x_ref, kb_ref, vb_ref, pre_g_ref, post_g_ref, conv_w_ref, conv_b_ref,
                         ln_g_ref, ln_b_ref, lb_logits_ref, hg_g_ref, w_in_ref, w_conv_out_ref,
                         w_hg_out_ref, w_mem_out_ref, w_out_ref, *rest):
    n_cast = (len(rest) - 6) // 2
    cast_src = rest[:n_cast]
    y_ref, new_conv_ref, new_hg_ref = rest[n_cast:n_cast + 3]
    cast_dst = rest[n_cast + 3:2 * n_cast + 3]
    shift_ref, tail_ref, state_t_ref = rest[2 * n_cast + 3:]
    _cast_blocks(cast_src, cast_dst)
    t = pl.program_id(1)
    last_t = pl.num_programs(1) - 1
    tl = PROMPT_TILE
    n_tiles = x_ref.shape[0] // tl

    @pl.when(t == 0)
    def _():
        tail_ref[...] = jnp.zeros_like(tail_ref)
        state_t_ref[...] = jnp.zeros_like(state_t_ref)

    lb = _lower_bound(lb_logits_ref[...])
    half = tl // 2
    row = lax.broadcasted_iota(jnp.int32, (tl, tl), 0)
    col = lax.broadcasted_iota(jnp.int32, (tl, tl), 1)
    span = row ^ col
    near = (span < 2 * CHUNK) & (col <= row)
    mid = (span < 4 * CHUNK) & (col < row)

    def mix_rows():
        n_rows = n_tiles * tl
        shift = shift_ref
        x = x_ref[...]
        hb = _rms(x, pre_g_ref[...]).astype(BF16)

        def proj(c0, c1):
            return _dot(hb, w_in_ref[:, c0:c1])

        u = proj(COL_CA, COL_CB) * _sigmoid(proj(COL_CB, COL_HQ))
        shift[0:CONV_ROWS, :] = tail_ref[...]
        shift[CONV_ROWS:CONV_ROWS + n_rows, :] = u
        tail_ref[...] = u[n_rows - CONV_ROWS:]

        lead = CONV_ROWS - CONV_PREFIX
        conv = None
        for r in range(SUBLANES):
            rows_r = n_rows if r == 0 else n_rows + SUBLANES
            v_r = None
            for a in range((CONV_WIDTH + lead) // SUBLANES + 1):
                j = a * SUBLANES + r - lead
                if 0 <= j < CONV_WIDTH:
                    term = shift[a * SUBLANES:a * SUBLANES + rows_r, :] * conv_w_ref[j:j + 1, :]
                    v_r = term if v_r is None else v_r + term
            if r:
                v_r = pltpu.roll(v_r, rows_r - r, axis=0)[:n_rows]
            conv = v_r if conv is None else conv + v_r
        c = _conv_ln_silu(conv, conv_b_ref[...], ln_g_ref[...], ln_b_ref[...]).astype(BF16)

        hq = proj(COL_HQ, COL_HF)
        hf = proj(COL_HF, COL_HI)
        levels = []
        for i in range(n_tiles):
            lv = _hgrn_tile_levels(hq[i * tl:(i + 1) * tl], hf[i * tl:(i + 1) * tl], lb)
            levels.append(_TileLevels(*[a.astype(BF16) for a in lv[:-1]], lv.decay))
        vb16 = proj(COL_HI, COL_HGATE).astype(BF16)
        hgate = _silu(proj(COL_HGATE, COL_MQ))
        mq = proj(COL_MQ, COL_GATES).astype(BF16)
        gates = [_branch_gate(hb, w_in_ref, g) for g in range(3)]

        head = lambda h: slice(h * MEM_HEAD_DIM, (h + 1) * MEM_HEAD_DIM)
        s = jnp.concatenate([_dot_nt(mq[:, head(h)], kb_ref[:, head(h)]) for h in range(MEM_HEADS)],
                            axis=0)
        p = _attention_softmax(s).astype(BF16)
        om = [_dot(p[h * n_rows:(h + 1) * n_rows], vb_ref[:, head(h)]) for h in range(MEM_HEADS)]

        o_tiles = []
        for i, lvb in enumerate(levels):
            o_heads = []
            for h in range(HG_HEADS):
                hs = slice(h * HG_DK, (h + 1) * HG_DK)
                vh = vb16[i * tl:(i + 1) * tl, hs]
                s_near = _dot_nt(lvb.q_mid[:, hs], lvb.k_mid[:, hs])
                s_mid = _dot_nt(lvb.q_64[:, hs], lvb.k_64[:, hs])
                s_far = _dot_nt(lvb.q_128[:, hs], lvb.k_128[:, hs])
                base = jnp.where(near, s_near, jnp.where(mid, s_mid, 0.0))
                scores = jnp.concatenate(
                    [base[:half], jnp.concatenate([s_far, base[half:, half:]], axis=1)], axis=0)
                st = state_t_ref[h]
                o_heads.append(_dot(scores.astype(BF16), vh)
                               + _dot_nt(lvb.q_tile[:, hs], st.astype(BF16)))
                state_t_ref[h] = st * lvb.decay[:, hs] + _dot_tn(vh, lvb.k_tile[:, hs])
            o_tiles.append(jnp.concatenate(o_heads, axis=-1))
        o = _head_rms(jnp.concatenate(o_tiles, axis=0), hg_g_ref[...]) * hgate

        p_conv = _dot(c, w_conv_out_ref[...])
        p_mem = _dot(jnp.concatenate(om, axis=-1).astype(BF16), w_mem_out_ref[...])
        p_hg = _dot(o.astype(BF16), w_hg_out_ref[...])
        y_ref[...] = _merge_out(lambda g: gates[g], x, (p_conv, p_hg, p_mem), w_out_ref, post_g_ref[...])

    mix_rows()

    @pl.when(t == last_t)
    def _():
        new_conv_ref[...] = tail_ref[CONV_ROWS - CONV_PREFIX:CONV_ROWS, :]
        for h in range(HG_HEADS):
            new_hg_ref[h] = state_t_ref[h].T


def _const_spec(shape):
    zeros = (0,) * len(shape)
    return pl.BlockSpec(shape, lambda *_: zeros, pipeline_mode=pl.Buffered(1))


def _mixer_weight_specs(w_in, w_conv_out, w_hg_out, w_mem_out, w_out):
    return [_const_spec(w.shape) for w in (w_in, w_conv_out, w_hg_out, w_mem_out, w_out)]


def _vector_specs():
    return [
        _const_spec((1, D_MODEL)), _const_spec((1, D_MODEL)),
        _const_spec((CONV_WIDTH, D_CONV)), _const_spec((1, D_CONV)),
        _const_spec((1, D_CONV)), _const_spec((1, D_CONV)),
        _const_spec((2, D_HGRN)), _const_spec((1, D_HGRN)),
    ]


def _prompt_mixer(x, mk, mv, vectors, weights, later_weights):
    batch, seq, _ = x.shape
    n_mem = mk.shape[1]
    tl = PROMPT_TILE * PROMPT_STEP_TILES
    n_t = seq // tl
    cast_specs = _cast_specs(later_weights, batch * n_t, lambda b, t: b * n_t + t)
    return pl.pallas_call(
        _prompt_mixer_kernel,
        grid=(batch, n_t),
        in_specs=[
            pl.BlockSpec((None, tl, D_MODEL), lambda b, t: (b, t, 0)),
            pl.BlockSpec((None, n_mem, D_MEM), lambda b, t: (b, 0, 0)),
            pl.BlockSpec((None, n_mem, D_MEM), lambda b, t: (b, 0, 0)),
        ] + _vector_specs() + _mixer_weight_specs(*weights) + cast_specs,
        out_specs=[
            pl.BlockSpec((None, tl, D_MODEL), lambda b, t: (b, t, 0)),
            pl.BlockSpec((None, CONV_PREFIX, D_CONV), lambda b, t: (b, 0, 0)),
            pl.BlockSpec((None, HG_HEADS, HG_DK, HG_DV), lambda b, t: (b, 0, 0, 0)),
        ] + cast_specs,
        out_shape=[
            jax.ShapeDtypeStruct((batch, seq, D_MODEL), F32),
            jax.ShapeDtypeStruct((batch, CONV_PREFIX, D_CONV), F32),
            jax.ShapeDtypeStruct((batch, HG_HEADS, HG_DK, HG_DV), F32),
        ] + [jax.ShapeDtypeStruct(w.shape, BF16) for w in later_weights],
        scratch_shapes=[
            pltpu.VMEM((tl + CONV_ROWS, D_CONV), F32),
            pltpu.VMEM((CONV_ROWS, D_CONV), F32),
            pltpu.VMEM((HG_HEADS, HG_DV, HG_DK), F32),
        ],
        compiler_params=pltpu.CompilerParams(
            dimension_semantics=("arbitrary", "arbitrary"), vmem_limit_bytes=VMEM_LIMIT),
        name="prompt_mixer",
    )(x, mk, mv, *vectors, *weights, *later_weights)


def _store_lane_chunks(ref, x):
    for i in range(ref.shape[0]):
        ref[i] = x[:, i * LANES:(i + 1) * LANES]


def _load_lane_chunks(ref, rows=slice(None)):
    return jnp.concatenate([ref[i, rows, :] for i in range(ref.shape[0])], axis=1)


def _lane_chunk_spec(tile, width):
    return pl.BlockSpec((width // LANES, tile, LANES), lambda i: (0, i, 0))


def _sample_in_kernel(x_ref, pre_g_ref, lb_logits_ref, w_in_ref,
                      u_ref, qd_ref, ki_ref, ke_ref, v_ref, blast_ref, q4_ref, hgate_ref, gates_ref):
    n_seq = x_ref.shape[0] // SUBLANES
    hb = _rms(x_ref[...], pre_g_ref[...]).astype(BF16)
    _store_lane_chunks(
        u_ref, _dot(hb, w_in_ref[:, COL_CA:COL_CB]) * _sigmoid(_dot(hb, w_in_ref[:, COL_CB:COL_HQ])))
    lb = _lower_bound(lb_logits_ref[...])
    q_dec, k_inv, k_end, b_last = _hgrn_gates(
        _dot(hb, w_in_ref[:, COL_HQ:COL_HF]), _dot(hb, w_in_ref[:, COL_HF:COL_HI]), lb, SUBLANES)
    qd_ref[...] = q_dec
    ki_ref[...] = k_inv
    ke_ref[...] = k_end
    blast_ref[...] = b_last
    v_ref[...] = _dot(hb, w_in_ref[:, COL_HI:COL_HGATE])
    hgate_ref[...] = _silu(_dot(hb, w_in_ref[:, COL_HGATE:COL_MQ]))
    mq = _dot(hb, w_in_ref[:, COL_MQ:COL_GATES])
    for h in range(MEM_HEADS):
        q4_ref[:, h * SUBLANES:(h + 1) * SUBLANES, :] = (
            mq[:, h * MEM_HEAD_DIM:(h + 1) * MEM_HEAD_DIM].reshape(n_seq, SUBLANES, MEM_HEAD_DIM))
    for i in range(3):
        gates_ref[:, i * D_MODEL:(i + 1) * D_MODEL] = _branch_gate(hb, w_in_ref, i)


def _sample_seq_kernel(u_ref, qd_ref, ki_ref, ke_ref, v_ref, blast_ref, q4_ref, sc_ref, sh_ref,
                       k2_ref, v2_ref, conv_w_ref,
                       c_ref, o_ref, om_ref, new_conv_ref, new_hg_ref):
    n_seq, dec = sc_ref.shape[1], SUBLANES
    rows = n_seq * dec

    q_rows = MEM_HEADS * dec
    s = jnp.concatenate([_dot_nt(q4_ref[g].astype(BF16), k2_ref[g].astype(BF16)) for g in range(n_seq)],
                        axis=0)
    assert dec & (dec - 1) == 0 and MEM_HEADS & (MEM_HEADS - 1) == 0
    q_head = (lax.broadcasted_iota(jnp.int32, s.shape, 0) >> (dec.bit_length() - 1)) & (MEM_HEADS - 1)
    k_head = lax.broadcasted_iota(jnp.int32, s.shape, 1) & (MEM_HEADS - 1)
    p = _attention_softmax(s, q_head == k_head).astype(BF16)
    om = []
    for g in range(n_seq):
        om_g = _dot(p[g * q_rows:(g + 1) * q_rows], v2_ref[g].astype(BF16))
        om.extend(om_g[h * dec:(h + 1) * dec] for h in range(MEM_HEADS))

    u_slabs = [_load_lane_chunks(u_ref, pl.ds(step, n_seq, stride=dec)) for step in range(dec)]
    conv = [None] * dec
    for s in range(CONV_PREFIX + dec):
        slab = sc_ref[s] if s < CONV_PREFIX else u_slabs[s - CONV_PREFIX]
        if s >= dec:
            new_conv_ref[s - dec] = slab
        for step in range(max(0, s - CONV_PREFIX), min(dec, s + 1)):
            term = slab * conv_w_ref[s - step:s - step + 1, :]
            conv[step] = term if conv[step] is None else conv[step] + term

    row = lax.broadcasted_iota(jnp.int32, (rows, rows), 0)
    col = lax.broadcasted_iota(jnp.int32, (rows, rows), 1)
    causal = ((row ^ col) < dec) & (col <= row)
    decay = jnp.exp(blast_ref[...])
    for h in range(HG_HEADS):
        hs = slice(h * HG_DK, (h + 1) * HG_DK)
        scores = jnp.where(causal, _dot_nt(qd_ref[:, hs].astype(BF16), ki_ref[:, hs].astype(BF16)), 0.0)
        o_intra = _dot(scores.astype(BF16), v_ref[:, hs].astype(BF16))
        for g in range(n_seq):
            rs = slice(g * dec, (g + 1) * dec)
            s0 = sh_ref[g, h]
            o_ref[rs, hs] = o_intra[rs] + _dot(qd_ref[rs, hs].astype(BF16), s0.astype(BF16))
            decay_col = jnp.broadcast_to(decay[g:g + 1, hs], (HG_DV, HG_DK)).T
            new_hg_ref[g, h] = decay_col * s0 + _dot_tn(ke_ref[rs, hs].astype(BF16),
                                                        v_ref[rs, hs].astype(BF16))

    for step in range(dec):
        for i in range(c_ref.shape[0]):
            c_ref[i, pl.ds(step, n_seq, stride=dec), :] = conv[step][:, i * LANES:(i + 1) * LANES]
    for g in range(n_seq):
        om_ref[g * dec:(g + 1) * dec, :] = jnp.concatenate(om[g * MEM_HEADS:(g + 1) * MEM_HEADS], axis=1)


def _sample_out_kernel(x_ref, c_ref, o_ref, om_ref, hgate_ref, gates_ref, post_g_ref, conv_b_ref,
                       ln_g_ref, ln_b_ref, hg_g_ref, w_conv_out_ref, w_hg_out_ref, w_mem_out_ref,
                       w_out_ref, ffn_pre_g_ref, ffn_post_g_ref, w_gate_ref, w_up_ref, w_down_ref, y_ref):
    c = _conv_ln_silu(_load_lane_chunks(c_ref), conv_b_ref[...], ln_g_ref[...], ln_b_ref[...])
    p_conv = _dot(c.astype(BF16), w_conv_out_ref[...])
    o = _head_rms(o_ref[...], hg_g_ref[...]) * hgate_ref[...]
    p_hg = _dot(o.astype(BF16), w_hg_out_ref[...])
    p_mem = _dot(om_ref[...].astype(BF16), w_mem_out_ref[...])
    x1 = _merge_out(lambda i: gates_ref[:, i * D_MODEL:(i + 1) * D_MODEL], x_ref[...],
                    (p_conv, p_hg, p_mem), w_out_ref, post_g_ref[...])
    y_ref[...] = _ffn_block(x1, ffn_pre_g_ref[...], ffn_post_g_ref[...], w_gate_ref, w_up_ref, w_down_ref)


def _row_spec(tile, width):
    return pl.BlockSpec((tile, width), lambda i: (i, 0))


def _sample_in(x2d, pre_g, lb_logits, w_in):
    rows = x2d.shape[0]
    tile = SAMPLE_TILE
    seqs = tile // SUBLANES
    wide = [D_HGRN, D_HGRN, D_HGRN, D_HGRN]
    return pl.pallas_call(
        _sample_in_kernel,
        grid=(rows // tile,),
        in_specs=[_row_spec(tile, D_MODEL), _const_spec((1, D_MODEL)), _const_spec((2, D_HGRN)),
                  _const_spec(w_in.shape)],
        out_specs=[_lane_chunk_spec(tile, D_CONV)] + [_row_spec(tile, w) for w in wide] + [
            _row_spec(seqs, D_HGRN),
            pl.BlockSpec((seqs, MEM_HEADS * SUBLANES, MEM_HEAD_DIM), lambda i: (i, 0, 0)),
            _row_spec(tile, D_HGRN),
            _row_spec(tile, 3 * D_MODEL),
        ],
        out_shape=[jax.ShapeDtypeStruct((D_CONV // LANES, rows, LANES), F32)]
        + [jax.ShapeDtypeStruct((rows, w), F32) for w in wide] + [
            jax.ShapeDtypeStruct((rows // SUBLANES, D_HGRN), F32),
            jax.ShapeDtypeStruct((rows // SUBLANES, MEM_HEADS * SUBLANES, MEM_HEAD_DIM), F32),
            jax.ShapeDtypeStruct((rows, D_HGRN), F32),
            jax.ShapeDtypeStruct((rows, 3 * D_MODEL), F32),
        ],
        compiler_params=pltpu.CompilerParams(
            dimension_semantics=("parallel",), vmem_limit_bytes=VMEM_LIMIT),
        name="sample_in",
    )(x2d, pre_g, lb_logits, w_in)


def _sample_seq(u, qd, ki, ke, v, blast, q4, conv_rows, state_hgrn, k2, v2, conv_w):
    n = conv_rows.shape[1]
    g = SAMPLE_GROUP
    rows = g * SUBLANES
    kv_rows = k2.shape[1]
    seq_block = lambda *tail: pl.BlockSpec((g,) + tail, lambda i: (i,) + (0,) * len(tail))
    conv_block = pl.BlockSpec((CONV_PREFIX, g, D_CONV), lambda i: (0, i, 0))
    return pl.pallas_call(
        _sample_seq_kernel,
        grid=(n // g,),
        in_specs=[_lane_chunk_spec(rows, D_CONV)] + [_row_spec(rows, D_HGRN)] * 4 + [
            _row_spec(g, D_HGRN),
            seq_block(MEM_HEADS * SUBLANES, MEM_HEAD_DIM),
            conv_block,
            seq_block(HG_HEADS, HG_DK, HG_DV),
            seq_block(kv_rows, MEM_HEAD_DIM),
            seq_block(kv_rows, MEM_HEAD_DIM),
            _const_spec((CONV_WIDTH, D_CONV)),
        ],
        out_specs=[_lane_chunk_spec(rows, D_CONV), _row_spec(rows, D_HGRN), _row_spec(rows, D_MEM),
                   conv_block, seq_block(HG_HEADS, HG_DK, HG_DV)],
        out_shape=[
            jax.ShapeDtypeStruct((D_CONV // LANES, n * SUBLANES, LANES), F32),
            jax.ShapeDtypeStruct((n * SUBLANES, D_HGRN), F32),
            jax.ShapeDtypeStruct((n * SUBLANES, D_MEM), F32),
            jax.ShapeDtypeStruct((CONV_PREFIX, n, D_CONV), F32),
            jax.ShapeDtypeStruct((n, HG_HEADS, HG_DK, HG_DV), F32),
        ],
        compiler_params=pltpu.CompilerParams(
            dimension_semantics=("parallel",), vmem_limit_bytes=VMEM_LIMIT),
        name="sample_seq",
    )(u, qd, ki, ke, v, blast, q4, conv_rows, state_hgrn, k2, v2, conv_w)


def _sample_out(x2d, c, o, om, hgate, gates, vectors, weights, ffn_vectors, ffn_weights):
    rows = x2d.shape[0]
    tile = SAMPLE_TILE
    acts = (x2d, c, o, om, hgate, gates)
    consts = tuple(vectors) + tuple(weights) + tuple(ffn_vectors) + tuple(ffn_weights)
    return pl.pallas_call(
        _sample_out_kernel,
        grid=(rows // tile,),
        in_specs=[_lane_chunk_spec(tile, D_CONV) if a is c else _row_spec(tile, a.shape[1]) for a in acts]
        + [_const_spec(a.shape) for a in consts],
        out_specs=_row_spec(tile, D_MODEL),
        out_shape=jax.ShapeDtypeStruct((rows, D_MODEL), F32),
        compiler_params=pltpu.CompilerParams(
            dimension_semantics=("parallel",), vmem_limit_bytes=VMEM_LIMIT),
        name="sample_out",
    )(*acts, *consts)


def _ffn_kernel(x_ref, pre_g_ref, post_g_ref, w_gate_ref, w_up_ref, w_down_ref, y_ref):
    for r0 in range(0, x_ref.shape[0], FFN_BLOCK):
        rows = slice(r0, r0 + FFN_BLOCK)
        y_ref[rows, :] = _ffn_block(x_ref[rows, :], pre_g_ref[...], post_g_ref[...],
                                    w_gate_ref, w_up_ref, w_down_ref)


def _ffn(x2d, pre_g, post_g, w_gate, w_up, w_down):
    rows = x2d.shape[0]
    tile = min(FFN_TILE, rows)
    return pl.pallas_call(
        _ffn_kernel,
        grid=(rows // tile,),
        in_specs=[
            pl.BlockSpec((tile, D_MODEL), lambda i: (i, 0)),
            _const_spec((1, D_MODEL)), _const_spec((1, D_MODEL)),
            _const_spec(w_gate.shape), _const_spec(w_up.shape), _const_spec(w_down.shape),
        ],
        out_specs=pl.BlockSpec((tile, D_MODEL), lambda i: (i, 0)),
        out_shape=jax.ShapeDtypeStruct((rows, D_MODEL), F32),
        compiler_params=pltpu.CompilerParams(
            dimension_semantics=("parallel",), vmem_limit_bytes=VMEM_LIMIT),
        name="ffn",
    )(x2d, pre_g, post_g, w_gate, w_up, w_down)


def kernel(x_prompt, x_sample, mem_prompt, state_conv, state_hgrn, cache_mem_k, cache_mem_v, norm_pre_mix, norm_post_mix, norm_pre_ffn, norm_post_ffn, w_in, conv_w, conv_b, conv_ln_g, conv_ln_b, w_conv_out, hg_lb_logits, hg_norm_g, w_hg_out, mem_norm_g, w_mem_kv, w_mem_out, w_out, w_ffn_gate, w_ffn_up, w_ffn_down):
    depth = w_in.shape[0]
    assert depth == 1 and hg_lb_logits.shape[0] == 2, "single-layer step"
    batch, seq, _ = x_prompt.shape
    n_dec, dec, _ = x_sample.shape
    n_mem = mem_prompt.shape[1]
    assert seq % PROMPT_TILE == 0 and n_dec % SAMPLE_GROUP == 0 and dec == SUBLANES
    assert (n_dec * dec) % SAMPLE_TILE == 0

    vectors = (norm_pre_mix, norm_post_mix, conv_w[0], conv_b, conv_ln_g, conv_ln_b,
               hg_lb_logits, jnp.tile(hg_norm_g, (1, HG_HEADS)))

    mk, mv, kb, vb, *weights = _memory_kv(
        mem_prompt.reshape(batch * n_mem, D_MODEL), mem_norm_g, w_mem_kv[0], n_mem,
        tuple(w[0] for w in (w_in, w_conv_out, w_hg_out, w_mem_out, w_out)))

    xp, conv_p, hg_p, *ffn_weights = _prompt_mixer(
        x_prompt, kb.reshape(batch, n_mem, D_MEM), vb.reshape(batch, n_mem, D_MEM), vectors, weights,
        (w_ffn_gate[0], w_ffn_up[0], w_ffn_down[0]))
    yp = _ffn(xp.reshape(batch * seq, D_MODEL), norm_pre_ffn, norm_post_ffn, *ffn_weights)

    xs2d = x_sample.reshape(n_dec * dec, D_MODEL)
    u, qd, ki, ke, v, blast, q4, hgate, gates = _sample_in(xs2d, norm_pre_mix, hg_lb_logits, weights[0])
    k2 = cache_mem_k[0].reshape(n_dec, n_mem * MEM_HEADS, MEM_HEAD_DIM)
    v2 = cache_mem_v[0].reshape(n_dec, n_mem * MEM_HEADS, MEM_HEAD_DIM)
    c, o, om, conv_rows, hg_s = _sample_seq(u, qd, ki, ke, v, blast, q4,
                                            jnp.transpose(state_conv[0], (1, 0, 2)), state_hgrn[0],
                                            k2, v2, conv_w[0])
    conv_s = jnp.transpose(conv_rows, (1, 0, 2))
    ys = _sample_out(xs2d, c, o, om, hgate, gates,
                     (norm_post_mix, conv_b, conv_ln_g, conv_ln_b, vectors[-1]), weights[1:],
                     (norm_pre_ffn, norm_post_ffn), ffn_weights)

    kv_shape = (1, batch, n_mem, MEM_HEADS, MEM_HEAD_DIM)
    return (yp.reshape(batch, seq, D_MODEL), ys.reshape(n_dec, dec, D_MODEL),
            conv_p[None], hg_p[None], mk.reshape(kv_shape), mv.reshape(kv_shape),
            conv_s[None], hg_s[None])
```

```python
from typing import NamedTuple

import jax
import jax.numpy as jnp
from jax import lax
from jax.experimental import pallas as pl
from jax.experimental.pallas import tpu as pltpu

D_MODEL = 1024
D_CONV = 512
CONV_WIDTH = 31
CONV_PREFIX = CONV_WIDTH - 1
HG_HEADS = 4
HG_DK = 128
HG_DV = 128
D_HGRN = HG_HEADS * HG_DK
MEM_HEADS = 4
MEM_HEAD_DIM = 128
D_MEM = MEM_HEADS * MEM_HEAD_DIM
CHUNK = 32
EPS = 1e-6
LOG2E = 1.4426950408889634

COL_CA = 0
COL_CB = COL_CA + D_CONV
COL_HQ = COL_CB + D_CONV
COL_HF = COL_HQ + D_HGRN
COL_HI = COL_HF + D_HGRN
COL_HGATE = COL_HI + D_HGRN
COL_MQ = COL_HGATE + D_HGRN
COL_GATES = COL_MQ + D_MEM

SUBLANES = 8
LANES = 128
VMEM_LIMIT = 56 * 1024 * 1024

PROMPT_TILE = 256
PROMPT_STEP_TILES = 2
SAMPLE_GROUP = 8
SAMPLE_TILE = 256
FFN_TILE = 512
MEMORY_TILE = 128
FFN_BLOCK = 256
CONV_ROWS = 32

BF16 = jnp.bfloat16
F32 = jnp.float32


def _dot(a, b):
    return jnp.dot(a, b, preferred_element_type=F32)


def _dot_nt(a, b):
    return lax.dot_general(a, b, (((1,), (1,)), ((), ())), preferred_element_type=F32)


def _dot_tn(a, b):
    return lax.dot_general(a, b, (((0,), (0,)), ((), ())), preferred_element_type=F32)


def _rms(x, gain):
    return x * lax.rsqrt(jnp.mean(x * x, axis=-1, keepdims=True) + EPS) * gain


def _sigmoid(x):
    return 1.0 / (1.0 + jnp.exp2(x * -LOG2E))


def _silu(x):
    return x * _sigmoid(x)


def _attention_softmax(scores, keep=None):
    t = scores * (MEM_HEAD_DIM ** -0.5 * LOG2E)
    if keep is not None:
        t = jnp.where(keep, t, -jnp.inf)
    e = jnp.exp2(t - jnp.max(t, axis=-1, keepdims=True))
    return e / jnp.sum(e, axis=-1, keepdims=True)


def _forget_terms(hf, lb):
    sig = _sigmoid(hf)
    return jnp.log(lb + (1.0 - lb) * sig), (1.0 - lb) * (1.0 - sig)


def _lower_bound(lb_logits):
    m = jnp.max(lb_logits, axis=0, keepdims=True)
    e = jnp.exp(lb_logits - m)
    return e[0:1] / jnp.sum(e, axis=0, keepdims=True)


def _segment_cumsum(x, seg):
    pos = lax.broadcasted_iota(jnp.int32, x.shape, 0) & (seg - 1)
    s = 1
    while s < seg:
        x = x + jnp.where(pos >= s, pltpu.roll(x, s, axis=0), 0.0)
        s *= 2
    return x


def _conv_ln_silu(c, conv_b, ln_g, ln_b):
    c = c + conv_b
    mu = jnp.mean(c, axis=-1, keepdims=True)
    d = c - mu
    var = jnp.mean(d * d, axis=-1, keepdims=True)
    return _silu(d * lax.rsqrt(var + EPS) * ln_g + ln_b)


def _hgrn_gates(hq, hf, lb, seg):
    q = _silu(hq)
    logf, k = _forget_terms(hf, lb)
    b = _segment_cumsum(logf, seg)
    rows = b.shape[0]
    b3 = b.reshape(rows // seg, seg, D_HGRN)
    b_last3 = b3[:, seg - 1:seg, :]
    rest = (b_last3 - b3).reshape(rows, D_HGRN)
    q_dec = q * jnp.exp(b)
    k_inv = k * jnp.exp(-b)
    k_end = k * jnp.exp(rest)
    return q_dec, k_inv, k_end, b_last3.reshape(rows // seg, D_HGRN)


class _TileLevels(NamedTuple):
    q_mid: jax.Array
    k_mid: jax.Array
    q_64: jax.Array
    k_64: jax.Array
    q_128: jax.Array
    k_128: jax.Array
    q_tile: jax.Array
    k_tile: jax.Array
    decay: jax.Array


def _hgrn_tile_levels(hq, hf, lb):
    tl = hq.shape[0]
    n_blocks = 4
    block = 2 * CHUNK
    assert tl == n_blocks * block
    q = _silu(hq)
    logf, k = _forget_terms(hf, lb)
    b = _segment_cumsum(logf, CHUNK)
    chunks = [b[n * CHUNK:(n + 1) * CHUNK] for n in range(2 * n_blocks)]
    totals = [c[CHUNK - 1:CHUNK] for c in chunks]
    e = jnp.concatenate([c - totals[n] if n % 2 == 0 else c for n, c in enumerate(chunks)], axis=0)
    q_mid = q * jnp.exp(e)
    k_mid = k * jnp.exp(-e)
    first = [totals[2 * j] for j in range(n_blocks)]
    second = [totals[2 * j + 1] for j in range(n_blocks)]
    both = [first[j] + second[j] for j in range(n_blocks)]

    def scaled(x, blocks, log_scales):
        return jnp.concatenate(
            [x[j * block:(j + 1) * block] * jnp.exp(s) for j, s in zip(blocks, log_scales)], axis=0)

    every = range(n_blocks)
    return _TileLevels(
        q_mid=q_mid, k_mid=k_mid,
        q_64=scaled(q_mid, every, first),
        k_64=scaled(k_mid, every, second),
        q_128=scaled(q_mid, (2, 3), (first[2], first[3] + both[2])),
        k_128=scaled(k_mid, (0, 1), (second[0] + both[1], second[1])),
        q_tile=scaled(q_mid, every, [first[j] + sum(both[:j], 0.0) for j in every]),
        k_tile=scaled(k_mid, every, [second[j] + sum(both[j + 1:], 0.0) for j in every]),
        decay=jnp.exp(sum(both[1:], both[0])))


def _head_rms(o, gain4):
    parts = []
    for h in range(HG_HEADS):
        oh = o[:, h * HG_DV:(h + 1) * HG_DV]
        parts.append(oh * lax.rsqrt(jnp.mean(oh * oh, axis=-1, keepdims=True) + EPS))
    return jnp.concatenate(parts, axis=-1) * gain4


def _branch_gate(hb, w_in_ref, i):
    c0 = COL_GATES + i * D_MODEL
    return _sigmoid(_dot(hb, w_in_ref[:, c0:c0 + D_MODEL]))


def _merge_out(gate, x, branches, w_out_ref, post_g):
    acc = None
    for i, p in enumerate(branches):
        acc = gate(i) * p if acc is None else acc + gate(i) * p
    m = _dot(acc.astype(BF16), w_out_ref[...])
    return x + _rms(m, post_g)


def _ffn_block(x, pre_g, post_g, w_gate_ref, w_up_ref, w_down_ref):
    hb = _rms(x, pre_g).astype(BF16)
    f = _silu(_dot(hb, w_gate_ref[...])) * _dot(hb, w_up_ref[...])
    d = _dot(f.astype(BF16), w_down_ref[...])
    return x + _rms(d, post_g)


def _cast_block_rows(rows, n_steps):
    bf16_sublanes = 2 * SUBLANES
    for n_blocks in range(min(n_steps, rows // bf16_sublanes), 0, -1):
        if rows % n_blocks == 0 and (rows // n_blocks) % bf16_sublanes == 0:
            return rows // n_blocks
    raise ValueError(f"no bf16-tileable split of {rows} rows")


def _cast_specs(weights, n_steps, step_of):
    specs = []
    for w in weights:
        rows_blk = _cast_block_rows(w.shape[0], n_steps)
        last = w.shape[0] // rows_blk - 1
        specs.append(pl.BlockSpec((rows_blk, w.shape[1]),
                                  lambda *idx, last=last: (jnp.minimum(step_of(*idx), last), 0)))
    return specs


def _cast_blocks(srcs, dsts):
    for src, dst in zip(srcs, dsts):
        dst[...] = src[...].astype(BF16)


def _memory_kv_kernel(mem_ref, g_ref, w_ref, *rest):
    n_cast = (len(rest) - 5) // 2
    k_rows_ref, v_rows_ref, k_ref, v_ref = rest[n_cast:n_cast + 4]
    wb_ref = rest[-1]
    _cast_blocks(rest[:n_cast], rest[n_cast + 4:-1])

    @pl.when(pl.program_id(0) == 0)
    def _():
        wb_ref[...] = w_ref[...].astype(BF16)

    m = _rms(mem_ref[...], g_ref[...]).astype(BF16)
    tokens = mem_ref.shape[0]
    for full, rows_ref, op_ref in ((_dot(m, wb_ref[:, :D_MEM]), k_rows_ref, k_ref),
                                   (_dot(m, wb_ref[:, D_MEM:]), v_rows_ref, v_ref)):
        op_ref[...] = full.astype(BF16)
        for h in range(MEM_HEADS):
            rows_ref[pl.ds(h, tokens, stride=MEM_HEADS), :] = full[:, h * MEM_HEAD_DIM:(h + 1) * MEM_HEAD_DIM]


def _memory_kv(mem2d, gain, w_kv, tile, later_weights):
    rows = mem2d.shape[0]
    n_steps = rows // tile
    cast_specs = _cast_specs(later_weights, n_steps, lambda i: i)
    return pl.pallas_call(
        _memory_kv_kernel,
        grid=(n_steps,),
        in_specs=[
            pl.BlockSpec((tile, D_MODEL), lambda i: (i, 0)),
            pl.BlockSpec((1, D_MODEL), lambda i: (0, 0)),
            _const_spec(w_kv.shape),
        ] + cast_specs,
        out_specs=[pl.BlockSpec((tile * MEM_HEADS, MEM_HEAD_DIM), lambda i: (i, 0))] * 2
        + [pl.BlockSpec((tile, D_MEM), lambda i: (i, 0))] * 2 + cast_specs,
        out_shape=[jax.ShapeDtypeStruct((rows * MEM_HEADS, MEM_HEAD_DIM), F32)] * 2
        + [jax.ShapeDtypeStruct((rows, D_MEM), BF16)] * 2
        + [jax.ShapeDtypeStruct(w.shape, BF16) for w in later_weights],
        scratch_shapes=[pltpu.VMEM(w_kv.shape, BF16)],
        compiler_params=pltpu.CompilerParams(
            dimension_semantics=("arbitrary",), vmem_limit_bytes=VMEM_LIMIT),
        name="memory_kv",
    )(mem2d, gain, w_kv, *later_weights)


def _prompt_mixer_kernel(x_ref, kb_ref, vb_ref, pre_g_ref, post_g_ref, conv_w_ref, conv_b_ref,
                         ln_g_ref, ln_b_ref, lb_logits_ref, hg_g_ref, w_in_ref, w_conv_out_ref,
                         w_hg_out_ref, w_mem_out_ref, w_out_ref, *rest):
    n_cast = (len(rest) - 6) // 2
    cast_src = rest[:n_cast]
    y_ref, new_conv_ref, new_hg_ref = rest[n_cast:n_cast + 3]
    cast_dst = rest[n_cast + 3:2 * n_cast + 3]
    shift_ref, tail_ref, state_t_ref = rest[2 * n_cast + 3:]
    _cast_blocks(cast_src, cast_dst)
    t = pl.program_id(1)
    last_t = pl.num_programs(1) - 1
    tl = PROMPT_TILE
    n_tiles = x_ref.shape[0] // tl

    @pl.when(t == 0)
    def _():
        tail_ref[...] = jnp.zeros_like(tail_ref)
        state_t_ref[...] = jnp.zeros_like(state_t_ref)

    lb = _lower_bound(lb_logits_ref[...])
    half = tl // 2
    row = lax.broadcasted_iota(jnp.int32, (tl, tl), 0)
    col = lax.broadcasted_iota(jnp.int32, (tl, tl), 1)
    span = row ^ col
    near = (span < 2 * CHUNK) & (col <= row)
    mid = (span < 4 * CHUNK) & (col < row)

    def mix_rows():
        n_rows = n_tiles * tl
        shift = shift_ref
        x = x_ref[...]
        hb = _rms(x, pre_g_ref[...]).astype(BF16)

        def proj(c0, c1):
            return _dot(hb, w_in_ref[:, c0:c1])

        u = proj(COL_CA, COL_CB) * _sigmoid(proj(COL_CB, COL_HQ))
        shift[0:CONV_ROWS, :] = tail_ref[...]
        shift[CONV_ROWS:CONV_ROWS + n_rows, :] = u
        tail_ref[...] = u[n_rows - CONV_ROWS:]

        lead = CONV_ROWS - CONV_PREFIX
        conv = None
        for r in range(SUBLANES):
            rows_r = n_rows if r == 0 else n_rows + SUBLANES
            v_r = None
            for a in range((CONV_WIDTH + lead) // SUBLANES + 1):
                j = a * SUBLANES + r - lead
                if 0 <= j < CONV_WIDTH:
                    term = shift[a * SUBLANES:a * SUBLANES + rows_r, :] * conv_w_ref[j:j + 1, :]
                    v_r = term if v_r is None else v_r + term
            if r:
                v_r = pltpu.roll(v_r, rows_r - r, axis=0)[:n_rows]
            conv = v_r if conv is None else conv + v_r
        c = _conv_ln_silu(conv, conv_b_ref[...], ln_g_ref[...], ln_b_ref[...]).astype(BF16)

        hq = proj(COL_HQ, COL_HF)
        hf = proj(COL_HF, COL_HI)
        levels = []
        for i in range(n_tiles):
            lv = _hgrn_tile_levels(hq[i * tl:(i + 1) * tl], hf[i * tl:(i + 1) * tl], lb)
            levels.append(_TileLevels(*[a.astype(BF16) for a in lv[:-1]], lv.decay))
        vb16 = proj(COL_HI, COL_HGATE).astype(BF16)
        hgate = _silu(proj(COL_HGATE, COL_MQ))
        mq = proj(COL_MQ, COL_GATES).astype(BF16)
        gates = [_branch_gate(hb, w_in_ref, g) for g in range(3)]

        head = lambda h: slice(h * MEM_HEAD_DIM, (h + 1) * MEM_HEAD_DIM)
        s = jnp.concatenate([_dot_nt(mq[:, head(h)], kb_ref[:, head(h)]) for h in range(MEM_HEADS)],
                            axis=0)
        p = _attention_softmax(s).astype(BF16)
        om = [_dot(p[h * n_rows:(h + 1) * n_rows], vb_ref[:, head(h)]) for h in range(MEM_HEADS)]

        o_tiles = []
        for i, lvb in enumerate(levels):
            o_heads = []
            for h in range(HG_HEADS):
                hs = slice(h * HG_DK, (h + 1) * HG_DK)
                vh = vb16[i * tl:(i + 1) * tl, hs]
                s_near = _dot_nt(lvb.q_mid[:, hs], lvb.k_mid[:, hs])
                s_mid = _dot_nt(lvb.q_64[:, hs], lvb.k_64[:, hs])
                s_far = _dot_nt(lvb.q_128[:, hs], lvb.k_128[:, hs])
                base = jnp.where(near, s_near, jnp.where(mid, s_mid, 0.0))
                scores = jnp.concatenate(
                    [base[:half], jnp.concatenate([s_far, base[half:, half:]], axis=1)], axis=0)
                st = state_t_ref[h]
                o_heads.append(_dot(scores.astype(BF16), vh)
                               + _dot_nt(lvb.q_tile[:, hs], st.astype(BF16)))
                state_t_ref[h] = st * lvb.decay[:, hs] + _dot_tn(vh, lvb.k_tile[:, hs])
            o_tiles.append(jnp.concatenate(o_heads, axis=-1))
        o = _head_rms(jnp.concatenate(o_tiles, axis=0), hg_g_ref[...]) * hgate

        p_conv = _dot(c, w_conv_out_ref[...])
        p_mem = _dot(jnp.concatenate(om, axis=-1).astype(BF16), w_mem_out_ref[...])
        p_hg = _dot(o.astype(BF16), w_hg_out_ref[...])
        y_ref[...] = _merge_out(lambda g: gates[g], x, (p_conv, p_hg, p_mem), w_out_ref, post_g_ref[...])

    mix_rows()

    @pl.when(t == last_t)
    def _():
        new_conv_ref[...] = tail_ref[CONV_ROWS - CONV_PREFIX:CONV_ROWS, :]
        for h in range(HG_HEADS):
            new_hg_ref[h] = state_t_ref[h].T


def _const_spec(shape):
    zeros = (0,) * len(shape)
    return pl.BlockSpec(shape, lambda *_: zeros, pipeline_mode=pl.Buffered(1))


def _mixer_weight_specs(w_in, w_conv_out, w_hg_out, w_mem_out, w_out):
    return [_const_spec(w.shape) for w in (w_in, w_conv_out, w_hg_out, w_mem_out, w_out)]


def _vector_specs():
    return [
        _const_spec((1, D_MODEL)), _const_spec((1, D_MODEL)),
        _const_spec((CONV_WIDTH, D_CONV)), _const_spec((1, D_CONV)),
        _const_spec((1, D_CONV)), _const_spec((1, D_CONV)),
        _const_spec((2, D_HGRN)), _const_spec((1, D_HGRN)),
    ]


def _prompt_mixer(x, mk, mv, vectors, weights, later_weights):
    batch, seq, _ = x.shape
    n_mem = mk.shape[1]
    tl = PROMPT_TILE * PROMPT_STEP_TILES
    n_t = seq // tl
    cast_specs = _cast_specs(later_weights, batch * n_t, lambda b, t: b * n_t + t)
    return pl.pallas_call(
        _prompt_mixer_kernel,
        grid=(batch, n_t),
        in_specs=[
            pl.BlockSpec((None, tl, D_MODEL), lambda b, t: (b, t, 0)),
            pl.BlockSpec((None, n_mem, D_MEM), lambda b, t: (b, 0, 0)),
            pl.BlockSpec((None, n_mem, D_MEM), lambda b, t: (b, 0, 0)),
        ] + _vector_specs() + _mixer_weight_specs(*weights) + cast_specs,
        out_specs=[
            pl.BlockSpec((None, tl, D_MODEL), lambda b, t: (b, t, 0)),
            pl.BlockSpec((None, CONV_PREFIX, D_CONV), lambda b, t: (b, 0, 0)),
            pl.BlockSpec((None, HG_HEADS, HG_DK, HG_DV), lambda b, t: (b, 0, 0, 0)),
        ] + cast_specs,
        out_shape=[
            jax.ShapeDtypeStruct((batch, seq, D_MODEL), F32),
            jax.ShapeDtypeStruct((batch, CONV_PREFIX, D_CONV), F32),
            jax.ShapeDtypeStruct((batch, HG_HEADS, HG_DK, HG_DV), F32),
        ] + [jax.ShapeDtypeStruct(w.shape, BF16) for w in later_weights],
        scratch_shapes=[
            pltpu.VMEM((tl + CONV_ROWS, D_CONV), F32),
            pltpu.VMEM((CONV_ROWS, D_CONV), F32),
            pltpu.VMEM((HG_HEADS, HG_DV, HG_DK), F32),
        ],
        compiler_params=pltpu.CompilerParams(
            dimension_semantics=("arbitrary", "arbitrary"), vmem_limit_bytes=VMEM_LIMIT),
        name="prompt_mixer",
    )(x, mk, mv, *vectors, *weights, *later_weights)


def _store_lane_chunks(ref, x):
    for i in range(ref.shape[0]):
        ref[i] = x[:, i * LANES:(i + 1) * LANES]


def _load_lane_chunks(ref, rows=slice(None)):
    return jnp.concatenate([ref[i, rows, :] for i in range(ref.shape[0])], axis=1)


def _lane_chunk_spec(tile, width):
    return pl.BlockSpec((width // LANES, tile, LANES), lambda i: (0, i, 0))


def _sample_in_kernel(x_ref, pre_g_ref, lb_logits_ref, w_in_ref,
                      u_ref, qd_ref, ki_ref, ke_ref, v_ref, blast_ref, q4_ref, hgate_ref, gates_ref):
    n_seq = x_ref.shape[0] // SUBLANES
    hb = _rms(x_ref[...], pre_g_ref[...]).astype(BF16)
    _store_lane_chunks(
        u_ref, _dot(hb, w_in_ref[:, COL_CA:COL_CB]) * _sigmoid(_dot(hb, w_in_ref[:, COL_CB:COL_HQ])))
    lb = _lower_bound(lb_logits_ref[...])
    q_dec, k_inv, k_end, b_last = _hgrn_gates(
        _dot(hb, w_in_ref[:, COL_HQ:COL_HF]), _dot(hb, w_in_ref[:, COL_HF:COL_HI]), lb, SUBLANES)
    qd_ref[...] = q_dec
    ki_ref[...] = k_inv
    ke_ref[...] = k_end
    blast_ref[...] = b_last
    v_ref[...] = _dot(hb, w_in_ref[:, COL_HI:COL_HGATE])
    hgate_ref[...] = _silu(_dot(hb, w_in_ref[:, COL_HGATE:COL_MQ]))
    mq = _dot(hb, w_in_ref[:, COL_MQ:COL_GATES])
    for h in range(MEM_HEADS):
        q4_ref[:, h * SUBLANES:(h + 1) * SUBLANES, :] = (
            mq[:, h * MEM_HEAD_DIM:(h + 1) * MEM_HEAD_DIM].reshape(n_seq, SUBLANES, MEM_HEAD_DIM))
    for i in range(3):
        gates_ref[:, i * D_MODEL:(i + 1) * D_MODEL] = _branch_gate(hb, w_in_ref, i)


def _sample_seq_kernel(u_ref, qd_ref, ki_ref, ke_ref, v_ref, blast_ref, q4_ref, sc_ref, sh_ref,
                       k2_ref, v2_ref, conv_w_ref,
                       c_ref, o_ref, om_ref, new_conv_ref, new_hg_ref):
    n_seq, dec = sc_ref.shape[1], SUBLANES
    rows = n_seq * dec

    q_rows = MEM_HEADS * dec
    s = jnp.concatenate([_dot_nt(q4_ref[g].astype(BF16), k2_ref[g].astype(BF16)) for g in range(n_seq)],
                        axis=0)
    assert dec & (dec - 1) == 0 and MEM_HEADS & (MEM_HEADS - 1) == 0
    q_head = (lax.broadcasted_iota(jnp.int32, s.shape, 0) >> (dec.bit_length() - 1)) & (MEM_HEADS - 1)
    k_head = lax.broadcasted_iota(jnp.int32, s.shape, 1) & (MEM_HEADS - 1)
    p = _attention_softmax(s, q_head == k_head).astype(BF16)
    om = []
    for g in range(n_seq):
        om_g = _dot(p[g * q_rows:(g + 1) * q_rows], v2_ref[g].astype(BF16))
        om.extend(om_g[h * dec:(h + 1) * dec] for h in range(MEM_HEADS))

    u_slabs = [_load_lane_chunks(u_ref, pl.ds(step, n_seq, stride=dec)) for step in range(dec)]
    conv = [None] * dec
    for s in range(CONV_PREFIX + dec):
        slab = sc_ref[s] if s < CONV_PREFIX else u_slabs[s - CONV_PREFIX]
        if s >= dec:
            new_conv_ref[s - dec] = slab
        for step in range(max(0, s - CONV_PREFIX), min(dec, s + 1)):
            term = slab * conv_w_ref[s - step:s - step + 1, :]
            conv[step] = term if conv[step] is None else conv[step] + term

    row = lax.broadcasted_iota(jnp.int32, (rows, rows), 0)
    col = lax.broadcasted_iota(jnp.int32, (rows, rows), 1)
    causal = ((row ^ col) < dec) & (col <= row)
    decay = jnp.exp(blast_ref[...])
    for h in range(HG_HEADS):
        hs = slice(h * HG_DK, (h + 1) * HG_DK)
        scores = jnp.where(causal, _dot_nt(qd_ref[:, hs].astype(BF16), ki_ref[:, hs].astype(BF16)), 0.0)
        o_intra = _dot(scores.astype(BF16), v_ref[:, hs].astype(BF16))
        for g in range(n_seq):
            rs = slice(g * dec, (g + 1) * dec)
            s0 = sh_ref[g, h]
            o_ref[rs, hs] = o_intra[rs] + _dot(qd_ref[rs, hs].astype(BF16), s0.astype(BF16))
            decay_col = jnp.broadcast_to(decay[g:g + 1, hs], (HG_DV, HG_DK)).T
            new_hg_ref[g, h] = decay_col * s0 + _dot_tn(ke_ref[rs, hs].astype(BF16),
                                                        v_ref[rs, hs].astype(BF16))

    for step in range(dec):
        for i in range(c_ref.shape[0]):
            c_ref[i, pl.ds(step, n_seq, stride=dec), :] = conv[step][:, i * LANES:(i + 1) * LANES]
    for g in range(n_seq):
        om_ref[g * dec:(g + 1) * dec, :] = jnp.concatenate(om[g * MEM_HEADS:(g + 1) * MEM_HEADS], axis=1)


def _sample_out_kernel(x_ref, c_ref, o_ref, om_ref, hgate_ref, gates_ref, post_g_ref, conv_b_ref,
                       ln_g_ref, ln_b_ref, hg_g_ref, w_conv_out_ref, w_hg_out_ref, w_mem_out_ref,
                       w_out_ref, ffn_pre_g_ref, ffn_post_g_ref, w_gate_ref, w_up_ref, w_down_ref, y_ref):
    c = _conv_ln_silu(_load_lane_chunks(c_ref), conv_b_ref[...], ln_g_ref[...], ln_b_ref[...])
    p_conv = _dot(c.astype(BF16), w_conv_out_ref[...])
    o = _head_rms(o_ref[...], hg_g_ref[...]) * hgate_ref[...]
    p_hg = _dot(o.astype(BF16), w_hg_out_ref[...])
    p_mem = _dot(om_ref[...].astype(BF16), w_mem_out_ref[...])
    x1 = _merge_out(lambda i: gates_ref[:, i * D_MODEL:(i + 1) * D_MODEL], x_ref[...],
                    (p_conv, p_hg, p_mem), w_out_ref, post_g_ref[...])
    y_ref[...] = _ffn_block(x1, ffn_pre_g_ref[...], ffn_post_g_ref[...], w_gate_ref, w_up_ref, w_down_ref)


def _row_spec(tile, width):
    return pl.BlockSpec((tile, width), lambda i: (i, 0))


def _sample_in(x2d, pre_g, lb_logits, w_in):
    rows = x2d.shape[0]
    tile = SAMPLE_TILE
    seqs = tile // SUBLANES
    wide = [D_HGRN, D_HGRN, D_HGRN, D_HGRN]
    return pl.pallas_call(
        _sample_in_kernel,
        grid=(rows // tile,),
        in_specs=[_row_spec(tile, D_MODEL), _const_spec((1, D_MODEL)), _const_spec((2, D_HGRN)),
                  _const_spec(w_in.shape)],
        out_specs=[_lane_chunk_spec(tile, D_CONV)] + [_row_spec(tile, w) for w in wide] + [
            _row_spec(seqs, D_HGRN),
            pl.BlockSpec((seqs, MEM_HEADS * SUBLANES, MEM_HEAD_DIM), lambda i: (i, 0, 0)),
            _row_spec(tile, D_HGRN),
            _row_spec(tile, 3 * D_MODEL),
        ],
        out_shape=[jax.ShapeDtypeStruct((D_CONV // LANES, rows, LANES), F32)]
        + [jax.ShapeDtypeStruct((rows, w), F32) for w in wide] + [
            jax.ShapeDtypeStruct((rows // SUBLANES, D_HGRN), F32),
            jax.ShapeDtypeStruct((rows // SUBLANES, MEM_HEADS * SUBLANES, MEM_HEAD_DIM), F32),
            jax.ShapeDtypeStruct((rows, D_HGRN), F32),
            jax.ShapeDtypeStruct((rows, 3 * D_MODEL), F32),
        ],
        compiler_params=pltpu.CompilerParams(
            dimension_semantics=("parallel",), vmem_limit_bytes=VMEM_LIMIT),
        name="sample_in",
    )(x2d, pre_g, lb_logits, w_in)


def _sample_seq(u, qd, ki, ke, v, blast, q4, conv_rows, state_hgrn, k2, v2, conv_w):
    n = conv_rows.shape[1]
    g = SAMPLE_GROUP
    rows = g * SUBLANES
    kv_rows = k2.shape[1]
    seq_block = lambda *tail: pl.BlockSpec((g,) + tail, lambda i: (i,) + (0,) * len(tail))
    conv_block = pl.BlockSpec((CONV_PREFIX, g, D_CONV), lambda i: (0, i, 0))
    return pl.pallas_call(
        _sample_seq_kernel,
        grid=(n // g,),
        in_specs=[_lane_chunk_spec(rows, D_CONV)] + [_row_spec(rows, D_HGRN)] * 4 + [
            _row_spec(g, D_HGRN),
            seq_block(MEM_HEADS * SUBLANES, MEM_HEAD_DIM),
            conv_block,
            seq_block(HG_HEADS, HG_DK, HG_DV),
            seq_block(kv_rows, MEM_HEAD_DIM),
            seq_block(kv_rows, MEM_HEAD_DIM),
            _const_spec((CONV_WIDTH, D_CONV)),
        ],
        out_specs=[_lane_chunk_spec(rows, D_CONV), _row_spec(rows, D_HGRN), _row_spec(rows, D_MEM),
                   conv_block, seq_block(HG_HEADS, HG_DK, HG_DV)],
        out_shape=[
            jax.ShapeDtypeStruct((D_CONV // LANES, n * SUBLANES, LANES), F32),
            jax.ShapeDtypeStruct((n * SUBLANES, D_HGRN), F32),
            jax.ShapeDtypeStruct((n * SUBLANES, D_MEM), F32),
            jax.ShapeDtypeStruct((CONV_PREFIX, n, D_CONV), F32),
            jax.ShapeDtypeStruct((n, HG_HEADS, HG_DK, HG_DV), F32),
        ],
        compiler_params=pltpu.CompilerParams(
            dimension_semantics=("parallel",), vmem_limit_bytes=VMEM_LIMIT),
        name="sample_seq",
    )(u, qd, ki, ke, v, blast, q4, conv_rows, state_hgrn, k2, v2, conv_w)


def _sample_out(x2d, c, o, om, hgate, gates, vectors, weights, ffn_vectors, ffn_weights):
    rows = x2d.shape[0]
    tile = SAMPLE_TILE
    acts = (x2d, c, o, om, hgate, gates)
    consts = tuple(vectors) + tuple(weights) + tuple(ffn_vectors) + tuple(ffn_weights)
    return pl.pallas_call(
        _sample_out_kernel,
        grid=(rows // tile,),
        in_specs=[_lane_chunk_spec(tile, D_CONV) if a is c else _row_spec(tile, a.shape[1]) for a in acts]
        + [_const_spec(a.shape) for a in consts],
        out_specs=_row_spec(tile, D_MODEL),
        out_shape=jax.ShapeDtypeStruct((rows, D_MODEL), F32),
        compiler_params=pltpu.CompilerParams(
            dimension_semantics=("parallel",), vmem_limit_bytes=VMEM_LIMIT),
        name="sample_out",
    )(*acts, *consts)


def _ffn_kernel(x_ref, pre_g_ref, post_g_ref, w_gate_ref, w_up_ref, w_down_ref, y_ref):
    for r0 in range(0, x_ref.shape[0], FFN_BLOCK):
        rows = slice(r0, r0 + FFN_BLOCK)
        y_ref[rows, :] = _ffn_block(x_ref[rows, :], pre_g_ref[...], post_g_ref[...],
                                    w_gate_ref, w_up_ref, w_down_ref)


def _ffn(x2d, pre_g, post_g, w_gate, w_up, w_down):
    rows = x2d.shape[0]
    tile = min(FFN_TILE, rows)
    return pl.pallas_call(
        _ffn_kernel,
        grid=(rows // tile,),
        in_specs=[
            pl.BlockSpec((tile, D_MODEL), lambda i: (i, 0)),
            _const_spec((1, D_MODEL)), _const_spec((1, D_MODEL)),
            _const_spec(w_gate.shape), _const_spec(w_up.shape), _const_spec(w_down.shape),
        ],
        out_specs=pl.BlockSpec((tile, D_MODEL), lambda i: (i, 0)),
        out_shape=jax.ShapeDtypeStruct((rows, D_MODEL), F32),
        compiler_params=pltpu.CompilerParams(
            dimension_semantics=("parallel",), vmem_limit_bytes=VMEM_LIMIT),
        name="ffn",
    )(x2d, pre_g, post_g, w_gate, w_up, w_down)


def kernel(x_prompt, x_sample, mem_prompt, state_conv, state_hgrn, cache_mem_k, cache_mem_v, norm_pre_mix, norm_post_mix, norm_pre_ffn, norm_post_ffn, w_in, conv_w, conv_b, conv_ln_g, conv_ln_b, w_conv_out, hg_lb_logits, hg_norm_g, w_hg_out, mem_norm_g, w_mem_kv, w_mem_out, w_out, w_ffn_gate, w_ffn_up, w_ffn_down):
    depth = w_in.shape[0]
    assert depth == 1 and hg_lb_logits.shape[0] == 2, "single-layer step"
    batch, seq, _ = x_prompt.shape
    n_dec, dec, _ = x_sample.shape
    n_mem = mem_prompt.shape[1]
    assert seq % PROMPT_TILE == 0 and n_dec % SAMPLE_GROUP == 0 and dec == SUBLANES
    assert (n_dec * dec) % SAMPLE_TILE == 0 and (batch * n_mem) % MEMORY_TILE == 0

    vectors = (norm_pre_mix, norm_post_mix, conv_w[0], conv_b, conv_ln_g, conv_ln_b,
               hg_lb_logits, jnp.tile(hg_norm_g, (1, HG_HEADS)))

    mk, mv, kb, vb, *weights = _memory_kv(
        mem_prompt.reshape(batch * n_mem, D_MODEL), mem_norm_g, w_mem_kv[0], MEMORY_TILE,
        tuple(w[0] for w in (w_in, w_conv_out, w_hg_out, w_mem_out, w_out)))

    xp, conv_p, hg_p, *ffn_weights = _prompt_mixer(
        x_prompt, kb.reshape(batch, n_mem, D_MEM), vb.reshape(batch, n_mem, D_MEM), vectors, weights,
        (w_ffn_gate[0], w_ffn_up[0], w_ffn_down[0]))
    yp = _ffn(xp.reshape(batch * seq, D_MODEL), norm_pre_ffn, norm_post_ffn, *ffn_weights)

    xs2d = x_sample.reshape(n_dec * dec, D_MODEL)
    u, qd, ki, ke, v, blast, q4, hgate, gates = _sample_in(xs2d, norm_pre_mix, hg_lb_logits, weights[0])
    k2 = cache_mem_k[0].reshape(n_dec, n_mem * MEM_HEADS, MEM_HEAD_DIM)
    v2 = cache_mem_v[0].reshape(n_dec, n_mem * MEM_HEADS, MEM_HEAD_DIM)
    c, o, om, conv_rows, hg_s = _sample_seq(u, qd, ki, ke, v, blast, q4,
                                            jnp.transpose(state_conv[0], (1, 0, 2)), state_hgrn[0],
                                            k2, v2, conv_w[0])
    conv_s = jnp.transpose(conv_rows, (1, 0, 2))
    ys = _sample_out(xs2d, c, o, om, hgate, gates,
                     (norm_post_mix, conv_b, conv_ln_g, conv_ln_b, vectors[-1]), weights[1:],
                     (norm_pre_ffn, norm_post_ffn), ffn_weights)

    kv_shape = (1, batch, n_mem, MEM_HEADS, MEM_HEAD_DIM)
    return (yp.reshape(batch, seq, D_MODEL), ys.reshape(n_dec, dec, D_MODEL),
            conv_p[None], hg_p[None], mk.reshape(kv_shape), mv.reshape(kv_shape),
            conv_s[None], hg_s[None])
```

```python
from typing import NamedTuple

import jax
import jax.numpy as jnp
from jax import lax
from jax.experimental import pallas as pl
from jax.experimental.pallas import tpu as pltpu

D_MODEL = 1024
D_CONV = 512
CONV_WIDTH = 31
CONV_PREFIX = CONV_WIDTH - 1
HG_HEADS = 4
HG_DK = 128
HG_DV = 128
D_HGRN = HG_HEADS * HG_DK
MEM_HEADS = 4
MEM_HEAD_DIM = 128
D_MEM = MEM_HEADS * MEM_HEAD_DIM
CHUNK = 32
EPS = 1e-6
LOG2E = 1.4426950408889634

COL_CA = 0
COL_CB = COL_CA + D_CONV
COL_HQ = COL_CB + D_CONV
COL_HF = COL_HQ + D_HGRN
COL_HI = COL_HF + D_HGRN
COL_HGATE = COL_HI + D_HGRN
COL_MQ = COL_HGATE + D_HGRN
COL_GATES = COL_MQ + D_MEM

SUBLANES = 8
LANES = 128
VMEM_LIMIT = 56 * 1024 * 1024

PROMPT_TILE = 256
PROMPT_STEP_TILES = 2
SAMPLE_GROUP = 8
SAMPLE_TILE = 256
FFN_TILE = 512
MEMORY_TILE = 256
FFN_BLOCK = 256
CONV_ROWS = 32

BF16 = jnp.bfloat16
F32 = jnp.float32


def _dot(a, b):
    return jnp.dot(a, b, preferred_element_type=F32)


def _dot_nt(a, b):
    return lax.dot_general(a, b, (((1,), (1,)), ((), ())), preferred_element_type=F32)


def _dot_tn(a, b):
    return lax.dot_general(a, b, (((0,), (0,)), ((), ())), preferred_element_type=F32)


def _rms(x, gain):
    return x * lax.rsqrt(jnp.mean(x * x, axis=-1, keepdims=True) + EPS) * gain


def _sigmoid(x):
    return 1.0 / (1.0 + jnp.exp2(x * -LOG2E))


def _silu(x):
    return x * _sigmoid(x)


def _attention_softmax(scores, keep=None):
    t = scores * (MEM_HEAD_DIM ** -0.5 * LOG2E)
    if keep is not None:
        t = jnp.where(keep, t, -jnp.inf)
    e = jnp.exp2(t - jnp.max(t, axis=-1, keepdims=True))
    return e / jnp.sum(e, axis=-1, keepdims=True)


def _forget_terms(hf, lb):
    sig = _sigmoid(hf)
    return jnp.log(lb + (1.0 - lb) * sig), (1.0 - lb) * (1.0 - sig)


def _lower_bound(lb_logits):
    m = jnp.max(lb_logits, axis=0, keepdims=True)
    e = jnp.exp(lb_logits - m)
    return e[0:1] / jnp.sum(e, axis=0, keepdims=True)


def _segment_cumsum(x, seg):
    pos = lax.broadcasted_iota(jnp.int32, x.shape, 0) & (seg - 1)
    s = 1
    while s < seg:
        x = x + jnp.where(pos >= s, pltpu.roll(x, s, axis=0), 0.0)
        s *= 2
    return x


def _conv_ln_silu(c, conv_b, ln_g, ln_b):
    c = c + conv_b
    mu = jnp.mean(c, axis=-1, keepdims=True)
    d = c - mu
    var = jnp.mean(d * d, axis=-1, keepdims=True)
    return _silu(d * lax.rsqrt(var + EPS) * ln_g + ln_b)


def _hgrn_gates(hq, hf, lb, seg):
    q = _silu(hq)
    logf, k = _forget_terms(hf, lb)
    b = _segment_cumsum(logf, seg)
    rows = b.shape[0]
    b3 = b.reshape(rows // seg, seg, D_HGRN)
    b_last3 = b3[:, seg - 1:seg, :]
    rest = (b_last3 - b3).reshape(rows, D_HGRN)
    q_dec = q * jnp.exp(b)
    k_inv = k * jnp.exp(-b)
    k_end = k * jnp.exp(rest)
    return q_dec, k_inv, k_end, b_last3.reshape(rows // seg, D_HGRN)


class _TileLevels(NamedTuple):
    q_mid: jax.Array
    k_mid: jax.Array
    q_64: jax.Array
    k_64: jax.Array
    q_128: jax.Array
    k_128: jax.Array
    q_tile: jax.Array
    k_tile: jax.Array
    decay: jax.Array


def _hgrn_tile_levels(hq, hf, lb):
    tl = hq.shape[0]
    n_blocks = 4
    block = 2 * CHUNK
    assert tl == n_blocks * block
    q = _silu(hq)
    logf, k = _forget_terms(hf, lb)
    b = _segment_cumsum(logf, CHUNK)
    chunks = [b[n * CHUNK:(n + 1) * CHUNK] for n in range(2 * n_blocks)]
    totals = [c[CHUNK - 1:CHUNK] for c in chunks]
    e = jnp.concatenate([c - totals[n] if n % 2 == 0 else c for n, c in enumerate(chunks)], axis=0)
    q_mid = q * jnp.exp(e)
    k_mid = k * jnp.exp(-e)
    first = [totals[2 * j] for j in range(n_blocks)]
    second = [totals[2 * j + 1] for j in range(n_blocks)]
    both = [first[j] + second[j] for j in range(n_blocks)]

    def scaled(x, blocks, log_scales):
        return jnp.concatenate(
            [x[j * block:(j + 1) * block] * jnp.exp(s) for j, s in zip(blocks, log_scales)], axis=0)

    every = range(n_blocks)
    return _TileLevels(
        q_mid=q_mid, k_mid=k_mid,
        q_64=scaled(q_mid, every, first),
        k_64=scaled(k_mid, every, second),
        q_128=scaled(q_mid, (2, 3), (first[2], first[3] + both[2])),
        k_128=scaled(k_mid, (0, 1), (second[0] + both[1], second[1])),
        q_tile=scaled(q_mid, every, [first[j] + sum(both[:j], 0.0) for j in every]),
        k_tile=scaled(k_mid, every, [second[j] + sum(both[j + 1:], 0.0) for j in every]),
        decay=jnp.exp(sum(both[1:], both[0])))


def _head_rms(o, gain4):
    parts = []
    for h in range(HG_HEADS):
        oh = o[:, h * HG_DV:(h + 1) * HG_DV]
        parts.append(oh * lax.rsqrt(jnp.mean(oh * oh, axis=-1, keepdims=True) + EPS))
    return jnp.concatenate(parts, axis=-1) * gain4


def _branch_gate(hb, w_in_ref, i):
    c0 = COL_GATES + i * D_MODEL
    return _sigmoid(_dot(hb, w_in_ref[:, c0:c0 + D_MODEL]))


def _merge_out(gate, x, branches, w_out_ref, post_g):
    acc = None
    for i, p in enumerate(branches):
        acc = gate(i) * p if acc is None else acc + gate(i) * p
    m = _dot(acc.astype(BF16), w_out_ref[...])
    return x + _rms(m, post_g)


def _ffn_block(x, pre_g, post_g, w_gate_ref, w_up_ref, w_down_ref):
    hb = _rms(x, pre_g).astype(BF16)
    f = _silu(_dot(hb, w_gate_ref[...])) * _dot(hb, w_up_ref[...])
    d = _dot(f.astype(BF16), w_down_ref[...])
    return x + _rms(d, post_g)


def _cast_block_rows(rows, n_steps):
    bf16_sublanes = 2 * SUBLANES
    for n_blocks in range(min(n_steps, rows // bf16_sublanes), 0, -1):
        if rows % n_blocks == 0 and (rows // n_blocks) % bf16_sublanes == 0:
            return rows // n_blocks
    raise ValueError(f"no bf16-tileable split of {rows} rows")


def _cast_specs(weights, n_steps, step_of):
    specs = []
    for w in weights:
        rows_blk = _cast_block_rows(w.shape[0], n_steps)
        last = w.shape[0] // rows_blk - 1
        specs.append(pl.BlockSpec((rows_blk, w.shape[1]),
                                  lambda *idx, last=last: (jnp.minimum(step_of(*idx), last), 0)))
    return specs


def _cast_blocks(srcs, dsts):
    for src, dst in zip(srcs, dsts):
        dst[...] = src[...].astype(BF16)


def _memory_kv_kernel(mem_ref, g_ref, w_ref, *rest):
    n_cast = (len(rest) - 5) // 2
    k_rows_ref, v_rows_ref, k_ref, v_ref = rest[n_cast:n_cast + 4]
    wb_ref = rest[-1]
    _cast_blocks(rest[:n_cast], rest[n_cast + 4:-1])

    @pl.when(pl.program_id(0) == 0)
    def _():
        wb_ref[...] = w_ref[...].astype(BF16)

    m = _rms(mem_ref[...], g_ref[...]).astype(BF16)
    tokens = mem_ref.shape[0]
    for full, rows_ref, op_ref in ((_dot(m, wb_ref[:, :D_MEM]), k_rows_ref, k_ref),
                                   (_dot(m, wb_ref[:, D_MEM:]), v_rows_ref, v_ref)):
        op_ref[...] = full.astype(BF16)
        for h in range(MEM_HEADS):
            rows_ref[pl.ds(h, tokens, stride=MEM_HEADS), :] = full[:, h * MEM_HEAD_DIM:(h + 1) * MEM_HEAD_DIM]


def _memory_kv(mem2d, gain, w_kv, tile, later_weights):
    rows = mem2d.shape[0]
    n_steps = rows // tile
    cast_specs = _cast_specs(later_weights, n_steps, lambda i: i)
    return pl.pallas_call(
        _memory_kv_kernel,
        grid=(n_steps,),
        in_specs=[
            pl.BlockSpec((tile, D_MODEL), lambda i: (i, 0)),
            pl.BlockSpec((1, D_MODEL), lambda i: (0, 0)),
            _const_spec(w_kv.shape),
        ] + cast_specs,
        out_specs=[pl.BlockSpec((tile * MEM_HEADS, MEM_HEAD_DIM), lambda i: (i, 0))] * 2
        + [pl.BlockSpec((tile, D_MEM), lambda i: (i, 0))] * 2 + cast_specs,
        out_shape=[jax.ShapeDtypeStruct((rows * MEM_HEADS, MEM_HEAD_DIM), F32)] * 2
        + [jax.ShapeDtypeStruct((rows, D_MEM), BF16)] * 2
        + [jax.ShapeDtypeStruct(w.shape, BF16) for w in later_weights],
        scratch_shapes=[pltpu.VMEM(w_kv.shape, BF16)],
        compiler_params=pltpu.CompilerParams(
            dimension_semantics=("arbitrary",), vmem_limit_bytes=VMEM_LIMIT),
        name="memory_kv",
    )(mem2d, gain, w_kv, *later_weights)


def _prompt_mixer_kernel(x_ref, kb_ref, vb_ref, pre_g_ref, post_g_ref, conv_w_ref, conv_b_ref,
                         ln_g_ref, ln_b_ref, lb_logits_ref, hg_g_ref, w_in_ref, w_conv_out_ref,
                         w_hg_out_ref, w_mem_out_ref, w_out_ref, *rest):
    n_cast = (len(rest) - 6) // 2
    cast_src = rest[:n_cast]
    y_ref, new_conv_ref, new_hg_ref = rest[n_cast:n_cast + 3]
    cast_dst = rest[n_cast + 3:2 * n_cast + 3]
    shift_ref, tail_ref, state_t_ref = rest[2 * n_cast + 3:]
    _cast_blocks(cast_src, cast_dst)
    t = pl.program_id(1)
    last_t = pl.num_programs(1) - 1
    tl = PROMPT_TILE
    n_tiles = x_ref.shape[0] // tl

    @pl.when(t == 0)
    def _():
        tail_ref[...] = jnp.zeros_like(tail_ref)
        state_t_ref[...] = jnp.zeros_like(state_t_ref)

    lb = _lower_bound(lb_logits_ref[...])
    half = tl // 2
    row = lax.broadcasted_iota(jnp.int32, (tl, tl), 0)
    col = lax.broadcasted_iota(jnp.int32, (tl, tl), 1)
    span = row ^ col
    near = (span < 2 * CHUNK) & (col <= row)
    mid = (span < 4 * CHUNK) & (col < row)

    def mix_rows():
        n_rows = n_tiles * tl
        shift = shift_ref
        x = x_ref[...]
        hb = _rms(x, pre_g_ref[...]).astype(BF16)

        def proj(c0, c1):
            return _dot(hb, w_in_ref[:, c0:c1])

        u = proj(COL_CA, COL_CB) * _sigmoid(proj(COL_CB, COL_HQ))
        shift[0:CONV_ROWS, :] = tail_ref[...]
        shift[CONV_ROWS:CONV_ROWS + n_rows, :] = u
        tail_ref[...] = u[n_rows - CONV_ROWS:]

        lead = CONV_ROWS - CONV_PREFIX
        conv = None
        for r in range(SUBLANES):
            rows_r = n_rows if r == 0 else n_rows + SUBLANES
            v_r = None
            for a in range((CONV_WIDTH + lead) // SUBLANES + 1):
                j = a * SUBLANES + r - lead
                if 0 <= j < CONV_WIDTH:
                    term = shift[a * SUBLANES:a * SUBLANES + rows_r, :] * conv_w_ref[j:j + 1, :]
                    v_r = term if v_r is None else v_r + term
            if r:
                v_r = pltpu.roll(v_r, rows_r - r, axis=0)[:n_rows]
            conv = v_r if conv is None else conv + v_r
        c = _conv_ln_silu(conv, conv_b_ref[...], ln_g_ref[...], ln_b_ref[...]).astype(BF16)

        hq = proj(COL_HQ, COL_HF)
        hf = proj(COL_HF, COL_HI)
        levels = []
        for i in range(n_tiles):
            lv = _hgrn_tile_levels(hq[i * tl:(i + 1) * tl], hf[i * tl:(i + 1) * tl], lb)
            levels.append(_TileLevels(*[a.astype(BF16) for a in lv[:-1]], lv.decay))
        vb16 = proj(COL_HI, COL_HGATE).astype(BF16)
        hgate = _silu(proj(COL_HGATE, COL_MQ))
        mq = proj(COL_MQ, COL_GATES).astype(BF16)
        gates = [_branch_gate(hb, w_in_ref, g) for g in range(3)]

        head = lambda h: slice(h * MEM_HEAD_DIM, (h + 1) * MEM_HEAD_DIM)
        s = jnp.concatenate([_dot_nt(mq[:, head(h)], kb_ref[:, head(h)]) for h in range(MEM_HEADS)],
                            axis=0)
        p = _attention_softmax(s).astype(BF16)
        om = [_dot(p[h * n_rows:(h + 1) * n_rows], vb_ref[:, head(h)]) for h in range(MEM_HEADS)]

        o_tiles = []
        for i, lvb in enumerate(levels):
            o_heads = []
            for h in range(HG_HEADS):
                hs = slice(h * HG_DK, (h + 1) * HG_DK)
                vh = vb16[i * tl:(i + 1) * tl, hs]
                s_near = _dot_nt(lvb.q_mid[:, hs], lvb.k_mid[:, hs])
                s_mid = _dot_nt(lvb.q_64[:, hs], lvb.k_64[:, hs])
                s_far = _dot_nt(lvb.q_128[:, hs], lvb.k_128[:, hs])
                base = jnp.where(near, s_near, jnp.where(mid, s_mid, 0.0))
                scores = jnp.concatenate(
                    [base[:half], jnp.concatenate([s_far, base[half:, half:]], axis=1)], axis=0)
                st = state_t_ref[h]
                o_heads.append(_dot(scores.astype(BF16), vh)
                               + _dot_nt(lvb.q_tile[:, hs], st.astype(BF16)))
                state_t_ref[h] = st * lvb.decay[:, hs] + _dot_tn(vh, lvb.k_tile[:, hs])
            o_tiles.append(jnp.concatenate(o_heads, axis=-1))
        o = _head_rms(jnp.concatenate(o_tiles, axis=0), hg_g_ref[...]) * hgate

        p_conv = _dot(c, w_conv_out_ref[...])
        p_mem = _dot(jnp.concatenate(om, axis=-1).astype(BF16), w_mem_out_ref[...])
        p_hg = _dot(o.astype(BF16), w_hg_out_ref[...])
        y_ref[...] = _merge_out(lambda g: gates[g], x, (p_conv, p_hg, p_mem), w_out_ref, post_g_ref[...])

    mix_rows()

    @pl.when(t == last_t)
    def _():
        new_conv_ref[...] = tail_ref[CONV_ROWS - CONV_PREFIX:CONV_ROWS, :]
        for h in range(HG_HEADS):
            new_hg_ref[h] = state_t_ref[h].T


def _const_spec(shape):
    zeros = (0,) * len(shape)
    return pl.BlockSpec(shape, lambda *_: zeros, pipeline_mode=pl.Buffered(1))


def _mixer_weight_specs(w_in, w_conv_out, w_hg_out, w_mem_out, w_out):
    return [_const_spec(w.shape) for w in (w_in, w_conv_out, w_hg_out, w_mem_out, w_out)]


def _vector_specs():
    return [
        _const_spec((1, D_MODEL)), _const_spec((1, D_MODEL)),
        _const_spec((CONV_WIDTH, D_CONV)), _const_spec((1, D_CONV)),
        _const_spec((1, D_CONV)), _const_spec((1, D_CONV)),
        _const_spec((2, D_HGRN)), _const_spec((1, D_HGRN)),
    ]


def _prompt_mixer(x, mk, mv, vectors, weights, later_weights):
    batch, seq, _ = x.shape
    n_mem = mk.shape[1]
    tl = PROMPT_TILE * PROMPT_STEP_TILES
    n_t = seq // tl
    cast_specs = _cast_specs(later_weights, batch * n_t, lambda b, t: b * n_t + t)
    return pl.pallas_call(
        _prompt_mixer_kernel,
        grid=(batch, n_t),
        in_specs=[
            pl.BlockSpec((None, tl, D_MODEL), lambda b, t: (b, t, 0)),
            pl.BlockSpec((None, n_mem, D_MEM), lambda b, t: (b, 0, 0)),
            pl.BlockSpec((None, n_mem, D_MEM), lambda b, t: (b, 0, 0)),
        ] + _vector_specs() + _mixer_weight_specs(*weights) + cast_specs,
        out_specs=[
            pl.BlockSpec((None, tl, D_MODEL), lambda b, t: (b, t, 0)),
            pl.BlockSpec((None, CONV_PREFIX, D_CONV), lambda b, t: (b, 0, 0)),
            pl.BlockSpec((None, HG_HEADS, HG_DK, HG_DV), lambda b, t: (b, 0, 0, 0)),
        ] + cast_specs,
        out_shape=[
            jax.ShapeDtypeStruct((batch, seq, D_MODEL), F32),
            jax.ShapeDtypeStruct((batch, CONV_PREFIX, D_CONV), F32),
            jax.ShapeDtypeStruct((batch, HG_HEADS, HG_DK, HG_DV), F32),
        ] + [jax.ShapeDtypeStruct(w.shape, BF16) for w in later_weights],
        scratch_shapes=[
            pltpu.VMEM((tl + CONV_ROWS, D_CONV), F32),
            pltpu.VMEM((CONV_ROWS, D_CONV), F32),
            pltpu.VMEM((HG_HEADS, HG_DV, HG_DK), F32),
        ],
        compiler_params=pltpu.CompilerParams(
            dimension_semantics=("arbitrary", "arbitrary"), vmem_limit_bytes=VMEM_LIMIT),
        name="prompt_mixer",
    )(x, mk, mv, *vectors, *weights, *later_weights)


def _store_lane_chunks(ref, x):
    for i in range(ref.shape[0]):
        ref[i] = x[:, i * LANES:(i + 1) * LANES]


def _load_lane_chunks(ref, rows=slice(None)):
    return jnp.concatenate([ref[i, rows, :] for i in range(ref.shape[0])], axis=1)


def _lane_chunk_spec(tile, width):
    return pl.BlockSpec((width // LANES, tile, LANES), lambda i: (0, i, 0))


def _sample_in_kernel(x_ref, pre_g_ref, lb_logits_ref, w_in_ref,
                      u_ref, qd_ref, ki_ref, ke_ref, v_ref, blast_ref, q4_ref, hgate_ref, gates_ref):
    n_seq = x_ref.shape[0] // SUBLANES
    hb = _rms(x_ref[...], pre_g_ref[...]).astype(BF16)
    _store_lane_chunks(
        u_ref, _dot(hb, w_in_ref[:, COL_CA:COL_CB]) * _sigmoid(_dot(hb, w_in_ref[:, COL_CB:COL_HQ])))
    lb = _lower_bound(lb_logits_ref[...])
    q_dec, k_inv, k_end, b_last = _hgrn_gates(
        _dot(hb, w_in_ref[:, COL_HQ:COL_HF]), _dot(hb, w_in_ref[:, COL_HF:COL_HI]), lb, SUBLANES)
    qd_ref[...] = q_dec
    ki_ref[...] = k_inv
    ke_ref[...] = k_end
    blast_ref[...] = b_last
    v_ref[...] = _dot(hb, w_in_ref[:, COL_HI:COL_HGATE])
    hgate_ref[...] = _silu(_dot(hb, w_in_ref[:, COL_HGATE:COL_MQ]))
    mq = _dot(hb, w_in_ref[:, COL_MQ:COL_GATES])
    for h in range(MEM_HEADS):
        q4_ref[:, h * SUBLANES:(h + 1) * SUBLANES, :] = (
            mq[:, h * MEM_HEAD_DIM:(h + 1) * MEM_HEAD_DIM].reshape(n_seq, SUBLANES, MEM_HEAD_DIM))
    for i in range(3):
        gates_ref[:, i * D_MODEL:(i + 1) * D_MODEL] = _branch_gate(hb, w_in_ref, i)


def _sample_seq_kernel(u_ref, qd_ref, ki_ref, ke_ref, v_ref, blast_ref, q4_ref, sc_ref, sh_ref,
                       k2_ref, v2_ref, conv_w_ref,
                       c_ref, o_ref, om_ref, new_conv_ref, new_hg_ref):
    n_seq, dec = sc_ref.shape[1], SUBLANES
    rows = n_seq * dec

    q_rows = MEM_HEADS * dec
    s = jnp.concatenate([_dot_nt(q4_ref[g].astype(BF16), k2_ref[g].astype(BF16)) for g in range(n_seq)],
                        axis=0)
    assert dec & (dec - 1) == 0 and MEM_HEADS & (MEM_HEADS - 1) == 0
    q_head = (lax.broadcasted_iota(jnp.int32, s.shape, 0) >> (dec.bit_length() - 1)) & (MEM_HEADS - 1)
    k_head = lax.broadcasted_iota(jnp.int32, s.shape, 1) & (MEM_HEADS - 1)
    p = _attention_softmax(s, q_head == k_head).astype(BF16)
    om = []
    for g in range(n_seq):
        om_g = _dot(p[g * q_rows:(g + 1) * q_rows], v2_ref[g].astype(BF16))
        om.extend(om_g[h * dec:(h + 1) * dec] for h in range(MEM_HEADS))

    u_slabs = [_load_lane_chunks(u_ref, pl.ds(step, n_seq, stride=dec)) for step in range(dec)]
    conv = [None] * dec
    for s in range(CONV_PREFIX + dec):
        slab = sc_ref[s] if s < CONV_PREFIX else u_slabs[s - CONV_PREFIX]
        if s >= dec:
            new_conv_ref[s - dec] = slab
        for step in range(max(0, s - CONV_PREFIX), min(dec, s + 1)):
            term = slab * conv_w_ref[s - step:s - step + 1, :]
            conv[step] = term if conv[step] is None else conv[step] + term

    row = lax.broadcasted_iota(jnp.int32, (rows, rows), 0)
    col = lax.broadcasted_iota(jnp.int32, (rows, rows), 1)
    causal = ((row ^ col) < dec) & (col <= row)
    decay = jnp.exp(blast_ref[...])
    for h in range(HG_HEADS):
        hs = slice(h * HG_DK, (h + 1) * HG_DK)
        scores = jnp.where(causal, _dot_nt(qd_ref[:, hs].astype(BF16), ki_ref[:, hs].astype(BF16)), 0.0)
        o_intra = _dot(scores.astype(BF16), v_ref[:, hs].astype(BF16))
        for g in range(n_seq):
            rs = slice(g * dec, (g + 1) * dec)
            s0 = sh_ref[g, h]
            o_ref[rs, hs] = o_intra[rs] + _dot(qd_ref[rs, hs].astype(BF16), s0.astype(BF16))
            decay_col = jnp.broadcast_to(decay[g:g + 1, hs], (HG_DV, HG_DK)).T
            new_hg_ref[g, h] = decay_col * s0 + _dot_tn(ke_ref[rs, hs].astype(BF16),
                                                        v_ref[rs, hs].astype(BF16))

    for step in range(dec):
        for i in range(c_ref.shape[0]):
            c_ref[i, pl.ds(step, n_seq, stride=dec), :] = conv[step][:, i * LANES:(i + 1) * LANES]
    for g in range(n_seq):
        om_ref[g * dec:(g + 1) * dec, :] = jnp.concatenate(om[g * MEM_HEADS:(g + 1) * MEM_HEADS], axis=1)


def _sample_out_kernel(x_ref, c_ref, o_ref, om_ref, hgate_ref, gates_ref, post_g_ref, conv_b_ref,
                       ln_g_ref, ln_b_ref, hg_g_ref, w_conv_out_ref, w_hg_out_ref, w_mem_out_ref,
                       w_out_ref, ffn_pre_g_ref, ffn_post_g_ref, w_gate_ref, w_up_ref, w_down_ref, y_ref):
    c = _conv_ln_silu(_load_lane_chunks(c_ref), conv_b_ref[...], ln_g_ref[...], ln_b_ref[...])
    p_conv = _dot(c.astype(BF16), w_conv_out_ref[...])
    o = _head_rms(o_ref[...], hg_g_ref[...]) * hgate_ref[...]
    p_hg = _dot(o.astype(BF16), w_hg_out_ref[...])
    p_mem = _dot(om_ref[...].astype(BF16), w_mem_out_ref[...])
    x1 = _merge_out(lambda i: gates_ref[:, i * D_MODEL:(i + 1) * D_MODEL], x_ref[...],
                    (p_conv, p_hg, p_mem), w_out_ref, post_g_ref[...])
    y_ref[...] = _ffn_block(x1, ffn_pre_g_ref[...], ffn_post_g_ref[...], w_gate_ref, w_up_ref, w_down_ref)


def _row_spec(tile, width):
    return pl.BlockSpec((tile, width), lambda i: (i, 0))


def _sample_in(x2d, pre_g, lb_logits, w_in):
    rows = x2d.shape[0]
    tile = SAMPLE_TILE
    seqs = tile // SUBLANES
    wide = [D_HGRN, D_HGRN, D_HGRN, D_HGRN]
    return pl.pallas_call(
        _sample_in_kernel,
        grid=(rows // tile,),
        in_specs=[_row_spec(tile, D_MODEL), _const_spec((1, D_MODEL)), _const_spec((2, D_HGRN)),
                  _const_spec(w_in.shape)],
        out_specs=[_lane_chunk_spec(tile, D_CONV)] + [_row_spec(tile, w) for w in wide] + [
            _row_spec(seqs, D_HGRN),
            pl.BlockSpec((seqs, MEM_HEADS * SUBLANES, MEM_HEAD_DIM), lambda i: (i, 0, 0)),
            _row_spec(tile, D_HGRN),
            _row_spec(tile, 3 * D_MODEL),
        ],
        out_shape=[jax.ShapeDtypeStruct((D_CONV // LANES, rows, LANES), F32)]
        + [jax.ShapeDtypeStruct((rows, w), F32) for w in wide] + [
            jax.ShapeDtypeStruct((rows // SUBLANES, D_HGRN), F32),
            jax.ShapeDtypeStruct((rows // SUBLANES, MEM_HEADS * SUBLANES, MEM_HEAD_DIM), F32),
            jax.ShapeDtypeStruct((rows, D_HGRN), F32),
            jax.ShapeDtypeStruct((rows, 3 * D_MODEL), F32),
        ],
        compiler_params=pltpu.CompilerParams(
            dimension_semantics=("parallel",), vmem_limit_bytes=VMEM_LIMIT),
        name="sample_in",
    )(x2d, pre_g, lb_logits, w_in)


def _sample_seq(u, qd, ki, ke, v, blast, q4, conv_rows, state_hgrn, k2, v2, conv_w):
    n = conv_rows.shape[1]
    g = SAMPLE_GROUP
    rows = g * SUBLANES
    kv_rows = k2.shape[1]
    seq_block = lambda *tail: pl.BlockSpec((g,) + tail, lambda i: (i,) + (0,) * len(tail))
    conv_block = pl.BlockSpec((CONV_PREFIX, g, D_CONV), lambda i: (0, i, 0))
    return pl.pallas_call(
        _sample_seq_kernel,
        grid=(n // g,),
        in_specs=[_lane_chunk_spec(rows, D_CONV)] + [_row_spec(rows, D_HGRN)] * 4 + [
            _row_spec(g, D_HGRN),
            seq_block(MEM_HEADS * SUBLANES, MEM_HEAD_DIM),
            conv_block,
            seq_block(HG_HEADS, HG_DK, HG_DV),
            seq_block(kv_rows, MEM_HEAD_DIM),
            seq_block(kv_rows, MEM_HEAD_DIM),
            _const_spec((CONV_WIDTH, D_CONV)),
        ],
        out_specs=[_lane_chunk_spec(rows, D_CONV), _row_spec(rows, D_HGRN), _row_spec(rows, D_MEM),
                   conv_block, seq_block(HG_HEADS, HG_DK, HG_DV)],
        out_shape=[
            jax.ShapeDtypeStruct((D_CONV // LANES, n * SUBLANES, LANES), F32),
            jax.ShapeDtypeStruct((n * SUBLANES, D_HGRN), F32),
            jax.ShapeDtypeStruct((n * SUBLANES, D_MEM), F32),
            jax.ShapeDtypeStruct((CONV_PREFIX, n, D_CONV), F32),
            jax.ShapeDtypeStruct((n, HG_HEADS, HG_DK, HG_DV), F32),
        ],
        compiler_params=pltpu.CompilerParams(
            dimension_semantics=("parallel",), vmem_limit_bytes=VMEM_LIMIT),
        name="sample_seq",
    )(u, qd, ki, ke, v, blast, q4, conv_rows, state_hgrn, k2, v2, conv_w)


def _sample_out(x2d, c, o, om, hgate, gates, vectors, weights, ffn_vectors, ffn_weights):
    rows = x2d.shape[0]
    tile = SAMPLE_TILE
    acts = (x2d, c, o, om, hgate, gates)
    consts = tuple(vectors) + tuple(weights) + tuple(ffn_vectors) + tuple(ffn_weights)
    return pl.pallas_call(
        _sample_out_kernel,
        grid=(rows // tile,),
        in_specs=[_lane_chunk_spec(tile, D_CONV) if a is c else _row_spec(tile, a.shape[1]) for a in acts]
        + [_const_spec(a.shape) for a in consts],
        out_specs=_row_spec(tile, D_MODEL),
        out_shape=jax.ShapeDtypeStruct((rows, D_MODEL), F32),
        compiler_params=pltpu.CompilerParams(
            dimension_semantics=("parallel",), vmem_limit_bytes=VMEM_LIMIT),
        name="sample_out",
    )(*acts, *consts)


def _ffn_kernel(x_ref, pre_g_ref, post_g_ref, w_gate_ref, w_up_ref, w_down_ref, y_ref):
    for r0 in range(0, x_ref.shape[0], FFN_BLOCK):
        rows = slice(r0, r0 + FFN_BLOCK)
        y_ref[rows, :] = _ffn_block(x_ref[rows, :], pre_g_ref[...], post_g_ref[...],
                                    w_gate_ref, w_up_ref, w_down_ref)


def _ffn(x2d, pre_g, post_g, w_gate, w_up, w_down):
    rows = x2d.shape[0]
    tile = min(FFN_TILE, rows)
    return pl.pallas_call(
        _ffn_kernel,
        grid=(rows // tile,),
        in_specs=[
            pl.BlockSpec((tile, D_MODEL), lambda i: (i, 0)),
            _const_spec((1, D_MODEL)), _const_spec((1, D_MODEL)),
            _const_spec(w_gate.shape), _const_spec(w_up.shape), _const_spec(w_down.shape),
        ],
        out_specs=pl.BlockSpec((tile, D_MODEL), lambda i: (i, 0)),
        out_shape=jax.ShapeDtypeStruct((rows, D_MODEL), F32),
        compiler_params=pltpu.CompilerParams(
            dimension_semantics=("parallel",), vmem_limit_bytes=VMEM_LIMIT),
        name="ffn",
    )(x2d, pre_g, post_g, w_gate, w_up, w_down)


def kernel(x_prompt, x_sample, mem_prompt, state_conv, state_hgrn, cache_mem_k, cache_mem_v, norm_pre_mix, norm_post_mix, norm_pre_ffn, norm_post_ffn, w_in, conv_w, conv_b, conv_ln_g, conv_ln_b, w_conv_out, hg_lb_logits, hg_norm_g, w_hg_out, mem_norm_g, w_mem_kv, w_mem_out, w_out, w_ffn_gate, w_ffn_up, w_ffn_down):
    depth = w_in.shape[0]
    assert depth == 1 and hg_lb_logits.shape[0] == 2, "single-layer step"
    batch, seq, _ = x_prompt.shape
    n_dec, dec, _ = x_sample.shape
    n_mem = mem_prompt.shape[1]
    assert seq % PROMPT_TILE == 0 and n_dec % SAMPLE_GROUP == 0 and dec == SUBLANES
    assert (n_dec * dec) % SAMPLE_TILE == 0 and (batch * n_mem) % MEMORY_TILE == 0

    vectors = (norm_pre_mix, norm_post_mix, conv_w[0], conv_b, conv_ln_g, conv_ln_b,
               hg_lb_logits, jnp.tile(hg_norm_g, (1, HG_HEADS)))

    mk, mv, kb, vb, *weights = _memory_kv(
        mem_prompt.reshape(batch * n_mem, D_MODEL), mem_norm_g, w_mem_kv[0], MEMORY_TILE,
        tuple(w[0] for w in (w_in, w_conv_out, w_hg_out, w_mem_out, w_out)))

    xp, conv_p, hg_p, *ffn_weights = _prompt_mixer(
        x_prompt, kb.reshape(batch, n_mem, D_MEM), vb.reshape(batch, n_mem, D_MEM), vectors, weights,
        (w_ffn_gate[0], w_ffn_up[0], w_ffn_down[0]))
    yp = _ffn(xp.reshape(batch * seq, D_MODEL), norm_pre_ffn, norm_post_ffn, *ffn_weights)

    xs2d = x_sample.reshape(n_dec * dec, D_MODEL)
    u, qd, ki, ke, v, blast, q4, hgate, gates = _sample_in(xs2d, norm_pre_mix, hg_lb_logits, weights[0])
    k2 = cache_mem_k[0].reshape(n_dec, n_mem * MEM_HEADS, MEM_HEAD_DIM)
    v2 = cache_mem_v[0].reshape(n_dec, n_mem * MEM_HEADS, MEM_HEAD_DIM)
    c, o, om, conv_rows, hg_s = _sample_seq(u, qd, ki, ke, v, blast, q4,
                                            jnp.transpose(state_conv[0], (1, 0, 2)), state_hgrn[0],
                                            k2, v2, conv_w[0])
    conv_s = jnp.transpose(conv_rows, (1, 0, 2))
    ys = _sample_out(xs2d, c, o, om, hgate, gates,
                     (norm_post_mix, conv_b, conv_ln_g, conv_ln_b, vectors[-1]), weights[1:],
                     (norm_pre_ffn, norm_post_ffn), ffn_weights)

    kv_shape = (1, batch, n_mem, MEM_HEADS, MEM_HEAD_DIM)
    return (yp.reshape(batch, seq, D_MODEL), ys.reshape(n_dec, dec, D_MODEL),
            conv_p[None], hg_p[None], mk.reshape(kv_shape), mv.reshape(kv_shape),
            conv_s[None], hg_s[None])
```

```python
from typing import NamedTuple

import jax
import jax.numpy as jnp
from jax import lax
from jax.experimental import pallas as pl
from jax.experimental.pallas import tpu as pltpu

D_MODEL = 1024
D_CONV = 512
CONV_WIDTH = 31
CONV_PREFIX = CONV_WIDTH - 1
HG_HEADS = 4
HG_DK = 128
HG_DV = 128
D_HGRN = HG_HEADS * HG_DK
MEM_HEADS = 4
MEM_HEAD_DIM = 128
D_MEM = MEM_HEADS * MEM_HEAD_DIM
CHUNK = 32
EPS = 1e-6
LOG2E = 1.4426950408889634

COL_CA = 0
COL_CB = COL_CA + D_CONV
COL_HQ = COL_CB + D_CONV
COL_HF = COL_HQ + D_HGRN
COL_HI = COL_HF + D_HGRN
COL_HGATE = COL_HI + D_HGRN
COL_MQ = COL_HGATE + D_HGRN
COL_GATES = COL_MQ + D_MEM

SUBLANES = 8
LANES = 128
VMEM_LIMIT = 56 * 1024 * 1024

PROMPT_TILE = 256
PROMPT_STEP_TILES = 2
SAMPLE_GROUP = 8
SAMPLE_TILE = 256
FFN_TILE = 512
MEMORY_TILE = 256
FFN_BLOCK = 256
CONV_ROWS = 32

BF16 = jnp.bfloat16
F32 = jnp.float32


def _dot(a, b):
    return jnp.dot(a, b, preferred_element_type=F32)


def _dot_nt(a, b):
    return lax.dot_general(a, b, (((1,), (1,)), ((), ())), preferred_element_type=F32)


def _dot_tn(a, b):
    return lax.dot_general(a, b, (((0,), (0,)), ((), ())), preferred_element_type=F32)


def _rms(x, gain):
    return x * lax.rsqrt(jnp.mean(x * x, axis=-1, keepdims=True) + EPS) * gain


def _sigmoid(x):
    return 1.0 / (1.0 + jnp.exp2(x * -LOG2E))


def _silu(x):
    return x * _sigmoid(x)


def _attention_softmax(scores, keep=None):
    t = scores * (MEM_HEAD_DIM ** -0.5 * LOG2E)
    if keep is not None:
        t = jnp.where(keep, t, -jnp.inf)
    e = jnp.exp2(t - jnp.max(t, axis=-1, keepdims=True))
    return e / jnp.sum(e, axis=-1, keepdims=True)


def _forget_terms(hf, lb):
    sig = _sigmoid(hf)
    return jnp.log(lb + (1.0 - lb) * sig), (1.0 - lb) * (1.0 - sig)


def _lower_bound(lb_logits):
    m = jnp.max(lb_logits, axis=0, keepdims=True)
    e = jnp.exp(lb_logits - m)
    return e[0:1] / jnp.sum(e, axis=0, keepdims=True)


def _segment_cumsum(x, seg):
    pos = lax.broadcasted_iota(jnp.int32, x.shape, 0) & (seg - 1)
    s = 1
    while s < seg:
        x = x + jnp.where(pos >= s, pltpu.roll(x, s, axis=0), 0.0)
        s *= 2
    return x


def _conv_ln_silu(c, conv_b, ln_g, ln_b):
    c = c + conv_b
    mu = jnp.mean(c, axis=-1, keepdims=True)
    d = c - mu
    var = jnp.mean(d * d, axis=-1, keepdims=True)
    return _silu(d * lax.rsqrt(var + EPS) * ln_g + ln_b)


def _hgrn_gates(hq, hf, lb, seg):
    q = _silu(hq)
    logf, k = _forget_terms(hf, lb)
    b = _segment_cumsum(logf, seg)
    rows = b.shape[0]
    b3 = b.reshape(rows // seg, seg, D_HGRN)
    b_last3 = b3[:, seg - 1:seg, :]
    rest = (b_last3 - b3).reshape(rows, D_HGRN)
    q_dec = q * jnp.exp(b)
    k_inv = k * jnp.exp(-b)
    k_end = k * jnp.exp(rest)
    return q_dec, k_inv, k_end, b_last3.reshape(rows // seg, D_HGRN)


class _TileLevels(NamedTuple):
    q_mid: jax.Array
    k_mid: jax.Array
    q_64: jax.Array
    k_64: jax.Array
    q_128: jax.Array
    k_128: jax.Array
    q_tile: jax.Array
    k_tile: jax.Array
    decay: jax.Array


def _hgrn_tile_levels(hq, hf, lb):
    tl = hq.shape[0]
    n_blocks = 4
    block = 2 * CHUNK
    assert tl == n_blocks * block
    q = _silu(hq)
    logf, k = _forget_terms(hf, lb)
    b = _segment_cumsum(logf, CHUNK)
    chunks = [b[n * CHUNK:(n + 1) * CHUNK] for n in range(2 * n_blocks)]
    totals = [c[CHUNK - 1:CHUNK] for c in chunks]
    e = jnp.concatenate([c - totals[n] if n % 2 == 0 else c for n, c in enumerate(chunks)], axis=0)
    q_mid = q * jnp.exp(e)
    k_mid = k * jnp.exp(-e)
    first = [totals[2 * j] for j in range(n_blocks)]
    second = [totals[2 * j + 1] for j in range(n_blocks)]
    both = [first[j] + second[j] for j in range(n_blocks)]

    def scaled(x, blocks, log_scales):
        return jnp.concatenate(
            [x[j * block:(j + 1) * block] * jnp.exp(s) for j, s in zip(blocks, log_scales)], axis=0)

    every = range(n_blocks)
    return _TileLevels(
        q_mid=q_mid, k_mid=k_mid,
        q_64=scaled(q_mid, every, first),
        k_64=scaled(k_mid, every, second),
        q_128=scaled(q_mid, (2, 3), (first[2], first[3] + both[2])),
        k_128=scaled(k_mid, (0, 1), (second[0] + both[1], second[1])),
        q_tile=scaled(q_mid, every, [first[j] + sum(both[:j], 0.0) for j in every]),
        k_tile=scaled(k_mid, every, [second[j] + sum(both[j + 1:], 0.0) for j in every]),
        decay=jnp.exp(sum(both[1:], both[0])))


def _head_rms(o, gain4):
    parts = []
    for h in range(HG_HEADS):
        oh = o[:, h * HG_DV:(h + 1) * HG_DV]
        parts.append(oh * lax.rsqrt(jnp.mean(oh * oh, axis=-1, keepdims=True) + EPS))
    return jnp.concatenate(parts, axis=-1) * gain4


def _branch_gate(hb, w_in_ref, i):
    c0 = COL_GATES + i * D_MODEL
    return _sigmoid(_dot(hb, w_in_ref[:, c0:c0 + D_MODEL]))


def _merge_out(gate, x, branches, w_out_ref, post_g):
    acc = None
    for i, p in enumerate(branches):
        acc = gate(i) * p if acc is None else acc + gate(i) * p
    m = _dot(acc.astype(BF16), w_out_ref[...])
    return x + _rms(m, post_g)


def _ffn_block(x, pre_g, post_g, w_gate_ref, w_up_ref, w_down_ref):
    hb = _rms(x, pre_g).astype(BF16)
    f = _silu(_dot(hb, w_gate_ref[...])) * _dot(hb, w_up_ref[...])
    d = _dot(f.astype(BF16), w_down_ref[...])
    return x + _rms(d, post_g)


def _cast_block_rows(rows, n_steps):
    bf16_sublanes = 2 * SUBLANES
    for n_blocks in range(min(n_steps, rows // bf16_sublanes), 0, -1):
        if rows % n_blocks == 0 and (rows // n_blocks) % bf16_sublanes == 0:
            return rows // n_blocks
    raise ValueError(f"no bf16-tileable split of {rows} rows")


def _cast_specs(weights, n_steps, step_of):
    specs = []
    for w in weights:
        rows_blk = _cast_block_rows(w.shape[0], n_steps)
        last = w.shape[0] // rows_blk - 1
        specs.append(pl.BlockSpec((rows_blk, w.shape[1]),
                                  lambda *idx, last=last: (jnp.minimum(step_of(*idx), last), 0)))
    return specs


def _cast_blocks(srcs, dsts):
    for src, dst in zip(srcs, dsts):
        dst[...] = src[...].astype(BF16)


def _memory_kv_kernel(mem_ref, g_ref, w_ref, *rest):
    n_cast = (len(rest) - 5) // 2
    k_rows_ref, v_rows_ref, k_ref, v_ref = rest[n_cast:n_cast + 4]
    wb_ref = rest[-1]
    _cast_blocks(rest[:n_cast], rest[n_cast + 4:-1])

    @pl.when(pl.program_id(0) == 0)
    def _():
        wb_ref[...] = w_ref[...].astype(BF16)

    m = _rms(mem_ref[...], g_ref[...]).astype(BF16)
    tokens = mem_ref.shape[0]
    for full, rows_ref, op_ref in ((_dot(m, wb_ref[:, :D_MEM]), k_rows_ref, k_ref),
                                   (_dot(m, wb_ref[:, D_MEM:]), v_rows_ref, v_ref)):
        op_ref[...] = full.astype(BF16)
        for h in range(MEM_HEADS):
            rows_ref[pl.ds(h, tokens, stride=MEM_HEADS), :] = full[:, h * MEM_HEAD_DIM:(h + 1) * MEM_HEAD_DIM]


def _memory_kv(mem2d, gain, w_kv, tile, later_weights):
    rows = mem2d.shape[0]
    n_steps = rows // tile
    cast_specs = _cast_specs(later_weights, n_steps, lambda i: i)
    return pl.pallas_call(
        _memory_kv_kernel,
        grid=(n_steps,),
        in_specs=[
            pl.BlockSpec((tile, D_MODEL), lambda i: (i, 0)),
            pl.BlockSpec((1, D_MODEL), lambda i: (0, 0)),
            _const_spec(w_kv.shape),
        ] + cast_specs,
        out_specs=[pl.BlockSpec((tile * MEM_HEADS, MEM_HEAD_DIM), lambda i: (i, 0))] * 2
        + [pl.BlockSpec((tile, D_MEM), lambda i: (i, 0))] * 2 + cast_specs,
        out_shape=[jax.ShapeDtypeStruct((rows * MEM_HEADS, MEM_HEAD_DIM), F32)] * 2
        + [jax.ShapeDtypeStruct((rows, D_MEM), BF16)] * 2
        + [jax.ShapeDtypeStruct(w.shape, BF16) for w in later_weights],
        scratch_shapes=[pltpu.VMEM(w_kv.shape, BF16)],
        compiler_params=pltpu.CompilerParams(
            dimension_semantics=("arbitrary",), vmem_limit_bytes=VMEM_LIMIT),
        name="memory_kv",
    )(mem2d, gain, w_kv, *later_weights)


def _prompt_mixer_kernel(x_ref, kb_ref, vb_ref, pre_g_ref, post_g_ref, conv_w_ref, conv_b_ref,
                         ln_g_ref, ln_b_ref, lb_logits_ref, hg_g_ref, w_in_ref, w_conv_out_ref,
                         w_hg_out_ref, w_mem_out_ref, w_out_ref, *rest):
    n_cast = (len(rest) - 6) // 2
    cast_src = rest[:n_cast]
    y_ref, new_conv_ref, new_hg_ref = rest[n_cast:n_cast + 3]
    cast_dst = rest[n_cast + 3:2 * n_cast + 3]
    shift_ref, tail_ref, state_t_ref = rest[2 * n_cast + 3:]
    _cast_blocks(cast_src, cast_dst)
    t = pl.program_id(1)
    last_t = pl.num_programs(1) - 1
    tl = PROMPT_TILE
    n_tiles = x_ref.shape[0] // tl

    @pl.when(t == 0)
    def _():
        tail_ref[...] = jnp.zeros_like(tail_ref)
        state_t_ref[...] = jnp.zeros_like(state_t_ref)

    lb = _lower_bound(lb_logits_ref[...])
    half = tl // 2
    row = lax.broadcasted_iota(jnp.int32, (tl, tl), 0)
    col = lax.broadcasted_iota(jnp.int32, (tl, tl), 1)
    span = row ^ col
    near = (span < 2 * CHUNK) & (col <= row)
    mid = (span < 4 * CHUNK) & (col < row)

    def mix_rows():
        n_rows = n_tiles * tl
        shift = shift_ref
        x = x_ref[...]
        hb = _rms(x, pre_g_ref[...]).astype(BF16)

        def proj(c0, c1):
            return _dot(hb, w_in_ref[:, c0:c1])

        u = proj(COL_CA, COL_CB) * _sigmoid(proj(COL_CB, COL_HQ))
        shift[0:CONV_ROWS, :] = tail_ref[...]
        shift[CONV_ROWS:CONV_ROWS + n_rows, :] = u
        tail_ref[...] = u[n_rows - CONV_ROWS:]

        lead = CONV_ROWS - CONV_PREFIX
        conv = None
        for r in range(SUBLANES):
            rows_r = n_rows if r == 0 else n_rows + SUBLANES
            v_r = None
            for a in range((CONV_WIDTH + lead) // SUBLANES + 1):
                j = a * SUBLANES + r - lead
                if 0 <= j < CONV_WIDTH:
                    term = shift[a * SUBLANES:a * SUBLANES + rows_r, :] * conv_w_ref[j:j + 1, :]
                    v_r = term if v_r is None else v_r + term
            if r:
                v_r = pltpu.roll(v_r, rows_r - r, axis=0)[:n_rows]
            conv = v_r if conv is None else conv + v_r
        c = _conv_ln_silu(conv, conv_b_ref[...], ln_g_ref[...], ln_b_ref[...]).astype(BF16)

        hq = proj(COL_HQ, COL_HF)
        hf = proj(COL_HF, COL_HI)
        levels = []
        for i in range(n_tiles):
            lv = _hgrn_tile_levels(hq[i * tl:(i + 1) * tl], hf[i * tl:(i + 1) * tl], lb)
            levels.append(_TileLevels(*[a.astype(BF16) for a in lv[:-1]], lv.decay))
        vb16 = proj(COL_HI, COL_HGATE).astype(BF16)
        hgate = _silu(proj(COL_HGATE, COL_MQ))
        mq = proj(COL_MQ, COL_GATES).astype(BF16)
        gates = [_branch_gate(hb, w_in_ref, g) for g in range(3)]

        head = lambda h: slice(h * MEM_HEAD_DIM, (h + 1) * MEM_HEAD_DIM)
        s = jnp.concatenate([_dot_nt(mq[:, head(h)], kb_ref[:, head(h)]) for h in range(MEM_HEADS)],
                            axis=0)
        t = s * (MEM_HEAD_DIM ** -0.5 * LOG2E)
        e = jnp.exp2(t - jnp.max(t, axis=-1, keepdims=True))
        inv = 1.0 / jnp.sum(e, axis=-1, keepdims=True)
        e = e.astype(BF16)
        om = [_dot(e[h * n_rows:(h + 1) * n_rows], vb_ref[:, head(h)]) * inv[h * n_rows:(h + 1) * n_rows]
              for h in range(MEM_HEADS)]

        o_tiles = []
        for i, lvb in enumerate(levels):
            o_heads = []
            for h in range(HG_HEADS):
                hs = slice(h * HG_DK, (h + 1) * HG_DK)
                vh = vb16[i * tl:(i + 1) * tl, hs]
                s_near = _dot_nt(lvb.q_mid[:, hs], lvb.k_mid[:, hs])
                s_mid = _dot_nt(lvb.q_64[:, hs], lvb.k_64[:, hs])
                s_far = _dot_nt(lvb.q_128[:, hs], lvb.k_128[:, hs])
                base = jnp.where(near, s_near, jnp.where(mid, s_mid, 0.0))
                scores = jnp.concatenate(
                    [base[:half], jnp.concatenate([s_far, base[half:, half:]], axis=1)], axis=0)
                st = state_t_ref[h]
                o_heads.append(_dot(scores.astype(BF16), vh)
                               + _dot_nt(lvb.q_tile[:, hs], st.astype(BF16)))
                state_t_ref[h] = st * lvb.decay[:, hs] + _dot_tn(vh, lvb.k_tile[:, hs])
            o_tiles.append(jnp.concatenate(o_heads, axis=-1))
        o = _head_rms(jnp.concatenate(o_tiles, axis=0), hg_g_ref[...]) * hgate

        p_conv = _dot(c, w_conv_out_ref[...])
        p_mem = _dot(jnp.concatenate(om, axis=-1).astype(BF16), w_mem_out_ref[...])
        p_hg = _dot(o.astype(BF16), w_hg_out_ref[...])
        y_ref[...] = _merge_out(lambda g: gates[g], x, (p_conv, p_hg, p_mem), w_out_ref, post_g_ref[...])

    mix_rows()

    @pl.when(t == last_t)
    def _():
        new_conv_ref[...] = tail_ref[CONV_ROWS - CONV_PREFIX:CONV_ROWS, :]
        for h in range(HG_HEADS):
            new_hg_ref[h] = state_t_ref[h].T


def _const_spec(shape):
    zeros = (0,) * len(shape)
    return pl.BlockSpec(shape, lambda *_: zeros, pipeline_mode=pl.Buffered(1))


def _mixer_weight_specs(w_in, w_conv_out, w_hg_out, w_mem_out, w_out):
    return [_const_spec(w.shape) for w in (w_in, w_conv_out, w_hg_out, w_mem_out, w_out)]


def _vector_specs():
    return [
        _const_spec((1, D_MODEL)), _const_spec((1, D_MODEL)),
        _const_spec((CONV_WIDTH, D_CONV)), _const_spec((1, D_CONV)),
        _const_spec((1, D_CONV)), _const_spec((1, D_CONV)),
        _const_spec((2, D_HGRN)), _const_spec((1, D_HGRN)),
    ]


def _prompt_mixer(x, mk, mv, vectors, weights, later_weights):
    batch, seq, _ = x.shape
    n_mem = mk.shape[1]
    tl = PROMPT_TILE * PROMPT_STEP_TILES
    n_t = seq // tl
    cast_specs = _cast_specs(later_weights, batch * n_t, lambda b, t: b * n_t + t)
    return pl.pallas_call(
        _prompt_mixer_kernel,
        grid=(batch, n_t),
        in_specs=[
            pl.BlockSpec((None, tl, D_MODEL), lambda b, t: (b, t, 0)),
            pl.BlockSpec((None, n_mem, D_MEM), lambda b, t: (b, 0, 0)),
            pl.BlockSpec((None, n_mem, D_MEM), lambda b, t: (b, 0, 0)),
        ] + _vector_specs() + _mixer_weight_specs(*weights) + cast_specs,
        out_specs=[
            pl.BlockSpec((None, tl, D_MODEL), lambda b, t: (b, t, 0)),
            pl.BlockSpec((None, CONV_PREFIX, D_CONV), lambda b, t: (b, 0, 0)),
            pl.BlockSpec((None, HG_HEADS, HG_DK, HG_DV), lambda b, t: (b, 0, 0, 0)),
        ] + cast_specs,
        out_shape=[
            jax.ShapeDtypeStruct((batch, seq, D_MODEL), F32),
            jax.ShapeDtypeStruct((batch, CONV_PREFIX, D_CONV), F32),
            jax.ShapeDtypeStruct((batch, HG_HEADS, HG_DK, HG_DV), F32),
        ] + [jax.ShapeDtypeStruct(w.shape, BF16) for w in later_weights],
        scratch_shapes=[
            pltpu.VMEM((tl + CONV_ROWS, D_CONV), F32),
            pltpu.VMEM((CONV_ROWS, D_CONV), F32),
            pltpu.VMEM((HG_HEADS, HG_DV, HG_DK), F32),
        ],
        compiler_params=pltpu.CompilerParams(
            dimension_semantics=("arbitrary", "arbitrary"), vmem_limit_bytes=VMEM_LIMIT),
        name="prompt_mixer",
    )(x, mk, mv, *vectors, *weights, *later_weights)


def _store_lane_chunks(ref, x):
    for i in range(ref.shape[0]):
        ref[i] = x[:, i * LANES:(i + 1) * LANES]


def _load_lane_chunks(ref, rows=slice(None)):
    return jnp.concatenate([ref[i, rows, :] for i in range(ref.shape[0])], axis=1)


def _lane_chunk_spec(tile, width):
    return pl.BlockSpec((width // LANES, tile, LANES), lambda i: (0, i, 0))


def _sample_in_kernel(x_ref, pre_g_ref, lb_logits_ref, w_in_ref,
                      u_ref, qd_ref, ki_ref, ke_ref, v_ref, blast_ref, q4_ref, hgate_ref, gates_ref):
    n_seq = x_ref.shape[0] // SUBLANES
    hb = _rms(x_ref[...], pre_g_ref[...]).astype(BF16)
    _store_lane_chunks(
        u_ref, _dot(hb, w_in_ref[:, COL_CA:COL_CB]) * _sigmoid(_dot(hb, w_in_ref[:, COL_CB:COL_HQ])))
    lb = _lower_bound(lb_logits_ref[...])
    q_dec, k_inv, k_end, b_last = _hgrn_gates(
        _dot(hb, w_in_ref[:, COL_HQ:COL_HF]), _dot(hb, w_in_ref[:, COL_HF:COL_HI]), lb, SUBLANES)
    qd_ref[...] = q_dec
    ki_ref[...] = k_inv
    ke_ref[...] = k_end
    blast_ref[...] = b_last
    v_ref[...] = _dot(hb, w_in_ref[:, COL_HI:COL_HGATE])
    hgate_ref[...] = _silu(_dot(hb, w_in_ref[:, COL_HGATE:COL_MQ]))
    mq = _dot(hb, w_in_ref[:, COL_MQ:COL_GATES])
    for h in range(MEM_HEADS):
        q4_ref[:, h * SUBLANES:(h + 1) * SUBLANES, :] = (
            mq[:, h * MEM_HEAD_DIM:(h + 1) * MEM_HEAD_DIM].reshape(n_seq, SUBLANES, MEM_HEAD_DIM))
    for i in range(3):
        gates_ref[:, i * D_MODEL:(i + 1) * D_MODEL] = _branch_gate(hb, w_in_ref, i)


def _sample_seq_kernel(u_ref, qd_ref, ki_ref, ke_ref, v_ref, blast_ref, q4_ref, sc_ref, sh_ref,
                       k2_ref, v2_ref, conv_w_ref,
                       c_ref, o_ref, om_ref, new_conv_ref, new_hg_ref):
    n_seq, dec = sc_ref.shape[1], SUBLANES
    rows = n_seq * dec

    q_rows = MEM_HEADS * dec
    s = jnp.concatenate([_dot_nt(q4_ref[g].astype(BF16), k2_ref[g].astype(BF16)) for g in range(n_seq)],
                        axis=0)
    assert dec & (dec - 1) == 0 and MEM_HEADS & (MEM_HEADS - 1) == 0
    q_head = (lax.broadcasted_iota(jnp.int32, s.shape, 0) >> (dec.bit_length() - 1)) & (MEM_HEADS - 1)
    k_head = lax.broadcasted_iota(jnp.int32, s.shape, 1) & (MEM_HEADS - 1)
    p = _attention_softmax(s, q_head == k_head).astype(BF16)
    om = []
    for g in range(n_seq):
        om_g = _dot(p[g * q_rows:(g + 1) * q_rows], v2_ref[g].astype(BF16))
        om.extend(om_g[h * dec:(h + 1) * dec] for h in range(MEM_HEADS))

    u_slabs = [_load_lane_chunks(u_ref, pl.ds(step, n_seq, stride=dec)) for step in range(dec)]
    conv = [None] * dec
    for s in range(CONV_PREFIX + dec):
        slab = sc_ref[s] if s < CONV_PREFIX else u_slabs[s - CONV_PREFIX]
        if s >= dec:
            new_conv_ref[s - dec] = slab
        for step in range(max(0, s - CONV_PREFIX), min(dec, s + 1)):
            term = slab * conv_w_ref[s - step:s - step + 1, :]
            conv[step] = term if conv[step] is None else conv[step] + term

    row = lax.broadcasted_iota(jnp.int32, (rows, rows), 0)
    col = lax.broadcasted_iota(jnp.int32, (rows, rows), 1)
    causal = ((row ^ col) < dec) & (col <= row)
    decay = jnp.exp(blast_ref[...])
    for h in range(HG_HEADS):
        hs = slice(h * HG_DK, (h + 1) * HG_DK)
        scores = jnp.where(causal, _dot_nt(qd_ref[:, hs].astype(BF16), ki_ref[:, hs].astype(BF16)), 0.0)
        o_intra = _dot(scores.astype(BF16), v_ref[:, hs].astype(BF16))
        for g in range(n_seq):
            rs = slice(g * dec, (g + 1) * dec)
            s0 = sh_ref[g, h]
            o_ref[rs, hs] = o_intra[rs] + _dot(qd_ref[rs, hs].astype(BF16), s0.astype(BF16))
            decay_col = jnp.broadcast_to(decay[g:g + 1, hs], (HG_DV, HG_DK)).T
            new_hg_ref[g, h] = decay_col * s0 + _dot_tn(ke_ref[rs, hs].astype(BF16),
                                                        v_ref[rs, hs].astype(BF16))

    for step in range(dec):
        for i in range(c_ref.shape[0]):
            c_ref[i, pl.ds(step, n_seq, stride=dec), :] = conv[step][:, i * LANES:(i + 1) * LANES]
    for g in range(n_seq):
        om_ref[g * dec:(g + 1) * dec, :] = jnp.concatenate(om[g * MEM_HEADS:(g + 1) * MEM_HEADS], axis=1)


def _sample_out_kernel(x_ref, c_ref, o_ref, om_ref, hgate_ref, gates_ref, post_g_ref, conv_b_ref,
                       ln_g_ref, ln_b_ref, hg_g_ref, w_conv_out_ref, w_hg_out_ref, w_mem_out_ref,
                       w_out_ref, ffn_pre_g_ref, ffn_post_g_ref, w_gate_ref, w_up_ref, w_down_ref, y_ref):
    c = _conv_ln_silu(_load_lane_chunks(c_ref), conv_b_ref[...], ln_g_ref[...], ln_b_ref[...])
    p_conv = _dot(c.astype(BF16), w_conv_out_ref[...])
    o = _head_rms(o_ref[...], hg_g_ref[...]) * hgate_ref[...]
    p_hg = _dot(o.astype(BF16), w_hg_out_ref[...])
    p_mem = _dot(om_ref[...].astype(BF16), w_mem_out_ref[...])
    x1 = _merge_out(lambda i: gates_ref[:, i * D_MODEL:(i + 1) * D_MODEL], x_ref[...],
                    (p_conv, p_hg, p_mem), w_out_ref, post_g_ref[...])
    y_ref[...] = _ffn_block(x1, ffn_pre_g_ref[...], ffn_post_g_ref[...], w_gate_ref, w_up_ref, w_down_ref)


def _row_spec(tile, width):
    return pl.BlockSpec((tile, width), lambda i: (i, 0))


def _sample_in(x2d, pre_g, lb_logits, w_in):
    rows = x2d.shape[0]
    tile = SAMPLE_TILE
    seqs = tile // SUBLANES
    wide = [D_HGRN, D_HGRN, D_HGRN, D_HGRN]
    return pl.pallas_call(
        _sample_in_kernel,
        grid=(rows // tile,),
        in_specs=[_row_spec(tile, D_MODEL), _const_spec((1, D_MODEL)), _const_spec((2, D_HGRN)),
                  _const_spec(w_in.shape)],
        out_specs=[_lane_chunk_spec(tile, D_CONV)] + [_row_spec(tile, w) for w in wide] + [
            _row_spec(seqs, D_HGRN),
            pl.BlockSpec((seqs, MEM_HEADS * SUBLANES, MEM_HEAD_DIM), lambda i: (i, 0, 0)),
            _row_spec(tile, D_HGRN),
            _row_spec(tile, 3 * D_MODEL),
        ],
        out_shape=[jax.ShapeDtypeStruct((D_CONV // LANES, rows, LANES), F32)]
        + [jax.ShapeDtypeStruct((rows, w), F32) for w in wide] + [
            jax.ShapeDtypeStruct((rows // SUBLANES, D_HGRN), F32),
            jax.ShapeDtypeStruct((rows // SUBLANES, MEM_HEADS * SUBLANES, MEM_HEAD_DIM), F32),
            jax.ShapeDtypeStruct((rows, D_HGRN), F32),
            jax.ShapeDtypeStruct((rows, 3 * D_MODEL), F32),
        ],
        compiler_params=pltpu.CompilerParams(
            dimension_semantics=("parallel",), vmem_limit_bytes=VMEM_LIMIT),
        name="sample_in",
    )(x2d, pre_g, lb_logits, w_in)


def _sample_seq(u, qd, ki, ke, v, blast, q4, conv_rows, state_hgrn, k2, v2, conv_w):
    n = conv_rows.shape[1]
    g = SAMPLE_GROUP
    rows = g * SUBLANES
    kv_rows = k2.shape[1]
    seq_block = lambda *tail: pl.BlockSpec((g,) + tail, lambda i: (i,) + (0,) * len(tail))
    conv_block = pl.BlockSpec((CONV_PREFIX, g, D_CONV), lambda i: (0, i, 0))
    return pl.pallas_call(
        _sample_seq_kernel,
        grid=(n // g,),
        in_specs=[_lane_chunk_spec(rows, D_CONV)] + [_row_spec(rows, D_HGRN)] * 4 + [
            _row_spec(g, D_HGRN),
            seq_block(MEM_HEADS * SUBLANES, MEM_HEAD_DIM),
            conv_block,
            seq_block(HG_HEADS, HG_DK, HG_DV),
            seq_block(kv_rows, MEM_HEAD_DIM),
            seq_block(kv_rows, MEM_HEAD_DIM),
            _const_spec((CONV_WIDTH, D_CONV)),
        ],
        out_specs=[_lane_chunk_spec(rows, D_CONV), _row_spec(rows, D_HGRN), _row_spec(rows, D_MEM),
                   conv_block, seq_block(HG_HEADS, HG_DK, HG_DV)],
        out_shape=[
            jax.ShapeDtypeStruct((D_CONV // LANES, n * SUBLANES, LANES), F32),
            jax.ShapeDtypeStruct((n * SUBLANES, D_HGRN), F32),
            jax.ShapeDtypeStruct((n * SUBLANES, D_MEM), F32),
            jax.ShapeDtypeStruct((CONV_PREFIX, n, D_CONV), F32),
            jax.ShapeDtypeStruct((n, HG_HEADS, HG_DK, HG_DV), F32),
        ],
        compiler_params=pltpu.CompilerParams(
            dimension_semantics=("parallel",), vmem_limit_bytes=VMEM_LIMIT),
        name="sample_seq",
    )(u, qd, ki, ke, v, blast, q4, conv_rows, state_hgrn, k2, v2, conv_w)


def _sample_out(x2d, c, o, om, hgate, gates, vectors, weights, ffn_vectors, ffn_weights):
    rows = x2d.shape[0]
    tile = SAMPLE_TILE
    acts = (x2d, c, o, om, hgate, gates)
    consts = tuple(vectors) + tuple(weights) + tuple(ffn_vectors) + tuple(ffn_weights)
    return pl.pallas_call(
        _sample_out_kernel,
        grid=(rows // tile,),
        in_specs=[_lane_chunk_spec(tile, D_CONV) if a is c else _row_spec(tile, a.shape[1]) for a in acts]
        + [_const_spec(a.shape) for a in consts],
        out_specs=_row_spec(tile, D_MODEL),
        out_shape=jax.ShapeDtypeStruct((rows, D_MODEL), F32),
        compiler_params=pltpu.CompilerParams(
            dimension_semantics=("parallel",), vmem_limit_bytes=VMEM_LIMIT),
        name="sample_out",
    )(*acts, *consts)


def _ffn_kernel(x_ref, pre_g_ref, post_g_ref, w_gate_ref, w_up_ref, w_down_ref, y_ref):
    for r0 in range(0, x_ref.shape[0], FFN_BLOCK):
        rows = slice(r0, r0 + FFN_BLOCK)
        y_ref[rows, :] = _ffn_block(x_ref[rows, :], pre_g_ref[...], post_g_ref[...],
                                    w_gate_ref, w_up_ref, w_down_ref)


def _ffn(x2d, pre_g, post_g, w_gate, w_up, w_down):
    rows = x2d.shape[0]
    tile = min(FFN_TILE, rows)
    return pl.pallas_call(
        _ffn_kernel,
        grid=(rows // tile,),
        in_specs=[
            pl.BlockSpec((tile, D_MODEL), lambda i: (i, 0)),
            _const_spec((1, D_MODEL)), _const_spec((1, D_MODEL)),
            _const_spec(w_gate.shape), _const_spec(w_up.shape), _const_spec(w_down.shape),
        ],
        out_specs=pl.BlockSpec((tile, D_MODEL), lambda i: (i, 0)),
        out_shape=jax.ShapeDtypeStruct((rows, D_MODEL), F32),
        compiler_params=pltpu.CompilerParams(
            dimension_semantics=("parallel",), vmem_limit_bytes=VMEM_LIMIT),
        name="ffn",
    )(x2d, pre_g, post_g, w_gate, w_up, w_down)


def kernel(x_prompt, x_sample, mem_prompt, state_conv, state_hgrn, cache_mem_k, cache_mem_v, norm_pre_mix, norm_post_mix, norm_pre_ffn, norm_post_ffn, w_in, conv_w, conv_b, conv_ln_g, conv_ln_b, w_conv_out, hg_lb_logits, hg_norm_g, w_hg_out, mem_norm_g, w_mem_kv, w_mem_out, w_out, w_ffn_gate, w_ffn_up, w_ffn_down):
    depth = w_in.shape[0]
    assert depth == 1 and hg_lb_logits.shape[0] == 2, "single-layer step"
    batch, seq, _ = x_prompt.shape
    n_dec, dec, _ = x_sample.shape
    n_mem = mem_prompt.shape[1]
    assert seq % PROMPT_TILE == 0 and n_dec % SAMPLE_GROUP == 0 and dec == SUBLANES
    assert (n_dec * dec) % SAMPLE_TILE == 0 and (batch * n_mem) % MEMORY_TILE == 0

    vectors = (norm_pre_mix, norm_post_mix, conv_w[0], conv_b, conv_ln_g, conv_ln_b,
               hg_lb_logits, jnp.tile(hg_norm_g, (1, HG_HEADS)))

    mk, mv, kb, vb, *weights = _memory_kv(
        mem_prompt.reshape(batch * n_mem, D_MODEL), mem_norm_g, w_mem_kv[0], MEMORY_TILE,
        tuple(w[0] for w in (w_in, w_conv_out, w_hg_out, w_mem_out, w_out)))

    xp, conv_p, hg_p, *ffn_weights = _prompt_mixer(
        x_prompt, kb.reshape(batch, n_mem, D_MEM), vb.reshape(batch, n_mem, D_MEM), vectors, weights,
        (w_ffn_gate[0], w_ffn_up[0], w_ffn_down[0]))
    yp = _ffn(xp.reshape(batch * seq, D_MODEL), norm_pre_ffn, norm_post_ffn, *ffn_weights)

    xs2d = x_sample.reshape(n_dec * dec, D_MODEL)
    u, qd, ki, ke, v, blast, q4, hgate, gates = _sample_in(xs2d, norm_pre_mix, hg_lb_logits, weights[0])
    k2 = cache_mem_k[0].reshape(n_dec, n_mem * MEM_HEADS, MEM_HEAD_DIM)
    v2 = cache_mem_v[0].reshape(n_dec, n_mem * MEM_HEADS, MEM_HEAD_DIM)
    c, o, om, conv_rows, hg_s = _sample_seq(u, qd, ki, ke, v, blast, q4,
                                            jnp.transpose(state_conv[0], (1, 0, 2)), state_hgrn[0],
                                            k2, v2, conv_w[0])
    conv_s = jnp.transpose(conv_rows, (1, 0, 2))
    ys = _sample_out(xs2d, c, o, om, hgate, gates,
                     (norm_post_mix, conv_b, conv_ln_g, conv_ln_b, vectors[-1]), weights[1:],
                     (norm_pre_ffn, norm_post_ffn), ffn_weights)

    kv_shape = (1, batch, n_mem, MEM_HEADS, MEM_HEAD_DIM)
    return (yp.reshape(batch, seq, D_MODEL), ys.reshape(n_dec, dec, D_MODEL),
            conv_p[None], hg_p[None], mk.reshape(kv_shape), mv.reshape(kv_shape),
            conv_s[None], hg_s[None])
```

```python
from typing import NamedTuple

import jax
import jax.numpy as jnp
from jax import lax
from jax.experimental import pallas as pl
from jax.experimental.pallas import tpu as pltpu

D_MODEL = 1024
D_CONV = 512
CONV_WIDTH = 31
CONV_PREFIX = CONV_WIDTH - 1
HG_HEADS = 4
HG_DK = 128
HG_DV = 128
D_HGRN = HG_HEADS * HG_DK
MEM_HEADS = 4
MEM_HEAD_DIM = 128
D_MEM = MEM_HEADS * MEM_HEAD_DIM
CHUNK = 32
EPS = 1e-6
LOG2E = 1.4426950408889634

COL_CA = 0
COL_CB = COL_CA + D_CONV
COL_HQ = COL_CB + D_CONV
COL_HF = COL_HQ + D_HGRN
COL_HI = COL_HF + D_HGRN
COL_HGATE = COL_HI + D_HGRN
COL_MQ = COL_HGATE + D_HGRN
COL_GATES = COL_MQ + D_MEM

SUBLANES = 8
LANES = 128
VMEM_LIMIT = 56 * 1024 * 1024

PROMPT_TILE = 256
PROMPT_STEP_TILES = 2
SAMPLE_GROUP = 8
SAMPLE_TILE = 256
FFN_TILE = 512
MEMORY_TILE = 256
FFN_BLOCK = 256
CONV_ROWS = 32

BF16 = jnp.bfloat16
F32 = jnp.float32


def _dot(a, b):
    return jnp.dot(a, b, preferred_element_type=F32)


def _dot_nt(a, b):
    return lax.dot_general(a, b, (((1,), (1,)), ((), ())), preferred_element_type=F32)


def _dot_tn(a, b):
    return lax.dot_general(a, b, (((0,), (0,)), ((), ())), preferred_element_type=F32)


def _rms(x, gain):
    return x * lax.rsqrt(jnp.mean(x * x, axis=-1, keepdims=True) + EPS) * gain


def _sigmoid(x):
    return 1.0 / (1.0 + jnp.exp2(x * -LOG2E))


def _silu(x):
    return x * _sigmoid(x)


def _attention_softmax(scores, keep=None):
    t = scores * (MEM_HEAD_DIM ** -0.5 * LOG2E)
    if keep is not None:
        t = jnp.where(keep, t, -jnp.inf)
    e = jnp.exp2(t - jnp.max(t, axis=-1, keepdims=True))
    return e / jnp.sum(e, axis=-1, keepdims=True)


def _forget_terms(hf, lb):
    sig = _sigmoid(hf)
    return jnp.log(lb + (1.0 - lb) * sig), (1.0 - lb) * (1.0 - sig)


def _lower_bound(lb_logits):
    m = jnp.max(lb_logits, axis=0, keepdims=True)
    e = jnp.exp(lb_logits - m)
    return e[0:1] / jnp.sum(e, axis=0, keepdims=True)


def _segment_cumsum(x, seg):
    pos = lax.broadcasted_iota(jnp.int32, x.shape, 0) & (seg - 1)
    s = 1
    while s < seg:
        x = x + jnp.where(pos >= s, pltpu.roll(x, s, axis=0), 0.0)
        s *= 2
    return x


def _conv_ln_silu(c, conv_b, ln_g, ln_b):
    c = c + conv_b
    mu = jnp.mean(c, axis=-1, keepdims=True)
    d = c - mu
    var = jnp.mean(d * d, axis=-1, keepdims=True)
    return _silu(d * lax.rsqrt(var + EPS) * ln_g + ln_b)


def _hgrn_gates(hq, hf, lb, seg):
    q = _silu(hq)
    logf, k = _forget_terms(hf, lb)
    b = _segment_cumsum(logf, seg)
    rows = b.shape[0]
    b3 = b.reshape(rows // seg, seg, D_HGRN)
    b_last3 = b3[:, seg - 1:seg, :]
    rest = (b_last3 - b3).reshape(rows, D_HGRN)
    q_dec = q * jnp.exp(b)
    k_inv = k * jnp.exp(-b)
    k_end = k * jnp.exp(rest)
    return q_dec, k_inv, k_end, b_last3.reshape(rows // seg, D_HGRN)


class _TileLevels(NamedTuple):
    q_mid: jax.Array
    k_mid: jax.Array
    q_64: jax.Array
    k_64: jax.Array
    q_128: jax.Array
    k_128: jax.Array
    q_tile: jax.Array
    k_tile: jax.Array
    decay: jax.Array


def _hgrn_tile_levels(hq, hf, lb):
    tl = hq.shape[0]
    n_blocks = 4
    block = 2 * CHUNK
    assert tl == n_blocks * block
    q = _silu(hq)
    logf, k = _forget_terms(hf, lb)
    b = _segment_cumsum(logf, CHUNK)
    chunks = [b[n * CHUNK:(n + 1) * CHUNK] for n in range(2 * n_blocks)]
    totals = [c[CHUNK - 1:CHUNK] for c in chunks]
    e = jnp.concatenate([c - totals[n] if n % 2 == 0 else c for n, c in enumerate(chunks)], axis=0)
    q_mid = q * jnp.exp(e)
    k_mid = k * jnp.exp(-e)
    first = [totals[2 * j] for j in range(n_blocks)]
    second = [totals[2 * j + 1] for j in range(n_blocks)]
    both = [first[j] + second[j] for j in range(n_blocks)]

    def scaled(x, blocks, log_scales):
        return jnp.concatenate(
            [x[j * block:(j + 1) * block] * jnp.exp(s) for j, s in zip(blocks, log_scales)], axis=0)

    every = range(n_blocks)
    return _TileLevels(
        q_mid=q_mid, k_mid=k_mid,
        q_64=scaled(q_mid, every, first),
        k_64=scaled(k_mid, every, second),
        q_128=scaled(q_mid, (2, 3), (first[2], first[3] + both[2])),
        k_128=scaled(k_mid, (0, 1), (second[0] + both[1], second[1])),
        q_tile=scaled(q_mid, every, [first[j] + sum(both[:j], 0.0) for j in every]),
        k_tile=scaled(k_mid, every, [second[j] + sum(both[j + 1:], 0.0) for j in every]),
        decay=jnp.exp(sum(both[1:], both[0])))


def _head_rms(o, gain4):
    parts = []
    for h in range(HG_HEADS):
        oh = o[:, h * HG_DV:(h + 1) * HG_DV]
        parts.append(oh * lax.rsqrt(jnp.mean(oh * oh, axis=-1, keepdims=True) + EPS))
    return jnp.concatenate(parts, axis=-1) * gain4


def _branch_gate(hb, w_in_ref, i):
    c0 = COL_GATES + i * D_MODEL
    return _sigmoid(_dot(hb, w_in_ref[:, c0:c0 + D_MODEL]))


def _merge_out(gate, x, branches, w_out_ref, post_g):
    acc = None
    for i, p in enumerate(branches):
        acc = gate(i) * p if acc is None else acc + gate(i) * p
    m = _dot(acc.astype(BF16), w_out_ref[...])
    return x + _rms(m, post_g)


def _ffn_block(x, pre_g, post_g, w_gate_ref, w_up_ref, w_down_ref):
    hb = _rms(x, pre_g).astype(BF16)
    f = _silu(_dot(hb, w_gate_ref[...])) * _dot(hb, w_up_ref[...])
    d = _dot(f.astype(BF16), w_down_ref[...])
    return x + _rms(d, post_g)


def _cast_block_rows(rows, n_steps):
    bf16_sublanes = 2 * SUBLANES
    for n_blocks in range(min(n_steps, rows // bf16_sublanes), 0, -1):
        if rows % n_blocks == 0 and (rows // n_blocks) % bf16_sublanes == 0:
            return rows // n_blocks
    raise ValueError(f"no bf16-tileable split of {rows} rows")


def _cast_specs(weights, n_steps, step_of):
    specs = []
    for w in weights:
        rows_blk = _cast_block_rows(w.shape[0], n_steps)
        last = w.shape[0] // rows_blk - 1
        specs.append(pl.BlockSpec((rows_blk, w.shape[1]),
                                  lambda *idx, last=last: (jnp.minimum(step_of(*idx), last), 0)))
    return specs


def _cast_blocks(srcs, dsts):
    for src, dst in zip(srcs, dsts):
        dst[...] = src[...].astype(BF16)


def _memory_kv_kernel(mem_ref, g_ref, w_ref, *rest):
    n_cast = (len(rest) - 5) // 2
    k_rows_ref, v_rows_ref, k_ref, v_ref = rest[n_cast:n_cast + 4]
    wb_ref = rest[-1]
    _cast_blocks(rest[:n_cast], rest[n_cast + 4:-1])

    @pl.when(pl.program_id(0) == 0)
    def _():
        wb_ref[...] = w_ref[...].astype(BF16)

    m = _rms(mem_ref[...], g_ref[...]).astype(BF16)
    tokens = mem_ref.shape[0]
    for full, rows_ref, op_ref in ((_dot(m, wb_ref[:, :D_MEM]), k_rows_ref, k_ref),
                                   (_dot(m, wb_ref[:, D_MEM:]), v_rows_ref, v_ref)):
        op_ref[...] = full.astype(BF16)
        for h in range(MEM_HEADS):
            rows_ref[pl.ds(h, tokens, stride=MEM_HEADS), :] = full[:, h * MEM_HEAD_DIM:(h + 1) * MEM_HEAD_DIM]


def _memory_kv(mem2d, gain, w_kv, tile, later_weights):
    rows = mem2d.shape[0]
    n_steps = rows // tile
    cast_specs = _cast_specs(later_weights, n_steps, lambda i: i)
    return pl.pallas_call(
        _memory_kv_kernel,
        grid=(n_steps,),
        in_specs=[
            pl.BlockSpec((tile, D_MODEL), lambda i: (i, 0)),
            pl.BlockSpec((1, D_MODEL), lambda i: (0, 0)),
            _const_spec(w_kv.shape),
        ] + cast_specs,
        out_specs=[pl.BlockSpec((tile * MEM_HEADS, MEM_HEAD_DIM), lambda i: (i, 0))] * 2
        + [pl.BlockSpec((tile, D_MEM), lambda i: (i, 0))] * 2 + cast_specs,
        out_shape=[jax.ShapeDtypeStruct((rows * MEM_HEADS, MEM_HEAD_DIM), F32)] * 2
        + [jax.ShapeDtypeStruct((rows, D_MEM), BF16)] * 2
        + [jax.ShapeDtypeStruct(w.shape, BF16) for w in later_weights],
        scratch_shapes=[pltpu.VMEM(w_kv.shape, BF16)],
        compiler_params=pltpu.CompilerParams(
            dimension_semantics=("arbitrary",), vmem_limit_bytes=VMEM_LIMIT),
        name="memory_kv",
    )(mem2d, gain, w_kv, *later_weights)


def _prompt_mixer_kernel(x_ref, kb_ref, vb_ref, pre_g_ref, post_g_ref, conv_w_ref, conv_b_ref,
                         ln_g_ref, ln_b_ref, lb_logits_ref, hg_g_ref, w_in_ref, w_conv_out_ref,
                         w_hg_out_ref, w_mem_out_ref, w_out_ref, *rest):
    n_cast = (len(rest) - 6) // 2
    cast_src = rest[:n_cast]
    y_ref, new_conv_ref, new_hg_ref = rest[n_cast:n_cast + 3]
    cast_dst = rest[n_cast + 3:2 * n_cast + 3]
    shift_ref, tail_ref, state_t_ref = rest[2 * n_cast + 3:]
    _cast_blocks(cast_src, cast_dst)
    t = pl.program_id(1)
    last_t = pl.num_programs(1) - 1
    tl = PROMPT_TILE
    n_tiles = x_ref.shape[0] // tl

    @pl.when(t == 0)
    def _():
        tail_ref[...] = jnp.zeros_like(tail_ref)
        state_t_ref[...] = jnp.zeros_like(state_t_ref)

    lb = _lower_bound(lb_logits_ref[...])
    half = tl // 2
    row = lax.broadcasted_iota(jnp.int32, (tl, tl), 0)
    col = lax.broadcasted_iota(jnp.int32, (tl, tl), 1)
    span = row ^ col
    near = (span < 2 * CHUNK) & (col <= row)
    mid = (span < 4 * CHUNK) & (col < row)

    def mix_rows():
        n_rows = n_tiles * tl
        shift = shift_ref
        x = x_ref[...]
        hb = _rms(x, pre_g_ref[...]).astype(BF16)

        def proj(c0, c1):
            return _dot(hb, w_in_ref[:, c0:c1])

        u = proj(COL_CA, COL_CB) * _sigmoid(proj(COL_CB, COL_HQ))
        shift[0:CONV_ROWS, :] = tail_ref[...]
        shift[CONV_ROWS:CONV_ROWS + n_rows, :] = u
        tail_ref[...] = u[n_rows - CONV_ROWS:]

        lead = CONV_ROWS - CONV_PREFIX
        conv = None
        for r in range(SUBLANES):
            rows_r = n_rows if r == 0 else n_rows + SUBLANES
            v_r = None
            for a in range((CONV_WIDTH + lead) // SUBLANES + 1):
                j = a * SUBLANES + r - lead
                if 0 <= j < CONV_WIDTH:
                    term = shift[a * SUBLANES:a * SUBLANES + rows_r, :] * conv_w_ref[j:j + 1, :]
                    v_r = term if v_r is None else v_r + term
            if r:
                v_r = pltpu.roll(v_r, rows_r - r, axis=0)[:n_rows]
            conv = v_r if conv is None else conv + v_r
        c = _conv_ln_silu(conv, conv_b_ref[...], ln_g_ref[...], ln_b_ref[...]).astype(BF16)

        hq = proj(COL_HQ, COL_HF)
        hf = proj(COL_HF, COL_HI)
        levels = []
        for i in range(n_tiles):
            lv = _hgrn_tile_levels(hq[i * tl:(i + 1) * tl], hf[i * tl:(i + 1) * tl], lb)
            levels.append(_TileLevels(*[a.astype(BF16) for a in lv[:-1]], lv.decay))
        vb16 = proj(COL_HI, COL_HGATE).astype(BF16)
        hgate = _silu(proj(COL_HGATE, COL_MQ))
        mq = proj(COL_MQ, COL_GATES).astype(BF16)
        gates = [_branch_gate(hb, w_in_ref, g) for g in range(3)]

        head = lambda h: slice(h * MEM_HEAD_DIM, (h + 1) * MEM_HEAD_DIM)
        s = jnp.concatenate([_dot_nt(mq[:, head(h)], kb_ref[:, head(h)]) for h in range(MEM_HEADS)],
                            axis=0)
        t = s * (MEM_HEAD_DIM ** -0.5 * LOG2E)
        e = jnp.exp2(t - jnp.max(t, axis=-1, keepdims=True))
        inv = 1.0 / jnp.sum(e, axis=-1, keepdims=True)
        e = e.astype(BF16)
        om = [_dot(e[h * n_rows:(h + 1) * n_rows], vb_ref[:, head(h)]) * inv[h * n_rows:(h + 1) * n_rows]
              for h in range(MEM_HEADS)]

        o_tiles = []
        for i, lvb in enumerate(levels):
            o_heads = []
            for h in range(HG_HEADS):
                hs = slice(h * HG_DK, (h + 1) * HG_DK)
                vh = vb16[i * tl:(i + 1) * tl, hs]
                s_near = _dot_nt(lvb.q_mid[:, hs], lvb.k_mid[:, hs])
                s_mid = _dot_nt(lvb.q_64[:, hs], lvb.k_64[:, hs])
                s_far = _dot_nt(lvb.q_128[:, hs], lvb.k_128[:, hs])
                base = jnp.where(near, s_near, jnp.where(mid, s_mid, 0.0))
                scores = jnp.concatenate(
                    [base[:half], jnp.concatenate([s_far, base[half:, half:]], axis=1)], axis=0)
                st = state_t_ref[h]
                o_heads.append(_dot(scores.astype(BF16), vh)
                               + _dot_nt(lvb.q_tile[:, hs], st.astype(BF16)))
                state_t_ref[h] = st * lvb.decay[:, hs] + _dot_tn(vh, lvb.k_tile[:, hs])
            o_tiles.append(jnp.concatenate(o_heads, axis=-1))
        o = _head_rms(jnp.concatenate(o_tiles, axis=0), hg_g_ref[...]) * hgate

        p_conv = _dot(c, w_conv_out_ref[...])
        p_mem = _dot(jnp.concatenate(om, axis=-1).astype(BF16), w_mem_out_ref[...])
        p_hg = _dot(o.astype(BF16), w_hg_out_ref[...])
        y_ref[...] = _merge_out(lambda g: gates[g], x, (p_conv, p_hg, p_mem), w_out_ref, post_g_ref[...])

    mix_rows()

    @pl.when(t == last_t)
    def _():
        new_conv_ref[...] = tail_ref[CONV_ROWS - CONV_PREFIX:CONV_ROWS, :]
        for h in range(HG_HEADS):
            new_hg_ref[h] = state_t_ref[h].T


def _const_spec(shape):
    zeros = (0,) * len(shape)
    return pl.BlockSpec(shape, lambda *_: zeros, pipeline_mode=pl.Buffered(1))


def _mixer_weight_specs(w_in, w_conv_out, w_hg_out, w_mem_out, w_out):
    return [_const_spec(w.shape) for w in (w_in, w_conv_out, w_hg_out, w_mem_out, w_out)]


def _vector_specs():
    return [
        _const_spec((1, D_MODEL)), _const_spec((1, D_MODEL)),
        _const_spec((CONV_WIDTH, D_CONV)), _const_spec((1, D_CONV)),
        _const_spec((1, D_CONV)), _const_spec((1, D_CONV)),
        _const_spec((2, D_HGRN)), _const_spec((1, D_HGRN)),
    ]


def _prompt_mixer(x, mk, mv, vectors, weights, later_weights):
    batch, seq, _ = x.shape
    n_mem = mk.shape[1]
    tl = PROMPT_TILE * PROMPT_STEP_TILES
    n_t = seq // tl
    cast_specs = _cast_specs(later_weights, batch * n_t, lambda b, t: b * n_t + t)
    return pl.pallas_call(
        _prompt_mixer_kernel,
        grid=(batch, n_t),
        in_specs=[
            pl.BlockSpec((None, tl, D_MODEL), lambda b, t: (b, t, 0)),
            pl.BlockSpec((None, n_mem, D_MEM), lambda b, t: (b, 0, 0)),
            pl.BlockSpec((None, n_mem, D_MEM), lambda b, t: (b, 0, 0)),
        ] + _vector_specs() + _mixer_weight_specs(*weights) + cast_specs,
        out_specs=[
            pl.BlockSpec((None, tl, D_MODEL), lambda b, t: (b, t, 0)),
            pl.BlockSpec((None, CONV_PREFIX, D_CONV), lambda b, t: (b, 0, 0)),
            pl.BlockSpec((None, HG_HEADS, HG_DK, HG_DV), lambda b, t: (b, 0, 0, 0)),
        ] + cast_specs,
        out_shape=[
            jax.ShapeDtypeStruct((batch, seq, D_MODEL), F32),
            jax.ShapeDtypeStruct((batch, CONV_PREFIX, D_CONV), F32),
            jax.ShapeDtypeStruct((batch, HG_HEADS, HG_DK, HG_DV), F32),
        ] + [jax.ShapeDtypeStruct(w.shape, BF16) for w in later_weights],
        scratch_shapes=[
            pltpu.VMEM((tl + CONV_ROWS, D_CONV), F32),
            pltpu.VMEM((CONV_ROWS, D_CONV), F32),
            pltpu.VMEM((HG_HEADS, HG_DV, HG_DK), F32),
        ],
        compiler_params=pltpu.CompilerParams(
            dimension_semantics=("arbitrary", "arbitrary"), vmem_limit_bytes=VMEM_LIMIT),
        name="prompt_mixer",
    )(x, mk, mv, *vectors, *weights, *later_weights)


def _store_lane_chunks(ref, x):
    for i in range(ref.shape[0]):
        ref[i] = x[:, i * LANES:(i + 1) * LANES]


def _load_lane_chunks(ref, rows=slice(None)):
    return jnp.concatenate([ref[i, rows, :] for i in range(ref.shape[0])], axis=1)


def _lane_chunk_spec(tile, width):
    return pl.BlockSpec((width // LANES, tile, LANES), lambda i: (0, i, 0))


def _sample_in_kernel(x_ref, pre_g_ref, lb_logits_ref, w_in_ref,
                      u_ref, qd_ref, ki_ref, ke_ref, v_ref, blast_ref, q4_ref, hgate_ref, gates_ref):
    n_seq = x_ref.shape[0] // SUBLANES
    hb = _rms(x_ref[...], pre_g_ref[...]).astype(BF16)
    _store_lane_chunks(
        u_ref, _dot(hb, w_in_ref[:, COL_CA:COL_CB]) * _sigmoid(_dot(hb, w_in_ref[:, COL_CB:COL_HQ])))
    lb = _lower_bound(lb_logits_ref[...])
    q_dec, k_inv, k_end, b_last = _hgrn_gates(
        _dot(hb, w_in_ref[:, COL_HQ:COL_HF]), _dot(hb, w_in_ref[:, COL_HF:COL_HI]), lb, SUBLANES)
    qd_ref[...] = q_dec
    ki_ref[...] = k_inv
    ke_ref[...] = k_end
    blast_ref[...] = b_last
    v_ref[...] = _dot(hb, w_in_ref[:, COL_HI:COL_HGATE])
    hgate_ref[...] = _silu(_dot(hb, w_in_ref[:, COL_HGATE:COL_MQ]))
    mq = _dot(hb, w_in_ref[:, COL_MQ:COL_GATES])
    for h in range(MEM_HEADS):
        q4_ref[:, h * SUBLANES:(h + 1) * SUBLANES, :] = (
            mq[:, h * MEM_HEAD_DIM:(h + 1) * MEM_HEAD_DIM].reshape(n_seq, SUBLANES, MEM_HEAD_DIM))
    for i in range(3):
        gates_ref[:, i * D_MODEL:(i + 1) * D_MODEL] = _branch_gate(hb, w_in_ref, i)


def _sample_seq_kernel(u_ref, qd_ref, ki_ref, ke_ref, v_ref, blast_ref, q4_ref, sc_ref, sh_ref,
                       k2_ref, v2_ref, conv_w_ref,
                       c_ref, o_ref, om_ref, new_conv_ref, new_hg_ref):
    n_seq, dec = sc_ref.shape[1], SUBLANES
    rows = n_seq * dec

    q_rows = MEM_HEADS * dec
    s = jnp.concatenate([_dot_nt(q4_ref[g].astype(BF16), k2_ref[g].astype(BF16)) for g in range(n_seq)],
                        axis=0)
    assert dec & (dec - 1) == 0 and MEM_HEADS & (MEM_HEADS - 1) == 0
    q_head = (lax.broadcasted_iota(jnp.int32, s.shape, 0) >> (dec.bit_length() - 1)) & (MEM_HEADS - 1)
    k_head = lax.broadcasted_iota(jnp.int32, s.shape, 1) & (MEM_HEADS - 1)
    p = _attention_softmax(s, q_head == k_head).astype(BF16)
    om = []
    for g in range(n_seq):
        om_g = _dot(p[g * q_rows:(g + 1) * q_rows], v2_ref[g].astype(BF16))
        om.extend(om_g[h * dec:(h + 1) * dec] for h in range(MEM_HEADS))

    u_slabs = [_load_lane_chunks(u_ref, pl.ds(step, n_seq, stride=dec)) for step in range(dec)]
    conv = [None] * dec
    for s in range(CONV_PREFIX + dec):
        slab = sc_ref[s] if s < CONV_PREFIX else u_slabs[s - CONV_PREFIX]
        if s >= dec:
            new_conv_ref[s - dec] = slab
        for step in range(max(0, s - CONV_PREFIX), min(dec, s + 1)):
            term = slab * conv_w_ref[s - step:s - step + 1, :]
            conv[step] = term if conv[step] is None else conv[step] + term

    row = lax.broadcasted_iota(jnp.int32, (rows, rows), 0)
    col = lax.broadcasted_iota(jnp.int32, (rows, rows), 1)
    causal = ((row ^ col) < dec) & (col <= row)
    decay = jnp.exp(blast_ref[...])
    for h in range(HG_HEADS):
        hs = slice(h * HG_DK, (h + 1) * HG_DK)
        scores = jnp.where(causal, _dot_nt(qd_ref[:, hs].astype(BF16), ki_ref[:, hs].astype(BF16)), 0.0)
        o_intra = _dot(scores.astype(BF16), v_ref[:, hs].astype(BF16))
        for g in range(n_seq):
            rs = slice(g * dec, (g + 1) * dec)
            s0 = sh_ref[g, h]
            o_ref[rs, hs] = o_intra[rs] + _dot(qd_ref[rs, hs].astype(BF16), s0.astype(BF16))
            decay_col = jnp.broadcast_to(decay[g:g + 1, hs], (HG_DV, HG_DK)).T
            new_hg_ref[g, h] = decay_col * s0 + _dot_tn(ke_ref[rs, hs].astype(BF16),
                                                        v_ref[rs, hs].astype(BF16))

    for step in range(dec):
        for i in range(c_ref.shape[0]):
            c_ref[i, pl.ds(step, n_seq, stride=dec), :] = conv[step][:, i * LANES:(i + 1) * LANES]
    for g in range(n_seq):
        om_ref[g * dec:(g + 1) * dec, :] = jnp.concatenate(om[g * MEM_HEADS:(g + 1) * MEM_HEADS], axis=1)


def _sample_out_kernel(x_ref, c_ref, o_ref, om_ref, hgate_ref, gates_ref, post_g_ref, conv_b_ref,
                       ln_g_ref, ln_b_ref, hg_g_ref, w_conv_out_ref, w_hg_out_ref, w_mem_out_ref,
                       w_out_ref, ffn_pre_g_ref, ffn_post_g_ref, w_gate_hbm, w_up_hbm, w_down_hbm, y_ref,
                       w_gate_ref, w_up_ref, w_down_ref, ffn_sems):
    first_step = pl.program_id(0) == 0
    ffn_copies = [pltpu.make_async_copy(src, dst, ffn_sems.at[i]) for i, (src, dst) in enumerate(
        ((w_gate_hbm, w_gate_ref), (w_up_hbm, w_up_ref), (w_down_hbm, w_down_ref)))]

    @pl.when(first_step)
    def _():
        for copy in ffn_copies:
            copy.start()

    c = _conv_ln_silu(_load_lane_chunks(c_ref), conv_b_ref[...], ln_g_ref[...], ln_b_ref[...])
    p_conv = _dot(c.astype(BF16), w_conv_out_ref[...])
    o = _head_rms(o_ref[...], hg_g_ref[...]) * hgate_ref[...]
    p_hg = _dot(o.astype(BF16), w_hg_out_ref[...])
    p_mem = _dot(om_ref[...].astype(BF16), w_mem_out_ref[...])
    x1 = _merge_out(lambda i: gates_ref[:, i * D_MODEL:(i + 1) * D_MODEL], x_ref[...],
                    (p_conv, p_hg, p_mem), w_out_ref, post_g_ref[...])

    @pl.when(first_step)
    def _():
        for copy in ffn_copies:
            copy.wait()

    y_ref[...] = _ffn_block(x1, ffn_pre_g_ref[...], ffn_post_g_ref[...], w_gate_ref, w_up_ref, w_down_ref)


def _row_spec(tile, width):
    return pl.BlockSpec((tile, width), lambda i: (i, 0))


def _sample_in(x2d, pre_g, lb_logits, w_in):
    rows = x2d.shape[0]
    tile = SAMPLE_TILE
    seqs = tile // SUBLANES
    wide = [D_HGRN, D_HGRN, D_HGRN, D_HGRN]
    return pl.pallas_call(
        _sample_in_kernel,
        grid=(rows // tile,),
        in_specs=[_row_spec(tile, D_MODEL), _const_spec((1, D_MODEL)), _const_spec((2, D_HGRN)),
                  _const_spec(w_in.shape)],
        out_specs=[_lane_chunk_spec(tile, D_CONV)] + [_row_spec(tile, w) for w in wide] + [
            _row_spec(seqs, D_HGRN),
            pl.BlockSpec((seqs, MEM_HEADS * SUBLANES, MEM_HEAD_DIM), lambda i: (i, 0, 0)),
            _row_spec(tile, D_HGRN),
            _row_spec(tile, 3 * D_MODEL),
        ],
        out_shape=[jax.ShapeDtypeStruct((D_CONV // LANES, rows, LANES), F32)]
        + [jax.ShapeDtypeStruct((rows, w), F32) for w in wide] + [
            jax.ShapeDtypeStruct((rows // SUBLANES, D_HGRN), F32),
            jax.ShapeDtypeStruct((rows // SUBLANES, MEM_HEADS * SUBLANES, MEM_HEAD_DIM), F32),
            jax.ShapeDtypeStruct((rows, D_HGRN), F32),
            jax.ShapeDtypeStruct((rows, 3 * D_MODEL), F32),
        ],
        compiler_params=pltpu.CompilerParams(
            dimension_semantics=("parallel",), vmem_limit_bytes=VMEM_LIMIT),
        name="sample_in",
    )(x2d, pre_g, lb_logits, w_in)


def _sample_seq(u, qd, ki, ke, v, blast, q4, conv_rows, state_hgrn, k2, v2, conv_w):
    n = conv_rows.shape[1]
    g = SAMPLE_GROUP
    rows = g * SUBLANES
    kv_rows = k2.shape[1]
    seq_block = lambda *tail: pl.BlockSpec((g,) + tail, lambda i: (i,) + (0,) * len(tail))
    conv_block = pl.BlockSpec((CONV_PREFIX, g, D_CONV), lambda i: (0, i, 0))
    return pl.pallas_call(
        _sample_seq_kernel,
        grid=(n // g,),
        in_specs=[_lane_chunk_spec(rows, D_CONV)] + [_row_spec(rows, D_HGRN)] * 4 + [
            _row_spec(g, D_HGRN),
            seq_block(MEM_HEADS * SUBLANES, MEM_HEAD_DIM),
            conv_block,
            seq_block(HG_HEADS, HG_DK, HG_DV),
            seq_block(kv_rows, MEM_HEAD_DIM),
            seq_block(kv_rows, MEM_HEAD_DIM),
            _const_spec((CONV_WIDTH, D_CONV)),
        ],
        out_specs=[_lane_chunk_spec(rows, D_CONV), _row_spec(rows, D_HGRN), _row_spec(rows, D_MEM),
                   conv_block, seq_block(HG_HEADS, HG_DK, HG_DV)],
        out_shape=[
            jax.ShapeDtypeStruct((D_CONV // LANES, n * SUBLANES, LANES), F32),
            jax.ShapeDtypeStruct((n * SUBLANES, D_HGRN), F32),
            jax.ShapeDtypeStruct((n * SUBLANES, D_MEM), F32),
            jax.ShapeDtypeStruct((CONV_PREFIX, n, D_CONV), F32),
            jax.ShapeDtypeStruct((n, HG_HEADS, HG_DK, HG_DV), F32),
        ],
        compiler_params=pltpu.CompilerParams(
            dimension_semantics=("parallel",), vmem_limit_bytes=VMEM_LIMIT),
        name="sample_seq",
    )(u, qd, ki, ke, v, blast, q4, conv_rows, state_hgrn, k2, v2, conv_w)


def _sample_out(x2d, c, o, om, hgate, gates, vectors, weights, ffn_vectors, ffn_weights):
    rows = x2d.shape[0]
    tile = SAMPLE_TILE
    acts = (x2d, c, o, om, hgate, gates)
    consts = tuple(vectors) + tuple(weights) + tuple(ffn_vectors)
    return pl.pallas_call(
        _sample_out_kernel,
        grid=(rows // tile,),
        in_specs=[_lane_chunk_spec(tile, D_CONV) if a is c else _row_spec(tile, a.shape[1]) for a in acts]
        + [_const_spec(a.shape) for a in consts]
        + [pl.BlockSpec(memory_space=pl.ANY) for _ in ffn_weights],
        out_specs=_row_spec(tile, D_MODEL),
        out_shape=jax.ShapeDtypeStruct((rows, D_MODEL), F32),
        scratch_shapes=[pltpu.VMEM(w.shape, BF16) for w in ffn_weights]
        + [pltpu.SemaphoreType.DMA((len(ffn_weights),))],
        compiler_params=pltpu.CompilerParams(
            dimension_semantics=("arbitrary",), vmem_limit_bytes=VMEM_LIMIT),
        name="sample_out",
    )(*acts, *consts, *ffn_weights)


def _ffn_kernel(x_ref, pre_g_ref, post_g_ref, w_gate_ref, w_up_ref, w_down_ref, y_ref):
    for r0 in range(0, x_ref.shape[0], FFN_BLOCK):
        rows = slice(r0, r0 + FFN_BLOCK)
        y_ref[rows, :] = _ffn_block(x_ref[rows, :], pre_g_ref[...], post_g_ref[...],
                                    w_gate_ref, w_up_ref, w_down_ref)


def _ffn(x2d, pre_g, post_g, w_gate, w_up, w_down):
    rows = x2d.shape[0]
    tile = min(FFN_TILE, rows)
    return pl.pallas_call(
        _ffn_kernel,
        grid=(rows // tile,),
        in_specs=[
            pl.BlockSpec((tile, D_MODEL), lambda i: (i, 0)),
            _const_spec((1, D_MODEL)), _const_spec((1, D_MODEL)),
            _const_spec(w_gate.shape), _const_spec(w_up.shape), _const_spec(w_down.shape),
        ],
        out_specs=pl.BlockSpec((tile, D_MODEL), lambda i: (i, 0)),
        out_shape=jax.ShapeDtypeStruct((rows, D_MODEL), F32),
        compiler_params=pltpu.CompilerParams(
            dimension_semantics=("parallel",), vmem_limit_bytes=VMEM_LIMIT),
        name="ffn",
    )(x2d, pre_g, post_g, w_gate, w_up, w_down)


def kernel(x_prompt, x_sample, mem_prompt, state_conv, state_hgrn, cache_mem_k, cache_mem_v, norm_pre_mix, norm_post_mix, norm_pre_ffn, norm_post_ffn, w_in, conv_w, conv_b, conv_ln_g, conv_ln_b, w_conv_out, hg_lb_logits, hg_norm_g, w_hg_out, mem_norm_g, w_mem_kv, w_mem_out, w_out, w_ffn_gate, w_ffn_up, w_ffn_down):
    depth = w_in.shape[0]
    assert depth == 1 and hg_lb_logits.shape[0] == 2, "single-layer step"
    batch, seq, _ = x_prompt.shape
    n_dec, dec, _ = x_sample.shape
    n_mem = mem_prompt.shape[1]
    assert seq % PROMPT_TILE == 0 and n_dec % SAMPLE_GROUP == 0 and dec == SUBLANES
    assert (n_dec * dec) % SAMPLE_TILE == 0 and (batch * n_mem) % MEMORY_TILE == 0

    vectors = (norm_pre_mix, norm_post_mix, conv_w[0], conv_b, conv_ln_g, conv_ln_b,
               hg_lb_logits, jnp.tile(hg_norm_g, (1, HG_HEADS)))

    mk, mv, kb, vb, *weights = _memory_kv(
        mem_prompt.reshape(batch * n_mem, D_MODEL), mem_norm_g, w_mem_kv[0], MEMORY_TILE,
        tuple(w[0] for w in (w_in, w_conv_out, w_hg_out, w_mem_out, w_out)))

    xp, conv_p, hg_p, *ffn_weights = _prompt_mixer(
        x_prompt, kb.reshape(batch, n_mem, D_MEM), vb.reshape(batch, n_mem, D_MEM), vectors, weights,
        (w_ffn_gate[0], w_ffn_up[0], w_ffn_down[0]))
    yp = _ffn(xp.reshape(batch * seq, D_MODEL), norm_pre_ffn, norm_post_ffn, *ffn_weights)

    xs2d = x_sample.reshape(n_dec * dec, D_MODEL)
    u, qd, ki, ke, v, blast, q4, hgate, gates = _sample_in(xs2d, norm_pre_mix, hg_lb_logits, weights[0])
    k2 = cache_mem_k[0].reshape(n_dec, n_mem * MEM_HEADS, MEM_HEAD_DIM)
    v2 = cache_mem_v[0].reshape(n_dec, n_mem * MEM_HEADS, MEM_HEAD_DIM)
    c, o, om, conv_rows, hg_s = _sample_seq(u, qd, ki, ke, v, blast, q4,
                                            jnp.transpose(state_conv[0], (1, 0, 2)), state_hgrn[0],
                                            k2, v2, conv_w[0])
    conv_s = jnp.transpose(conv_rows, (1, 0, 2))
    ys = _sample_out(xs2d, c, o, om, hgate, gates,
                     (norm_post_mix, conv_b, conv_ln_g, conv_ln_b, vectors[-1]), weights[1:],
                     (norm_pre_ffn, norm_post_ffn), ffn_weights)

    kv_shape = (1, batch, n_mem, MEM_HEADS, MEM_HEAD_DIM)
    return (yp.reshape(batch, seq, D_MODEL), ys.reshape(n_dec, dec, D_MODEL),
            conv_p[None], hg_p[None], mk.reshape(kv_shape), mv.reshape(kv_shape),
            conv_s[None], hg_s[None])
```

```python
from typing import NamedTuple

import jax
import jax.numpy as jnp
from jax import lax
from jax.experimental import pallas as pl
from jax.experimental.pallas import tpu as pltpu

D_MODEL = 1024
D_CONV = 512
CONV_WIDTH = 31
CONV_PREFIX = CONV_WIDTH - 1
HG_HEADS = 4
HG_DK = 128
HG_DV = 128
D_HGRN = HG_HEADS * HG_DK
MEM_HEADS = 4
MEM_HEAD_DIM = 128
D_MEM = MEM_HEADS * MEM_HEAD_DIM
CHUNK = 32
EPS = 1e-6
LOG2E = 1.4426950408889634

COL_CA = 0
COL_CB = COL_CA + D_CONV
COL_HQ = COL_CB + D_CONV
COL_HF = COL_HQ + D_HGRN
COL_HI = COL_HF + D_HGRN
COL_HGATE = COL_HI + D_HGRN
COL_MQ = COL_HGATE + D_HGRN
COL_GATES = COL_MQ + D_MEM

SUBLANES = 8
LANES = 128
VMEM_LIMIT = 56 * 1024 * 1024

PROMPT_TILE = 256
PROMPT_STEP_TILES = 2
SAMPLE_GROUP = 8
SAMPLE_TILE = 256
FFN_TILE = 512
MEMORY_TILE = 256
FFN_BLOCK = 256
FFN_CHUNK = 768
CONV_ROWS = 32

BF16 = jnp.bfloat16
F32 = jnp.float32


def _dot(a, b):
    return jnp.dot(a, b, preferred_element_type=F32)


def _dot_nt(a, b):
    return lax.dot_general(a, b, (((1,), (1,)), ((), ())), preferred_element_type=F32)


def _dot_tn(a, b):
    return lax.dot_general(a, b, (((0,), (0,)), ((), ())), preferred_element_type=F32)


def _rms(x, gain):
    return x * lax.rsqrt(jnp.mean(x * x, axis=-1, keepdims=True) + EPS) * gain


def _sigmoid(x):
    return 1.0 / (1.0 + jnp.exp2(x * -LOG2E))


def _silu(x):
    return x * _sigmoid(x)


def _attention_softmax(scores, keep=None):
    t = scores * (MEM_HEAD_DIM ** -0.5 * LOG2E)
    if keep is not None:
        t = jnp.where(keep, t, -jnp.inf)
    e = jnp.exp2(t - jnp.max(t, axis=-1, keepdims=True))
    return e / jnp.sum(e, axis=-1, keepdims=True)


def _forget_terms(hf, lb):
    sig = _sigmoid(hf)
    return jnp.log(lb + (1.0 - lb) * sig), (1.0 - lb) * (1.0 - sig)


def _lower_bound(lb_logits):
    m = jnp.max(lb_logits, axis=0, keepdims=True)
    e = jnp.exp(lb_logits - m)
    return e[0:1] / jnp.sum(e, axis=0, keepdims=True)


def _segment_cumsum(x, seg):
    pos = lax.broadcasted_iota(jnp.int32, x.shape, 0) & (seg - 1)
    s = 1
    while s < seg:
        x = x + jnp.where(pos >= s, pltpu.roll(x, s, axis=0), 0.0)
        s *= 2
    return x


def _conv_ln_silu(c, conv_b, ln_g, ln_b):
    c = c + conv_b
    mu = jnp.mean(c, axis=-1, keepdims=True)
    d = c - mu
    var = jnp.mean(d * d, axis=-1, keepdims=True)
    return _silu(d * lax.rsqrt(var + EPS) * ln_g + ln_b)


def _hgrn_gates(hq, hf, lb, seg):
    q = _silu(hq)
    logf, k = _forget_terms(hf, lb)
    b = _segment_cumsum(logf, seg)
    rows = b.shape[0]
    b3 = b.reshape(rows // seg, seg, D_HGRN)
    b_last3 = b3[:, seg - 1:seg, :]
    rest = (b_last3 - b3).reshape(rows, D_HGRN)
    q_dec = q * jnp.exp(b)
    k_inv = k * jnp.exp(-b)
    k_end = k * jnp.exp(rest)
    return q_dec, k_inv, k_end, b_last3.reshape(rows // seg, D_HGRN)


class _TileLevels(NamedTuple):
    q_mid: jax.Array
    k_mid: jax.Array
    q_64: jax.Array
    k_64: jax.Array
    q_128: jax.Array
    k_128: jax.Array
    q_tile: jax.Array
    k_tile: jax.Array
    decay: jax.Array


def _hgrn_tile_levels(hq, hf, lb):
    tl = hq.shape[0]
    n_blocks = 4
    block = 2 * CHUNK
    assert tl == n_blocks * block
    q = _silu(hq)
    logf, k = _forget_terms(hf, lb)
    b = _segment_cumsum(logf, CHUNK)
    chunks = [b[n * CHUNK:(n + 1) * CHUNK] for n in range(2 * n_blocks)]
    totals = [c[CHUNK - 1:CHUNK] for c in chunks]
    e = jnp.concatenate([c - totals[n] if n % 2 == 0 else c for n, c in enumerate(chunks)], axis=0)
    q_mid = q * jnp.exp(e)
    k_mid = k * jnp.exp(-e)
    first = [totals[2 * j] for j in range(n_blocks)]
    second = [totals[2 * j + 1] for j in range(n_blocks)]
    both = [first[j] + second[j] for j in range(n_blocks)]

    def scaled(x, blocks, log_scales):
        return jnp.concatenate(
            [x[j * block:(j + 1) * block] * jnp.exp(s) for j, s in zip(blocks, log_scales)], axis=0)

    every = range(n_blocks)
    return _TileLevels(
        q_mid=q_mid, k_mid=k_mid,
        q_64=scaled(q_mid, every, first),
        k_64=scaled(k_mid, every, second),
        q_128=scaled(q_mid, (2, 3), (first[2], first[3] + both[2])),
        k_128=scaled(k_mid, (0, 1), (second[0] + both[1], second[1])),
        q_tile=scaled(q_mid, every, [first[j] + sum(both[:j], 0.0) for j in every]),
        k_tile=scaled(k_mid, every, [second[j] + sum(both[j + 1:], 0.0) for j in every]),
        decay=jnp.exp(sum(both[1:], both[0])))


def _head_rms(o, gain4):
    parts = []
    for h in range(HG_HEADS):
        oh = o[:, h * HG_DV:(h + 1) * HG_DV]
        parts.append(oh * lax.rsqrt(jnp.mean(oh * oh, axis=-1, keepdims=True) + EPS))
    return jnp.concatenate(parts, axis=-1) * gain4


def _branch_gate(hb, w_in_ref, i):
    c0 = COL_GATES + i * D_MODEL
    return _sigmoid(_dot(hb, w_in_ref[:, c0:c0 + D_MODEL]))


def _merge_out(gate, x, branches, w_out_ref, post_g):
    acc = None
    for i, p in enumerate(branches):
        acc = gate(i) * p if acc is None else acc + gate(i) * p
    m = _dot(acc.astype(BF16), w_out_ref[...])
    return x + _rms(m, post_g)


def _ffn_block(x, pre_g, post_g, w_gate_ref, w_up_ref, w_down_ref):
    hb = _rms(x, pre_g).astype(BF16)
    d = None
    d_ff = w_gate_ref.shape[1]
    for c0 in range(0, d_ff, FFN_CHUNK):
        c1 = min(c0 + FFN_CHUNK, d_ff)
        f = _silu(_dot(hb, w_gate_ref[:, c0:c1])) * _dot(hb, w_up_ref[:, c0:c1])
        part = _dot(f.astype(BF16), w_down_ref[c0:c1, :])
        d = part if d is None else d + part
    return x + _rms(d, post_g)


def _cast_block_rows(rows, n_steps):
    bf16_sublanes = 2 * SUBLANES
    for n_blocks in range(min(n_steps, rows // bf16_sublanes), 0, -1):
        if rows % n_blocks == 0 and (rows // n_blocks) % bf16_sublanes == 0:
            return rows // n_blocks
    raise ValueError(f"no bf16-tileable split of {rows} rows")


def _cast_specs(weights, n_steps, step_of):
    specs = []
    for w in weights:
        rows_blk = _cast_block_rows(w.shape[0], n_steps)
        last = w.shape[0] // rows_blk - 1
        specs.append(pl.BlockSpec((rows_blk, w.shape[1]),
                                  lambda *idx, last=last: (jnp.minimum(step_of(*idx), last), 0)))
    return specs


def _cast_blocks(srcs, dsts):
    for src, dst in zip(srcs, dsts):
        dst[...] = src[...].astype(BF16)


def _memory_kv_kernel(mem_ref, g_ref, w_ref, *rest):
    n_cast = (len(rest) - 5) // 2
    k_rows_ref, v_rows_ref, k_ref, v_ref = rest[n_cast:n_cast + 4]
    wb_ref = rest[-1]
    _cast_blocks(rest[:n_cast], rest[n_cast + 4:-1])

    @pl.when(pl.program_id(0) == 0)
    def _():
        wb_ref[...] = w_ref[...].astype(BF16)

    m = _rms(mem_ref[...], g_ref[...]).astype(BF16)
    tokens = mem_ref.shape[0]
    for full, rows_ref, op_ref in ((_dot(m, wb_ref[:, :D_MEM]), k_rows_ref, k_ref),
                                   (_dot(m, wb_ref[:, D_MEM:]), v_rows_ref, v_ref)):
        op_ref[...] = full.astype(BF16)
        for h in range(MEM_HEADS):
            rows_ref[pl.ds(h, tokens, stride=MEM_HEADS), :] = full[:, h * MEM_HEAD_DIM:(h + 1) * MEM_HEAD_DIM]


def _memory_kv(mem2d, gain, w_kv, tile, later_weights):
    rows = mem2d.shape[0]
    n_steps = rows // tile
    cast_specs = _cast_specs(later_weights, n_steps, lambda i: i)
    return pl.pallas_call(
        _memory_kv_kernel,
        grid=(n_steps,),
        in_specs=[
            pl.BlockSpec((tile, D_MODEL), lambda i: (i, 0)),
            pl.BlockSpec((1, D_MODEL), lambda i: (0, 0)),
            _const_spec(w_kv.shape),
        ] + cast_specs,
        out_specs=[pl.BlockSpec((tile * MEM_HEADS, MEM_HEAD_DIM), lambda i: (i, 0))] * 2
        + [pl.BlockSpec((tile, D_MEM), lambda i: (i, 0))] * 2 + cast_specs,
        out_shape=[jax.ShapeDtypeStruct((rows * MEM_HEADS, MEM_HEAD_DIM), F32)] * 2
        + [jax.ShapeDtypeStruct((rows, D_MEM), BF16)] * 2
        + [jax.ShapeDtypeStruct(w.shape, BF16) for w in later_weights],
        scratch_shapes=[pltpu.VMEM(w_kv.shape, BF16)],
        compiler_params=pltpu.CompilerParams(
            dimension_semantics=("arbitrary",), vmem_limit_bytes=VMEM_LIMIT),
        name="memory_kv",
    )(mem2d, gain, w_kv, *later_weights)


def _prompt_mixer_kernel(x_ref, kb_ref, vb_ref, pre_g_ref, post_g_ref, conv_w_ref, conv_b_ref,
                         ln_g_ref, ln_b_ref, lb_logits_ref, hg_g_ref, w_in_ref, w_conv_out_ref,
                         w_hg_out_ref, w_mem_out_ref, w_out_ref, *rest):
    n_cast = (len(rest) - 6) // 2
    cast_src = rest[:n_cast]
    y_ref, new_conv_ref, new_hg_ref = rest[n_cast:n_cast + 3]
    cast_dst = rest[n_cast + 3:2 * n_cast + 3]
    shift_ref, tail_ref, state_t_ref = rest[2 * n_cast + 3:]
    _cast_blocks(cast_src, cast_dst)
    t = pl.program_id(1)
    last_t = pl.num_programs(1) - 1
    tl = PROMPT_TILE
    n_tiles = x_ref.shape[0] // tl

    @pl.when(t == 0)
    def _():
        tail_ref[...] = jnp.zeros_like(tail_ref)
        state_t_ref[...] = jnp.zeros_like(state_t_ref)

    lb = _lower_bound(lb_logits_ref[...])
    half = tl // 2
    row = lax.broadcasted_iota(jnp.int32, (tl, tl), 0)
    col = lax.broadcasted_iota(jnp.int32, (tl, tl), 1)
    span = row ^ col
    near = (span < 2 * CHUNK) & (col <= row)
    mid = (span < 4 * CHUNK) & (col < row)

    def mix_rows():
        n_rows = n_tiles * tl
        shift = shift_ref
        x = x_ref[...]
        hb = _rms(x, pre_g_ref[...]).astype(BF16)

        def proj(c0, c1):
            return _dot(hb, w_in_ref[:, c0:c1])

        u = proj(COL_CA, COL_CB) * _sigmoid(proj(COL_CB, COL_HQ))
        shift[0:CONV_ROWS, :] = tail_ref[...]
        shift[CONV_ROWS:CONV_ROWS + n_rows, :] = u
        tail_ref[...] = u[n_rows - CONV_ROWS:]

        lead = CONV_ROWS - CONV_PREFIX
        conv = None
        for r in range(SUBLANES):
            rows_r = n_rows if r == 0 else n_rows + SUBLANES
            v_r = None
            for a in range((CONV_WIDTH + lead) // SUBLANES + 1):
                j = a * SUBLANES + r - lead
                if 0 <= j < CONV_WIDTH:
                    term = shift[a * SUBLANES:a * SUBLANES + rows_r, :] * conv_w_ref[j:j + 1, :]
                    v_r = term if v_r is None else v_r + term
            if r:
                v_r = pltpu.roll(v_r, rows_r - r, axis=0)[:n_rows]
            conv = v_r if conv is None else conv + v_r
        c = _conv_ln_silu(conv, conv_b_ref[...], ln_g_ref[...], ln_b_ref[...]).astype(BF16)

        hq = proj(COL_HQ, COL_HF)
        hf = proj(COL_HF, COL_HI)
        levels = []
        for i in range(n_tiles):
            lv = _hgrn_tile_levels(hq[i * tl:(i + 1) * tl], hf[i * tl:(i + 1) * tl], lb)
            levels.append(_TileLevels(*[a.astype(BF16) for a in lv[:-1]], lv.decay))
        vb16 = proj(COL_HI, COL_HGATE).astype(BF16)
        hgate = _silu(proj(COL_HGATE, COL_MQ))
        mq = proj(COL_MQ, COL_GATES).astype(BF16)
        gates = [_branch_gate(hb, w_in_ref, g) for g in range(3)]

        head = lambda h: slice(h * MEM_HEAD_DIM, (h + 1) * MEM_HEAD_DIM)
        s = jnp.concatenate([_dot_nt(mq[:, head(h)], kb_ref[:, head(h)]) for h in range(MEM_HEADS)],
                            axis=0)
        t = s * (MEM_HEAD_DIM ** -0.5 * LOG2E)
        e = jnp.exp2(t - jnp.max(t, axis=-1, keepdims=True))
        inv = 1.0 / jnp.sum(e, axis=-1, keepdims=True)
        e = e.astype(BF16)
        om = [_dot(e[h * n_rows:(h + 1) * n_rows], vb_ref[:, head(h)]) * inv[h * n_rows:(h + 1) * n_rows]
              for h in range(MEM_HEADS)]

        o_tiles = []
        for i, lvb in enumerate(levels):
            o_heads = []
            for h in range(HG_HEADS):
                hs = slice(h * HG_DK, (h + 1) * HG_DK)
                vh = vb16[i * tl:(i + 1) * tl, hs]
                s_near = _dot_nt(lvb.q_mid[:, hs], lvb.k_mid[:, hs])
                s_mid = _dot_nt(lvb.q_64[:, hs], lvb.k_64[:, hs])
                s_far = _dot_nt(lvb.q_128[:, hs], lvb.k_128[:, hs])
                base = jnp.where(near, s_near, jnp.where(mid, s_mid, 0.0))
                scores = jnp.concatenate(
                    [base[:half], jnp.concatenate([s_far, base[half:, half:]], axis=1)], axis=0)
                st = state_t_ref[h]
                o_heads.append(_dot(scores.astype(BF16), vh)
                               + _dot_nt(lvb.q_tile[:, hs], st.astype(BF16)))
                state_t_ref[h] = st * lvb.decay[:, hs] + _dot_tn(vh, lvb.k_tile[:, hs])
            o_tiles.append(jnp.concatenate(o_heads, axis=-1))
        o = _head_rms(jnp.concatenate(o_tiles, axis=0), hg_g_ref[...]) * hgate

        p_conv = _dot(c, w_conv_out_ref[...])
        p_mem = _dot(jnp.concatenate(om, axis=-1).astype(BF16), w_mem_out_ref[...])
        p_hg = _dot(o.astype(BF16), w_hg_out_ref[...])
        y_ref[...] = _merge_out(lambda g: gates[g], x, (p_conv, p_hg, p_mem), w_out_ref, post_g_ref[...])

    mix_rows()

    @pl.when(t == last_t)
    def _():
        new_conv_ref[...] = tail_ref[CONV_ROWS - CONV_PREFIX:CONV_ROWS, :]
        for h in range(HG_HEADS):
            new_hg_ref[h] = state_t_ref[h].T


def _const_spec(shape):
    zeros = (0,) * len(shape)
    return pl.BlockSpec(shape, lambda *_: zeros, pipeline_mode=pl.Buffered(1))


def _mixer_weight_specs(w_in, w_conv_out, w_hg_out, w_mem_out, w_out):
    return [_const_spec(w.shape) for w in (w_in, w_conv_out, w_hg_out, w_mem_out, w_out)]


def _vector_specs():
    return [
        _const_spec((1, D_MODEL)), _const_spec((1, D_MODEL)),
        _const_spec((CONV_WIDTH, D_CONV)), _const_spec((1, D_CONV)),
        _const_spec((1, D_CONV)), _const_spec((1, D_CONV)),
        _const_spec((2, D_HGRN)), _const_spec((1, D_HGRN)),
    ]


def _prompt_mixer(x, mk, mv, vectors, weights, later_weights):
    batch, seq, _ = x.shape
    n_mem = mk.shape[1]
    tl = PROMPT_TILE * PROMPT_STEP_TILES
    n_t = seq // tl
    cast_specs = _cast_specs(later_weights, batch * n_t, lambda b, t: b * n_t + t)
    return pl.pallas_call(
        _prompt_mixer_kernel,
        grid=(batch, n_t),
        in_specs=[
            pl.BlockSpec((None, tl, D_MODEL), lambda b, t: (b, t, 0)),
            pl.BlockSpec((None, n_mem, D_MEM), lambda b, t: (b, 0, 0)),
            pl.BlockSpec((None, n_mem, D_MEM), lambda b, t: (b, 0, 0)),
        ] + _vector_specs() + _mixer_weight_specs(*weights) + cast_specs,
        out_specs=[
            pl.BlockSpec((None, tl, D_MODEL), lambda b, t: (b, t, 0)),
            pl.BlockSpec((None, CONV_PREFIX, D_CONV), lambda b, t: (b, 0, 0)),
            pl.BlockSpec((None, HG_HEADS, HG_DK, HG_DV), lambda b, t: (b, 0, 0, 0)),
        ] + cast_specs,
        out_shape=[
            jax.ShapeDtypeStruct((batch, seq, D_MODEL), F32),
            jax.ShapeDtypeStruct((batch, CONV_PREFIX, D_CONV), F32),
            jax.ShapeDtypeStruct((batch, HG_HEADS, HG_DK, HG_DV), F32),
        ] + [jax.ShapeDtypeStruct(w.shape, BF16) for w in later_weights],
        scratch_shapes=[
            pltpu.VMEM((tl + CONV_ROWS, D_CONV), F32),
            pltpu.VMEM((CONV_ROWS, D_CONV), F32),
            pltpu.VMEM((HG_HEADS, HG_DV, HG_DK), F32),
        ],
        compiler_params=pltpu.CompilerParams(
            dimension_semantics=("arbitrary", "arbitrary"), vmem_limit_bytes=VMEM_LIMIT),
        name="prompt_mixer",
    )(x, mk, mv, *vectors, *weights, *later_weights)


def _store_lane_chunks(ref, x):
    for i in range(ref.shape[0]):
        ref[i] = x[:, i * LANES:(i + 1) * LANES]


def _load_lane_chunks(ref, rows=slice(None)):
    return jnp.concatenate([ref[i, rows, :] for i in range(ref.shape[0])], axis=1)


def _lane_chunk_spec(tile, width):
    return pl.BlockSpec((width // LANES, tile, LANES), lambda i: (0, i, 0))


def _sample_in_kernel(x_ref, pre_g_ref, lb_logits_ref, w_in_ref,
                      u_ref, qd_ref, ki_ref, ke_ref, v_ref, blast_ref, q4_ref, hgate_ref, gates_ref):
    n_seq = x_ref.shape[0] // SUBLANES
    hb = _rms(x_ref[...], pre_g_ref[...]).astype(BF16)
    _store_lane_chunks(
        u_ref, _dot(hb, w_in_ref[:, COL_CA:COL_CB]) * _sigmoid(_dot(hb, w_in_ref[:, COL_CB:COL_HQ])))
    lb = _lower_bound(lb_logits_ref[...])
    q_dec, k_inv, k_end, b_last = _hgrn_gates(
        _dot(hb, w_in_ref[:, COL_HQ:COL_HF]), _dot(hb, w_in_ref[:, COL_HF:COL_HI]), lb, SUBLANES)
    qd_ref[...] = q_dec
    ki_ref[...] = k_inv
    ke_ref[...] = k_end
    blast_ref[...] = b_last
    v_ref[...] = _dot(hb, w_in_ref[:, COL_HI:COL_HGATE])
    hgate_ref[...] = _silu(_dot(hb, w_in_ref[:, COL_HGATE:COL_MQ]))
    mq = _dot(hb, w_in_ref[:, COL_MQ:COL_GATES])
    for h in range(MEM_HEADS):
        q4_ref[:, h * SUBLANES:(h + 1) * SUBLANES, :] = (
            mq[:, h * MEM_HEAD_DIM:(h + 1) * MEM_HEAD_DIM].reshape(n_seq, SUBLANES, MEM_HEAD_DIM))
    for i in range(3):
        gates_ref[:, i * D_MODEL:(i + 1) * D_MODEL] = _branch_gate(hb, w_in_ref, i)


def _sample_seq_kernel(u_ref, qd_ref, ki_ref, ke_ref, v_ref, blast_ref, q4_ref, sc_ref, sh_ref,
                       k2_ref, v2_ref, conv_w_ref,
                       c_ref, o_ref, om_ref, new_conv_ref, new_hg_ref):
    n_seq, dec = sc_ref.shape[1], SUBLANES
    rows = n_seq * dec

    q_rows = MEM_HEADS * dec
    s = jnp.concatenate([_dot_nt(q4_ref[g].astype(BF16), k2_ref[g].astype(BF16)) for g in range(n_seq)],
                        axis=0)
    assert dec & (dec - 1) == 0 and MEM_HEADS & (MEM_HEADS - 1) == 0
    q_head = (lax.broadcasted_iota(jnp.int32, s.shape, 0) >> (dec.bit_length() - 1)) & (MEM_HEADS - 1)
    k_head = lax.broadcasted_iota(jnp.int32, s.shape, 1) & (MEM_HEADS - 1)
    p = _attention_softmax(s, q_head == k_head).astype(BF16)
    om = []
    for g in range(n_seq):
        om_g = _dot(p[g * q_rows:(g + 1) * q_rows], v2_ref[g].astype(BF16))
        om.extend(om_g[h * dec:(h + 1) * dec] for h in range(MEM_HEADS))

    u_slabs = [_load_lane_chunks(u_ref, pl.ds(step, n_seq, stride=dec)) for step in range(dec)]
    conv = [None] * dec
    for s in range(CONV_PREFIX + dec):
        slab = sc_ref[s] if s < CONV_PREFIX else u_slabs[s - CONV_PREFIX]
        if s >= dec:
            new_conv_ref[s - dec] = slab
        for step in range(max(0, s - CONV_PREFIX), min(dec, s + 1)):
            term = slab * conv_w_ref[s - step:s - step + 1, :]
            conv[step] = term if conv[step] is None else conv[step] + term

    row = lax.broadcasted_iota(jnp.int32, (rows, rows), 0)
    col = lax.broadcasted_iota(jnp.int32, (rows, rows), 1)
    causal = ((row ^ col) < dec) & (col <= row)
    decay = jnp.exp(blast_ref[...])
    for h in range(HG_HEADS):
        hs = slice(h * HG_DK, (h + 1) * HG_DK)
        scores = jnp.where(causal, _dot_nt(qd_ref[:, hs].astype(BF16), ki_ref[:, hs].astype(BF16)), 0.0)
        o_intra = _dot(scores.astype(BF16), v_ref[:, hs].astype(BF16))
        for g in range(n_seq):
            rs = slice(g * dec, (g + 1) * dec)
            s0 = sh_ref[g, h]
            o_ref[rs, hs] = o_intra[rs] + _dot(qd_ref[rs, hs].astype(BF16), s0.astype(BF16))
            decay_col = jnp.broadcast_to(decay[g:g + 1, hs], (HG_DV, HG_DK)).T
            new_hg_ref[g, h] = decay_col * s0 + _dot_tn(ke_ref[rs, hs].astype(BF16),
                                                        v_ref[rs, hs].astype(BF16))

    for step in range(dec):
        for i in range(c_ref.shape[0]):
            c_ref[i, pl.ds(step, n_seq, stride=dec), :] = conv[step][:, i * LANES:(i + 1) * LANES]
    for g in range(n_seq):
        om_ref[g * dec:(g + 1) * dec, :] = jnp.concatenate(om[g * MEM_HEADS:(g + 1) * MEM_HEADS], axis=1)


def _sample_out_kernel(x_ref, c_ref, o_ref, om_ref, hgate_ref, gates_ref, post_g_ref, conv_b_ref,
                       ln_g_ref, ln_b_ref, hg_g_ref, w_conv_out_ref, w_hg_out_ref, w_mem_out_ref,
                       w_out_ref, ffn_pre_g_ref, ffn_post_g_ref, w_gate_ref, w_up_ref, w_down_ref, y_ref):
    c = _conv_ln_silu(_load_lane_chunks(c_ref), conv_b_ref[...], ln_g_ref[...], ln_b_ref[...])
    p_conv = _dot(c.astype(BF16), w_conv_out_ref[...])
    o = _head_rms(o_ref[...], hg_g_ref[...]) * hgate_ref[...]
    p_hg = _dot(o.astype(BF16), w_hg_out_ref[...])
    p_mem = _dot(om_ref[...].astype(BF16), w_mem_out_ref[...])
    x1 = _merge_out(lambda i: gates_ref[:, i * D_MODEL:(i + 1) * D_MODEL], x_ref[...],
                    (p_conv, p_hg, p_mem), w_out_ref, post_g_ref[...])
    y_ref[...] = _ffn_block(x1, ffn_pre_g_ref[...], ffn_post_g_ref[...], w_gate_ref, w_up_ref, w_down_ref)


def _row_spec(tile, width):
    return pl.BlockSpec((tile, width), lambda i: (i, 0))


def _sample_in(x2d, pre_g, lb_logits, w_in):
    rows = x2d.shape[0]
    tile = SAMPLE_TILE
    seqs = tile // SUBLANES
    wide = [D_HGRN, D_HGRN, D_HGRN, D_HGRN]
    return pl.pallas_call(
        _sample_in_kernel,
        grid=(rows // tile,),
        in_specs=[_row_spec(tile, D_MODEL), _const_spec((1, D_MODEL)), _const_spec((2, D_HGRN)),
                  _const_spec(w_in.shape)],
        out_specs=[_lane_chunk_spec(tile, D_CONV)] + [_row_spec(tile, w) for w in wide] + [
            _row_spec(seqs, D_HGRN),
            pl.BlockSpec((seqs, MEM_HEADS * SUBLANES, MEM_HEAD_DIM), lambda i: (i, 0, 0)),
            _row_spec(tile, D_HGRN),
            _row_spec(tile, 3 * D_MODEL),
        ],
        out_shape=[jax.ShapeDtypeStruct((D_CONV // LANES, rows, LANES), F32)]
        + [jax.ShapeDtypeStruct((rows, w), F32) for w in wide] + [
            jax.ShapeDtypeStruct((rows // SUBLANES, D_HGRN), F32),
            jax.ShapeDtypeStruct((rows // SUBLANES, MEM_HEADS * SUBLANES, MEM_HEAD_DIM), F32),
            jax.ShapeDtypeStruct((rows, D_HGRN), F32),
            jax.ShapeDtypeStruct((rows, 3 * D_MODEL), F32),
        ],
        compiler_params=pltpu.CompilerParams(
            dimension_semantics=("parallel",), vmem_limit_bytes=VMEM_LIMIT),
        name="sample_in",
    )(x2d, pre_g, lb_logits, w_in)


def _sample_seq(u, qd, ki, ke, v, blast, q4, conv_rows, state_hgrn, k2, v2, conv_w):
    n = conv_rows.shape[1]
    g = SAMPLE_GROUP
    rows = g * SUBLANES
    kv_rows = k2.shape[1]
    seq_block = lambda *tail: pl.BlockSpec((g,) + tail, lambda i: (i,) + (0,) * len(tail))
    conv_block = pl.BlockSpec((CONV_PREFIX, g, D_CONV), lambda i: (0, i, 0))
    return pl.pallas_call(
        _sample_seq_kernel,
        grid=(n // g,),
        in_specs=[_lane_chunk_spec(rows, D_CONV)] + [_row_spec(rows, D_HGRN)] * 4 + [
            _row_spec(g, D_HGRN),
            seq_block(MEM_HEADS * SUBLANES, MEM_HEAD_DIM),
            conv_block,
            seq_block(HG_HEADS, HG_DK, HG_DV),
            seq_block(kv_rows, MEM_HEAD_DIM),
            seq_block(kv_rows, MEM_HEAD_DIM),
            _const_spec((CONV_WIDTH, D_CONV)),
        ],
        out_specs=[_lane_chunk_spec(rows, D_CONV), _row_spec(rows, D_HGRN), _row_spec(rows, D_MEM),
                   conv_block, seq_block(HG_HEADS, HG_DK, HG_DV)],
        out_shape=[
            jax.ShapeDtypeStruct((D_CONV // LANES, n * SUBLANES, LANES), F32),
            jax.ShapeDtypeStruct((n * SUBLANES, D_HGRN), F32),
            jax.ShapeDtypeStruct((n * SUBLANES, D_MEM), F32),
            jax.ShapeDtypeStruct((CONV_PREFIX, n, D_CONV), F32),
            jax.ShapeDtypeStruct((n, HG_HEADS, HG_DK, HG_DV), F32),
        ],
        compiler_params=pltpu.CompilerParams(
            dimension_semantics=("parallel",), vmem_limit_bytes=VMEM_LIMIT),
        name="sample_seq",
    )(u, qd, ki, ke, v, blast, q4, conv_rows, state_hgrn, k2, v2, conv_w)


def _sample_out(x2d, c, o, om, hgate, gates, vectors, weights, ffn_vectors, ffn_weights):
    rows = x2d.shape[0]
    tile = SAMPLE_TILE
    acts = (x2d, c, o, om, hgate, gates)
    consts = tuple(vectors) + tuple(weights) + tuple(ffn_vectors) + tuple(ffn_weights)
    return pl.pallas_call(
        _sample_out_kernel,
        grid=(rows // tile,),
        in_specs=[_lane_chunk_spec(tile, D_CONV) if a is c else _row_spec(tile, a.shape[1]) for a in acts]
        + [_const_spec(a.shape) for a in consts],
        out_specs=_row_spec(tile, D_MODEL),
        out_shape=jax.ShapeDtypeStruct((rows, D_MODEL), F32),
        compiler_params=pltpu.CompilerParams(
            dimension_semantics=("parallel",), vmem_limit_bytes=VMEM_LIMIT),
        name="sample_out",
    )(*acts, *consts)


def _ffn_kernel(x_ref, pre_g_ref, post_g_ref, w_gate_ref, w_up_ref, w_down_ref, y_ref):
    for r0 in range(0, x_ref.shape[0], FFN_BLOCK):
        rows = slice(r0, r0 + FFN_BLOCK)
        y_ref[rows, :] = _ffn_block(x_ref[rows, :], pre_g_ref[...], post_g_ref[...],
                                    w_gate_ref, w_up_ref, w_down_ref)


def _ffn(x2d, pre_g, post_g, w_gate, w_up, w_down):
    rows = x2d.shape[0]
    tile = min(FFN_TILE, rows)
    return pl.pallas_call(
        _ffn_kernel,
        grid=(rows // tile,),
        in_specs=[
            pl.BlockSpec((tile, D_MODEL), lambda i: (i, 0)),
            _const_spec((1, D_MODEL)), _const_spec((1, D_MODEL)),
            _const_spec(w_gate.shape), _const_spec(w_up.shape), _const_spec(w_down.shape),
        ],
        out_specs=pl.BlockSpec((tile, D_MODEL), lambda i: (i, 0)),
        out_shape=jax.ShapeDtypeStruct((rows, D_MODEL), F32),
        compiler_params=pltpu.CompilerParams(
            dimension_semantics=("parallel",), vmem_limit_bytes=VMEM_LIMIT),
        name="ffn",
    )(x2d, pre_g, post_g, w_gate, w_up, w_down)


def kernel(x_prompt, x_sample, mem_prompt, state_conv, state_hgrn, cache_mem_k, cache_mem_v, norm_pre_mix, norm_post_mix, norm_pre_ffn, norm_post_ffn, w_in, conv_w, conv_b, conv_ln_g, conv_ln_b, w_conv_out, hg_lb_logits, hg_norm_g, w_hg_out, mem_norm_g, w_mem_kv, w_mem_out, w_out, w_ffn_gate, w_ffn_up, w_ffn_down):
    depth = w_in.shape[0]
    assert depth == 1 and hg_lb_logits.shape[0] == 2, "single-layer step"
    batch, seq, _ = x_prompt.shape
    n_dec, dec, _ = x_sample.shape
    n_mem = mem_prompt.shape[1]
    assert seq % PROMPT_TILE == 0 and n_dec % SAMPLE_GROUP == 0 and dec == SUBLANES
    assert (n_dec * dec) % SAMPLE_TILE == 0 and (batch * n_mem) % MEMORY_TILE == 0

    vectors = (norm_pre_mix, norm_post_mix, conv_w[0], conv_b, conv_ln_g, conv_ln_b,
               hg_lb_logits, jnp.tile(hg_norm_g, (1, HG_HEADS)))

    mk, mv, kb, vb, *weights = _memory_kv(
        mem_prompt.reshape(batch * n_mem, D_MODEL), mem_norm_g, w_mem_kv[0], MEMORY_TILE,
        tuple(w[0] for w in (w_in, w_conv_out, w_hg_out, w_mem_out, w_out)))

    xp, conv_p, hg_p, *ffn_weights = _prompt_mixer(
        x_prompt, kb.reshape(batch, n_mem, D_MEM), vb.reshape(batch, n_mem, D_MEM), vectors, weights,
        (w_ffn_gate[0], w_ffn_up[0], w_ffn_down[0]))
    yp = _ffn(xp.reshape(batch * seq, D_MODEL), norm_pre_ffn, norm_post_ffn, *ffn_weights)

    xs2d = x_sample.reshape(n_dec * dec, D_MODEL)
    u, qd, ki, ke, v, blast, q4, hgate, gates = _sample_in(xs2d, norm_pre_mix, hg_lb_logits, weights[0])
    k2 = cache_mem_k[0].reshape(n_dec, n_mem * MEM_HEADS, MEM_HEAD_DIM)
    v2 = cache_mem_v[0].reshape(n_dec, n_mem * MEM_HEADS, MEM_HEAD_DIM)
    c, o, om, conv_rows, hg_s = _sample_seq(u, qd, ki, ke, v, blast, q4,
                                            jnp.transpose(state_conv[0], (1, 0, 2)), state_hgrn[0],
                                            k2, v2, conv_w[0])
    conv_s = jnp.transpose(conv_rows, (1, 0, 2))
    ys = _sample_out(xs2d, c, o, om, hgate, gates,
                     (norm_post_mix, conv_b, conv_ln_g, conv_ln_b, vectors[-1]), weights[1:],
                     (norm_pre_ffn, norm_post_ffn), ffn_weights)

    kv_shape = (1, batch, n_mem, MEM_HEADS, MEM_HEAD_DIM)
    return (yp.reshape(batch, seq, D_MODEL), ys.reshape(n_dec, dec, D_MODEL),
            conv_p[None], hg_p[None], mk.reshape(kv_shape), mv.reshape(kv_shape),
            conv_s[None], hg_s[None])
```

```python
from typing import NamedTuple

import jax
import jax.numpy as jnp
from jax import lax
from jax.experimental import pallas as pl
from jax.experimental.pallas import tpu as pltpu

D_MODEL = 1024
D_CONV = 512
CONV_WIDTH = 31
CONV_PREFIX = CONV_WIDTH - 1
HG_HEADS = 4
HG_DK = 128
HG_DV = 128
D_HGRN = HG_HEADS * HG_DK
MEM_HEADS = 4
MEM_HEAD_DIM = 128
D_MEM = MEM_HEADS * MEM_HEAD_DIM
CHUNK = 32
EPS = 1e-6
LOG2E = 1.4426950408889634

COL_CA = 0
COL_CB = COL_CA + D_CONV
COL_HQ = COL_CB + D_CONV
COL_HF = COL_HQ + D_HGRN
COL_HI = COL_HF + D_HGRN
COL_HGATE = COL_HI + D_HGRN
COL_MQ = COL_HGATE + D_HGRN
COL_GATES = COL_MQ + D_MEM

SUBLANES = 8
LANES = 128
VMEM_LIMIT = 56 * 1024 * 1024

PROMPT_TILE = 256
PROMPT_STEP_TILES = 2
SAMPLE_GROUP = 8
SAMPLE_TILE = 256
FFN_TILE = 512
MEMORY_TILE = 256
FFN_BLOCK = 256
CONV_ROWS = 32

BF16 = jnp.bfloat16
F32 = jnp.float32


def _dot(a, b):
    return jnp.dot(a, b, preferred_element_type=F32)


def _dot_nt(a, b):
    return lax.dot_general(a, b, (((1,), (1,)), ((), ())), preferred_element_type=F32)


def _dot_tn(a, b):
    return lax.dot_general(a, b, (((0,), (0,)), ((), ())), preferred_element_type=F32)


def _rms(x, gain):
    return x * lax.rsqrt(jnp.mean(x * x, axis=-1, keepdims=True) + EPS) * gain


def _sigmoid(x):
    return 1.0 / (1.0 + jnp.exp2(x * -LOG2E))


def _silu(x):
    return x * _sigmoid(x)


def _attention_softmax(scores, keep=None):
    t = scores * (MEM_HEAD_DIM ** -0.5 * LOG2E)
    if keep is not None:
        t = jnp.where(keep, t, -jnp.inf)
    e = jnp.exp2(t - jnp.max(t, axis=-1, keepdims=True))
    return e / jnp.sum(e, axis=-1, keepdims=True)


def _forget_terms(hf, lb):
    sig = _sigmoid(hf)
    return jnp.log(lb + (1.0 - lb) * sig), (1.0 - lb) * (1.0 - sig)


def _lower_bound(lb_logits):
    m = jnp.max(lb_logits, axis=0, keepdims=True)
    e = jnp.exp(lb_logits - m)
    return e[0:1] / jnp.sum(e, axis=0, keepdims=True)


def _segment_cumsum(x, seg):
    pos = lax.broadcasted_iota(jnp.int32, x.shape, 0) & (seg - 1)
    s = 1
    while s < seg:
        x = x + jnp.where(pos >= s, pltpu.roll(x, s, axis=0), 0.0)
        s *= 2
    return x


def _conv_ln_silu(c, conv_b, ln_g, ln_b):
    c = c + conv_b
    mu = jnp.mean(c, axis=-1, keepdims=True)
    d = c - mu
    var = jnp.mean(d * d, axis=-1, keepdims=True)
    return _silu(d * lax.rsqrt(var + EPS) * ln_g + ln_b)


def _hgrn_gates(hq, hf, lb, seg):
    q = _silu(hq)
    logf, k = _forget_terms(hf, lb)
    b = _segment_cumsum(logf, seg)
    rows = b.shape[0]
    b3 = b.reshape(rows // seg, seg, D_HGRN)
    b_last3 = b3[:, seg - 1:seg, :]
    rest = (b_last3 - b3).reshape(rows, D_HGRN)
    q_dec = q * jnp.exp(b)
    k_inv = k * jnp.exp(-b)
    k_end = k * jnp.exp(rest)
    return q_dec, k_inv, k_end, b_last3.reshape(rows // seg, D_HGRN)


class _TileLevels(NamedTuple):
    q_mid: jax.Array
    k_mid: jax.Array
    q_64: jax.Array
    k_64: jax.Array
    q_128: jax.Array
    k_128: jax.Array
    q_tile: jax.Array
    k_tile: jax.Array
    decay: jax.Array


def _hgrn_tile_levels(hq, hf, lb):
    tl = hq.shape[0]
    n_blocks = 4
    block = 2 * CHUNK
    assert tl == n_blocks * block
    q = _silu(hq)
    logf, k = _forget_terms(hf, lb)
    b = _segment_cumsum(logf, CHUNK)
    chunks = [b[n * CHUNK:(n + 1) * CHUNK] for n in range(2 * n_blocks)]
    totals = [c[CHUNK - 1:CHUNK] for c in chunks]
    e = jnp.concatenate([c - totals[n] if n % 2 == 0 else c for n, c in enumerate(chunks)], axis=0)
    q_mid = q * jnp.exp(e)
    k_mid = k * jnp.exp(-e)
    first = [totals[2 * j] for j in range(n_blocks)]
    second = [totals[2 * j + 1] for j in range(n_blocks)]
    both = [first[j] + second[j] for j in range(n_blocks)]

    def scaled(x, blocks, log_scales):
        return jnp.concatenate(
            [x[j * block:(j + 1) * block] * jnp.exp(s) for j, s in zip(blocks, log_scales)], axis=0)

    every = range(n_blocks)
    return _TileLevels(
        q_mid=q_mid, k_mid=k_mid,
        q_64=scaled(q_mid, every, first),
        k_64=scaled(k_mid, every, second),
        q_128=scaled(q_mid, (2, 3), (first[2], first[3] + both[2])),
        k_128=scaled(k_mid, (0, 1), (second[0] + both[1], second[1])),
        q_tile=scaled(q_mid, every, [first[j] + sum(both[:j], 0.0) for j in every]),
        k_tile=scaled(k_mid, every, [second[j] + sum(both[j + 1:], 0.0) for j in every]),
        decay=jnp.exp(sum(both[1:], both[0])))


def _head_rms(o, gain4):
    parts = []
    for h in range(HG_HEADS):
        oh = o[:, h * HG_DV:(h + 1) * HG_DV]
        parts.append(oh * lax.rsqrt(jnp.mean(oh * oh, axis=-1, keepdims=True) + EPS))
    return jnp.concatenate(parts, axis=-1) * gain4


def _branch_gate(hb, w_in_ref, i):
    c0 = COL_GATES + i * D_MODEL
    return _sigmoid(_dot(hb, w_in_ref[:, c0:c0 + D_MODEL]))


def _merge_out(gate, x, branches, w_out_ref, post_g):
    acc = None
    for i, p in enumerate(branches):
        acc = gate(i) * p if acc is None else acc + gate(i) * p
    m = _dot(acc.astype(BF16), w_out_ref[...])
    return x + _rms(m, post_g)


def _ffn_block(x, pre_g, post_g, w_gate_ref, w_up_ref, w_down_ref):
    hb = _rms(x, pre_g).astype(BF16)
    f = _silu(_dot(hb, w_gate_ref[...])) * _dot(hb, w_up_ref[...])
    d = _dot(f.astype(BF16), w_down_ref[...])
    return x + _rms(d, post_g)


def _cast_block_rows(rows, n_steps):
    bf16_sublanes = 2 * SUBLANES
    for n_blocks in range(min(n_steps, rows // bf16_sublanes), 0, -1):
        if rows % n_blocks == 0 and (rows // n_blocks) % bf16_sublanes == 0:
            return rows // n_blocks
    raise ValueError(f"no bf16-tileable split of {rows} rows")


def _cast_specs(weights, n_steps, step_of):
    specs = []
    for w in weights:
        rows_blk = _cast_block_rows(w.shape[0], n_steps)
        last = w.shape[0] // rows_blk - 1
        specs.append(pl.BlockSpec((rows_blk, w.shape[1]),
                                  lambda *idx, last=last: (jnp.minimum(step_of(*idx), last), 0)))
    return specs


def _cast_blocks(srcs, dsts):
    for src, dst in zip(srcs, dsts):
        dst[...] = src[...].astype(BF16)


def _memory_kv_kernel(mem_ref, g_ref, w_ref, *rest):
    n_cast = (len(rest) - 5) // 2
    k_rows_ref, v_rows_ref, k_ref, v_ref = rest[n_cast:n_cast + 4]
    wb_ref = rest[-1]
    _cast_blocks(rest[:n_cast], rest[n_cast + 4:-1])

    @pl.when(pl.program_id(0) == 0)
    def _():
        wb_ref[...] = w_ref[...].astype(BF16)

    m = _rms(mem_ref[...], g_ref[...]).astype(BF16)
    tokens = mem_ref.shape[0]
    for full, rows_ref, op_ref in ((_dot(m, wb_ref[:, :D_MEM]), k_rows_ref, k_ref),
                                   (_dot(m, wb_ref[:, D_MEM:]), v_rows_ref, v_ref)):
        op_ref[...] = full.astype(BF16)
        for h in range(MEM_HEADS):
            rows_ref[pl.ds(h, tokens, stride=MEM_HEADS), :] = full[:, h * MEM_HEAD_DIM:(h + 1) * MEM_HEAD_DIM]


def _memory_kv(mem2d, gain, w_kv, tile, later_weights):
    rows = mem2d.shape[0]
    n_steps = rows // tile
    cast_specs = _cast_specs(later_weights, n_steps, lambda i: i)
    return pl.pallas_call(
        _memory_kv_kernel,
        grid=(n_steps,),
        in_specs=[
            pl.BlockSpec((tile, D_MODEL), lambda i: (i, 0)),
            pl.BlockSpec((1, D_MODEL), lambda i: (0, 0)),
            _const_spec(w_kv.shape),
        ] + cast_specs,
        out_specs=[pl.BlockSpec((tile * MEM_HEADS, MEM_HEAD_DIM), lambda i: (i, 0))] * 2
        + [pl.BlockSpec((tile, D_MEM), lambda i: (i, 0))] * 2 + cast_specs,
        out_shape=[jax.ShapeDtypeStruct((rows * MEM_HEADS, MEM_HEAD_DIM), F32)] * 2
        + [jax.ShapeDtypeStruct((rows, D_MEM), BF16)] * 2
        + [jax.ShapeDtypeStruct(w.shape, BF16) for w in later_weights],
        scratch_shapes=[pltpu.VMEM(w_kv.shape, BF16)],
        compiler_params=pltpu.CompilerParams(
            dimension_semantics=("arbitrary",), vmem_limit_bytes=VMEM_LIMIT),
        name="memory_kv",
    )(mem2d, gain, w_kv, *later_weights)


def _prompt_mixer_kernel(x_ref, kb_ref, vb_ref, pre_g_ref, post_g_ref, conv_w_ref, conv_b_ref,
                         ln_g_ref, ln_b_ref, lb_logits_ref, hg_g_ref, w_in_ref, w_conv_out_ref,
                         w_hg_out_ref, w_mem_out_ref, w_out_ref, *rest):
    n_cast = (len(rest) - 6) // 2
    cast_src = rest[:n_cast]
    y_ref, new_conv_ref, new_hg_ref = rest[n_cast:n_cast + 3]
    cast_dst = rest[n_cast + 3:2 * n_cast + 3]
    shift_ref, tail_ref, state_t_ref = rest[2 * n_cast + 3:]
    _cast_blocks(cast_src, cast_dst)
    t = pl.program_id(1)
    last_t = pl.num_programs(1) - 1
    tl = PROMPT_TILE
    n_tiles = x_ref.shape[0] // tl

    @pl.when(t == 0)
    def _():
        tail_ref[...] = jnp.zeros_like(tail_ref)
        state_t_ref[...] = jnp.zeros_like(state_t_ref)

    lb = _lower_bound(lb_logits_ref[...])
    half = tl // 2
    row = lax.broadcasted_iota(jnp.int32, (tl, tl), 0)
    col = lax.broadcasted_iota(jnp.int32, (tl, tl), 1)
    span = row ^ col
    near = (span < 2 * CHUNK) & (col <= row)
    mid = (span < 4 * CHUNK) & (col < row)

    def mix_rows():
        n_rows = n_tiles * tl
        shift = shift_ref
        x = x_ref[...]
        hb = _rms(x, pre_g_ref[...]).astype(BF16)

        def proj(c0, c1):
            return _dot(hb, w_in_ref[:, c0:c1])

        u = proj(COL_CA, COL_CB) * _sigmoid(proj(COL_CB, COL_HQ))
        shift[0:CONV_ROWS, :] = tail_ref[...]
        shift[CONV_ROWS:CONV_ROWS + n_rows, :] = u
        tail_ref[...] = u[n_rows - CONV_ROWS:]

        lead = CONV_ROWS - CONV_PREFIX
        conv = None
        for r in range(SUBLANES):
            rows_r = n_rows if r == 0 else n_rows + SUBLANES
            v_r = None
            for a in range((CONV_WIDTH + lead) // SUBLANES + 1):
                j = a * SUBLANES + r - lead
                if 0 <= j < CONV_WIDTH:
                    term = shift[a * SUBLANES:a * SUBLANES + rows_r, :] * conv_w_ref[j:j + 1, :]
                    v_r = term if v_r is None else v_r + term
            if r:
                v_r = pltpu.roll(v_r, rows_r - r, axis=0)[:n_rows]
            conv = v_r if conv is None else conv + v_r
        c = _conv_ln_silu(conv, conv_b_ref[...], ln_g_ref[...], ln_b_ref[...]).astype(BF16)

        hq = proj(COL_HQ, COL_HF)
        hf = proj(COL_HF, COL_HI)
        levels = []
        for i in range(n_tiles):
            lv = _hgrn_tile_levels(hq[i * tl:(i + 1) * tl], hf[i * tl:(i + 1) * tl], lb)
            levels.append(_TileLevels(*[a.astype(BF16) for a in lv[:-1]], lv.decay))
        vb16 = proj(COL_HI, COL_HGATE).astype(BF16)
        hgate = _silu(proj(COL_HGATE, COL_MQ))
        mq = proj(COL_MQ, COL_GATES).astype(BF16)
        gates = [_branch_gate(hb, w_in_ref, g) for g in range(3)]

        head = lambda h: slice(h * MEM_HEAD_DIM, (h + 1) * MEM_HEAD_DIM)
        s = jnp.concatenate([_dot_nt(mq[:, head(h)], kb_ref[:, head(h)]) for h in range(MEM_HEADS)],
                            axis=0)
        t = s * (MEM_HEAD_DIM ** -0.5 * LOG2E)
        e = jnp.exp2(t - jnp.max(t, axis=-1, keepdims=True))
        inv = 1.0 / jnp.sum(e, axis=-1, keepdims=True)
        e = e.astype(BF16)
        om = [_dot(e[h * n_rows:(h + 1) * n_rows], vb_ref[:, head(h)]) * inv[h * n_rows:(h + 1) * n_rows]
              for h in range(MEM_HEADS)]

        o_tiles = []
        for i, lvb in enumerate(levels):
            o_heads = []
            for h in range(HG_HEADS):
                hs = slice(h * HG_DK, (h + 1) * HG_DK)
                vh = vb16[i * tl:(i + 1) * tl, hs]
                s_near = _dot_nt(lvb.q_mid[:, hs], lvb.k_mid[:, hs])
                s_mid = _dot_nt(lvb.q_64[:, hs], lvb.k_64[:, hs])
                s_far = _dot_nt(lvb.q_128[:, hs], lvb.k_128[:, hs])
                base = jnp.where(near, s_near, jnp.where(mid, s_mid, 0.0))
                scores = jnp.concatenate(
                    [base[:half], jnp.concatenate([s_far, base[half:, half:]], axis=1)], axis=0)
                st = state_t_ref[h]
                o_heads.append(_dot(scores.astype(BF16), vh)
                               + _dot_nt(lvb.q_tile[:, hs], st.astype(BF16)))
                state_t_ref[h] = st * lvb.decay[:, hs] + _dot_tn(vh, lvb.k_tile[:, hs])
            o_tiles.append(jnp.concatenate(o_heads, axis=-1))
        o = _head_rms(jnp.concatenate(o_tiles, axis=0), hg_g_ref[...]) * hgate

        p_conv = _dot(c, w_conv_out_ref[...])
        p_mem = _dot(jnp.concatenate(om, axis=-1).astype(BF16), w_mem_out_ref[...])
        p_hg = _dot(o.astype(BF16), w_hg_out_ref[...])
        y_ref[...] = _merge_out(lambda g: gates[g], x, (p_conv, p_hg, p_mem), w_out_ref, post_g_ref[...])

    mix_rows()

    @pl.when(t == last_t)
    def _():
        new_conv_ref[...] = tail_ref[CONV_ROWS - CONV_PREFIX:CONV_ROWS, :]
        for h in range(HG_HEADS):
            new_hg_ref[h] = state_t_ref[h].T


def _const_spec(shape):
    zeros = (0,) * len(shape)
    return pl.BlockSpec(shape, lambda *_: zeros, pipeline_mode=pl.Buffered(1))


def _mixer_weight_specs(w_in, w_conv_out, w_hg_out, w_mem_out, w_out):
    return [_const_spec(w.shape) for w in (w_in, w_conv_out, w_hg_out, w_mem_out, w_out)]


def _vector_specs():
    return [
        _const_spec((1, D_MODEL)), _const_spec((1, D_MODEL)),
        _const_spec((CONV_WIDTH, D_CONV)), _const_spec((1, D_CONV)),
        _const_spec((1, D_CONV)), _const_spec((1, D_CONV)),
        _const_spec((2, D_HGRN)), _const_spec((1, D_HGRN)),
    ]


def _prompt_mixer(x, mk, mv, vectors, weights, later_weights):
    batch, seq, _ = x.shape
    n_mem = mk.shape[1]
    tl = PROMPT_TILE * PROMPT_STEP_TILES
    n_t = seq // tl
    cast_specs = _cast_specs(later_weights, batch * n_t, lambda b, t: b * n_t + t)
    return pl.pallas_call(
        _prompt_mixer_kernel,
        grid=(batch, n_t),
        in_specs=[
            pl.BlockSpec((None, tl, D_MODEL), lambda b, t: (b, t, 0)),
            pl.BlockSpec((None, n_mem, D_MEM), lambda b, t: (b, 0, 0)),
            pl.BlockSpec((None, n_mem, D_MEM), lambda b, t: (b, 0, 0)),
        ] + _vector_specs() + _mixer_weight_specs(*weights) + cast_specs,
        out_specs=[
            pl.BlockSpec((None, tl, D_MODEL), lambda b, t: (b, t, 0)),
            pl.BlockSpec((None, CONV_PREFIX, D_CONV), lambda b, t: (b, 0, 0)),
            pl.BlockSpec((None, HG_HEADS, HG_DK, HG_DV), lambda b, t: (b, 0, 0, 0)),
        ] + cast_specs,
        out_shape=[
            jax.ShapeDtypeStruct((batch, seq, D_MODEL), F32),
            jax.ShapeDtypeStruct((batch, CONV_PREFIX, D_CONV), F32),
            jax.ShapeDtypeStruct((batch, HG_HEADS, HG_DK, HG_DV), F32),
        ] + [jax.ShapeDtypeStruct(w.shape, BF16) for w in later_weights],
        scratch_shapes=[
            pltpu.VMEM((tl + CONV_ROWS, D_CONV), F32),
            pltpu.VMEM((CONV_ROWS, D_CONV), F32),
            pltpu.VMEM((HG_HEADS, HG_DV, HG_DK), F32),
        ],
        compiler_params=pltpu.CompilerParams(
            dimension_semantics=("arbitrary", "arbitrary"), vmem_limit_bytes=VMEM_LIMIT),
        name="prompt_mixer",
    )(x, mk, mv, *vectors, *weights, *later_weights)


def _store_lane_chunks(ref, x):
    for i in range(ref.shape[0]):
        ref[i] = x[:, i * LANES:(i + 1) * LANES]


def _load_lane_chunks(ref, rows=slice(None)):
    return jnp.concatenate([ref[i, rows, :] for i in range(ref.shape[0])], axis=1)


def _lane_chunk_spec(tile, width):
    return pl.BlockSpec((width // LANES, tile, LANES), lambda i: (0, i, 0))


def _sample_in_kernel(x_ref, pre_g_ref, lb_logits_ref, w_in_ref,
                      u_ref, qd_ref, ki_ref, ke_ref, v_ref, blast_ref, q4_ref, hgate_ref, gates_ref):
    n_seq = x_ref.shape[0] // SUBLANES
    hb = _rms(x_ref[...], pre_g_ref[...]).astype(BF16)
    _store_lane_chunks(
        u_ref, _dot(hb, w_in_ref[:, COL_CA:COL_CB]) * _sigmoid(_dot(hb, w_in_ref[:, COL_CB:COL_HQ])))
    lb = _lower_bound(lb_logits_ref[...])
    q_dec, k_inv, k_end, b_last = _hgrn_gates(
        _dot(hb, w_in_ref[:, COL_HQ:COL_HF]), _dot(hb, w_in_ref[:, COL_HF:COL_HI]), lb, SUBLANES)
    qd_ref[...] = q_dec
    ki_ref[...] = k_inv
    ke_ref[...] = k_end
    blast_ref[...] = b_last
    v_ref[...] = _dot(hb, w_in_ref[:, COL_HI:COL_HGATE])
    hgate_ref[...] = _silu(_dot(hb, w_in_ref[:, COL_HGATE:COL_MQ]))
    mq = _dot(hb, w_in_ref[:, COL_MQ:COL_GATES])
    for h in range(MEM_HEADS):
        q4_ref[:, h * SUBLANES:(h + 1) * SUBLANES, :] = (
            mq[:, h * MEM_HEAD_DIM:(h + 1) * MEM_HEAD_DIM].reshape(n_seq, SUBLANES, MEM_HEAD_DIM))
    for i in range(3):
        gates_ref[:, i * D_MODEL:(i + 1) * D_MODEL] = _branch_gate(hb, w_in_ref, i)


def _sample_seq_kernel(u_ref, qd_ref, ki_ref, ke_ref, v_ref, blast_ref, q4_ref, sc_ref, sh_ref,
                       k2_ref, v2_ref, conv_w_ref,
                       c_ref, o_ref, om_ref, new_conv_ref, new_hg_ref):
    n_seq, dec = sc_ref.shape[1], SUBLANES
    rows = n_seq * dec

    q_rows = MEM_HEADS * dec
    s = jnp.concatenate([_dot_nt(q4_ref[g].astype(BF16), k2_ref[g].astype(BF16)) for g in range(n_seq)],
                        axis=0)
    assert dec & (dec - 1) == 0 and MEM_HEADS & (MEM_HEADS - 1) == 0
    q_head = (lax.broadcasted_iota(jnp.int32, s.shape, 0) >> (dec.bit_length() - 1)) & (MEM_HEADS - 1)
    k_head = lax.broadcasted_iota(jnp.int32, s.shape, 1) & (MEM_HEADS - 1)
    p = _attention_softmax(s, q_head == k_head).astype(BF16)
    om = []
    for g in range(n_seq):
        om_g = _dot(p[g * q_rows:(g + 1) * q_rows], v2_ref[g].astype(BF16))
        om.extend(om_g[h * dec:(h + 1) * dec] for h in range(MEM_HEADS))

    u_slabs = [_load_lane_chunks(u_ref, pl.ds(step, n_seq, stride=dec)) for step in range(dec)]
    conv = [None] * dec
    for s in range(CONV_PREFIX + dec):
        slab = sc_ref[s] if s < CONV_PREFIX else u_slabs[s - CONV_PREFIX]
        if s >= dec:
            new_conv_ref[s - dec] = slab
        for step in range(max(0, s - CONV_PREFIX), min(dec, s + 1)):
            term = slab * conv_w_ref[s - step:s - step + 1, :]
            conv[step] = term if conv[step] is None else conv[step] + term

    row = lax.broadcasted_iota(jnp.int32, (rows, rows), 0)
    col = lax.broadcasted_iota(jnp.int32, (rows, rows), 1)
    causal = ((row ^ col) < dec) & (col <= row)
    decay = jnp.exp(blast_ref[...])
    for h in range(HG_HEADS):
        hs = slice(h * HG_DK, (h + 1) * HG_DK)
        scores = jnp.where(causal, _dot_nt(qd_ref[:, hs].astype(BF16), ki_ref[:, hs].astype(BF16)), 0.0)
        o_intra = _dot(scores.astype(BF16), v_ref[:, hs].astype(BF16))
        for g in range(n_seq):
            rs = slice(g * dec, (g + 1) * dec)
            s0 = sh_ref[g, h]
            o_ref[rs, hs] = o_intra[rs] + _dot(qd_ref[rs, hs].astype(BF16), s0.astype(BF16))
            decay_col = jnp.broadcast_to(decay[g:g + 1, hs], (HG_DV, HG_DK)).T
            new_hg_ref[g, h] = decay_col * s0 + _dot_tn(ke_ref[rs, hs].astype(BF16),
                                                        v_ref[rs, hs].astype(BF16))

    for step in range(dec):
        for i in range(c_ref.shape[0]):
            c_ref[i, pl.ds(step, n_seq, stride=dec), :] = conv[step][:, i * LANES:(i + 1) * LANES]
    for g in range(n_seq):
        om_ref[g * dec:(g + 1) * dec, :] = jnp.concatenate(om[g * MEM_HEADS:(g + 1) * MEM_HEADS], axis=1)


def _sample_out_kernel(x_ref, c_ref, o_ref, om_ref, hgate_ref, gates_ref, post_g_ref, conv_b_ref,
                       ln_g_ref, ln_b_ref, hg_g_ref, w_conv_out_ref, w_hg_out_ref, w_mem_out_ref,
                       w_out_ref, ffn_pre_g_ref, ffn_post_g_ref, w_gate_ref, w_up_ref, w_down_ref, y_ref):
    c = _conv_ln_silu(_load_lane_chunks(c_ref), conv_b_ref[...], ln_g_ref[...], ln_b_ref[...])
    p_conv = _dot(c.astype(BF16), w_conv_out_ref[...])
    o = _head_rms(o_ref[...], hg_g_ref[...]) * hgate_ref[...]
    p_hg = _dot(o.astype(BF16), w_hg_out_ref[...])
    p_mem = _dot(om_ref[...].astype(BF16), w_mem_out_ref[...])
    x1 = _merge_out(lambda i: gates_ref[:, i * D_MODEL:(i + 1) * D_MODEL], x_ref[...],
                    (p_conv, p_hg, p_mem), w_out_ref, post_g_ref[...])
    y_ref[...] = _ffn_block(x1, ffn_pre_g_ref[...], ffn_post_g_ref[...], w_gate_ref, w_up_ref, w_down_ref)


def _row_spec(tile, width):
    return pl.BlockSpec((tile, width), lambda i: (i, 0))


def _sample_in(x2d, pre_g, lb_logits, w_in):
    rows = x2d.shape[0]
    tile = SAMPLE_TILE
    seqs = tile // SUBLANES
    wide = [D_HGRN, D_HGRN, D_HGRN, D_HGRN]
    return pl.pallas_call(
        _sample_in_kernel,
        grid=(rows // tile,),
        in_specs=[_row_spec(tile, D_MODEL), _const_spec((1, D_MODEL)), _const_spec((2, D_HGRN)),
                  _const_spec(w_in.shape)],
        out_specs=[_lane_chunk_spec(tile, D_CONV)] + [_row_spec(tile, w) for w in wide] + [
            _row_spec(seqs, D_HGRN),
            pl.BlockSpec((seqs, MEM_HEADS * SUBLANES, MEM_HEAD_DIM), lambda i: (i, 0, 0)),
            _row_spec(tile, D_HGRN),
            _row_spec(tile, 3 * D_MODEL),
        ],
        out_shape=[jax.ShapeDtypeStruct((D_CONV // LANES, rows, LANES), F32)]
        + [jax.ShapeDtypeStruct((rows, w), F32) for w in wide] + [
            jax.ShapeDtypeStruct((rows // SUBLANES, D_HGRN), F32),
            jax.ShapeDtypeStruct((rows // SUBLANES, MEM_HEADS * SUBLANES, MEM_HEAD_DIM), F32),
            jax.ShapeDtypeStruct((rows, D_HGRN), F32),
            jax.ShapeDtypeStruct((rows, 3 * D_MODEL), F32),
        ],
        compiler_params=pltpu.CompilerParams(
            dimension_semantics=("parallel",), vmem_limit_bytes=VMEM_LIMIT),
        name="sample_in",
    )(x2d, pre_g, lb_logits, w_in)


def _sample_seq(u, qd, ki, ke, v, blast, q4, conv_rows, state_hgrn, k2, v2, conv_w):
    n = conv_rows.shape[1]
    g = SAMPLE_GROUP
    rows = g * SUBLANES
    kv_rows = k2.shape[1]
    seq_block = lambda *tail: pl.BlockSpec((g,) + tail, lambda i: (i,) + (0,) * len(tail))
    kv_block = pl.BlockSpec((g, kv_rows, MEM_HEAD_DIM), lambda i: (i, 0, 0), pipeline_mode=pl.Buffered(3))
    conv_block = pl.BlockSpec((CONV_PREFIX, g, D_CONV), lambda i: (0, i, 0))
    in_specs = [_lane_chunk_spec(rows, D_CONV)] + [_row_spec(rows, D_HGRN)] * 4 + [
        _row_spec(g, D_HGRN),
        seq_block(MEM_HEADS * SUBLANES, MEM_HEAD_DIM),
        conv_block,
        seq_block(HG_HEADS, HG_DK, HG_DV),
        kv_block, kv_block,
        pl.BlockSpec((CONV_WIDTH, D_CONV), lambda i: (0, 0)),
    ]
    out_specs = [_lane_chunk_spec(rows, D_CONV), _row_spec(rows, D_HGRN), _row_spec(rows, D_MEM),
                 conv_block, seq_block(HG_HEADS, HG_DK, HG_DV)]

    def streamed(*refs):
        pltpu.emit_pipeline(_sample_seq_kernel, grid=(n // g,), in_specs=in_specs, out_specs=out_specs)(*refs)

    any_spec = pl.BlockSpec(memory_space=pl.ANY)
    return pl.pallas_call(
        streamed,
        in_specs=[any_spec] * len(in_specs),
        out_specs=[any_spec] * len(out_specs),
        out_shape=[
            jax.ShapeDtypeStruct((D_CONV // LANES, n * SUBLANES, LANES), F32),
            jax.ShapeDtypeStruct((n * SUBLANES, D_HGRN), F32),
            jax.ShapeDtypeStruct((n * SUBLANES, D_MEM), F32),
            jax.ShapeDtypeStruct((CONV_PREFIX, n, D_CONV), F32),
            jax.ShapeDtypeStruct((n, HG_HEADS, HG_DK, HG_DV), F32),
        ],
        compiler_params=pltpu.CompilerParams(vmem_limit_bytes=VMEM_LIMIT),
        name="sample_seq",
    )(u, qd, ki, ke, v, blast, q4, conv_rows, state_hgrn, k2, v2, conv_w)


def _sample_out(x2d, c, o, om, hgate, gates, vectors, weights, ffn_vectors, ffn_weights):
    rows = x2d.shape[0]
    tile = SAMPLE_TILE
    acts = (x2d, c, o, om, hgate, gates)
    consts = tuple(vectors) + tuple(weights) + tuple(ffn_vectors) + tuple(ffn_weights)
    return pl.pallas_call(
        _sample_out_kernel,
        grid=(rows // tile,),
        in_specs=[_lane_chunk_spec(tile, D_CONV) if a is c else _row_spec(tile, a.shape[1]) for a in acts]
        + [_const_spec(a.shape) for a in consts],
        out_specs=_row_spec(tile, D_MODEL),
        out_shape=jax.ShapeDtypeStruct((rows, D_MODEL), F32),
        compiler_params=pltpu.CompilerParams(
            dimension_semantics=("parallel",), vmem_limit_bytes=VMEM_LIMIT),
        name="sample_out",
    )(*acts, *consts)


def _ffn_kernel(x_ref, pre_g_ref, post_g_ref, w_gate_ref, w_up_ref, w_down_ref, y_ref):
    for r0 in range(0, x_ref.shape[0], FFN_BLOCK):
        rows = slice(r0, r0 + FFN_BLOCK)
        y_ref[rows, :] = _ffn_block(x_ref[rows, :], pre_g_ref[...], post_g_ref[...],
                                    w_gate_ref, w_up_ref, w_down_ref)


def _ffn(x2d, pre_g, post_g, w_gate, w_up, w_down):
    rows = x2d.shape[0]
    tile = min(FFN_TILE, rows)
    return pl.pallas_call(
        _ffn_kernel,
        grid=(rows // tile,),
        in_specs=[
            pl.BlockSpec((tile, D_MODEL), lambda i: (i, 0)),
            _const_spec((1, D_MODEL)), _const_spec((1, D_MODEL)),
            _const_spec(w_gate.shape), _const_spec(w_up.shape), _const_spec(w_down.shape),
        ],
        out_specs=pl.BlockSpec((tile, D_MODEL), lambda i: (i, 0)),
        out_shape=jax.ShapeDtypeStruct((rows, D_MODEL), F32),
        compiler_params=pltpu.CompilerParams(
            dimension_semantics=("parallel",), vmem_limit_bytes=VMEM_LIMIT),
        name="ffn",
    )(x2d, pre_g, post_g, w_gate, w_up, w_down)


def kernel(x_prompt, x_sample, mem_prompt, state_conv, state_hgrn, cache_mem_k, cache_mem_v, norm_pre_mix, norm_post_mix, norm_pre_ffn, norm_post_ffn, w_in, conv_w, conv_b, conv_ln_g, conv_ln_b, w_conv_out, hg_lb_logits, hg_norm_g, w_hg_out, mem_norm_g, w_mem_kv, w_mem_out, w_out, w_ffn_gate, w_ffn_up, w_ffn_down):
    depth = w_in.shape[0]
    assert depth == 1 and hg_lb_logits.shape[0] == 2, "single-layer step"
    batch, seq, _ = x_prompt.shape
    n_dec, dec, _ = x_sample.shape
    n_mem = mem_prompt.shape[1]
    assert seq % PROMPT_TILE == 0 and n_dec % SAMPLE_GROUP == 0 and dec == SUBLANES
    assert (n_dec * dec) % SAMPLE_TILE == 0 and (batch * n_mem) % MEMORY_TILE == 0

    vectors = (norm_pre_mix, norm_post_mix, conv_w[0], conv_b, conv_ln_g, conv_ln_b,
               hg_lb_logits, jnp.tile(hg_norm_g, (1, HG_HEADS)))

    mk, mv, kb, vb, *weights = _memory_kv(
        mem_prompt.reshape(batch * n_mem, D_MODEL), mem_norm_g, w_mem_kv[0], MEMORY_TILE,
        tuple(w[0] for w in (w_in, w_conv_out, w_hg_out, w_mem_out, w_out)))

    xp, conv_p, hg_p, *ffn_weights = _prompt_mixer(
        x_prompt, kb.reshape(batch, n_mem, D_MEM), vb.reshape(batch, n_mem, D_MEM), vectors, weights,
        (w_ffn_gate[0], w_ffn_up[0], w_ffn_down[0]))
    yp = _ffn(xp.reshape(batch * seq, D_MODEL), norm_pre_ffn, norm_post_ffn, *ffn_weights)

    xs2d = x_sample.reshape(n_dec * dec, D_MODEL)
    u, qd, ki, ke, v, blast, q4, hgate, gates = _sample_in(xs2d, norm_pre_mix, hg_lb_logits, weights[0])
    k2 = cache_mem_k[0].reshape(n_dec, n_mem * MEM_HEADS, MEM_HEAD_DIM)
    v2 = cache_mem_v[0].reshape(n_dec, n_mem * MEM_HEADS, MEM_HEAD_DIM)
    c, o, om, conv_rows, hg_s = _sample_seq(u, qd, ki, ke, v, blast, q4,
                                            jnp.transpose(state_conv[0], (1, 0, 2)), state_hgrn[0],
                                            k2, v2, conv_w[0])
    conv_s = jnp.transpose(conv_rows, (1, 0, 2))
    ys = _sample_out(xs2d, c, o, om, hgate, gates,
                     (norm_post_mix, conv_b, conv_ln_g, conv_ln_b, vectors[-1]), weights[1:],
                     (norm_pre_ffn, norm_post_ffn), ffn_weights)

    kv_shape = (1, batch, n_mem, MEM_HEADS, MEM_HEAD_DIM)
    return (yp.reshape(batch, seq, D_MODEL), ys.reshape(n_dec, dec, D_MODEL),
            conv_p[None], hg_p[None], mk.reshape(kv_shape), mv.reshape(kv_shape),
            conv_s[None], hg_s[None])
```
